```python
import jax, jax.numpy as jnp
from jax import lax
import numpy as np

D_MODEL = 1024
BATCH = 4
SEQ = 4096
DEPTH = 1
DEC_BATCH = 128
DEC_SEQ = 1
PAST_LEN = 8192
PAGE_SIZE = 128

HEAD_DIM = 64
N_Q_HEADS = 8
N_KV_HEADS = 2
GQA_GROUP = N_Q_HEADS // N_KV_HEADS
WINDOW = 128
ROT_DIM = HEAD_DIM // 4
ROPE_THETA = 500000.0
N_HGRN_HEADS = 8
HGRN_DK = 64
HGRN_DV = 64
HGRN_CHUNK = 64
D_ATTN = N_Q_HEADS * HEAD_DIM
D_KV = N_KV_HEADS * HEAD_DIM
D_HGRN_K = N_HGRN_HEADS * HGRN_DK
D_HGRN = N_HGRN_HEADS * HGRN_DV
D_MIX = D_ATTN + D_HGRN
IN_SPLITS = (D_ATTN, D_KV, D_KV, D_HGRN_K, D_HGRN_K, D_HGRN, D_HGRN)
IN_COLS = sum(IN_SPLITS)
D_FF = 2816
EPS = 1e-6
NEG_INF = -1e30

kernel_name = "hymba_swa_sink_hgrn2_macaron_step"


def rms_norm(x, g):
    xf = x.astype(jnp.float32)
    r = xf * lax.rsqrt(jnp.mean(xf * xf, axis=-1, keepdims=True) + EPS)
    return (r * g.astype(jnp.float32)).astype(x.dtype)


def half_step_ffn(x, g_pre, g_post, w_gu, w_down):
    h = rms_norm(x, g_pre)
    gate, up = jnp.split(h @ w_gu, 2, axis=-1)
    y = (jax.nn.silu(gate) * up) @ w_down
    return x + 0.5 * rms_norm(y, g_post)


def partial_rope(x, pos):
    half = ROT_DIM // 2
    inv = ROPE_THETA ** (-jnp.arange(half, dtype=jnp.float32) / half)
    ang = pos[:, None] * inv[None, :]
    cos = jnp.cos(ang)[:, None, :]
    sin = jnp.sin(ang)[:, None, :]
    xf = x.astype(jnp.float32)
    x1, x2, rest = xf[..., :half], xf[..., half:ROT_DIM], xf[..., ROT_DIM:]
    out = jnp.concatenate([x1 * cos - x2 * sin, x2 * cos + x1 * sin, rest], axis=-1)
    return out.astype(x.dtype)


def sink_softmax(s, mask, sink):
    s = jnp.where(mask, s, NEG_INF)
    m = jnp.maximum(jnp.max(s, axis=-1, keepdims=True), sink)
    p = jnp.exp(s - m)
    denom = jnp.sum(p, axis=-1, keepdims=True) + jnp.exp(sink - m)
    return p / denom


def mixer_inputs(proj, pos, lb):
    B, T, _ = proj.shape
    offs = [int(o) for o in np.cumsum(IN_SPLITS)[:-1]]
    q, k, v, hq, hf, hi, hg = jnp.split(proj, offs, axis=-1)
    q = partial_rope(q.reshape(B, T, N_Q_HEADS, HEAD_DIM), pos)
    k = partial_rope(k.reshape(B, T, N_KV_HEADS, HEAD_DIM), pos)
    v = v.reshape(B, T, N_KV_HEADS, HEAD_DIM)
    hq = hq.reshape(B, T, N_HGRN_HEADS, HGRN_DK).astype(jnp.float32) * (HGRN_DK ** -0.5)
    lbh = lb.reshape(N_HGRN_HEADS, HGRN_DK)
    f = lbh + (1.0 - lbh) * jax.nn.sigmoid(hf.reshape(B, T, N_HGRN_HEADS, HGRN_DK).astype(jnp.float32))
    logf = jnp.log(f)
    hk = 1.0 - f
    hv = hi.reshape(B, T, N_HGRN_HEADS, HGRN_DV).astype(jnp.float32)
    hg = hg.reshape(B, T, N_HGRN_HEADS, HGRN_DV)
    return q, k, v, hq, hk, logf, hv, hg


def swa_prompt(q, k, v, sinks):
    B, T, _, _ = q.shape
    nb = T // WINDOW
    qb = q.reshape(B, nb, WINDOW, N_KV_HEADS, GQA_GROUP, HEAD_DIM)
    pad = jnp.zeros((B, WINDOW, N_KV_HEADS, HEAD_DIM), k.dtype)
    kp = jnp.concatenate([pad, k], axis=1)
    vp = jnp.concatenate([pad.astype(v.dtype), v], axis=1)
    kb = jnp.concatenate([kp[:, :T].reshape(B, nb, WINDOW, N_KV_HEADS, HEAD_DIM),
                          k.reshape(B, nb, WINDOW, N_KV_HEADS, HEAD_DIM)], axis=2)
    vb = jnp.concatenate([vp[:, :T].reshape(B, nb, WINDOW, N_KV_HEADS, HEAD_DIM),
                          v.reshape(B, nb, WINDOW, N_KV_HEADS, HEAD_DIM)], axis=2)
    s = jnp.einsum('bnqhgd,bnkhd->bnhgqk', qb, kb).astype(jnp.float32) * (HEAD_DIM ** -0.5)
    i = jnp.arange(WINDOW)[:, None]
    j = jnp.arange(2 * WINDOW)[None, :]
    rel = i + WINDOW - j
    band = (rel >= 0) & (rel < WINDOW)
    valid = band[None] & ((jnp.arange(nb)[:, None, None] > 0) | (j[None] >= WINDOW))
    sink = sinks.astype(jnp.float32).reshape(1, 1, N_KV_HEADS, GQA_GROUP, 1, 1)
    p = sink_softmax(s, valid[None, :, None, None], sink)
    o = jnp.einsum('bnhgqk,bnkhd->bnqhgd', p.astype(vb.dtype), vb)
    keep = min(WINDOW, T)
    return o.reshape(B, T, D_ATTN), k[:, T - keep:], v[:, T - keep:]


def swa_sample(q, k, v, win_k, win_v, sinks):
    B, T, _, _ = q.shape
    W = win_k.shape[1]
    keys = jnp.concatenate([win_k.astype(k.dtype), k], axis=1)
    vals = jnp.concatenate([win_v.astype(v.dtype), v], axis=1)
    qg = q.reshape(B, T, N_KV_HEADS, GQA_GROUP, HEAD_DIM)
    s = jnp.einsum('bqhgd,bkhd->bhgqk', qg, keys).astype(jnp.float32) * (HEAD_DIM ** -0.5)
    kpos = jnp.concatenate([PAST_LEN - W + jnp.arange(W), PAST_LEN + jnp.arange(T)])
    qpos = PAST_LEN + jnp.arange(T)
    rel = qpos[:, None] - kpos[None, :]
    valid = (rel >= 0) & (rel < WINDOW)
    sink = sinks.astype(jnp.float32).reshape(1, N_KV_HEADS, GQA_GROUP, 1, 1)
    p = sink_softmax(s, valid, sink)
    o = jnp.einsum('bhgqk,bkhd->bqhgd', p.astype(vals.dtype), vals)
    return o.reshape(B, T, D_ATTN), keys[:, T:], vals[:, T:]


def hgrn2_chunk(S, q, k, logf, v):
    C = q.shape[1]
    G = jnp.cumsum(logf, axis=1)
    o_inter = jnp.einsum('bchk,bhkv->bchv', q * jnp.exp(G), S)
    tri = jnp.tril(jnp.ones((C, C), dtype=bool))
    diff = G[:, :, None] - G[:, None, :]
    decay = jnp.where(tri[None, :, :, None, None], jnp.exp(jnp.minimum(diff, 0.0)), 0.0)
    A = jnp.einsum('bthk,bshk,btshk->bhts', q, k, decay)
    o_intra = jnp.einsum('bhts,bshv->bthv', A, v)
    GC = G[:, -1]
    kd = k * jnp.exp(GC[:, None] - G)
    S_new = jnp.exp(GC)[..., None] * S + jnp.einsum('bshk,bshv->bhkv', kd, v)
    return S_new, o_inter + o_intra


def hgrn2_prompt(hq, hk, logf, hv):
    B, T = hq.shape[:2]
    C = min(HGRN_CHUNK, T)
    nc = T // C

    def to_chunks(a):
        return a.reshape(B, nc, C, *a.shape[2:]).swapaxes(0, 1)

    S0 = jnp.zeros((B, N_HGRN_HEADS, HGRN_DK, HGRN_DV), jnp.float32)
    S, o = lax.scan(lambda S, xs: hgrn2_chunk(S, *xs), S0,
                    (to_chunks(hq), to_chunks(hk), to_chunks(logf), to_chunks(hv)))
    return o.swapaxes(0, 1).reshape(B, T, N_HGRN_HEADS, HGRN_DV), S


def mixer_output(att, ho, hg, out_g, w_o):
    B, T, _ = att.shape
    hn = rms_norm(ho, out_g) * jax.nn.silu(hg.astype(jnp.float32))
    cat = jnp.concatenate([att, hn.reshape(B, T, D_HGRN).astype(att.dtype)], axis=-1)
    return cat @ w_o


def setup_inputs(seed: int = 0) -> dict:
    key = jax.random.key(seed)
    ks = jax.random.split(key, 24)
    f32 = jnp.float32

    def w(k, shape, fan_in):
        return jax.random.normal(k, shape, f32) * fan_in ** -0.5

    def gain(k, shape):
        return 1.0 + 0.05 * jax.random.normal(k, shape, f32)

    win_keep = min(WINDOW, PAST_LEN)
    return {
        "x_prompt": jax.random.normal(ks[0], (BATCH, SEQ, D_MODEL), f32),
        "x_sample": jax.random.normal(ks[1], (DEC_BATCH, DEC_SEQ, D_MODEL), f32),
        "cache_win_k": jax.random.normal(ks[2], (DEPTH, DEC_BATCH, win_keep, N_KV_HEADS, HEAD_DIM), f32),
        "cache_win_v": jax.random.normal(ks[3], (DEPTH, DEC_BATCH, win_keep, N_KV_HEADS, HEAD_DIM), f32),
        "state_hgrn": 0.3 * jax.random.normal(ks[4], (DEPTH, DEC_BATCH, N_HGRN_HEADS, HGRN_DK, HGRN_DV), f32),
        "ffn1_pre_g": gain(ks[5], (DEPTH, D_MODEL)),
        "ffn1_post_g": gain(ks[6], (DEPTH, D_MODEL)),
        "ffn1_w_gu": w(ks[7], (DEPTH, D_MODEL, 2 * D_FF), D_MODEL),
        "ffn1_w_down": w(ks[8], (DEPTH, D_FF, D_MODEL), D_FF),
        "mix_pre_g": gain(ks[9], (DEPTH, D_MODEL)),
        "mix_post_g": gain(ks[10], (DEPTH, D_MODEL)),
        "w_in": w(ks[11], (DEPTH, D_MODEL, IN_COLS), D_MODEL),
        "attn_sinks": 0.5 * jax.random.normal(ks[12], (DEPTH, N_Q_HEADS), f32),
        "hgrn_lb": 0.1 * jax.random.normal(ks[13], (DEPTH + 1, D_HGRN_K), f32),
        "hgrn_out_g": gain(ks[14], (DEPTH, HGRN_DV)),
        "w_out": w(ks[15], (DEPTH, D_MIX, D_MODEL), D_MIX),
        "ffn2_pre_g": gain(ks[16], (DEPTH, D_MODEL)),
        "ffn2_post_g": gain(ks[17], (DEPTH, D_MODEL)),
        "ffn2_w_gu": w(ks[18], (DEPTH, D_MODEL, 2 * D_FF), D_MODEL),
        "ffn2_w_down": w(ks[19], (DEPTH, D_FF, D_MODEL), D_FF),
    }


def reference(x_prompt, x_sample, cache_win_k, cache_win_v, state_hgrn,
              ffn1_pre_g, ffn1_post_g, ffn1_w_gu, ffn1_w_down,
              mix_pre_g, mix_post_g, w_in, attn_sinks, hgrn_lb, hgrn_out_g, w_out,
              ffn2_pre_g, ffn2_post_g, ffn2_w_gu, ffn2_w_down):
    T_p = x_prompt.shape[1]
    T_s = x_sample.shape[1]
    pos_p = jnp.arange(T_p, dtype=jnp.float32)
    pos_s = PAST_LEN + jnp.arange(T_s, dtype=jnp.float32)
    lb_all = jnp.cumsum(jax.nn.softmax(hgrn_lb.astype(jnp.float32), axis=0), axis=0)

    xp, xs = x_prompt, x_sample
    wk_p, wv_p, S_p_all, wk_s, wv_s, S_s_all = [], [], [], [], [], []
    for l in range(DEPTH):
        xp = half_step_ffn(xp, ffn1_pre_g[l], ffn1_post_g[l], ffn1_w_gu[l], ffn1_w_down[l])
        xs = half_step_ffn(xs, ffn1_pre_g[l], ffn1_post_g[l], ffn1_w_gu[l], ffn1_w_down[l])

        q, k, v, hq, hk, logf, hv, hg = mixer_inputs(rms_norm(xp, mix_pre_g[l]) @ w_in[l], pos_p, lb_all[l])
        att, nk, nv = swa_prompt(q, k, v, attn_sinks[l])
        ho, S_p = hgrn2_prompt(hq, hk, logf, hv)
        xp = xp + rms_norm(mixer_output(att, ho, hg, hgrn_out_g[l], w_out[l]), mix_post_g[l])
        wk_p.append(nk)
        wv_p.append(nv)
        S_p_all.append(S_p.astype(x_prompt.dtype))

        q, k, v, hq, hk, logf, hv, hg = mixer_inputs(rms_norm(xs, mix_pre_g[l]) @ w_in[l], pos_s, lb_all[l])
        att, nk, nv = swa_sample(q, k, v, cache_win_k[l], cache_win_v[l], attn_sinks[l])
        S_s, ho = hgrn2_chunk(state_hgrn[l].astype(jnp.float32), hq, hk, logf, hv)
        xs = xs + rms_norm(mixer_output(att, ho, hg, hgrn_out_g[l], w_out[l]), mix_post_g[l])
        wk_s.append(nk.astype(cache_win_k.dtype))
        wv_s.append(nv.astype(cache_win_v.dtype))
        S_s_all.append(S_s.astype(state_hgrn.dtype))

        xp = half_step_ffn(xp, ffn2_pre_g[l], ffn2_post_g[l], ffn2_w_gu[l], ffn2_w_down[l])
        xs = half_step_ffn(xs, ffn2_pre_g[l], ffn2_post_g[l], ffn2_w_gu[l], ffn2_w_down[l])

    return (xp, xs, jnp.stack(wk_p), jnp.stack(wv_p), jnp.stack(S_p_all),
            jnp.stack(wk_s), jnp.stack(wv_s), jnp.stack(S_s_all))
```

```python
import functools

import jax
import jax.numpy as jnp
import numpy as np
from jax import lax
from jax.experimental import pallas as pl
from jax.experimental.pallas import tpu as pltpu

F32 = jnp.float32
BF16 = jnp.bfloat16

D_MODEL = 1024
D_FF = 2816
HEAD_DIM = 64
N_Q_HEADS = 8
N_KV_HEADS = 2
GQA_GROUP = N_Q_HEADS // N_KV_HEADS
WINDOW = 128
PAST_LEN = 8192
ROT_DIM = HEAD_DIM // 4
ROPE_THETA = 500000.0
N_HGRN_HEADS = 8
HGRN_D = 64
D_ATTN = N_Q_HEADS * HEAD_DIM
D_KV = N_KV_HEADS * HEAD_DIM
D_HGRN = N_HGRN_HEADS * HGRN_D
IN_COLS = D_ATTN + 2 * D_KV + 4 * D_HGRN
OFF_Q, OFF_K, OFF_V = 0, D_ATTN, D_ATTN + D_KV
OFF_HQ = D_ATTN + 2 * D_KV
OFF_HF, OFF_HI, OFF_HG = OFF_HQ + D_HGRN, OFF_HQ + 2 * D_HGRN, OFF_HQ + 3 * D_HGRN
EPS = 1e-6
NEG_INF = -1e30
LANES = 128
HALF = LANES // 2

FFN_TILE = 512
MIX_TILE = 256
CHUNK = 64
SUB = 16
N_SUB = CHUNK // SUB
SAMPLE_TILE = 8
VMEM_LIMIT = 56 * 1024 * 1024


def _rms(x, g):
    return x * lax.rsqrt(jnp.mean(x * x, axis=-1, keepdims=True) + EPS) * g


def _silu(x):
    return x * jax.nn.sigmoid(x)


def _dot(a, b):
    return jnp.dot(a, b, preferred_element_type=F32)


def _dot_nt(a, b):
    return lax.dot_general(a, b, (((1,), (1,)), ((), ())), preferred_element_type=F32)


def _dot_tn(a, b):
    return lax.dot_general(a, b, (((0,), (0,)), ((), ())), preferred_element_type=F32)


def _split3(x):
    hi = x.astype(BF16)
    r = x - hi.astype(F32)
    mid = r.astype(BF16)
    lo = (r - mid.astype(F32)).astype(BF16)
    return hi, mid, lo


def _const_spec(shape):
    nd = len(shape)
    return pl.BlockSpec(shape, lambda *_: (0,) * nd, pipeline_mode=pl.Buffered(1))


def _ffn_kernel(x_ref, gpre_ref, gpost_ref, wgu_ref, wd_ref, o_ref):
    x = x_ref[...]
    h = _rms(x, gpre_ref[...]).astype(BF16)
    gate = _dot(h, wgu_ref[:, :D_FF])
    up = _dot(h, wgu_ref[:, D_FF:])
    act = (_silu(gate) * up).astype(BF16)
    y = _dot(act, wd_ref[...])
    o_ref[...] = x + 0.5 * _rms(y, gpost_ref[...])


def _ffn(x, g_pre, g_post, w_gu, w_down, tile):
    n = x.shape[0]
    return pl.pallas_call(
        _ffn_kernel,
        grid=(n // tile,),
        in_specs=[
            pl.BlockSpec((tile, D_MODEL), lambda i: (i, 0)),
            _const_spec((1, D_MODEL)),
            _const_spec((1, D_MODEL)),
            _const_spec((D_MODEL, 2 * D_FF)),
            _const_spec((D_FF, D_MODEL)),
        ],
        out_specs=pl.BlockSpec((tile, D_MODEL), lambda i: (i, 0)),
        out_shape=jax.ShapeDtypeStruct((n, D_MODEL), F32),
        compiler_params=pltpu.CompilerParams(
            dimension_semantics=("arbitrary",), vmem_limit_bytes=VMEM_LIMIT),
        name="ffn",
    )(x, g_pre.reshape(1, D_MODEL), g_post.reshape(1, D_MODEL), w_gu, w_down)


def _lower_bound(lb_raw, layer):
    m = jnp.max(lb_raw, axis=0, keepdims=True)
    e = jnp.exp(lb_raw - m)
    return jnp.sum(e[: layer + 1], axis=0, keepdims=True) / jnp.sum(e, axis=0, keepdims=True)


def _rope(x, cos, sin_hi, sin_lo):
    return x * cos + pltpu.roll(x, ROT_DIM // 2, 1) * sin_hi + pltpu.roll(x, LANES - ROT_DIM // 2, 1) * sin_lo


def _rope_tables(pos):
    half = ROT_DIM // 2
    inv = ROPE_THETA ** (-jnp.arange(half, dtype=F32) / half)
    ang = pos[:, None] * inv[None, :]
    cos, sin = jnp.cos(ang), jnp.sin(ang)
    t = pos.shape[0]
    one = jnp.ones((t, HEAD_DIM - ROT_DIM), F32)
    zero = jnp.zeros((t, HEAD_DIM - ROT_DIM), F32)
    zh = jnp.zeros((t, half), F32)
    c = jnp.concatenate([cos, cos, one], axis=1)
    s_hi = jnp.concatenate([zh, sin, zero], axis=1)
    s_lo = jnp.concatenate([-sin, zh, zero], axis=1)
    return tuple(jnp.tile(a, (1, LANES // HEAD_DIM)) for a in (c, s_hi, s_lo))


def _dup_half(x, g, lo_half):
    xr = pltpu.roll(x, HALF, 1)
    return jnp.where(lo_half, x, xr) if g == 0 else jnp.where(lo_half, xr, x)


def _prompt_mix_kernel(sink_ref, x_ref, gpre_ref, win_ref, cos_ref, shi_ref, slo_ref, lb_ref, gout_ref,
                       wout_ref, gpost_ref,
                       y_ref, wk_ref, wv_ref, s_ref,
                       kk_scr, vv_scr, st_scr, g_scr, ho_scr, *, layer):
    tb = MIX_TILE
    n = pl.program_id(1)
    last = pl.num_programs(1) - 1

    @pl.when(n == 0)
    def _():
        kk_scr[...] = jnp.zeros_like(kk_scr)
        vv_scr[...] = jnp.zeros_like(vv_scr)
        st_scr[...] = jnp.zeros_like(st_scr)

    x = x_ref[...]
    u = _rms(x, gpre_ref[...]).astype(BF16)
    proj = _dot(u, win_ref[...])

    cos, s_hi, s_lo = cos_ref[...], shi_ref[...], slo_ref[...]
    scale = HEAD_DIM ** -0.5
    q_cols = [_rope(proj[:, OFF_Q + LANES * j: OFF_Q + LANES * (j + 1)], cos, s_hi, s_lo) * scale
              for j in range(D_ATTN // LANES)]
    k_rot = _rope(proj[:, OFF_K:OFF_K + D_KV], cos, s_hi, s_lo)
    v_new = proj[:, OFF_V:OFF_V + D_KV]

    @pl.when(n == last)
    def _():
        wk_ref[0] = k_rot[tb - WINDOW:]
        wv_ref[0] = v_new[tb - WINDOW:]

    lane = lax.broadcasted_iota(jnp.int32, (WINDOW, LANES), 1)
    lo_half = lane < HALF
    rows4 = lax.broadcasted_iota(jnp.int32, (GQA_GROUP * WINDOW, 2 * WINDOW), 0) % WINDOW
    cols4 = lax.broadcasted_iota(jnp.int32, (GQA_GROUP * WINDOW, 2 * WINDOW), 1)
    rel = rows4 + WINDOW - cols4
    band = (rel >= 0) & (rel < WINDOW)

    k_prev = [kk_scr[g] for g in range(N_KV_HEADS)]
    v_prev = [vv_scr[g] for g in range(N_KV_HEADS)]
    att_blocks = []
    for i in range(tb // WINDOW):
        r0 = i * WINDOW
        kc, vc = k_rot[r0:r0 + WINDOW], v_new[r0:r0 + WINDOW]
        k_cur = [_dup_half(kc, g, lo_half).astype(BF16) for g in range(N_KV_HEADS)]
        v_cur = [_dup_half(vc, g, lo_half).astype(BF16) for g in range(N_KV_HEADS)]
        valid = band if i > 0 else band & ((cols4 >= WINDOW) | (n > 0))
        att_cols = []
        for g in range(N_KV_HEADS):
            keys = jnp.concatenate([k_prev[g], k_cur[g]], axis=0)
            vals = jnp.concatenate([v_prev[g], v_cur[g]], axis=0)
            heads = range(g * GQA_GROUP, (g + 1) * GQA_GROUP)
            qg = jnp.concatenate(
                [jnp.where(lo_half if h % 2 == 0 else ~lo_half, q_cols[h // 2][r0:r0 + WINDOW], 0.0)
                 for h in heads], axis=0).astype(BF16)
            s = _dot_nt(qg, keys)
            s = jnp.where(valid, s, NEG_INF)
            sink = jnp.concatenate(
                [jnp.full((WINDOW, 1), sink_ref[h], F32) for h in heads], axis=0)
            m = jnp.maximum(jnp.max(s, axis=-1, keepdims=True), sink)
            p = jnp.exp(s - m)
            denom = jnp.sum(p, axis=-1, keepdims=True) + jnp.exp(sink - m)
            o = _dot(p.astype(BF16), vals) / denom
            for jj in range(GQA_GROUP // 2):
                o_even = o[(2 * jj) * WINDOW:(2 * jj + 1) * WINDOW]
                o_odd = o[(2 * jj + 1) * WINDOW:(2 * jj + 2) * WINDOW]
                att_cols.append(jnp.where(lo_half, o_even, o_odd))
        att_blocks.append(jnp.concatenate(att_cols, axis=1))
        k_prev, v_prev = k_cur, v_cur
    att = jnp.concatenate(att_blocks, axis=0)
    for g in range(N_KV_HEADS):
        kk_scr[g] = k_prev[g]
        vv_scr[g] = v_prev[g]

    lb = _lower_bound(lb_ref[...], layer)
    f = lb + (1.0 - lb) * jax.nn.sigmoid(proj[:, OFF_HF:OFF_HF + D_HGRN])
    logf = jnp.log(f)
    hk = 1.0 - f
    hq = proj[:, OFF_HQ:OFF_HQ + D_HGRN] * (HGRN_D ** -0.5)
    hv_bf = proj[:, OFF_HI:OFF_HI + D_HGRN].astype(BF16)

    tr = lax.broadcasted_iota(jnp.int32, (tb, tb), 0)
    tc = lax.broadcasted_iota(jnp.int32, (tb, tb), 1)
    tri = ((tr // CHUNK == tc // CHUNK) & (tc <= tr)).astype(BF16)
    g_cum = sum(_dot(tri, part) for part in _split3(logf))
    g_scr[...] = g_cum

    def bcast_row(r, rows):
        return jnp.broadcast_to(g_scr[r:r + 1, :], (rows, D_HGRN))

    zeros_sub = jnp.zeros((SUB, D_HGRN), F32)
    g_ref_q = jnp.concatenate(
        [zeros_sub if sb % N_SUB == 0 else bcast_row(sb * SUB - 1, SUB) for sb in range(tb // SUB)], axis=0)
    q_loc = (hq * jnp.exp(g_cum - g_ref_q)).astype(BF16)
    q_glob = (hq * jnp.exp(g_cum)).astype(BF16)
    g_end = jnp.concatenate(
        [bcast_row(c * CHUNK + CHUNK - 1, CHUNK) for c in range(tb // CHUNK)], axis=0)
    k_end = (hk * jnp.exp(g_end - g_cum)).astype(BF16)
    row_in_chunk = lax.broadcasted_iota(jnp.int32, (tb, D_HGRN), 0) % CHUNK
    k_sub = []
    for i in range(N_SUB):
        g_ref_i = jnp.concatenate(
            [jnp.zeros((CHUNK, D_HGRN), F32) if i == 0 else bcast_row(c * CHUNK + i * SUB - 1, CHUNK)
             for c in range(tb // CHUNK)], axis=0)
        expo = jnp.where(row_in_chunk < (i + 1) * SUB, g_ref_i - g_cum, NEG_INF)
        k_sub.append((hk * jnp.exp(expo)).astype(BF16))

    sub_of_row = lax.broadcasted_iota(jnp.int32, (CHUNK, HGRN_D), 0) // SUB
    tril = (lax.broadcasted_iota(jnp.int32, (CHUNK, CHUNK), 1)
            <= lax.broadcasted_iota(jnp.int32, (CHUNK, CHUNK), 0))
    state = [st_scr[h] for h in range(N_HGRN_HEADS)]
    for c in range(tb // CHUNK):
        rows = slice(c * CHUNK, (c + 1) * CHUNK)
        decay_c = jnp.exp(g_scr[c * CHUNK + CHUNK - 1:c * CHUNK + CHUNK, :])
        for h in range(N_HGRN_HEADS):
            ls = slice(h * HGRN_D, (h + 1) * HGRN_D)
            q_h = q_loc[rows, ls]
            q_stack = jnp.concatenate(
                [jnp.where(sub_of_row == i, q_h, jnp.zeros_like(q_h)) for i in range(N_SUB)], axis=1)
            k_stack = jnp.concatenate([k_sub[i][rows, ls] for i in range(N_SUB)], axis=1)
            a = _dot_nt(q_stack, k_stack)
            a = jnp.where(tril, a, 0.0).astype(BF16)
            v_h = hv_bf[rows, ls]
            o = _dot(a, v_h) + _dot_nt(q_glob[rows, ls], state[h].astype(BF16))
            state[h] = state[h] * decay_c[:, ls] + _dot_tn(v_h, k_end[rows, ls])
            o = o * lax.rsqrt(jnp.mean(o * o, axis=-1, keepdims=True) + EPS)
            ho_scr[rows, ls] = o
    for h in range(N_HGRN_HEADS):
        st_scr[h] = state[h]

    @pl.when(n == last)
    def _():
        for h in range(N_HGRN_HEADS):
            s_ref[0, h] = state[h].T

    hn = ho_scr[...] * gout_ref[...] * _silu(proj[:, OFF_HG:OFF_HG + D_HGRN])
    cat = jnp.concatenate([att, hn], axis=1).astype(BF16)
    y = _dot(cat, wout_ref[...])
    y_ref[...] = x + _rms(y, gpost_ref[...])


def _prompt_mix(x, sinks, g_pre, w_in, lb_raw, g_out, w_out, g_post, batch, seq, layer):
    tb = MIX_TILE
    nt = seq // tb
    cos, s_hi, s_lo = _rope_tables(jnp.arange(seq, dtype=F32))
    tok = lambda b, n: (b * nt + n, 0)
    tab = lambda b, n: (n, 0)
    per_b3 = lambda b, n: (b, 0, 0)
    lb_rows = lb_raw.shape[0]
    return pl.pallas_call(
        functools.partial(_prompt_mix_kernel, layer=layer),
        grid=(batch, nt),
        in_specs=[
            pl.BlockSpec(memory_space=pltpu.SMEM),
            pl.BlockSpec((tb, D_MODEL), tok),
            _const_spec((1, D_MODEL)),
            _const_spec((D_MODEL, IN_COLS)),
            pl.BlockSpec((tb, LANES), tab),
            pl.BlockSpec((tb, LANES), tab),
            pl.BlockSpec((tb, LANES), tab),
            _const_spec((lb_rows, D_HGRN)),
            _const_spec((1, D_HGRN)),
            _const_spec((D_MODEL, D_MODEL)),
            _const_spec((1, D_MODEL)),
        ],
        out_specs=[
            pl.BlockSpec((tb, D_MODEL), tok),
            pl.BlockSpec((1, WINDOW, D_KV), per_b3),
            pl.BlockSpec((1, WINDOW, D_KV), per_b3),
            pl.BlockSpec((1, N_HGRN_HEADS, HGRN_D, HGRN_D), lambda b, n: (b, 0, 0, 0)),
        ],
        out_shape=[
            jax.ShapeDtypeStruct((batch * seq, D_MODEL), F32),
            jax.ShapeDtypeStruct((batch, WINDOW, D_KV), F32),
            jax.ShapeDtypeStruct((batch, WINDOW, D_KV), F32),
            jax.ShapeDtypeStruct((batch, N_HGRN_HEADS, HGRN_D, HGRN_D), F32),
        ],
        scratch_shapes=[
            pltpu.VMEM((N_KV_HEADS, WINDOW, LANES), BF16),
            pltpu.VMEM((N_KV_HEADS, WINDOW, LANES), BF16),
            pltpu.VMEM((N_HGRN_HEADS, HGRN_D, HGRN_D), F32),
            pltpu.VMEM((tb, D_HGRN), F32),
            pltpu.VMEM((tb, D_HGRN), F32),
        ],
        compiler_params=pltpu.CompilerParams(
            dimension_semantics=("arbitrary", "arbitrary"), vmem_limit_bytes=VMEM_LIMIT),
        name="prompt_mix",
    )(sinks, x, g_pre.reshape(1, D_MODEL), w_in, cos, s_hi, s_lo, lb_raw, g_out.reshape(1, D_HGRN),
      w_out, g_post.reshape(1, D_MODEL))


def _sample_mix_kernel(sink_ref, x_ref, gpre_ref, win_ref, cos_ref, shi_ref, slo_ref, lb_ref, gout_ref,
                       wout_ref, gpost_ref, ck_ref, cv_ref, sin_ref,
                       y_ref, wk_ref, wv_ref, sout_ref,
                       q_scr, kn_scr, vn_scr, hv_scr, hg_scr, colt_scr, hkt_scr, att_scr, ho_scr,
                       *, layer, nb):
    step = pl.program_id(0)
    last = pl.num_programs(0) - 1

    @pl.when(step == 0)
    def _():
        u = _rms(x_ref[...], gpre_ref[...]).astype(BF16)
        proj = _dot(u, win_ref[...])
        cos, s_hi, s_lo = cos_ref[...], shi_ref[...], slo_ref[...]
        scale = HEAD_DIM ** -0.5
        for j in range(D_ATTN // LANES):
            q_scr[:, LANES * j:LANES * (j + 1)] = _rope(
                proj[:, OFF_Q + LANES * j:OFF_Q + LANES * (j + 1)], cos, s_hi, s_lo) * scale
        kn_scr[...] = _rope(proj[:, OFF_K:OFF_K + D_KV], cos, s_hi, s_lo)
        vn_scr[...] = proj[:, OFF_V:OFF_V + D_KV]
        hv_scr[...] = proj[:, OFF_HI:OFF_HI + D_HGRN]
        hg_scr[...] = proj[:, OFF_HG:OFF_HG + D_HGRN]
        lb = _lower_bound(lb_ref[...], layer)
        f = lb + (1.0 - lb) * jax.nn.sigmoid(proj[:, OFF_HF:OFF_HF + D_HGRN])
        f_t = f.T
        f_hi = f_t.astype(BF16)
        colt_scr[0:D_HGRN, :] = f_hi
        colt_scr[D_HGRN:2 * D_HGRN, :] = (f_t - f_hi.astype(F32)).astype(BF16)
        colt_scr[2 * D_HGRN:3 * D_HGRN, :] = (proj[:, OFF_HQ:OFF_HQ + D_HGRN] * (HGRN_D ** -0.5)).T.astype(BF16)
        hkt_scr[...] = (1.0 - f_t).astype(BF16)

    lane8 = lax.broadcasted_iota(jnp.int32, (N_Q_HEADS, LANES), 1)
    row8 = lax.broadcasted_iota(jnp.int32, (N_Q_HEADS, LANES), 0)
    keep8 = (lane8 >= HALF) == (row8 >= GQA_GROUP)
    key_row = lax.broadcasted_iota(jnp.int32, (WINDOW, LANES), 0)
    key_col = lax.broadcasted_iota(jnp.int32, (N_Q_HEADS, WINDOW), 1)
    brow = lax.broadcasted_iota(jnp.int32, (nb, HGRN_D), 0)
    head_row = lax.broadcasted_iota(jnp.int32, (N_HGRN_HEADS, HGRN_D), 0)
    sink = sink_ref[...]

    for bi in range(SAMPLE_TILE):
        b = step * SAMPLE_TILE + bi
        q_b = jnp.broadcast_to(q_scr[pl.ds(b, 1), :], (N_Q_HEADS, D_ATTN))
        qm = jnp.zeros((N_Q_HEADS, LANES), F32)
        for h in range(N_Q_HEADS):
            c = q_b[:, LANES * (h // 2):LANES * (h // 2 + 1)]
            if h % 2 != h // GQA_GROUP:
                c = pltpu.roll(c, HALF, 1)
            qm = jnp.where(row8 == h, c, qm)
        qm = jnp.where(keep8, qm, 0.0)
        k_new = kn_scr[pl.ds(b, 1), :]
        v_new = vn_scr[pl.ds(b, 1), :]
        k_old, v_old = ck_ref[bi], cv_ref[bi]
        s = _dot_nt(qm.astype(BF16), k_old.astype(BF16))
        s = jnp.where(key_col >= 1, s, NEG_INF)
        s_new = jnp.sum(qm * k_new, axis=-1, keepdims=True)
        m = jnp.maximum(jnp.maximum(jnp.max(s, axis=-1, keepdims=True), s_new), sink)
        p = jnp.exp(s - m)
        p_new = jnp.exp(s_new - m)
        denom = jnp.sum(p, axis=-1, keepdims=True) + p_new + jnp.exp(sink - m)
        o = (_dot(p.astype(BF16), v_old.astype(BF16)) + p_new * v_new) / denom
        att_scr[pl.ds(pl.multiple_of(b * N_Q_HEADS, N_Q_HEADS), N_Q_HEADS), :] = o
        wk_ref[bi] = jnp.where(key_row == WINDOW - 1, k_new, pltpu.roll(k_old, WINDOW - 1, 0))
        wv_ref[bi] = jnp.where(key_row == WINDOW - 1, v_new, pltpu.roll(v_old, WINDOW - 1, 0))

        sel = brow == b
        onehot = sel.astype(BF16)
        cols = _dot(colt_scr[...], onehot)
        f_col = cols[0:D_HGRN] + cols[D_HGRN:2 * D_HGRN]
        q_col = cols[2 * D_HGRN:3 * D_HGRN]
        hv_b = hv_scr[pl.ds(b, 1), :]
        ho_b = jnp.zeros((N_HGRN_HEADS, HGRN_D), F32)
        for h in range(N_HGRN_HEADS):
            ks = slice(h * HGRN_D, (h + 1) * HGRN_D)
            v_sel = jnp.where(sel, hv_b[:, ks], 0.0).astype(BF16)
            outer = _dot(hkt_scr[ks, :], v_sel)
            s_new_h = f_col[ks] * sin_ref[bi, h] + outer
            sout_ref[bi, h] = s_new_h
            o_h = jnp.sum(q_col[ks] * s_new_h, axis=0, keepdims=True)
            ho_b = jnp.where(head_row == h, o_h, ho_b)
        ho_scr[pl.ds(pl.multiple_of(b * N_HGRN_HEADS, N_HGRN_HEADS), N_HGRN_HEADS), :] = ho_b

    @pl.when(step == last)
    def _():
        y = jnp.zeros((nb, D_MODEL), F32)
        for h in range(N_HGRN_HEADS):
            ks = slice(h * HGRN_D, (h + 1) * HGRN_D)
            ho_h = ho_scr[pl.ds(h, nb, stride=N_HGRN_HEADS), :]
            hn_h = _rms(ho_h, gout_ref[:, ks]) * _silu(hg_scr[:, ks])
            y = y + _dot(hn_h.astype(BF16), wout_ref[D_ATTN + h * HGRN_D:D_ATTN + (h + 1) * HGRN_D, :])
        for h in range(N_Q_HEADS):
            g = h // GQA_GROUP
            a_h = att_scr[pl.ds(h, nb, stride=N_Q_HEADS), :][:, g * HALF:(g + 1) * HALF]
            y = y + _dot(a_h.astype(BF16), wout_ref[h * HEAD_DIM:(h + 1) * HEAD_DIM, :])
        y_ref[...] = x_ref[...] + _rms(y, gpost_ref[...])


def _sample_mix(x, sinks, g_pre, w_in, lb_raw, g_out, w_out, g_post, cache_k, cache_v, state, pos, layer):
    nb = x.shape[0]
    cos, s_hi, s_lo = _rope_tables(pos)
    lb_rows = lb_raw.shape[0]
    blk3 = pl.BlockSpec((SAMPLE_TILE, WINDOW, D_KV), lambda i: (i, 0, 0))
    blk4 = pl.BlockSpec((SAMPLE_TILE, N_HGRN_HEADS, HGRN_D, HGRN_D), lambda i: (i, 0, 0, 0))
    return pl.pallas_call(
        functools.partial(_sample_mix_kernel, layer=layer, nb=nb),
        grid=(nb // SAMPLE_TILE,),
        in_specs=[
            _const_spec((N_Q_HEADS, 1)),
            _const_spec((nb, D_MODEL)),
            _const_spec((1, D_MODEL)),
            _const_spec((D_MODEL, IN_COLS)),
            _const_spec((1, LANES)),
            _const_spec((1, LANES)),
            _const_spec((1, LANES)),
            _const_spec((lb_rows, D_HGRN)),
            _const_spec((1, D_HGRN)),
            _const_spec((D_MODEL, D_MODEL)),
            _const_spec((1, D_MODEL)),
            blk3, blk3, blk4,
        ],
        out_specs=[pl.BlockSpec((nb, D_MODEL), lambda i: (0, 0)), blk3, blk3, blk4],
        out_shape=[
            jax.ShapeDtypeStruct((nb, D_MODEL), F32),
            jax.ShapeDtypeStruct(cache_k.shape, F32),
            jax.ShapeDtypeStruct(cache_v.shape, F32),
            jax.ShapeDtypeStruct(state.shape, F32),
        ],
        scratch_shapes=[
            pltpu.VMEM((nb, D_ATTN), F32),
            pltpu.VMEM((nb, D_KV), F32),
            pltpu.VMEM((nb, D_KV), F32),
            pltpu.VMEM((nb, D_HGRN), F32),
            pltpu.VMEM((nb, D_HGRN), F32),
            pltpu.VMEM((3 * D_HGRN, nb), BF16),
            pltpu.VMEM((D_HGRN, nb), BF16),
            pltpu.VMEM((nb * N_Q_HEADS, LANES), F32),
            pltpu.VMEM((nb * N_HGRN_HEADS, HGRN_D), F32),
        ],
        compiler_params=pltpu.CompilerParams(
            dimension_semantics=("arbitrary",), vmem_limit_bytes=VMEM_LIMIT),
        name="sample_mix",
    )(sinks.reshape(N_Q_HEADS, 1), x, g_pre.reshape(1, D_MODEL), w_in, cos, s_hi, s_lo, lb_raw,
      g_out.reshape(1, D_HGRN), w_out, g_post.reshape(1, D_MODEL), cache_k, cache_v, state)


def kernel(x_prompt, x_sample, cache_win_k, cache_win_v, state_hgrn, ffn1_pre_g, ffn1_post_g, ffn1_w_gu,
           ffn1_w_down, mix_pre_g, mix_post_g, w_in, attn_sinks, hgrn_lb, hgrn_out_g, w_out, ffn2_pre_g,
           ffn2_post_g, ffn2_w_gu, ffn2_w_down):
    batch, seq, _ = x_prompt.shape
    nb, t_s, _ = x_sample.shape
    depth = w_in.shape[0]
    assert t_s == 1 and seq % MIX_TILE == 0 and (batch * seq) % FFN_TILE == 0 and nb % SAMPLE_TILE == 0
    assert cache_win_k.shape[2:] == (WINDOW, N_KV_HEADS, HEAD_DIM)

    xp = x_prompt.reshape(batch * seq, D_MODEL)
    xs = x_sample.reshape(nb, D_MODEL)
    pos_s = PAST_LEN + jnp.arange(t_s, dtype=F32)
    outs = [[] for _ in range(6)]
    for l in range(depth):
        w_gu1, w_d1 = ffn1_w_gu[l].astype(BF16), ffn1_w_down[l].astype(BF16)
        w_gu2, w_d2 = ffn2_w_gu[l].astype(BF16), ffn2_w_down[l].astype(BF16)
        w_in_l, w_out_l = w_in[l].astype(BF16), w_out[l].astype(BF16)
        g_out = jnp.tile(hgrn_out_g[l], N_HGRN_HEADS)

        xp = _ffn(xp, ffn1_pre_g[l], ffn1_post_g[l], w_gu1, w_d1, FFN_TILE)
        xs = _ffn(xs, ffn1_pre_g[l], ffn1_post_g[l], w_gu1, w_d1, nb)

        xp, wk_p, wv_p, s_p = _prompt_mix(xp, attn_sinks[l], mix_pre_g[l], w_in_l, hgrn_lb, g_out, w_out_l,
                                          mix_post_g[l], batch, seq, l)
        xs, wk_s, wv_s, s_s = _sample_mix(
            xs, attn_sinks[l], mix_pre_g[l], w_in_l, hgrn_lb, g_out, w_out_l, mix_post_g[l],
            cache_win_k[l].reshape(nb, WINDOW, D_KV), cache_win_v[l].reshape(nb, WINDOW, D_KV),
            state_hgrn[l], pos_s, l)

        xp = _ffn(xp, ffn2_pre_g[l], ffn2_post_g[l], w_gu2, w_d2, FFN_TILE)
        xs = _ffn(xs, ffn2_pre_g[l], ffn2_post_g[l], w_gu2, w_d2, nb)

        kv_shape = (WINDOW, N_KV_HEADS, HEAD_DIM)
        for lst, val in zip(outs, (wk_p.reshape(batch, *kv_shape), wv_p.reshape(batch, *kv_shape), s_p,
                                   wk_s.reshape(nb, *kv_shape), wv_s.reshape(nb, *kv_shape), s_s)):
            lst.append(val)

    return (xp.reshape(batch, seq, D_MODEL), xs.reshape(nb, t_s, D_MODEL)) + tuple(jnp.stack(o) for o in outs)
```

```python
import functools

import jax
import jax.numpy as jnp
import numpy as np
from jax import lax
from jax.experimental import pallas as pl
from jax.experimental.pallas import tpu as pltpu

F32 = jnp.float32
BF16 = jnp.bfloat16

D_MODEL = 1024
D_FF = 2816
HEAD_DIM = 64
N_Q_HEADS = 8
N_KV_HEADS = 2
GQA_GROUP = N_Q_HEADS // N_KV_HEADS
WINDOW = 128
PAST_LEN = 8192
ROT_DIM = HEAD_DIM // 4
ROPE_THETA = 500000.0
N_HGRN_HEADS = 8
HGRN_D = 64
D_ATTN = N_Q_HEADS * HEAD_DIM
D_KV = N_KV_HEADS * HEAD_DIM
D_HGRN = N_HGRN_HEADS * HGRN_D
IN_COLS = D_ATTN + 2 * D_KV + 4 * D_HGRN
OFF_Q, OFF_K, OFF_V = 0, D_ATTN, D_ATTN + D_KV
OFF_HQ = D_ATTN + 2 * D_KV
OFF_HF, OFF_HI, OFF_HG = OFF_HQ + D_HGRN, OFF_HQ + 2 * D_HGRN, OFF_HQ + 3 * D_HGRN
EPS = 1e-6
NEG_INF = -1e30
LANES = 128
HALF = LANES // 2

FFN_TILE = 512
MIX_TILE = 256
CHUNK = 64
SUB = 16
N_SUB = CHUNK // SUB
SAMPLE_TILE = 8
VMEM_LIMIT = 56 * 1024 * 1024


def _rms(x, g):
    return x * lax.rsqrt(jnp.mean(x * x, axis=-1, keepdims=True) + EPS) * g


def _silu(x):
    return x * jax.nn.sigmoid(x)


def _dot(a, b):
    return jnp.dot(a, b, preferred_element_type=F32)


def _dot_nt(a, b):
    return lax.dot_general(a, b, (((1,), (1,)), ((), ())), preferred_element_type=F32)


def _dot_tn(a, b):
    return lax.dot_general(a, b, (((0,), (0,)), ((), ())), preferred_element_type=F32)


def _split3(x):
    hi = x.astype(BF16)
    r = x - hi.astype(F32)
    mid = r.astype(BF16)
    lo = (r - mid.astype(F32)).astype(BF16)
    return hi, mid, lo


def _const_spec(shape):
    nd = len(shape)
    return pl.BlockSpec(shape, lambda *_: (0,) * nd, pipeline_mode=pl.Buffered(1))


def _ffn_kernel(x_ref, gpre_ref, gpost_ref, wgu_ref, wd_ref, o_ref):
    x = x_ref[...]
    h = _rms(x, gpre_ref[...]).astype(BF16)
    gate = _dot(h, wgu_ref[:, :D_FF])
    up = _dot(h, wgu_ref[:, D_FF:])
    act = (_silu(gate) * up).astype(BF16)
    y = _dot(act, wd_ref[...])
    o_ref[...] = x + 0.5 * _rms(y, gpost_ref[...])


def _ffn(x, g_pre, g_post, w_gu, w_down, tile):
    n = x.shape[0]
    return pl.pallas_call(
        _ffn_kernel,
        grid=(n // tile,),
        in_specs=[
            pl.BlockSpec((tile, D_MODEL), lambda i: (i, 0)),
            _const_spec((1, D_MODEL)),
            _const_spec((1, D_MODEL)),
            _const_spec((D_MODEL, 2 * D_FF)),
            _const_spec((D_FF, D_MODEL)),
        ],
        out_specs=pl.BlockSpec((tile, D_MODEL), lambda i: (i, 0)),
        out_shape=jax.ShapeDtypeStruct((n, D_MODEL), F32),
        compiler_params=pltpu.CompilerParams(
            dimension_semantics=("arbitrary",), vmem_limit_bytes=VMEM_LIMIT),
        name="ffn",
    )(x, g_pre.reshape(1, D_MODEL), g_post.reshape(1, D_MODEL), w_gu, w_down)


def _lower_bound(lb_raw, layer):
    m = jnp.max(lb_raw, axis=0, keepdims=True)
    e = jnp.exp(lb_raw - m)
    return jnp.sum(e[: layer + 1], axis=0, keepdims=True) / jnp.sum(e, axis=0, keepdims=True)


def _rope(x, cos, sin_hi, sin_lo):
    return x * cos + pltpu.roll(x, ROT_DIM // 2, 1) * sin_hi + pltpu.roll(x, LANES - ROT_DIM // 2, 1) * sin_lo


def _rope_tables(pos):
    half = ROT_DIM // 2
    inv = ROPE_THETA ** (-jnp.arange(half, dtype=F32) / half)
    ang = pos[:, None] * inv[None, :]
    cos, sin = jnp.cos(ang), jnp.sin(ang)
    t = pos.shape[0]
    one = jnp.ones((t, HEAD_DIM - ROT_DIM), F32)
    zero = jnp.zeros((t, HEAD_DIM - ROT_DIM), F32)
    zh = jnp.zeros((t, half), F32)
    c = jnp.concatenate([cos, cos, one], axis=1)
    s_hi = jnp.concatenate([zh, sin, zero], axis=1)
    s_lo = jnp.concatenate([-sin, zh, zero], axis=1)
    return tuple(jnp.tile(a, (1, LANES // HEAD_DIM)) for a in (c, s_hi, s_lo))


def _dup_half(x, g, lo_half):
    xr = pltpu.roll(x, HALF, 1)
    return jnp.where(lo_half, x, xr) if g == 0 else jnp.where(lo_half, xr, x)


def _prompt_mix_kernel(sink_ref, x_ref, gpre_ref, win_ref, cos_ref, shi_ref, slo_ref, lb_ref, gout_ref,
                       wout_ref, gpost_ref,
                       y_ref, wk_ref, wv_ref, s_ref,
                       kk_scr, vv_scr, st_scr, g_scr, ho_scr, *, layer):
    tb = MIX_TILE
    n = pl.program_id(1)
    last = pl.num_programs(1) - 1

    @pl.when(n == 0)
    def _():
        kk_scr[...] = jnp.zeros_like(kk_scr)
        vv_scr[...] = jnp.zeros_like(vv_scr)
        st_scr[...] = jnp.zeros_like(st_scr)

    x = x_ref[...]
    u = _rms(x, gpre_ref[...]).astype(BF16)
    proj = _dot(u, win_ref[...])

    cos, s_hi, s_lo = cos_ref[...], shi_ref[...], slo_ref[...]
    scale = HEAD_DIM ** -0.5
    q_cols = [_rope(proj[:, OFF_Q + LANES * j: OFF_Q + LANES * (j + 1)], cos, s_hi, s_lo) * scale
              for j in range(D_ATTN // LANES)]
    k_rot = _rope(proj[:, OFF_K:OFF_K + D_KV], cos, s_hi, s_lo)
    v_new = proj[:, OFF_V:OFF_V + D_KV]

    @pl.when(n == last)
    def _():
        wk_ref[0] = k_rot[tb - WINDOW:]
        wv_ref[0] = v_new[tb - WINDOW:]

    lane = lax.broadcasted_iota(jnp.int32, (WINDOW, LANES), 1)
    lo_half = lane < HALF
    key_i = lax.broadcasted_iota(jnp.int32, (2 * WINDOW, GQA_GROUP * WINDOW), 0)
    qry_i = lax.broadcasted_iota(jnp.int32, (2 * WINDOW, GQA_GROUP * WINDOW), 1) % WINDOW
    rel = qry_i + WINDOW - key_i
    band = (rel >= 0) & (rel < WINDOW)

    k_prev = [kk_scr[g] for g in range(N_KV_HEADS)]
    v_prev = [vv_scr[g] for g in range(N_KV_HEADS)]
    att_blocks = []
    for i in range(tb // WINDOW):
        r0 = i * WINDOW
        kc, vc = k_rot[r0:r0 + WINDOW], v_new[r0:r0 + WINDOW]
        k_cur = [_dup_half(kc, g, lo_half).astype(BF16) for g in range(N_KV_HEADS)]
        v_cur = [_dup_half(vc, g, lo_half).astype(BF16) for g in range(N_KV_HEADS)]
        valid = band if i > 0 else band & ((key_i >= WINDOW) | (n > 0))
        att_cols = []
        for g in range(N_KV_HEADS):
            keys = jnp.concatenate([k_prev[g], k_cur[g]], axis=0)
            vals = jnp.concatenate([v_prev[g], v_cur[g]], axis=0)
            heads = range(g * GQA_GROUP, (g + 1) * GQA_GROUP)
            qg = jnp.concatenate(
                [jnp.where(lo_half if h % 2 == 0 else ~lo_half, q_cols[h // 2][r0:r0 + WINDOW], 0.0)
                 for h in heads], axis=0).astype(BF16)
            s = _dot_nt(keys, qg)
            s = jnp.where(valid, s, NEG_INF)
            sink = jnp.concatenate(
                [jnp.full((1, WINDOW), sink_ref[h], F32) for h in heads], axis=1)
            m = jnp.maximum(jnp.max(s, axis=0, keepdims=True), sink)
            p = jnp.exp(s - m)
            denom = jnp.sum(p, axis=0, keepdims=True) + jnp.exp(sink - m)
            p = (p * (1.0 / denom)).astype(BF16)
            o = _dot_tn(p, vals)
            for jj in range(GQA_GROUP // 2):
                o_even = o[(2 * jj) * WINDOW:(2 * jj + 1) * WINDOW]
                o_odd = o[(2 * jj + 1) * WINDOW:(2 * jj + 2) * WINDOW]
                att_cols.append(jnp.where(lo_half, o_even, o_odd))
        att_blocks.append(jnp.concatenate(att_cols, axis=1))
        k_prev, v_prev = k_cur, v_cur
    att = jnp.concatenate(att_blocks, axis=0)
    for g in range(N_KV_HEADS):
        kk_scr[g] = k_prev[g]
        vv_scr[g] = v_prev[g]

    lb = _lower_bound(lb_ref[...], layer)
    f = lb + (1.0 - lb) * jax.nn.sigmoid(proj[:, OFF_HF:OFF_HF + D_HGRN])
    logf = jnp.log(f)
    hk = 1.0 - f
    hq = proj[:, OFF_HQ:OFF_HQ + D_HGRN] * (HGRN_D ** -0.5)
    hv_bf = proj[:, OFF_HI:OFF_HI + D_HGRN].astype(BF16)

    tr = lax.broadcasted_iota(jnp.int32, (tb, tb), 0)
    tc = lax.broadcasted_iota(jnp.int32, (tb, tb), 1)
    tri = ((tr // CHUNK == tc // CHUNK) & (tc <= tr)).astype(BF16)
    g_cum = sum(_dot(tri, part) for part in _split3(logf))
    g_scr[...] = g_cum

    def bcast_row(r, rows):
        return jnp.broadcast_to(g_scr[r:r + 1, :], (rows, D_HGRN))

    zeros_sub = jnp.zeros((SUB, D_HGRN), F32)
    g_ref_q = jnp.concatenate(
        [zeros_sub if sb % N_SUB == 0 else bcast_row(sb * SUB - 1, SUB) for sb in range(tb // SUB)], axis=0)
    q_loc = (hq * jnp.exp(g_cum - g_ref_q)).astype(BF16)
    g_end = jnp.concatenate(
        [bcast_row(c * CHUNK + CHUNK - 1, CHUNK) for c in range(tb // CHUNK)], axis=0)
    row_in_chunk = lax.broadcasted_iota(jnp.int32, (tb, D_HGRN), 0) % CHUNK
    k_sub = []
    for i in range(N_SUB):
        g_ref_i = jnp.concatenate(
            [jnp.zeros((CHUNK, D_HGRN), F32) if i == 0 else bcast_row(c * CHUNK + i * SUB - 1, CHUNK)
             for c in range(tb // CHUNK)], axis=0)
        expo = jnp.where(row_in_chunk < (i + 1) * SUB, g_ref_i - g_cum, NEG_INF)
        k_sub.append((hk * jnp.exp(expo)).astype(BF16))

    n_ch = tb // CHUNK
    gc = [g_scr[c * CHUNK + CHUNK - 1:c * CHUNK + CHUNK, :] for c in range(n_ch)]

    def span(lo, hi):
        if hi <= lo:
            return jnp.ones((CHUNK, D_HGRN), F32)
        return jnp.broadcast_to(jnp.exp(sum(gc[lo:hi])), (CHUNK, D_HGRN))

    zeros_chunk = jnp.zeros((CHUNK, D_HGRN), F32)
    k_end32 = hk * jnp.exp(g_end - g_cum)
    q_glob32 = hq * jnp.exp(g_cum)
    k_cross = [(k_end32 * jnp.concatenate(
        [span(cp + 1, c) if cp < c else zeros_chunk for cp in range(n_ch)], axis=0)).astype(BF16)
        for c in range(1, n_ch)]
    q_tile = (q_glob32 * jnp.concatenate([span(0, c) for c in range(n_ch)], axis=0)).astype(BF16)
    k_tile_end = (k_end32 * jnp.concatenate([span(cp + 1, n_ch) for cp in range(n_ch)], axis=0)).astype(BF16)
    decay_tile = jnp.exp(sum(gc))
    q_glob = q_glob32.astype(BF16)

    row_t = lax.broadcasted_iota(jnp.int32, (tb, HGRN_D), 0)
    at = lax.broadcasted_iota(jnp.int32, (tb, tb), 0)
    as_ = lax.broadcasted_iota(jnp.int32, (tb, tb), 1)
    same_chunk_causal = (at // CHUNK == as_ // CHUNK) & (as_ <= at)
    earlier_chunk = as_ // CHUNK < at // CHUNK
    for h in range(N_HGRN_HEADS):
        ls = slice(h * HGRN_D, (h + 1) * HGRN_D)
        q_h, qg_h = q_loc[:, ls], q_glob[:, ls]
        zero_h = jnp.zeros_like(q_h)
        q_stack = jnp.concatenate(
            [jnp.where((row_t % CHUNK) // SUB == i, q_h, zero_h) for i in range(N_SUB)], axis=1)
        k_stack = jnp.concatenate([k_sub[i][:, ls] for i in range(N_SUB)], axis=1)
        qc_stack = jnp.concatenate(
            [jnp.where(row_t // CHUNK == c, qg_h, zero_h) for c in range(1, n_ch)], axis=1)
        kc_stack = jnp.concatenate([k_cross[c - 1][:, ls] for c in range(1, n_ch)], axis=1)
        a = jnp.where(same_chunk_causal, _dot_nt(q_stack, k_stack),
                      jnp.where(earlier_chunk, _dot_nt(qc_stack, kc_stack), 0.0)).astype(BF16)
        v_h = hv_bf[:, ls]
        state = st_scr[h]
        o = _dot(a, v_h) + _dot_nt(q_tile[:, ls], state.astype(BF16))
        st_scr[h] = state * decay_tile[:, ls] + _dot_tn(v_h, k_tile_end[:, ls])
        o = o * lax.rsqrt(jnp.mean(o * o, axis=-1, keepdims=True) + EPS)
        ho_scr[:, ls] = o

    @pl.when(n == last)
    def _():
        for h in range(N_HGRN_HEADS):
            s_ref[0, h] = st_scr[h].T

    hn = ho_scr[...] * gout_ref[...] * _silu(proj[:, OFF_HG:OFF_HG + D_HGRN])
    cat = jnp.concatenate([att, hn], axis=1).astype(BF16)
    y = _dot(cat, wout_ref[...])
    y_ref[...] = x + _rms(y, gpost_ref[...])


def _prompt_mix(x, sinks, g_pre, w_in, lb_raw, g_out, w_out, g_post, batch, seq, layer):
    tb = MIX_TILE
    nt = seq // tb
    cos, s_hi, s_lo = _rope_tables(jnp.arange(seq, dtype=F32))
    tok = lambda b, n: (b * nt + n, 0)
    tab = lambda b, n: (n, 0)
    per_b3 = lambda b, n: (b, 0, 0)
    lb_rows = lb_raw.shape[0]
    return pl.pallas_call(
        functools.partial(_prompt_mix_kernel, layer=layer),
        grid=(batch, nt),
        in_specs=[
            pl.BlockSpec(memory_space=pltpu.SMEM),
            pl.BlockSpec((tb, D_MODEL), tok),
            _const_spec((1, D_MODEL)),
            _const_spec((D_MODEL, IN_COLS)),
            pl.BlockSpec((tb, LANES), tab),
            pl.BlockSpec((tb, LANES), tab),
            pl.BlockSpec((tb, LANES), tab),
            _const_spec((lb_rows, D_HGRN)),
            _const_spec((1, D_HGRN)),
            _const_spec((D_MODEL, D_MODEL)),
            _const_spec((1, D_MODEL)),
        ],
        out_specs=[
            pl.BlockSpec((tb, D_MODEL), tok),
            pl.BlockSpec((1, WINDOW, D_KV), per_b3),
            pl.BlockSpec((1, WINDOW, D_KV), per_b3),
            pl.BlockSpec((1, N_HGRN_HEADS, HGRN_D, HGRN_D), lambda b, n: (b, 0, 0, 0)),
        ],
        out_shape=[
            jax.ShapeDtypeStruct((batch * seq, D_MODEL), F32),
            jax.ShapeDtypeStruct((batch, WINDOW, D_KV), F32),
            jax.ShapeDtypeStruct((batch, WINDOW, D_KV), F32),
            jax.ShapeDtypeStruct((batch, N_HGRN_HEADS, HGRN_D, HGRN_D), F32),
        ],
        scratch_shapes=[
            pltpu.VMEM((N_KV_HEADS, WINDOW, LANES), BF16),
            pltpu.VMEM((N_KV_HEADS, WINDOW, LANES), BF16),
            pltpu.VMEM((N_HGRN_HEADS, HGRN_D, HGRN_D), F32),
            pltpu.VMEM((tb, D_HGRN), F32),
            pltpu.VMEM((tb, D_HGRN), F32),
        ],
        compiler_params=pltpu.CompilerParams(
            dimension_semantics=("arbitrary", "arbitrary"), vmem_limit_bytes=VMEM_LIMIT),
        name="prompt_mix",
    )(sinks, x, g_pre.reshape(1, D_MODEL), w_in, cos, s_hi, s_lo, lb_raw, g_out.reshape(1, D_HGRN),
      w_out, g_post.reshape(1, D_MODEL))


def _sample_mix_kernel(sink_ref, x_ref, gpre_ref, win_ref, cos_ref, shi_ref, slo_ref, lb_ref, gout_ref,
                       wout_ref, gpost_ref, ck_ref, cv_ref, sin_ref,
                       y_ref, wk_ref, wv_ref, sout_ref,
                       q_scr, kn_scr, vn_scr, hv_scr, hg_scr, colt_scr, hkt_scr, att_scr, ho_scr,
                       *, layer, nb):
    step = pl.program_id(0)
    last = pl.num_programs(0) - 1

    @pl.when(step == 0)
    def _():
        u = _rms(x_ref[...], gpre_ref[...]).astype(BF16)
        proj = _dot(u, win_ref[...])
        cos, s_hi, s_lo = cos_ref[...], shi_ref[...], slo_ref[...]
        scale = HEAD_DIM ** -0.5
        for j in range(D_ATTN // LANES):
            q_scr[:, LANES * j:LANES * (j + 1)] = _rope(
                proj[:, OFF_Q + LANES * j:OFF_Q + LANES * (j + 1)], cos, s_hi, s_lo) * scale
        kn_scr[...] = _rope(proj[:, OFF_K:OFF_K + D_KV], cos, s_hi, s_lo)
        vn_scr[...] = proj[:, OFF_V:OFF_V + D_KV]
        hv_scr[...] = proj[:, OFF_HI:OFF_HI + D_HGRN]
        hg_scr[...] = proj[:, OFF_HG:OFF_HG + D_HGRN]
        lb = _lower_bound(lb_ref[...], layer)
        f = lb + (1.0 - lb) * jax.nn.sigmoid(proj[:, OFF_HF:OFF_HF + D_HGRN])
        f_t = f.T
        f_hi = f_t.astype(BF16)
        colt_scr[0:D_HGRN, :] = f_hi
        colt_scr[D_HGRN:2 * D_HGRN, :] = (f_t - f_hi.astype(F32)).astype(BF16)
        colt_scr[2 * D_HGRN:3 * D_HGRN, :] = (proj[:, OFF_HQ:OFF_HQ + D_HGRN] * (HGRN_D ** -0.5)).T.astype(BF16)
        hkt_scr[...] = (1.0 - f_t).astype(BF16)

    lane8 = lax.broadcasted_iota(jnp.int32, (N_Q_HEADS, LANES), 1)
    row8 = lax.broadcasted_iota(jnp.int32, (N_Q_HEADS, LANES), 0)
    keep8 = (lane8 >= HALF) == (row8 >= GQA_GROUP)
    key_row = lax.broadcasted_iota(jnp.int32, (WINDOW, LANES), 0)
    key_col = lax.broadcasted_iota(jnp.int32, (N_Q_HEADS, WINDOW), 1)
    brow = lax.broadcasted_iota(jnp.int32, (nb, HGRN_D), 0)
    head_row = lax.broadcasted_iota(jnp.int32, (N_HGRN_HEADS, HGRN_D), 0)
    sink = sink_ref[...]

    for bi in range(SAMPLE_TILE):
        b = step * SAMPLE_TILE + bi
        q_b = jnp.broadcast_to(q_scr[pl.ds(b, 1), :], (N_Q_HEADS, D_ATTN))
        qm = jnp.zeros((N_Q_HEADS, LANES), F32)
        for h in range(N_Q_HEADS):
            c = q_b[:, LANES * (h // 2):LANES * (h // 2 + 1)]
            if h % 2 != h // GQA_GROUP:
                c = pltpu.roll(c, HALF, 1)
            qm = jnp.where(row8 == h, c, qm)
        qm = jnp.where(keep8, qm, 0.0)
        k_new = kn_scr[pl.ds(b, 1), :]
        v_new = vn_scr[pl.ds(b, 1), :]
        k_old, v_old = ck_ref[bi], cv_ref[bi]
        s = _dot_nt(qm.astype(BF16), k_old.astype(BF16))
        s = jnp.where(key_col >= 1, s, NEG_INF)
        s_new = jnp.sum(qm * k_new, axis=-1, keepdims=True)
        m = jnp.maximum(jnp.maximum(jnp.max(s, axis=-1, keepdims=True), s_new), sink)
        p = jnp.exp(s - m)
        p_new = jnp.exp(s_new - m)
        denom = jnp.sum(p, axis=-1, keepdims=True) + p_new + jnp.exp(sink - m)
        o = (_dot(p.astype(BF16), v_old.astype(BF16)) + p_new * v_new) / denom
        att_scr[pl.ds(pl.multiple_of(b * N_Q_HEADS, N_Q_HEADS), N_Q_HEADS), :] = o
        wk_ref[bi] = jnp.where(key_row == WINDOW - 1, k_new, pltpu.roll(k_old, WINDOW - 1, 0))
        wv_ref[bi] = jnp.where(key_row == WINDOW - 1, v_new, pltpu.roll(v_old, WINDOW - 1, 0))

        sel = brow == b
        onehot = sel.astype(BF16)
        cols = _dot(colt_scr[...], onehot)
        f_col = cols[0:D_HGRN] + cols[D_HGRN:2 * D_HGRN]
        q_col = cols[2 * D_HGRN:3 * D_HGRN]
        hv_b = hv_scr[pl.ds(b, 1), :]
        ho_b = jnp.zeros((N_HGRN_HEADS, HGRN_D), F32)
        for h in range(N_HGRN_HEADS):
            ks = slice(h * HGRN_D, (h + 1) * HGRN_D)
            v_sel = jnp.where(sel, hv_b[:, ks], 0.0).astype(BF16)
            outer = _dot(hkt_scr[ks, :], v_sel)
            s_new_h = f_col[ks] * sin_ref[bi, h] + outer
            sout_ref[bi, h] = s_new_h
            o_h = jnp.sum(q_col[ks] * s_new_h, axis=0, keepdims=True)
            ho_b = jnp.where(head_row == h, o_h, ho_b)
        ho_scr[pl.ds(pl.multiple_of(b * N_HGRN_HEADS, N_HGRN_HEADS), N_HGRN_HEADS), :] = ho_b

    @pl.when(step == last)
    def _():
        y = jnp.zeros((nb, D_MODEL), F32)
        for h in range(N_HGRN_HEADS):
            ks = slice(h * HGRN_D, (h + 1) * HGRN_D)
            ho_h = ho_scr[pl.ds(h, nb, stride=N_HGRN_HEADS), :]
            hn_h = _rms(ho_h, gout_ref[:, ks]) * _silu(hg_scr[:, ks])
            y = y + _dot(hn_h.astype(BF16), wout_ref[D_ATTN + h * HGRN_D:D_ATTN + (h + 1) * HGRN_D, :])
        for h in range(N_Q_HEADS):
            g = h // GQA_GROUP
            a_h = att_scr[pl.ds(h, nb, stride=N_Q_HEADS), :][:, g * HALF:(g + 1) * HALF]
            y = y + _dot(a_h.astype(BF16), wout_ref[h * HEAD_DIM:(h + 1) * HEAD_DIM, :])
        y_ref[...] = x_ref[...] + _rms(y, gpost_ref[...])


def _sample_mix(x, sinks, g_pre, w_in, lb_raw, g_out, w_out, g_post, cache_k, cache_v, state, pos, layer):
    nb = x.shape[0]
    cos, s_hi, s_lo = _rope_tables(pos)
    lb_rows = lb_raw.shape[0]
    blk3 = pl.BlockSpec((SAMPLE_TILE, WINDOW, D_KV), lambda i: (i, 0, 0))
    blk4 = pl.BlockSpec((SAMPLE_TILE, N_HGRN_HEADS, HGRN_D, HGRN_D), lambda i: (i, 0, 0, 0))
    return pl.pallas_call(
        functools.partial(_sample_mix_kernel, layer=layer, nb=nb),
        grid=(nb // SAMPLE_TILE,),
        in_specs=[
            _const_spec((N_Q_HEADS, 1)),
            _const_spec((nb, D_MODEL)),
            _const_spec((1, D_MODEL)),
            _const_spec((D_MODEL, IN_COLS)),
            _const_spec((1, LANES)),
            _const_spec((1, LANES)),
            _const_spec((1, LANES)),
            _const_spec((lb_rows, D_HGRN)),
            _const_spec((1, D_HGRN)),
            _const_spec((D_MODEL, D_MODEL)),
            _const_spec((1, D_MODEL)),
            blk3, blk3, blk4,
        ],
        out_specs=[pl.BlockSpec((nb, D_MODEL), lambda i: (0, 0)), blk3, blk3, blk4],
        out_shape=[
            jax.ShapeDtypeStruct((nb, D_MODEL), F32),
            jax.ShapeDtypeStruct(cache_k.shape, F32),
            jax.ShapeDtypeStruct(cache_v.shape, F32),
            jax.ShapeDtypeStruct(state.shape, F32),
        ],
        scratch_shapes=[
            pltpu.VMEM((nb, D_ATTN), F32),
            pltpu.VMEM((nb, D_KV), F32),
            pltpu.VMEM((nb, D_KV), F32),
            pltpu.VMEM((nb, D_HGRN), F32),
            pltpu.VMEM((nb, D_HGRN), F32),
            pltpu.VMEM((3 * D_HGRN, nb), BF16),
            pltpu.VMEM((D_HGRN, nb), BF16),
            pltpu.VMEM((nb * N_Q_HEADS, LANES), F32),
            pltpu.VMEM((nb * N_HGRN_HEADS, HGRN_D), F32),
        ],
        compiler_params=pltpu.CompilerParams(
            dimension_semantics=("arbitrary",), vmem_limit_bytes=VMEM_LIMIT),
        name="sample_mix",
    )(sinks.reshape(N_Q_HEADS, 1), x, g_pre.reshape(1, D_MODEL), w_in, cos, s_hi, s_lo, lb_raw,
      g_out.reshape(1, D_HGRN), w_out, g_post.reshape(1, D_MODEL), cache_k, cache_v, state)


def kernel(x_prompt, x_sample, cache_win_k, cache_win_v, state_hgrn, ffn1_pre_g, ffn1_post_g, ffn1_w_gu,
           ffn1_w_down, mix_pre_g, mix_post_g, w_in, attn_sinks, hgrn_lb, hgrn_out_g, w_out, ffn2_pre_g,
           ffn2_post_g, ffn2_w_gu, ffn2_w_down):
    batch, seq, _ = x_prompt.shape
    nb, t_s, _ = x_sample.shape
    depth = w_in.shape[0]
    assert t_s == 1 and seq % MIX_TILE == 0 and (batch * seq) % FFN_TILE == 0 and nb % SAMPLE_TILE == 0
    assert cache_win_k.shape[2:] == (WINDOW, N_KV_HEADS, HEAD_DIM)

    xp = x_prompt.reshape(batch * seq, D_MODEL)
    xs = x_sample.reshape(nb, D_MODEL)
    pos_s = PAST_LEN + jnp.arange(t_s, dtype=F32)
    outs = [[] for _ in range(6)]
    for l in range(depth):
        w_gu1, w_d1 = ffn1_w_gu[l].astype(BF16), ffn1_w_down[l].astype(BF16)
        w_gu2, w_d2 = ffn2_w_gu[l].astype(BF16), ffn2_w_down[l].astype(BF16)
        w_in_l, w_out_l = w_in[l].astype(BF16), w_out[l].astype(BF16)
        g_out = jnp.tile(hgrn_out_g[l], N_HGRN_HEADS)

        xp = _ffn(xp, ffn1_pre_g[l], ffn1_post_g[l], w_gu1, w_d1, FFN_TILE)
        xs = _ffn(xs, ffn1_pre_g[l], ffn1_post_g[l], w_gu1, w_d1, nb)

        xp, wk_p, wv_p, s_p = _prompt_mix(xp, attn_sinks[l], mix_pre_g[l], w_in_l, hgrn_lb, g_out, w_out_l,
                                          mix_post_g[l], batch, seq, l)
        xs, wk_s, wv_s, s_s = _sample_mix(
            xs, attn_sinks[l], mix_pre_g[l], w_in_l, hgrn_lb, g_out, w_out_l, mix_post_g[l],
            cache_win_k[l].reshape(nb, WINDOW, D_KV), cache_win_v[l].reshape(nb, WINDOW, D_KV),
            state_hgrn[l], pos_s, l)

        xp = _ffn(xp, ffn2_pre_g[l], ffn2_post_g[l], w_gu2, w_d2, FFN_TILE)
        xs = _ffn(xs, ffn2_pre_g[l], ffn2_post_g[l], w_gu2, w_d2, nb)

        kv_shape = (WINDOW, N_KV_HEADS, HEAD_DIM)
        for lst, val in zip(outs, (wk_p.reshape(batch, *kv_shape), wv_p.reshape(batch, *kv_shape), s_p,
                                   wk_s.reshape(nb, *kv_shape), wv_s.reshape(nb, *kv_shape), s_s)):
            lst.append(val)

    return (xp.reshape(batch, seq, D_MODEL), xs.reshape(nb, t_s, D_MODEL)) + tuple(jnp.stack(o) for o in outs)
```

```python
import functools

import jax
import jax.numpy as jnp
import numpy as np
from jax import lax
from jax.experimental import pallas as pl
from jax.experimental.pallas import tpu as pltpu

F32 = jnp.float32
BF16 = jnp.bfloat16

D_MODEL = 1024
D_FF = 2816
HEAD_DIM = 64
N_Q_HEADS = 8
N_KV_HEADS = 2
GQA_GROUP = N_Q_HEADS // N_KV_HEADS
WINDOW = 128
PAST_LEN = 8192
ROT_DIM = HEAD_DIM // 4
ROPE_THETA = 500000.0
N_HGRN_HEADS = 8
HGRN_D = 64
D_ATTN = N_Q_HEADS * HEAD_DIM
D_KV = N_KV_HEADS * HEAD_DIM
D_HGRN = N_HGRN_HEADS * HGRN_D
IN_COLS = D_ATTN + 2 * D_KV + 4 * D_HGRN
OFF_Q, OFF_K, OFF_V = 0, D_ATTN, D_ATTN + D_KV
OFF_HQ = D_ATTN + 2 * D_KV
OFF_HF, OFF_HI, OFF_HG = OFF_HQ + D_HGRN, OFF_HQ + 2 * D_HGRN, OFF_HQ + 3 * D_HGRN
EPS = 1e-6
NEG_INF = -1e30
LANES = 128
HALF = LANES // 2

FFN_TILE = 512
MIX_TILE = 256
CHUNK = 64
SUB = 16
N_SUB = CHUNK // SUB
VMEM_LIMIT = 56 * 1024 * 1024


def _rms(x, g):
    return x * lax.rsqrt(jnp.mean(x * x, axis=-1, keepdims=True) + EPS) * g


def _silu(x):
    return x * jax.nn.sigmoid(x)


def _dot(a, b):
    return jnp.dot(a, b, preferred_element_type=F32)


def _dot_nt(a, b):
    return lax.dot_general(a, b, (((1,), (1,)), ((), ())), preferred_element_type=F32)


def _dot_tn(a, b):
    return lax.dot_general(a, b, (((0,), (0,)), ((), ())), preferred_element_type=F32)


def _split3(x):
    hi = x.astype(BF16)
    r = x - hi.astype(F32)
    mid = r.astype(BF16)
    lo = (r - mid.astype(F32)).astype(BF16)
    return hi, mid, lo


def _const_spec(shape):
    nd = len(shape)
    return pl.BlockSpec(shape, lambda *_: (0,) * nd, pipeline_mode=pl.Buffered(1))


def _ffn_kernel(x_ref, gpre_ref, gpost_ref, wgu_ref, wd_ref, o_ref):
    x = x_ref[...]
    h = _rms(x, gpre_ref[...]).astype(BF16)
    gate = _dot(h, wgu_ref[:, :D_FF])
    up = _dot(h, wgu_ref[:, D_FF:])
    act = (_silu(gate) * up).astype(BF16)
    y = _dot(act, wd_ref[...])
    o_ref[...] = x + 0.5 * _rms(y, gpost_ref[...])


def _ffn(x, g_pre, g_post, w_gu, w_down, tile):
    n = x.shape[0]
    return pl.pallas_call(
        _ffn_kernel,
        grid=(n // tile,),
        in_specs=[
            pl.BlockSpec((tile, D_MODEL), lambda i: (i, 0)),
            _const_spec((1, D_MODEL)),
            _const_spec((1, D_MODEL)),
            _const_spec((D_MODEL, 2 * D_FF)),
            _const_spec((D_FF, D_MODEL)),
        ],
        out_specs=pl.BlockSpec((tile, D_MODEL), lambda i: (i, 0)),
        out_shape=jax.ShapeDtypeStruct((n, D_MODEL), F32),
        compiler_params=pltpu.CompilerParams(
            dimension_semantics=("arbitrary",), vmem_limit_bytes=VMEM_LIMIT),
        name="ffn",
    )(x, g_pre.reshape(1, D_MODEL), g_post.reshape(1, D_MODEL), w_gu, w_down)


def _lower_bound(lb_raw, layer):
    m = jnp.max(lb_raw, axis=0, keepdims=True)
    e = jnp.exp(lb_raw - m)
    return jnp.sum(e[: layer + 1], axis=0, keepdims=True) / jnp.sum(e, axis=0, keepdims=True)


def _rope(x, cos, sin_hi, sin_lo):
    return x * cos + pltpu.roll(x, ROT_DIM // 2, 1) * sin_hi + pltpu.roll(x, LANES - ROT_DIM // 2, 1) * sin_lo


def _rope_tables(pos):
    half = ROT_DIM // 2
    inv = ROPE_THETA ** (-jnp.arange(half, dtype=F32) / half)
    ang = pos[:, None] * inv[None, :]
    cos, sin = jnp.cos(ang), jnp.sin(ang)
    t = pos.shape[0]
    one = jnp.ones((t, HEAD_DIM - ROT_DIM), F32)
    zero = jnp.zeros((t, HEAD_DIM - ROT_DIM), F32)
    zh = jnp.zeros((t, half), F32)
    c = jnp.concatenate([cos, cos, one], axis=1)
    s_hi = jnp.concatenate([zh, sin, zero], axis=1)
    s_lo = jnp.concatenate([-sin, zh, zero], axis=1)
    return tuple(jnp.tile(a, (1, LANES // HEAD_DIM)) for a in (c, s_hi, s_lo))


def _dup_half(x, g, lo_half):
    xr = pltpu.roll(x, HALF, 1)
    return jnp.where(lo_half, x, xr) if g == 0 else jnp.where(lo_half, xr, x)


def _prompt_mix_kernel(sink_ref, x_ref, gpre_ref, win_ref, cos_ref, shi_ref, slo_ref, lb_ref, gout_ref,
                       wout_ref, gpost_ref,
                       y_ref, wk_ref, wv_ref, s_ref,
                       kk_scr, vv_scr, st_scr, g_scr, ho_scr, *, layer):
    tb = MIX_TILE
    n = pl.program_id(1)
    last = pl.num_programs(1) - 1

    @pl.when(n == 0)
    def _():
        kk_scr[...] = jnp.zeros_like(kk_scr)
        vv_scr[...] = jnp.zeros_like(vv_scr)
        st_scr[...] = jnp.zeros_like(st_scr)

    x = x_ref[...]
    u = _rms(x, gpre_ref[...]).astype(BF16)
    proj = _dot(u, win_ref[...])

    cos, s_hi, s_lo = cos_ref[...], shi_ref[...], slo_ref[...]
    scale = HEAD_DIM ** -0.5
    q_cols = [_rope(proj[:, OFF_Q + LANES * j: OFF_Q + LANES * (j + 1)], cos, s_hi, s_lo) * scale
              for j in range(D_ATTN // LANES)]
    k_rot = _rope(proj[:, OFF_K:OFF_K + D_KV], cos, s_hi, s_lo)
    v_new = proj[:, OFF_V:OFF_V + D_KV]

    @pl.when(n == last)
    def _():
        wk_ref[0] = k_rot[tb - WINDOW:]
        wv_ref[0] = v_new[tb - WINDOW:]

    lane = lax.broadcasted_iota(jnp.int32, (WINDOW, LANES), 1)
    lo_half = lane < HALF
    key_i = lax.broadcasted_iota(jnp.int32, (2 * WINDOW, GQA_GROUP * WINDOW), 0)
    qry_i = lax.broadcasted_iota(jnp.int32, (2 * WINDOW, GQA_GROUP * WINDOW), 1) % WINDOW
    rel = qry_i + WINDOW - key_i
    band = (rel >= 0) & (rel < WINDOW)

    k_prev = [kk_scr[g] for g in range(N_KV_HEADS)]
    v_prev = [vv_scr[g] for g in range(N_KV_HEADS)]
    att_blocks = []
    for i in range(tb // WINDOW):
        r0 = i * WINDOW
        kc, vc = k_rot[r0:r0 + WINDOW], v_new[r0:r0 + WINDOW]
        k_cur = [_dup_half(kc, g, lo_half).astype(BF16) for g in range(N_KV_HEADS)]
        v_cur = [_dup_half(vc, g, lo_half).astype(BF16) for g in range(N_KV_HEADS)]
        valid = band if i > 0 else band & ((key_i >= WINDOW) | (n > 0))
        att_cols = []
        for g in range(N_KV_HEADS):
            keys = jnp.concatenate([k_prev[g], k_cur[g]], axis=0)
            vals = jnp.concatenate([v_prev[g], v_cur[g]], axis=0)
            heads = range(g * GQA_GROUP, (g + 1) * GQA_GROUP)
            qg = jnp.concatenate(
                [jnp.where(lo_half if h % 2 == 0 else ~lo_half, q_cols[h // 2][r0:r0 + WINDOW], 0.0)
                 for h in heads], axis=0).astype(BF16)
            s = _dot_nt(keys, qg)
            s = jnp.where(valid, s, NEG_INF)
            sink = jnp.concatenate(
                [jnp.full((1, WINDOW), sink_ref[h], F32) for h in heads], axis=1)
            m = jnp.maximum(jnp.max(s, axis=0, keepdims=True), sink)
            p = jnp.exp(s - m)
            denom = jnp.sum(p, axis=0, keepdims=True) + jnp.exp(sink - m)
            p = (p * (1.0 / denom)).astype(BF16)
            o = _dot_tn(p, vals)
            for jj in range(GQA_GROUP // 2):
                o_even = o[(2 * jj) * WINDOW:(2 * jj + 1) * WINDOW]
                o_odd = o[(2 * jj + 1) * WINDOW:(2 * jj + 2) * WINDOW]
                att_cols.append(jnp.where(lo_half, o_even, o_odd))
        att_blocks.append(jnp.concatenate(att_cols, axis=1))
        k_prev, v_prev = k_cur, v_cur
    att = jnp.concatenate(att_blocks, axis=0)
    for g in range(N_KV_HEADS):
        kk_scr[g] = k_prev[g]
        vv_scr[g] = v_prev[g]

    lb = _lower_bound(lb_ref[...], layer)
    f = lb + (1.0 - lb) * jax.nn.sigmoid(proj[:, OFF_HF:OFF_HF + D_HGRN])
    logf = jnp.log(f)
    hk = 1.0 - f
    hq = proj[:, OFF_HQ:OFF_HQ + D_HGRN] * (HGRN_D ** -0.5)
    hv_bf = proj[:, OFF_HI:OFF_HI + D_HGRN].astype(BF16)

    tr = lax.broadcasted_iota(jnp.int32, (tb, tb), 0)
    tc = lax.broadcasted_iota(jnp.int32, (tb, tb), 1)
    tri = ((tr // CHUNK == tc // CHUNK) & (tc <= tr)).astype(BF16)
    g_cum = sum(_dot(tri, part) for part in _split3(logf))
    g_scr[...] = g_cum

    def bcast_row(r, rows):
        return jnp.broadcast_to(g_scr[r:r + 1, :], (rows, D_HGRN))

    zeros_sub = jnp.zeros((SUB, D_HGRN), F32)
    g_ref_q = jnp.concatenate(
        [zeros_sub if sb % N_SUB == 0 else bcast_row(sb * SUB - 1, SUB) for sb in range(tb // SUB)], axis=0)
    q_loc = (hq * jnp.exp(g_cum - g_ref_q)).astype(BF16)
    g_end = jnp.concatenate(
        [bcast_row(c * CHUNK + CHUNK - 1, CHUNK) for c in range(tb // CHUNK)], axis=0)
    row_in_chunk = lax.broadcasted_iota(jnp.int32, (tb, D_HGRN), 0) % CHUNK
    k_sub = []
    for i in range(N_SUB):
        g_ref_i = jnp.concatenate(
            [jnp.zeros((CHUNK, D_HGRN), F32) if i == 0 else bcast_row(c * CHUNK + i * SUB - 1, CHUNK)
             for c in range(tb // CHUNK)], axis=0)
        expo = jnp.where(row_in_chunk < (i + 1) * SUB, g_ref_i - g_cum, NEG_INF)
        k_sub.append((hk * jnp.exp(expo)).astype(BF16))

    n_ch = tb // CHUNK
    gc = [g_scr[c * CHUNK + CHUNK - 1:c * CHUNK + CHUNK, :] for c in range(n_ch)]

    def span(lo, hi):
        if hi <= lo:
            return jnp.ones((CHUNK, D_HGRN), F32)
        return jnp.broadcast_to(jnp.exp(sum(gc[lo:hi])), (CHUNK, D_HGRN))

    zeros_chunk = jnp.zeros((CHUNK, D_HGRN), F32)
    k_end32 = hk * jnp.exp(g_end - g_cum)
    q_glob32 = hq * jnp.exp(g_cum)
    k_cross = [(k_end32 * jnp.concatenate(
        [span(cp + 1, c) if cp < c else zeros_chunk for cp in range(n_ch)], axis=0)).astype(BF16)
        for c in range(1, n_ch)]
    q_tile = (q_glob32 * jnp.concatenate([span(0, c) for c in range(n_ch)], axis=0)).astype(BF16)
    k_tile_end = (k_end32 * jnp.concatenate([span(cp + 1, n_ch) for cp in range(n_ch)], axis=0)).astype(BF16)
    decay_tile = jnp.exp(sum(gc))
    q_glob = q_glob32.astype(BF16)

    row_t = lax.broadcasted_iota(jnp.int32, (tb, HGRN_D), 0)
    at = lax.broadcasted_iota(jnp.int32, (tb, tb), 0)
    as_ = lax.broadcasted_iota(jnp.int32, (tb, tb), 1)
    same_chunk_causal = (at // CHUNK == as_ // CHUNK) & (as_ <= at)
    earlier_chunk = as_ // CHUNK < at // CHUNK
    for h in range(N_HGRN_HEADS):
        ls = slice(h * HGRN_D, (h + 1) * HGRN_D)
        q_h, qg_h = q_loc[:, ls], q_glob[:, ls]
        zero_h = jnp.zeros_like(q_h)
        q_stack = jnp.concatenate(
            [jnp.where((row_t % CHUNK) // SUB == i, q_h, zero_h) for i in range(N_SUB)], axis=1)
        k_stack = jnp.concatenate([k_sub[i][:, ls] for i in range(N_SUB)], axis=1)
        qc_stack = jnp.concatenate(
            [jnp.where(row_t // CHUNK == c, qg_h, zero_h) for c in range(1, n_ch)], axis=1)
        kc_stack = jnp.concatenate([k_cross[c - 1][:, ls] for c in range(1, n_ch)], axis=1)
        a = jnp.where(same_chunk_causal, _dot_nt(q_stack, k_stack),
                      jnp.where(earlier_chunk, _dot_nt(qc_stack, kc_stack), 0.0)).astype(BF16)
        v_h = hv_bf[:, ls]
        state = st_scr[h]
        o = _dot(a, v_h) + _dot_nt(q_tile[:, ls], state.astype(BF16))
        st_scr[h] = state * decay_tile[:, ls] + _dot_tn(v_h, k_tile_end[:, ls])
        o = o * lax.rsqrt(jnp.mean(o * o, axis=-1, keepdims=True) + EPS)
        ho_scr[:, ls] = o

    @pl.when(n == last)
    def _():
        for h in range(N_HGRN_HEADS):
            s_ref[0, h] = st_scr[h].T

    hn = ho_scr[...] * gout_ref[...] * _silu(proj[:, OFF_HG:OFF_HG + D_HGRN])
    cat = jnp.concatenate([att, hn], axis=1).astype(BF16)
    y = _dot(cat, wout_ref[...])
    y_ref[...] = x + _rms(y, gpost_ref[...])


def _prompt_mix(x, sinks, g_pre, w_in, lb_raw, g_out, w_out, g_post, batch, seq, layer):
    tb = MIX_TILE
    nt = seq // tb
    cos, s_hi, s_lo = _rope_tables(jnp.arange(seq, dtype=F32))
    tok = lambda b, n: (b * nt + n, 0)
    tab = lambda b, n: (n, 0)
    per_b3 = lambda b, n: (b, 0, 0)
    lb_rows = lb_raw.shape[0]
    return pl.pallas_call(
        functools.partial(_prompt_mix_kernel, layer=layer),
        grid=(batch, nt),
        in_specs=[
            pl.BlockSpec(memory_space=pltpu.SMEM),
            pl.BlockSpec((tb, D_MODEL), tok),
            _const_spec((1, D_MODEL)),
            _const_spec((D_MODEL, IN_COLS)),
            pl.BlockSpec((tb, LANES), tab),
            pl.BlockSpec((tb, LANES), tab),
            pl.BlockSpec((tb, LANES), tab),
            _const_spec((lb_rows, D_HGRN)),
            _const_spec((1, D_HGRN)),
            _const_spec((D_MODEL, D_MODEL)),
            _const_spec((1, D_MODEL)),
        ],
        out_specs=[
            pl.BlockSpec((tb, D_MODEL), tok),
            pl.BlockSpec((1, WINDOW, D_KV), per_b3),
            pl.BlockSpec((1, WINDOW, D_KV), per_b3),
            pl.BlockSpec((1, N_HGRN_HEADS, HGRN_D, HGRN_D), lambda b, n: (b, 0, 0, 0)),
        ],
        out_shape=[
            jax.ShapeDtypeStruct((batch * seq, D_MODEL), F32),
            jax.ShapeDtypeStruct((batch, WINDOW, D_KV), F32),
            jax.ShapeDtypeStruct((batch, WINDOW, D_KV), F32),
            jax.ShapeDtypeStruct((batch, N_HGRN_HEADS, HGRN_D, HGRN_D), F32),
        ],
        scratch_shapes=[
            pltpu.VMEM((N_KV_HEADS, WINDOW, LANES), BF16),
            pltpu.VMEM((N_KV_HEADS, WINDOW, LANES), BF16),
            pltpu.VMEM((N_HGRN_HEADS, HGRN_D, HGRN_D), F32),
            pltpu.VMEM((tb, D_HGRN), F32),
            pltpu.VMEM((tb, D_HGRN), F32),
        ],
        compiler_params=pltpu.CompilerParams(
            dimension_semantics=("arbitrary", "arbitrary"), vmem_limit_bytes=VMEM_LIMIT),
        name="prompt_mix",
    )(sinks, x, g_pre.reshape(1, D_MODEL), w_in, cos, s_hi, s_lo, lb_raw, g_out.reshape(1, D_HGRN),
      w_out, g_post.reshape(1, D_MODEL))


def _sample_mix_kernel(sink_ref, x_ref, gpre_ref, win_ref, cos_ref, shi_ref, slo_ref, lb_ref, goutc_ref,
                       wout_ref, gpost_ref, ckt_ref, cvt_ref, sin_ref,
                       y_ref, wkt_ref, wvt_ref, sout_ref,
                       q_scr, kn_scr, vn_scr, knt_scr, vnt_scr, ft_scr, hkt_scr, hqt_scr, hvt_scr, hgt_scr,
                       ot_scr, att_scr, *, layer, nb):
    step = pl.program_id(0)
    last = pl.num_programs(0) - 1
    bt = nb // N_HGRN_HEADS

    @pl.when(step == 0)
    def _():
        u = _rms(x_ref[...], gpre_ref[...]).astype(BF16)
        proj = _dot(u, win_ref[...])
        cos, s_hi, s_lo = cos_ref[...], shi_ref[...], slo_ref[...]
        scale = HEAD_DIM ** -0.5
        for j in range(D_ATTN // LANES):
            q_scr[:, LANES * j:LANES * (j + 1)] = _rope(
                proj[:, OFF_Q + LANES * j:OFF_Q + LANES * (j + 1)], cos, s_hi, s_lo) * scale
        k_new = _rope(proj[:, OFF_K:OFF_K + D_KV], cos, s_hi, s_lo)
        v_new = proj[:, OFF_V:OFF_V + D_KV]
        kn_scr[...] = k_new
        vn_scr[...] = v_new
        for scr, val in ((knt_scr, k_new), (vnt_scr, v_new)):
            for i, part in enumerate(_split3(val.T)):
                scr[i] = part
        lb = _lower_bound(lb_ref[...], layer)
        f_t = (lb + (1.0 - lb) * jax.nn.sigmoid(proj[:, OFF_HF:OFF_HF + D_HGRN])).T
        ft_scr[...] = f_t
        hkt_scr[...] = 1.0 - f_t
        hqt_scr[...] = (proj[:, OFF_HQ:OFF_HQ + D_HGRN] * (HGRN_D ** -0.5)).T
        hvt_scr[...] = proj[:, OFF_HI:OFF_HI + D_HGRN].T
        hgt_scr[...] = proj[:, OFF_HG:OFF_HG + D_HGRN].T

    base = pl.multiple_of(step * HGRN_D, HGRN_D)
    hv_t = hvt_scr[pl.ds(base, HGRN_D), :]

    def hgrn_row(k, o_acc):
        f_row = ft_scr[pl.ds(base + k, 1), :]
        s_new = f_row * sin_ref[k] + hkt_scr[pl.ds(base + k, 1), :] * hv_t
        sout_ref[k] = s_new
        return o_acc + hqt_scr[pl.ds(base + k, 1), :] * s_new

    ot_scr[pl.ds(base, HGRN_D), :] = lax.fori_loop(
        0, HGRN_D, hgrn_row, jnp.zeros((HGRN_D, nb), F32), unroll=8)

    lane8 = lax.broadcasted_iota(jnp.int32, (N_Q_HEADS, LANES), 1)
    row8 = lax.broadcasted_iota(jnp.int32, (N_Q_HEADS, LANES), 0)
    keep8 = (lane8 >= HALF) == (row8 >= GQA_GROUP)
    win_lane = lax.broadcasted_iota(jnp.int32, (D_KV, WINDOW), 1)
    sink = sink_ref[...]
    b0 = step * bt
    sel = (lax.broadcasted_iota(jnp.int32, (nb, bt), 0)
           == b0 + lax.broadcasted_iota(jnp.int32, (nb, bt), 1)).astype(BF16)
    k_cols = sum(_dot(knt_scr[i], sel) for i in range(3))
    v_cols = sum(_dot(vnt_scr[i], sel) for i in range(3))
    for bi in range(bt):
        b = b0 + bi
        q_b = jnp.broadcast_to(q_scr[pl.ds(b, 1), :], (N_Q_HEADS, D_ATTN))
        qm = jnp.zeros((N_Q_HEADS, LANES), F32)
        for h in range(N_Q_HEADS):
            c = q_b[:, LANES * (h // 2):LANES * (h // 2 + 1)]
            if h % 2 != h // GQA_GROUP:
                c = pltpu.roll(c, HALF, 1)
            qm = jnp.where(row8 == h, c, qm)
        qm = jnp.where(keep8, qm, 0.0)
        k_new = kn_scr[pl.ds(b, 1), :]
        v_new = vn_scr[pl.ds(b, 1), :]
        k_old, v_old = ckt_ref[bi], cvt_ref[bi]
        s = _dot(qm.astype(BF16), k_old.astype(BF16))
        s = jnp.where(lane8 >= 1, s, NEG_INF)
        s_new = jnp.sum(qm * k_new, axis=-1, keepdims=True)
        m = jnp.maximum(jnp.maximum(jnp.max(s, axis=-1, keepdims=True), s_new), sink)
        p = jnp.exp(s - m)
        p_new = jnp.exp(s_new - m)
        denom = jnp.sum(p, axis=-1, keepdims=True) + p_new + jnp.exp(sink - m)
        o = (_dot_nt(p.astype(BF16), v_old.astype(BF16)) + p_new * v_new) / denom
        att_scr[pl.ds(pl.multiple_of(b * N_Q_HEADS, N_Q_HEADS), N_Q_HEADS), :] = o
        wkt_ref[bi] = jnp.where(win_lane == WINDOW - 1, k_cols[:, bi:bi + 1], pltpu.roll(k_old, WINDOW - 1, 1))
        wvt_ref[bi] = jnp.where(win_lane == WINDOW - 1, v_cols[:, bi:bi + 1], pltpu.roll(v_old, WINDOW - 1, 1))

    @pl.when(step == last)
    def _():
        y = jnp.zeros((nb, D_MODEL), F32)
        for h in range(N_HGRN_HEADS):
            ks = slice(h * HGRN_D, (h + 1) * HGRN_D)
            o_t = ot_scr[ks, :]
            hn_t = (o_t * lax.rsqrt(jnp.mean(o_t * o_t, axis=0, keepdims=True) + EPS)
                    * goutc_ref[...] * _silu(hgt_scr[ks, :]))
            y = y + _dot_tn(hn_t.astype(BF16), wout_ref[D_ATTN + h * HGRN_D:D_ATTN + (h + 1) * HGRN_D, :])
        for h in range(N_Q_HEADS):
            g = h // GQA_GROUP
            a_h = att_scr[pl.ds(h, nb, stride=N_Q_HEADS), :][:, g * HALF:(g + 1) * HALF]
            y = y + _dot(a_h.astype(BF16), wout_ref[h * HEAD_DIM:(h + 1) * HEAD_DIM, :])
        y_ref[...] = x_ref[...] + _rms(y, gpost_ref[...])


def _sample_mix(x, sinks, g_pre, w_in, lb_raw, g_out_head, w_out, g_post, cache_kt, cache_vt, state_t, pos, layer):
    nb = x.shape[0]
    bt = nb // N_HGRN_HEADS
    cos, s_hi, s_lo = _rope_tables(pos)
    lb_rows = lb_raw.shape[0]
    blk3 = pl.BlockSpec((bt, D_KV, WINDOW), lambda i: (i, 0, 0))
    blk_s = pl.BlockSpec((HGRN_D, HGRN_D, nb), lambda i: (i, 0, 0))
    chan_major = pltpu.VMEM((D_HGRN, nb), F32)
    return pl.pallas_call(
        functools.partial(_sample_mix_kernel, layer=layer, nb=nb),
        grid=(N_HGRN_HEADS,),
        in_specs=[
            _const_spec((N_Q_HEADS, 1)),
            _const_spec((nb, D_MODEL)),
            _const_spec((1, D_MODEL)),
            _const_spec((D_MODEL, IN_COLS)),
            _const_spec((1, LANES)),
            _const_spec((1, LANES)),
            _const_spec((1, LANES)),
            _const_spec((lb_rows, D_HGRN)),
            _const_spec((HGRN_D, 1)),
            _const_spec((D_MODEL, D_MODEL)),
            _const_spec((1, D_MODEL)),
            blk3, blk3, blk_s,
        ],
        out_specs=[pl.BlockSpec((nb, D_MODEL), lambda i: (0, 0)), blk3, blk3, blk_s],
        out_shape=[
            jax.ShapeDtypeStruct((nb, D_MODEL), F32),
            jax.ShapeDtypeStruct(cache_kt.shape, F32),
            jax.ShapeDtypeStruct(cache_vt.shape, F32),
            jax.ShapeDtypeStruct(state_t.shape, F32),
        ],
        scratch_shapes=[
            pltpu.VMEM((nb, D_ATTN), F32),
            pltpu.VMEM((nb, D_KV), F32),
            pltpu.VMEM((nb, D_KV), F32),
            pltpu.VMEM((3, D_KV, nb), BF16),
            pltpu.VMEM((3, D_KV, nb), BF16),
            chan_major, chan_major, chan_major, chan_major, chan_major, chan_major,
            pltpu.VMEM((nb * N_Q_HEADS, LANES), F32),
        ],
        compiler_params=pltpu.CompilerParams(
            dimension_semantics=("arbitrary",), vmem_limit_bytes=VMEM_LIMIT),
        name="sample_mix",
    )(sinks.reshape(N_Q_HEADS, 1), x, g_pre.reshape(1, D_MODEL), w_in, cos, s_hi, s_lo, lb_raw,
      g_out_head.reshape(HGRN_D, 1), w_out, g_post.reshape(1, D_MODEL), cache_kt, cache_vt, state_t)


def kernel(x_prompt, x_sample, cache_win_k, cache_win_v, state_hgrn, ffn1_pre_g, ffn1_post_g, ffn1_w_gu,
           ffn1_w_down, mix_pre_g, mix_post_g, w_in, attn_sinks, hgrn_lb, hgrn_out_g, w_out, ffn2_pre_g,
           ffn2_post_g, ffn2_w_gu, ffn2_w_down):
    batch, seq, _ = x_prompt.shape
    nb, t_s, _ = x_sample.shape
    depth = w_in.shape[0]
    assert t_s == 1 and seq % MIX_TILE == 0 and (batch * seq) % FFN_TILE == 0 and nb == LANES
    assert cache_win_k.shape[2:] == (WINDOW, N_KV_HEADS, HEAD_DIM)

    xp = x_prompt.reshape(batch * seq, D_MODEL)
    xs = x_sample.reshape(nb, D_MODEL)
    pos_s = PAST_LEN + jnp.arange(t_s, dtype=F32)
    outs = [[] for _ in range(6)]
    for l in range(depth):
        w_gu1, w_d1 = ffn1_w_gu[l].astype(BF16), ffn1_w_down[l].astype(BF16)
        w_gu2, w_d2 = ffn2_w_gu[l].astype(BF16), ffn2_w_down[l].astype(BF16)
        w_in_l, w_out_l = w_in[l].astype(BF16), w_out[l].astype(BF16)
        g_out = jnp.tile(hgrn_out_g[l], N_HGRN_HEADS)

        xp = _ffn(xp, ffn1_pre_g[l], ffn1_post_g[l], w_gu1, w_d1, FFN_TILE)
        xs = _ffn(xs, ffn1_pre_g[l], ffn1_post_g[l], w_gu1, w_d1, nb)

        xp, wk_p, wv_p, s_p = _prompt_mix(xp, attn_sinks[l], mix_pre_g[l], w_in_l, hgrn_lb, g_out, w_out_l,
                                          mix_post_g[l], batch, seq, l)
        to_kt = lambda c: jnp.transpose(c, (0, 2, 3, 1)).reshape(nb, D_KV, WINDOW)
        from_kt = lambda c: jnp.transpose(c.reshape(nb, N_KV_HEADS, HEAD_DIM, WINDOW), (0, 3, 1, 2))
        state_t = jnp.transpose(state_hgrn[l], (1, 2, 3, 0)).reshape(D_HGRN, HGRN_D, nb)
        xs, wkt_s, wvt_s, st_s = _sample_mix(
            xs, attn_sinks[l], mix_pre_g[l], w_in_l, hgrn_lb, hgrn_out_g[l], w_out_l, mix_post_g[l],
            to_kt(cache_win_k[l]), to_kt(cache_win_v[l]), state_t, pos_s, l)
        s_s = jnp.transpose(st_s.reshape(N_HGRN_HEADS, HGRN_D, HGRN_D, nb), (3, 0, 1, 2))

        xp = _ffn(xp, ffn2_pre_g[l], ffn2_post_g[l], w_gu2, w_d2, FFN_TILE)
        xs = _ffn(xs, ffn2_pre_g[l], ffn2_post_g[l], w_gu2, w_d2, nb)

        kv_shape = (WINDOW, N_KV_HEADS, HEAD_DIM)
        for lst, val in zip(outs, (wk_p.reshape(batch, *kv_shape), wv_p.reshape(batch, *kv_shape), s_p,
                                   from_kt(wkt_s), from_kt(wvt_s), s_s)):
            lst.append(val)

    return (xp.reshape(batch, seq, D_MODEL), xs.reshape(nb, t_s, D_MODEL)) + tuple(jnp.stack(o) for o in outs)
```

```python
import functools

import jax
import jax.numpy as jnp
import numpy as np
from jax import lax
from jax.experimental import pallas as pl
from jax.experimental.pallas import tpu as pltpu

F32 = jnp.float32
BF16 = jnp.bfloat16

D_MODEL = 1024
D_FF = 2816
HEAD_DIM = 64
N_Q_HEADS = 8
N_KV_HEADS = 2
GQA_GROUP = N_Q_HEADS // N_KV_HEADS
WINDOW = 128
PAST_LEN = 8192
ROT_DIM = HEAD_DIM // 4
ROPE_THETA = 500000.0
N_HGRN_HEADS = 8
HGRN_D = 64
D_ATTN = N_Q_HEADS * HEAD_DIM
D_KV = N_KV_HEADS * HEAD_DIM
D_HGRN = N_HGRN_HEADS * HGRN_D
IN_COLS = D_ATTN + 2 * D_KV + 4 * D_HGRN
OFF_Q, OFF_K, OFF_V = 0, D_ATTN, D_ATTN + D_KV
OFF_HQ = D_ATTN + 2 * D_KV
OFF_HF, OFF_HI, OFF_HG = OFF_HQ + D_HGRN, OFF_HQ + 2 * D_HGRN, OFF_HQ + 3 * D_HGRN
EPS = 1e-6
NEG_INF = -1e30
LOG2E = 1.4426950408889634
LANES = 128
HALF = LANES // 2

FFN_TILE = 512
MIX_TILE = 256
CHUNK = 64
SUB = 16
N_SUB = CHUNK // SUB
VMEM_LIMIT = 56 * 1024 * 1024


def _rms(x, g):
    return x * lax.rsqrt(jnp.mean(x * x, axis=-1, keepdims=True) + EPS) * g


def _silu(x):
    return x * jax.nn.sigmoid(x)


def _dot(a, b):
    return jnp.dot(a, b, preferred_element_type=F32)


def _dot_nt(a, b):
    return lax.dot_general(a, b, (((1,), (1,)), ((), ())), preferred_element_type=F32)


def _dot_tn(a, b):
    return lax.dot_general(a, b, (((0,), (0,)), ((), ())), preferred_element_type=F32)


def _split3(x):
    hi = x.astype(BF16)
    r = x - hi.astype(F32)
    mid = r.astype(BF16)
    lo = (r - mid.astype(F32)).astype(BF16)
    return hi, mid, lo


def _const_spec(shape):
    nd = len(shape)
    return pl.BlockSpec(shape, lambda *_: (0,) * nd, pipeline_mode=pl.Buffered(1))


def _ffn_kernel(x_ref, gpre_ref, gpost_ref, wgu_ref, wd_ref, o_ref):
    x = x_ref[...]
    h = _rms(x, gpre_ref[...]).astype(BF16)
    gate = _dot(h, wgu_ref[:, :D_FF])
    up = _dot(h, wgu_ref[:, D_FF:])
    act = (_silu(gate) * up).astype(BF16)
    y = _dot(act, wd_ref[...])
    o_ref[...] = x + 0.5 * _rms(y, gpost_ref[...])


def _ffn(x, g_pre, g_post, w_gu, w_down, tile):
    n = x.shape[0]
    return pl.pallas_call(
        _ffn_kernel,
        grid=(n // tile,),
        in_specs=[
            pl.BlockSpec((tile, D_MODEL), lambda i: (i, 0)),
            _const_spec((1, D_MODEL)),
            _const_spec((1, D_MODEL)),
            _const_spec((D_MODEL, 2 * D_FF)),
            _const_spec((D_FF, D_MODEL)),
        ],
        out_specs=pl.BlockSpec((tile, D_MODEL), lambda i: (i, 0)),
        out_shape=jax.ShapeDtypeStruct((n, D_MODEL), F32),
        compiler_params=pltpu.CompilerParams(
            dimension_semantics=("arbitrary",), vmem_limit_bytes=VMEM_LIMIT),
        name="ffn",
    )(x, g_pre.reshape(1, D_MODEL), g_post.reshape(1, D_MODEL), w_gu, w_down)


def _lower_bound(lb_raw, layer):
    m = jnp.max(lb_raw, axis=0, keepdims=True)
    e = jnp.exp(lb_raw - m)
    return jnp.sum(e[: layer + 1], axis=0, keepdims=True) / jnp.sum(e, axis=0, keepdims=True)


def _rope(x, cos, sin_hi, sin_lo):
    return x * cos + pltpu.roll(x, ROT_DIM // 2, 1) * sin_hi + pltpu.roll(x, LANES - ROT_DIM // 2, 1) * sin_lo


def _rope_tables(pos):
    half = ROT_DIM // 2
    inv = ROPE_THETA ** (-jnp.arange(half, dtype=F32) / half)
    ang = pos[:, None] * inv[None, :]
    cos, sin = jnp.cos(ang), jnp.sin(ang)
    t = pos.shape[0]
    one = jnp.ones((t, HEAD_DIM - ROT_DIM), F32)
    zero = jnp.zeros((t, HEAD_DIM - ROT_DIM), F32)
    zh = jnp.zeros((t, half), F32)
    c = jnp.concatenate([cos, cos, one], axis=1)
    s_hi = jnp.concatenate([zh, sin, zero], axis=1)
    s_lo = jnp.concatenate([-sin, zh, zero], axis=1)
    return tuple(jnp.tile(a, (1, LANES // HEAD_DIM)) for a in (c, s_hi, s_lo))


def _dup_half(x, g, lo_half):
    xr = pltpu.roll(x, HALF, 1)
    return jnp.where(lo_half, x, xr) if g == 0 else jnp.where(lo_half, xr, x)


def _prompt_mix_kernel(sink_ref, x_ref, gpre_ref, win_ref, cos_ref, shi_ref, slo_ref, lb_ref, gout_ref,
                       wout_ref, gpost_ref, bias_ref, amask_ref,
                       y_ref, wk_ref, wv_ref, s_ref,
                       kk_scr, vv_scr, st_scr, g_scr, ho_scr, *, layer):
    tb = MIX_TILE
    n = pl.program_id(1)
    last = pl.num_programs(1) - 1

    @pl.when(n == 0)
    def _():
        kk_scr[...] = jnp.zeros_like(kk_scr)
        vv_scr[...] = jnp.zeros_like(vv_scr)
        st_scr[...] = jnp.zeros_like(st_scr)

    x = x_ref[...]
    u = _rms(x, gpre_ref[...]).astype(BF16)

    def proj(off, width):
        return _dot(u, win_ref[:, off:off + width])

    p_attn = proj(OFF_Q, D_ATTN + 2 * D_KV)
    p_hf = proj(OFF_HF, D_HGRN)
    cos, s_hi, s_lo = cos_ref[...], shi_ref[...], slo_ref[...]
    scale = HEAD_DIM ** -0.5 * LOG2E
    q_cols = [_rope(p_attn[:, LANES * j: LANES * (j + 1)], cos, s_hi, s_lo) * scale
              for j in range(D_ATTN // LANES)]
    k_rot = _rope(p_attn[:, OFF_K:OFF_K + D_KV], cos, s_hi, s_lo)
    v_new = p_attn[:, OFF_V:OFF_V + D_KV]

    @pl.when(n == last)
    def _():
        wk_ref[0] = k_rot[tb - WINDOW:]
        wv_ref[0] = v_new[tb - WINDOW:]

    lane = lax.broadcasted_iota(jnp.int32, (WINDOW, LANES), 1)
    lo_half = lane < HALF
    bias = bias_ref[...]
    no_prev = jnp.where(n > 0, 0.0, NEG_INF)
    bias_first = jnp.concatenate([bias[:WINDOW] + no_prev, bias[WINDOW:]], axis=0)

    k_prev = [kk_scr[g] for g in range(N_KV_HEADS)]
    v_prev = [vv_scr[g] for g in range(N_KV_HEADS)]
    att_blocks = []
    for i in range(tb // WINDOW):
        r0 = i * WINDOW
        kc, vc = k_rot[r0:r0 + WINDOW], v_new[r0:r0 + WINDOW]
        k_cur = [_dup_half(kc, g, lo_half).astype(BF16) for g in range(N_KV_HEADS)]
        v_cur = [_dup_half(vc, g, lo_half).astype(BF16) for g in range(N_KV_HEADS)]
        att_cols = []
        for g in range(N_KV_HEADS):
            keys = jnp.concatenate([k_prev[g], k_cur[g]], axis=0)
            vals = jnp.concatenate([v_prev[g], v_cur[g]], axis=0)
            heads = range(g * GQA_GROUP, (g + 1) * GQA_GROUP)
            qg = jnp.concatenate(
                [jnp.where(lo_half if h % 2 == 0 else ~lo_half, q_cols[h // 2][r0:r0 + WINDOW], 0.0)
                 for h in heads], axis=0).astype(BF16)
            s = _dot_nt(keys, qg) + (bias_first if i == 0 else bias)
            sink = jnp.concatenate(
                [jnp.full((1, WINDOW), sink_ref[h] * LOG2E, F32) for h in heads], axis=1)
            m = jnp.maximum(jnp.max(s, axis=0, keepdims=True), sink)
            p = jnp.exp2(s - m)
            denom = jnp.sum(p, axis=0, keepdims=True) + jnp.exp2(sink - m)
            p = (p * (1.0 / denom)).astype(BF16)
            o = _dot_tn(p, vals)
            for jj in range(GQA_GROUP // 2):
                o_even = o[(2 * jj) * WINDOW:(2 * jj + 1) * WINDOW]
                o_odd = o[(2 * jj + 1) * WINDOW:(2 * jj + 2) * WINDOW]
                att_cols.append(jnp.where(lo_half, o_even, o_odd))
        att_blocks.append(jnp.concatenate(att_cols, axis=1))
        k_prev, v_prev = k_cur, v_cur
    att = jnp.concatenate(att_blocks, axis=0)
    for g in range(N_KV_HEADS):
        kk_scr[g] = k_prev[g]
        vv_scr[g] = v_prev[g]

    lb = _lower_bound(lb_ref[...], layer)
    f = lb + (1.0 - lb) * jax.nn.sigmoid(p_hf)
    logf = jnp.log(f)
    hk = 1.0 - f
    hq = proj(OFF_HQ, D_HGRN) * (HGRN_D ** -0.5)
    hv_bf = proj(OFF_HI, D_HGRN).astype(BF16)

    tr = lax.broadcasted_iota(jnp.int32, (tb, tb), 0)
    tc = lax.broadcasted_iota(jnp.int32, (tb, tb), 1)
    tri = ((tr // CHUNK == tc // CHUNK) & (tc <= tr)).astype(BF16)
    g_cum = sum(_dot(tri, part) for part in _split3(logf))
    g_scr[...] = g_cum

    def bcast_row(r, rows):
        return jnp.broadcast_to(g_scr[r:r + 1, :], (rows, D_HGRN))

    zeros_sub = jnp.zeros((SUB, D_HGRN), F32)
    g_ref_q = jnp.concatenate(
        [zeros_sub if sb % N_SUB == 0 else bcast_row(sb * SUB - 1, SUB) for sb in range(tb // SUB)], axis=0)
    q_loc = (hq * jnp.exp(g_cum - g_ref_q)).astype(BF16)
    g_end = jnp.concatenate(
        [bcast_row(c * CHUNK + CHUNK - 1, CHUNK) for c in range(tb // CHUNK)], axis=0)
    k_sub = []
    for i in range(N_SUB):
        live = (i + 1) * SUB
        pieces = []
        for c in range(tb // CHUNK):
            r0 = c * CHUNK
            g_ref_i = 0.0 if i == 0 else bcast_row(r0 + i * SUB - 1, live)
            pieces.append(hk[r0:r0 + live] * jnp.exp(g_ref_i - g_cum[r0:r0 + live]))
            if live < CHUNK:
                pieces.append(jnp.zeros((CHUNK - live, D_HGRN), F32))
        k_sub.append(jnp.concatenate(pieces, axis=0).astype(BF16))

    n_ch = tb // CHUNK
    gc = [g_scr[c * CHUNK + CHUNK - 1:c * CHUNK + CHUNK, :] for c in range(n_ch)]

    def span(lo, hi):
        if hi <= lo:
            return jnp.ones((CHUNK, D_HGRN), F32)
        return jnp.broadcast_to(jnp.exp(sum(gc[lo:hi])), (CHUNK, D_HGRN))

    zeros_chunk = jnp.zeros((CHUNK, D_HGRN), F32)
    k_end32 = hk * jnp.exp(g_end - g_cum)
    q_glob32 = hq * jnp.exp(g_cum)
    k_cross = [(k_end32 * jnp.concatenate(
        [span(cp + 1, c) if cp < c else zeros_chunk for cp in range(n_ch)], axis=0)).astype(BF16)
        for c in range(1, n_ch)]
    q_tile = (q_glob32 * jnp.concatenate([span(0, c) for c in range(n_ch)], axis=0)).astype(BF16)
    k_tile_end = (k_end32 * jnp.concatenate([span(cp + 1, n_ch) for cp in range(n_ch)], axis=0)).astype(BF16)
    decay_tile = jnp.exp(sum(gc))
    q_glob = q_glob32.astype(BF16)

    def keep_rows(a, block, wanted):
        zero = jnp.zeros((block, a.shape[1]), a.dtype)
        return jnp.concatenate(
            [a[r * block:(r + 1) * block] if wanted(r) else zero for r in range(a.shape[0] // block)], axis=0)

    q_sub = [keep_rows(q_loc, SUB, lambda r, i=i: r % N_SUB == i) for i in range(N_SUB)]
    q_cross = [keep_rows(q_glob, CHUNK, lambda r, c=c: r == c) for c in range(1, n_ch)]
    hg_act = _silu(proj(OFF_HG, D_HGRN))
    y_att = _dot(att.astype(BF16), wout_ref[:D_ATTN, :])
    amask = amask_ref[...]
    for h in range(N_HGRN_HEADS):
        ls = slice(h * HGRN_D, (h + 1) * HGRN_D)
        q_stack = jnp.concatenate([q_sub[i][:, ls] for i in range(N_SUB)], axis=1)
        k_stack = jnp.concatenate([k_sub[i][:, ls] for i in range(N_SUB)], axis=1)
        qc_stack = jnp.concatenate([q_cross[c - 1][:, ls] for c in range(1, n_ch)], axis=1)
        kc_stack = jnp.concatenate([k_cross[c - 1][:, ls] for c in range(1, n_ch)], axis=1)
        a = (_dot_nt(q_stack, k_stack) * amask + _dot_nt(qc_stack, kc_stack)).astype(BF16)
        v_h = hv_bf[:, ls]
        state = st_scr[h]
        o = _dot(a, v_h) + _dot_nt(q_tile[:, ls], state.astype(BF16))
        st_scr[h] = state * decay_tile[:, ls] + _dot_tn(v_h, k_tile_end[:, ls])
        o = o * lax.rsqrt(jnp.mean(o * o, axis=-1, keepdims=True) + EPS)
        ho_scr[:, ls] = o

    @pl.when(n == last)
    def _():
        for h in range(N_HGRN_HEADS):
            s_ref[0, h] = st_scr[h].T

    hn = ho_scr[...] * gout_ref[...] * hg_act
    y = y_att + _dot(hn.astype(BF16), wout_ref[D_ATTN:, :])
    y_ref[...] = x + _rms(y, gpost_ref[...])


def _prompt_mix(x, sinks, g_pre, w_in, lb_raw, g_out, w_out, g_post, batch, seq, layer):
    tb = MIX_TILE
    nt = seq // tb
    cos, s_hi, s_lo = _rope_tables(jnp.arange(seq, dtype=F32))
    tok = lambda b, n: (b * nt + n, 0)
    tab = lambda b, n: (n, 0)
    per_b3 = lambda b, n: (b, 0, 0)
    lb_rows = lb_raw.shape[0]
    key_i = jnp.arange(2 * WINDOW)[:, None]
    rel = jnp.arange(WINDOW)[None, :] + WINDOW - key_i
    bias = jnp.tile(jnp.where((rel >= 0) & (rel < WINDOW), 0.0, NEG_INF).astype(F32), (1, GQA_GROUP))
    t_i = jnp.arange(tb)
    amask = ((t_i[:, None] // CHUNK == t_i[None, :] // CHUNK) & (t_i[None, :] <= t_i[:, None])).astype(F32)
    return pl.pallas_call(
        functools.partial(_prompt_mix_kernel, layer=layer),
        grid=(batch, nt),
        in_specs=[
            pl.BlockSpec(memory_space=pltpu.SMEM),
            pl.BlockSpec((tb, D_MODEL), tok),
            _const_spec((1, D_MODEL)),
            _const_spec((D_MODEL, IN_COLS)),
            pl.BlockSpec((tb, LANES), tab),
            pl.BlockSpec((tb, LANES), tab),
            pl.BlockSpec((tb, LANES), tab),
            _const_spec((lb_rows, D_HGRN)),
            _const_spec((1, D_HGRN)),
            _const_spec((D_MODEL, D_MODEL)),
            _const_spec((1, D_MODEL)),
            _const_spec((2 * WINDOW, GQA_GROUP * WINDOW)),
            _const_spec((tb, tb)),
        ],
        out_specs=[
            pl.BlockSpec((tb, D_MODEL), tok),
            pl.BlockSpec((1, WINDOW, D_KV), per_b3),
            pl.BlockSpec((1, WINDOW, D_KV), per_b3),
            pl.BlockSpec((1, N_HGRN_HEADS, HGRN_D, HGRN_D), lambda b, n: (b, 0, 0, 0)),
        ],
        out_shape=[
            jax.ShapeDtypeStruct((batch * seq, D_MODEL), F32),
            jax.ShapeDtypeStruct((batch, WINDOW, D_KV), F32),
            jax.ShapeDtypeStruct((batch, WINDOW, D_KV), F32),
            jax.ShapeDtypeStruct((batch, N_HGRN_HEADS, HGRN_D, HGRN_D), F32),
        ],
        scratch_shapes=[
            pltpu.VMEM((N_KV_HEADS, WINDOW, LANES), BF16),
            pltpu.VMEM((N_KV_HEADS, WINDOW, LANES), BF16),
            pltpu.VMEM((N_HGRN_HEADS, HGRN_D, HGRN_D), F32),
            pltpu.VMEM((tb, D_HGRN), F32),
            pltpu.VMEM((tb, D_HGRN), F32),
        ],
        compiler_params=pltpu.CompilerParams(
            dimension_semantics=("arbitrary", "arbitrary"), vmem_limit_bytes=VMEM_LIMIT),
        name="prompt_mix",
    )(sinks, x, g_pre.reshape(1, D_MODEL), w_in, cos, s_hi, s_lo, lb_raw, g_out.reshape(1, D_HGRN),
      w_out, g_post.reshape(1, D_MODEL), bias, amask)


def _sample_mix_kernel(sink_ref, x_ref, gpre_ref, win_ref, cos_ref, shi_ref, slo_ref, lb_ref, goutc_ref,
                       wout_ref, gpost_ref, ckt_ref, cvt_ref, sin_ref,
                       y_ref, wkt_ref, wvt_ref, sout_ref,
                       q_scr, kn_scr, vn_scr, knt_scr, vnt_scr, ft_scr, hkt_scr, hqt_scr, hvt_scr, hgt_scr,
                       ot_scr, att_scr, *, layer, nb):
    step = pl.program_id(0)
    last = pl.num_programs(0) - 1
    bt = nb // N_HGRN_HEADS

    @pl.when(step == 0)
    def _():
        u = _rms(x_ref[...], gpre_ref[...]).astype(BF16)
        proj = _dot(u, win_ref[...])
        cos, s_hi, s_lo = cos_ref[...], shi_ref[...], slo_ref[...]
        scale = HEAD_DIM ** -0.5
        for j in range(D_ATTN // LANES):
            q_scr[:, LANES * j:LANES * (j + 1)] = _rope(
                proj[:, OFF_Q + LANES * j:OFF_Q + LANES * (j + 1)], cos, s_hi, s_lo) * scale
        k_new = _rope(proj[:, OFF_K:OFF_K + D_KV], cos, s_hi, s_lo)
        v_new = proj[:, OFF_V:OFF_V + D_KV]
        kn_scr[...] = k_new
        vn_scr[...] = v_new
        for scr, val in ((knt_scr, k_new), (vnt_scr, v_new)):
            for i, part in enumerate(_split3(val.T)):
                scr[i] = part
        lb = _lower_bound(lb_ref[...], layer)
        f_t = (lb + (1.0 - lb) * jax.nn.sigmoid(proj[:, OFF_HF:OFF_HF + D_HGRN])).T
        ft_scr[...] = f_t
        hkt_scr[...] = 1.0 - f_t
        hqt_scr[...] = (proj[:, OFF_HQ:OFF_HQ + D_HGRN] * (HGRN_D ** -0.5)).T
        hvt_scr[...] = proj[:, OFF_HI:OFF_HI + D_HGRN].T
        hgt_scr[...] = proj[:, OFF_HG:OFF_HG + D_HGRN].T

    base = pl.multiple_of(step * HGRN_D, HGRN_D)
    hv_t = hvt_scr[pl.ds(base, HGRN_D), :]

    def hgrn_row(k, o_acc):
        f_row = ft_scr[pl.ds(base + k, 1), :]
        s_new = f_row * sin_ref[k] + hkt_scr[pl.ds(base + k, 1), :] * hv_t
        sout_ref[k] = s_new
        return o_acc + hqt_scr[pl.ds(base + k, 1), :] * s_new

    ot_scr[pl.ds(base, HGRN_D), :] = lax.fori_loop(
        0, HGRN_D, hgrn_row, jnp.zeros((HGRN_D, nb), F32), unroll=8)

    lane8 = lax.broadcasted_iota(jnp.int32, (N_Q_HEADS, LANES), 1)
    row8 = lax.broadcasted_iota(jnp.int32, (N_Q_HEADS, LANES), 0)
    keep8 = (lane8 >= HALF) == (row8 >= GQA_GROUP)
    win_lane = lax.broadcasted_iota(jnp.int32, (D_KV, WINDOW), 1)
    sink = sink_ref[...]
    b0 = step * bt
    sel = (lax.broadcasted_iota(jnp.int32, (nb, bt), 0)
           == b0 + lax.broadcasted_iota(jnp.int32, (nb, bt), 1)).astype(BF16)
    k_cols = sum(_dot(knt_scr[i], sel) for i in range(3))
    v_cols = sum(_dot(vnt_scr[i], sel) for i in range(3))
    for bi in range(bt):
        b = b0 + bi
        q_b = jnp.broadcast_to(q_scr[pl.ds(b, 1), :], (N_Q_HEADS, D_ATTN))
        qm = jnp.zeros((N_Q_HEADS, LANES), F32)
        for h in range(N_Q_HEADS):
            c = q_b[:, LANES * (h // 2):LANES * (h // 2 + 1)]
            if h % 2 != h // GQA_GROUP:
                c = pltpu.roll(c, HALF, 1)
            qm = jnp.where(row8 == h, c, qm)
        qm = jnp.where(keep8, qm, 0.0)
        k_new = kn_scr[pl.ds(b, 1), :]
        v_new = vn_scr[pl.ds(b, 1), :]
        k_old, v_old = ckt_ref[bi], cvt_ref[bi]
        s = _dot(qm.astype(BF16), k_old.astype(BF16))
        s = jnp.where(lane8 >= 1, s, NEG_INF)
        s_new = jnp.sum(qm * k_new, axis=-1, keepdims=True)
        m = jnp.maximum(jnp.maximum(jnp.max(s, axis=-1, keepdims=True), s_new), sink)
        p = jnp.exp(s - m)
        p_new = jnp.exp(s_new - m)
        denom = jnp.sum(p, axis=-1, keepdims=True) + p_new + jnp.exp(sink - m)
        o = (_dot_nt(p.astype(BF16), v_old.astype(BF16)) + p_new * v_new) / denom
        att_scr[pl.ds(pl.multiple_of(b * N_Q_HEADS, N_Q_HEADS), N_Q_HEADS), :] = o
        wkt_ref[bi] = jnp.where(win_lane == WINDOW - 1, k_cols[:, bi:bi + 1], pltpu.roll(k_old, WINDOW - 1, 1))
        wvt_ref[bi] = jnp.where(win_lane == WINDOW - 1, v_cols[:, bi:bi + 1], pltpu.roll(v_old, WINDOW - 1, 1))

    @pl.when(step == last)
    def _():
        y = jnp.zeros((nb, D_MODEL), F32)
        for h in range(N_HGRN_HEADS):
            ks = slice(h * HGRN_D, (h + 1) * HGRN_D)
            o_t = ot_scr[ks, :]
            hn_t = (o_t * lax.rsqrt(jnp.mean(o_t * o_t, axis=0, keepdims=True) + EPS)
                    * goutc_ref[...] * _silu(hgt_scr[ks, :]))
            y = y + _dot_tn(hn_t.astype(BF16), wout_ref[D_ATTN + h * HGRN_D:D_ATTN + (h + 1) * HGRN_D, :])
        for h in range(N_Q_HEADS):
            g = h // GQA_GROUP
            a_h = att_scr[pl.ds(h, nb, stride=N_Q_HEADS), :][:, g * HALF:(g + 1) * HALF]
            y = y + _dot(a_h.astype(BF16), wout_ref[h * HEAD_DIM:(h + 1) * HEAD_DIM, :])
        y_ref[...] = x_ref[...] + _rms(y, gpost_ref[...])


def _sample_mix(x, sinks, g_pre, w_in, lb_raw, g_out_head, w_out, g_post, cache_kt, cache_vt, state_t, pos, layer):
    nb = x.shape[0]
    bt = nb // N_HGRN_HEADS
    cos, s_hi, s_lo = _rope_tables(pos)
    lb_rows = lb_raw.shape[0]
    blk3 = pl.BlockSpec((bt, D_KV, WINDOW), lambda i: (i, 0, 0))
    blk_s = pl.BlockSpec((HGRN_D, HGRN_D, nb), lambda i: (i, 0, 0))
    chan_major = pltpu.VMEM((D_HGRN, nb), F32)
    return pl.pallas_call(
        functools.partial(_sample_mix_kernel, layer=layer, nb=nb),
        grid=(N_HGRN_HEADS,),
        in_specs=[
            _const_spec((N_Q_HEADS, 1)),
            _const_spec((nb, D_MODEL)),
            _const_spec((1, D_MODEL)),
            _const_spec((D_MODEL, IN_COLS)),
            _const_spec((1, LANES)),
            _const_spec((1, LANES)),
            _const_spec((1, LANES)),
            _const_spec((lb_rows, D_HGRN)),
            _const_spec((HGRN_D, 1)),
            _const_spec((D_MODEL, D_MODEL)),
            _const_spec((1, D_MODEL)),
            blk3, blk3, blk_s,
        ],
        out_specs=[pl.BlockSpec((nb, D_MODEL), lambda i: (0, 0)), blk3, blk3, blk_s],
        out_shape=[
            jax.ShapeDtypeStruct((nb, D_MODEL), F32),
            jax.ShapeDtypeStruct(cache_kt.shape, F32),
            jax.ShapeDtypeStruct(cache_vt.shape, F32),
            jax.ShapeDtypeStruct(state_t.shape, F32),
        ],
        scratch_shapes=[
            pltpu.VMEM((nb, D_ATTN), F32),
            pltpu.VMEM((nb, D_KV), F32),
            pltpu.VMEM((nb, D_KV), F32),
            pltpu.VMEM((3, D_KV, nb), BF16),
            pltpu.VMEM((3, D_KV, nb), BF16),
            chan_major, chan_major, chan_major, chan_major, chan_major, chan_major,
            pltpu.VMEM((nb * N_Q_HEADS, LANES), F32),
        ],
        compiler_params=pltpu.CompilerParams(
            dimension_semantics=("arbitrary",), vmem_limit_bytes=VMEM_LIMIT),
        name="sample_mix",
    )(sinks.reshape(N_Q_HEADS, 1), x, g_pre.reshape(1, D_MODEL), w_in, cos, s_hi, s_lo, lb_raw,
      g_out_head.reshape(HGRN_D, 1), w_out, g_post.reshape(1, D_MODEL), cache_kt, cache_vt, state_t)


def kernel(x_prompt, x_sample, cache_win_k, cache_win_v, state_hgrn, ffn1_pre_g, ffn1_post_g, ffn1_w_gu,
           ffn1_w_down, mix_pre_g, mix_post_g, w_in, attn_sinks, hgrn_lb, hgrn_out_g, w_out, ffn2_pre_g,
           ffn2_post_g, ffn2_w_gu, ffn2_w_down):
    batch, seq, _ = x_prompt.shape
    nb, t_s, _ = x_sample.shape
    depth = w_in.shape[0]
    assert t_s == 1 and seq % MIX_TILE == 0 and (batch * seq) % FFN_TILE == 0 and nb == LANES
    assert cache_win_k.shape[2:] == (WINDOW, N_KV_HEADS, HEAD_DIM)

    xp = x_prompt.reshape(batch * seq, D_MODEL)
    xs = x_sample.reshape(nb, D_MODEL)
    pos_s = PAST_LEN + jnp.arange(t_s, dtype=F32)
    outs = [[] for _ in range(6)]
    for l in range(depth):
        w_gu1, w_d1 = ffn1_w_gu[l].astype(BF16), ffn1_w_down[l].astype(BF16)
        w_gu2, w_d2 = ffn2_w_gu[l].astype(BF16), ffn2_w_down[l].astype(BF16)
        w_in_l, w_out_l = w_in[l].astype(BF16), w_out[l].astype(BF16)
        g_out = jnp.tile(hgrn_out_g[l], N_HGRN_HEADS)

        xp = _ffn(xp, ffn1_pre_g[l], ffn1_post_g[l], w_gu1, w_d1, FFN_TILE)
        xs = _ffn(xs, ffn1_pre_g[l], ffn1_post_g[l], w_gu1, w_d1, nb)

        xp, wk_p, wv_p, s_p = _prompt_mix(xp, attn_sinks[l], mix_pre_g[l], w_in_l, hgrn_lb, g_out, w_out_l,
                                          mix_post_g[l], batch, seq, l)
        to_kt = lambda c: jnp.transpose(c, (0, 2, 3, 1)).reshape(nb, D_KV, WINDOW)
        from_kt = lambda c: jnp.transpose(c.reshape(nb, N_KV_HEADS, HEAD_DIM, WINDOW), (0, 3, 1, 2))
        state_t = jnp.transpose(state_hgrn[l], (1, 2, 3, 0)).reshape(D_HGRN, HGRN_D, nb)
        xs, wkt_s, wvt_s, st_s = _sample_mix(
            xs, attn_sinks[l], mix_pre_g[l], w_in_l, hgrn_lb, hgrn_out_g[l], w_out_l, mix_post_g[l],
            to_kt(cache_win_k[l]), to_kt(cache_win_v[l]), state_t, pos_s, l)
        s_s = jnp.transpose(st_s.reshape(N_HGRN_HEADS, HGRN_D, HGRN_D, nb), (3, 0, 1, 2))

        xp = _ffn(xp, ffn2_pre_g[l], ffn2_post_g[l], w_gu2, w_d2, FFN_TILE)
        xs = _ffn(xs, ffn2_pre_g[l], ffn2_post_g[l], w_gu2, w_d2, nb)

        kv_shape = (WINDOW, N_KV_HEADS, HEAD_DIM)
        for lst, val in zip(outs, (wk_p.reshape(batch, *kv_shape), wv_p.reshape(batch, *kv_shape), s_p,
                                   from_kt(wkt_s), from_kt(wvt_s), s_s)):
            lst.append(val)

    return (xp.reshape(batch, seq, D_MODEL), xs.reshape(nb, t_s, D_MODEL)) + tuple(jnp.stack(o) for o in outs)
```

```python
import functools

import jax
import jax.numpy as jnp
import numpy as np
from jax import lax
from jax.experimental import pallas as pl
from jax.experimental.pallas import tpu as pltpu

F32 = jnp.float32
BF16 = jnp.bfloat16

D_MODEL = 1024
D_FF = 2816
HEAD_DIM = 64
N_Q_HEADS = 8
N_KV_HEADS = 2
GQA_GROUP = N_Q_HEADS // N_KV_HEADS
WINDOW = 128
PAST_LEN = 8192
ROT_DIM = HEAD_DIM // 4
ROPE_THETA = 500000.0
N_HGRN_HEADS = 8
HGRN_D = 64
D_ATTN = N_Q_HEADS * HEAD_DIM
D_KV = N_KV_HEADS * HEAD_DIM
D_HGRN = N_HGRN_HEADS * HGRN_D
IN_COLS = D_ATTN + 2 * D_KV + 4 * D_HGRN
OFF_Q, OFF_K, OFF_V = 0, D_ATTN, D_ATTN + D_KV
OFF_HQ = D_ATTN + 2 * D_KV
OFF_HF, OFF_HI, OFF_HG = OFF_HQ + D_HGRN, OFF_HQ + 2 * D_HGRN, OFF_HQ + 3 * D_HGRN
EPS = 1e-6
NEG_INF = -1e30
LOG2E = 1.4426950408889634
LANES = 128
HALF = LANES // 2

FFN_TILE = 1024
FFN_PARTS = 4
MIX_TILE = 256
CHUNK = 64
SUB = 16
N_SUB = CHUNK // SUB
VMEM_LIMIT = 56 * 1024 * 1024


def _rms(x, g):
    return x * lax.rsqrt(jnp.mean(x * x, axis=-1, keepdims=True) + EPS) * g


def _silu(x):
    return x * jax.nn.sigmoid(x)


def _dot(a, b):
    return jnp.dot(a, b, preferred_element_type=F32)


def _dot_nt(a, b):
    return lax.dot_general(a, b, (((1,), (1,)), ((), ())), preferred_element_type=F32)


def _dot_tn(a, b):
    return lax.dot_general(a, b, (((0,), (0,)), ((), ())), preferred_element_type=F32)


def _split3(x):
    hi = x.astype(BF16)
    r = x - hi.astype(F32)
    mid = r.astype(BF16)
    lo = (r - mid.astype(F32)).astype(BF16)
    return hi, mid, lo


def _const_spec(shape):
    nd = len(shape)
    return pl.BlockSpec(shape, lambda *_: (0,) * nd, pipeline_mode=pl.Buffered(1))


def _ffn_kernel(x_ref, xs_ref, gpre_ref, gpost_ref, wgu_ref, wd_ref, o_ref, os_ref):
    def half_step(x):
        h = _rms(x, gpre_ref[...]).astype(BF16)
        gate = _dot(h, wgu_ref[:, :D_FF])
        up = _dot(h, wgu_ref[:, D_FF:])
        act = (_silu(gate) * up).astype(BF16)
        y = _dot(act, wd_ref[...])
        return x + 0.5 * _rms(y, gpost_ref[...])

    rows = x_ref.shape[0] // FFN_PARTS
    for r in range(FFN_PARTS):
        rs = slice(r * rows, (r + 1) * rows)
        o_ref[rs, :] = half_step(x_ref[rs, :])

    @pl.when(pl.program_id(0) == pl.num_programs(0) - 1)
    def _():
        os_ref[...] = half_step(xs_ref[...])


def _ffn(x, xs, g_pre, g_post, w_gu, w_down):
    n, nb = x.shape[0], xs.shape[0]
    return pl.pallas_call(
        _ffn_kernel,
        grid=(n // FFN_TILE,),
        in_specs=[
            pl.BlockSpec((FFN_TILE, D_MODEL), lambda i: (i, 0)),
            _const_spec((nb, D_MODEL)),
            _const_spec((1, D_MODEL)),
            _const_spec((1, D_MODEL)),
            _const_spec((D_MODEL, 2 * D_FF)),
            _const_spec((D_FF, D_MODEL)),
        ],
        out_specs=[pl.BlockSpec((FFN_TILE, D_MODEL), lambda i: (i, 0)),
                   pl.BlockSpec((nb, D_MODEL), lambda i: (0, 0))],
        out_shape=[jax.ShapeDtypeStruct((n, D_MODEL), F32), jax.ShapeDtypeStruct((nb, D_MODEL), F32)],
        compiler_params=pltpu.CompilerParams(
            dimension_semantics=("arbitrary",), vmem_limit_bytes=VMEM_LIMIT),
        name="ffn",
    )(x, xs, g_pre.reshape(1, D_MODEL), g_post.reshape(1, D_MODEL), w_gu, w_down)


def _lower_bound(lb_raw, layer):
    m = jnp.max(lb_raw, axis=0, keepdims=True)
    e = jnp.exp(lb_raw - m)
    return jnp.sum(e[: layer + 1], axis=0, keepdims=True) / jnp.sum(e, axis=0, keepdims=True)


def _rope(x, cos, sin_hi, sin_lo):
    return x * cos + pltpu.roll(x, ROT_DIM // 2, 1) * sin_hi + pltpu.roll(x, LANES - ROT_DIM // 2, 1) * sin_lo


def _rope_tables(pos):
    half = ROT_DIM // 2
    inv = ROPE_THETA ** (-jnp.arange(half, dtype=F32) / half)
    ang = pos[:, None] * inv[None, :]
    cos, sin = jnp.cos(ang), jnp.sin(ang)
    t = pos.shape[0]
    one = jnp.ones((t, HEAD_DIM - ROT_DIM), F32)
    zero = jnp.zeros((t, HEAD_DIM - ROT_DIM), F32)
    zh = jnp.zeros((t, half), F32)
    c = jnp.concatenate([cos, cos, one], axis=1)
    s_hi = jnp.concatenate([zh, sin, zero], axis=1)
    s_lo = jnp.concatenate([-sin, zh, zero], axis=1)
    return tuple(jnp.tile(a, (1, LANES // HEAD_DIM)) for a in (c, s_hi, s_lo))


def _dup_half(x, g, lo_half):
    xr = pltpu.roll(x, HALF, 1)
    return jnp.where(lo_half, x, xr) if g == 0 else jnp.where(lo_half, xr, x)


def _prompt_mix_kernel(sink_ref, x_ref, gpre_ref, win_ref, cos_ref, shi_ref, slo_ref, lb_ref, gout_ref,
                       wout_ref, gpost_ref, bias_ref, amask_ref,
                       y_ref, wk_ref, wv_ref, s_ref,
                       kk_scr, vv_scr, st_scr, g_scr, ho_scr, *, layer):
    tb = MIX_TILE
    n = pl.program_id(1)
    last = pl.num_programs(1) - 1

    @pl.when(n == 0)
    def _():
        kk_scr[...] = jnp.zeros_like(kk_scr)
        vv_scr[...] = jnp.zeros_like(vv_scr)
        st_scr[...] = jnp.zeros_like(st_scr)

    x = x_ref[...]
    u = _rms(x, gpre_ref[...]).astype(BF16)

    def proj(off, width):
        return _dot(u, win_ref[:, off:off + width])

    p_attn = proj(OFF_Q, D_ATTN + 2 * D_KV)
    p_hf = proj(OFF_HF, D_HGRN)
    cos, s_hi, s_lo = cos_ref[...], shi_ref[...], slo_ref[...]
    scale = HEAD_DIM ** -0.5 * LOG2E
    q_cols = [_rope(p_attn[:, LANES * j: LANES * (j + 1)], cos, s_hi, s_lo) * scale
              for j in range(D_ATTN // LANES)]
    k_rot = _rope(p_attn[:, OFF_K:OFF_K + D_KV], cos, s_hi, s_lo)
    v_new = p_attn[:, OFF_V:OFF_V + D_KV]

    @pl.when(n == last)
    def _():
        wk_ref[0] = k_rot[tb - WINDOW:]
        wv_ref[0] = v_new[tb - WINDOW:]

    lane = lax.broadcasted_iota(jnp.int32, (WINDOW, LANES), 1)
    lo_half = lane < HALF
    bias = bias_ref[...]
    no_prev = jnp.where(n > 0, 0.0, NEG_INF)
    bias_first = jnp.concatenate([bias[:WINDOW] + no_prev, bias[WINDOW:]], axis=0)

    k_prev = [kk_scr[g] for g in range(N_KV_HEADS)]
    v_prev = [vv_scr[g] for g in range(N_KV_HEADS)]
    att_blocks = []
    for i in range(tb // WINDOW):
        r0 = i * WINDOW
        kc, vc = k_rot[r0:r0 + WINDOW], v_new[r0:r0 + WINDOW]
        k_cur = [_dup_half(kc, g, lo_half).astype(BF16) for g in range(N_KV_HEADS)]
        v_cur = [_dup_half(vc, g, lo_half).astype(BF16) for g in range(N_KV_HEADS)]
        att_cols = []
        for g in range(N_KV_HEADS):
            keys = jnp.concatenate([k_prev[g], k_cur[g]], axis=0)
            vals = jnp.concatenate([v_prev[g], v_cur[g]], axis=0)
            heads = range(g * GQA_GROUP, (g + 1) * GQA_GROUP)
            qg = jnp.concatenate(
                [jnp.where(lo_half if h % 2 == 0 else ~lo_half, q_cols[h // 2][r0:r0 + WINDOW], 0.0)
                 for h in heads], axis=0).astype(BF16)
            s = _dot_nt(keys, qg) + (bias_first if i == 0 else bias)
            sink = jnp.concatenate(
                [jnp.full((1, WINDOW), sink_ref[h] * LOG2E, F32) for h in heads], axis=1)
            m = jnp.maximum(jnp.max(s, axis=0, keepdims=True), sink)
            p = jnp.exp2(s - m)
            denom = jnp.sum(p, axis=0, keepdims=True) + jnp.exp2(sink - m)
            p = (p * (1.0 / denom)).astype(BF16)
            o = _dot_tn(p, vals)
            for jj in range(GQA_GROUP // 2):
                o_even = o[(2 * jj) * WINDOW:(2 * jj + 1) * WINDOW]
                o_odd = o[(2 * jj + 1) * WINDOW:(2 * jj + 2) * WINDOW]
                att_cols.append(jnp.where(lo_half, o_even, o_odd))
        att_blocks.append(jnp.concatenate(att_cols, axis=1))
        k_prev, v_prev = k_cur, v_cur
    att = jnp.concatenate(att_blocks, axis=0)
    for g in range(N_KV_HEADS):
        kk_scr[g] = k_prev[g]
        vv_scr[g] = v_prev[g]

    lb = _lower_bound(lb_ref[...], layer)
    f = lb + (1.0 - lb) * jax.nn.sigmoid(p_hf)
    logf = jnp.log(f)
    hk = 1.0 - f
    hq = proj(OFF_HQ, D_HGRN) * (HGRN_D ** -0.5)
    hv_bf = proj(OFF_HI, D_HGRN).astype(BF16)

    tr = lax.broadcasted_iota(jnp.int32, (tb, tb), 0)
    tc = lax.broadcasted_iota(jnp.int32, (tb, tb), 1)
    tri = ((tr // CHUNK == tc // CHUNK) & (tc <= tr)).astype(BF16)
    g_cum = sum(_dot(tri, part) for part in _split3(logf))
    g_scr[...] = g_cum

    def bcast_row(r, rows):
        return jnp.broadcast_to(g_scr[r:r + 1, :], (rows, D_HGRN))

    zeros_sub = jnp.zeros((SUB, D_HGRN), F32)
    g_ref_q = jnp.concatenate(
        [zeros_sub if sb % N_SUB == 0 else bcast_row(sb * SUB - 1, SUB) for sb in range(tb // SUB)], axis=0)
    q_loc = (hq * jnp.exp(g_cum - g_ref_q)).astype(BF16)
    g_end = jnp.concatenate(
        [bcast_row(c * CHUNK + CHUNK - 1, CHUNK) for c in range(tb // CHUNK)], axis=0)
    k_sub = []
    for i in range(N_SUB):
        live = (i + 1) * SUB
        pieces = []
        for c in range(tb // CHUNK):
            r0 = c * CHUNK
            g_ref_i = 0.0 if i == 0 else bcast_row(r0 + i * SUB - 1, live)
            pieces.append(hk[r0:r0 + live] * jnp.exp(g_ref_i - g_cum[r0:r0 + live]))
            if live < CHUNK:
                pieces.append(jnp.zeros((CHUNK - live, D_HGRN), F32))
        k_sub.append(jnp.concatenate(pieces, axis=0).astype(BF16))

    n_ch = tb // CHUNK
    gc = [g_scr[c * CHUNK + CHUNK - 1:c * CHUNK + CHUNK, :] for c in range(n_ch)]

    def span(lo, hi):
        if hi <= lo:
            return jnp.ones((CHUNK, D_HGRN), F32)
        return jnp.broadcast_to(jnp.exp(sum(gc[lo:hi])), (CHUNK, D_HGRN))

    zeros_chunk = jnp.zeros((CHUNK, D_HGRN), F32)
    k_end32 = hk * jnp.exp(g_end - g_cum)
    q_glob32 = hq * jnp.exp(g_cum)
    k_cross = [(k_end32 * jnp.concatenate(
        [span(cp + 1, c) if cp < c else zeros_chunk for cp in range(n_ch)], axis=0)).astype(BF16)
        for c in range(1, n_ch)]
    q_tile = (q_glob32 * jnp.concatenate([span(0, c) for c in range(n_ch)], axis=0)).astype(BF16)
    k_tile_end = (k_end32 * jnp.concatenate([span(cp + 1, n_ch) for cp in range(n_ch)], axis=0)).astype(BF16)
    decay_tile = jnp.exp(sum(gc))
    q_glob = q_glob32.astype(BF16)

    def keep_rows(a, block, wanted):
        zero = jnp.zeros((block, a.shape[1]), a.dtype)
        return jnp.concatenate(
            [a[r * block:(r + 1) * block] if wanted(r) else zero for r in range(a.shape[0] // block)], axis=0)

    q_sub = [keep_rows(q_loc, SUB, lambda r, i=i: r % N_SUB == i) for i in range(N_SUB)]
    q_cross = [keep_rows(q_glob, CHUNK, lambda r, c=c: r == c) for c in range(1, n_ch)]
    hg_act = _silu(proj(OFF_HG, D_HGRN))
    y_att = _dot(att.astype(BF16), wout_ref[:D_ATTN, :])
    amask = amask_ref[...]
    for h in range(N_HGRN_HEADS):
        ls = slice(h * HGRN_D, (h + 1) * HGRN_D)
        q_stack = jnp.concatenate([q_sub[i][:, ls] for i in range(N_SUB)], axis=1)
        k_stack = jnp.concatenate([k_sub[i][:, ls] for i in range(N_SUB)], axis=1)
        qc_stack = jnp.concatenate([q_cross[c - 1][:, ls] for c in range(1, n_ch)], axis=1)
        kc_stack = jnp.concatenate([k_cross[c - 1][:, ls] for c in range(1, n_ch)], axis=1)
        a = (_dot_nt(q_stack, k_stack) * amask + _dot_nt(qc_stack, kc_stack)).astype(BF16)
        v_h = hv_bf[:, ls]
        state = st_scr[h]
        o = _dot(a, v_h) + _dot_nt(q_tile[:, ls], state.astype(BF16))
        st_scr[h] = state * decay_tile[:, ls] + _dot_tn(v_h, k_tile_end[:, ls])
        o = o * lax.rsqrt(jnp.mean(o * o, axis=-1, keepdims=True) + EPS)
        ho_scr[:, ls] = o

    @pl.when(n == last)
    def _():
        for h in range(N_HGRN_HEADS):
            s_ref[0, h] = st_scr[h].T

    hn = ho_scr[...] * gout_ref[...] * hg_act
    y = y_att + _dot(hn.astype(BF16), wout_ref[D_ATTN:, :])
    y_ref[...] = x + _rms(y, gpost_ref[...])


def _prompt_mix(x, sinks, g_pre, w_in, lb_raw, g_out, w_out, g_post, batch, seq, layer):
    tb = MIX_TILE
    nt = seq // tb
    cos, s_hi, s_lo = _rope_tables(jnp.arange(seq, dtype=F32))
    tok = lambda b, n: (b * nt + n, 0)
    tab = lambda b, n: (n, 0)
    per_b3 = lambda b, n: (b, 0, 0)
    lb_rows = lb_raw.shape[0]
    key_i = jnp.arange(2 * WINDOW)[:, None]
    rel = jnp.arange(WINDOW)[None, :] + WINDOW - key_i
    bias = jnp.tile(jnp.where((rel >= 0) & (rel < WINDOW), 0.0, NEG_INF).astype(F32), (1, GQA_GROUP))
    t_i = jnp.arange(tb)
    amask = ((t_i[:, None] // CHUNK == t_i[None, :] // CHUNK) & (t_i[None, :] <= t_i[:, None])).astype(F32)
    return pl.pallas_call(
        functools.partial(_prompt_mix_kernel, layer=layer),
        grid=(batch, nt),
        in_specs=[
            pl.BlockSpec(memory_space=pltpu.SMEM),
            pl.BlockSpec((tb, D_MODEL), tok),
            _const_spec((1, D_MODEL)),
            _const_spec((D_MODEL, IN_COLS)),
            pl.BlockSpec((tb, LANES), tab),
            pl.BlockSpec((tb, LANES), tab),
            pl.BlockSpec((tb, LANES), tab),
            _const_spec((lb_rows, D_HGRN)),
            _const_spec((1, D_HGRN)),
            _const_spec((D_MODEL, D_MODEL)),
            _const_spec((1, D_MODEL)),
            _const_spec((2 * WINDOW, GQA_GROUP * WINDOW)),
            _const_spec((tb, tb)),
        ],
        out_specs=[
            pl.BlockSpec((tb, D_MODEL), tok),
            pl.BlockSpec((1, WINDOW, D_KV), per_b3),
            pl.BlockSpec((1, WINDOW, D_KV), per_b3),
            pl.BlockSpec((1, N_HGRN_HEADS, HGRN_D, HGRN_D), lambda b, n: (b, 0, 0, 0)),
        ],
        out_shape=[
            jax.ShapeDtypeStruct((batch * seq, D_MODEL), F32),
            jax.ShapeDtypeStruct((batch, WINDOW, D_KV), F32),
            jax.ShapeDtypeStruct((batch, WINDOW, D_KV), F32),
            jax.ShapeDtypeStruct((batch, N_HGRN_HEADS, HGRN_D, HGRN_D), F32),
        ],
        scratch_shapes=[
            pltpu.VMEM((N_KV_HEADS, WINDOW, LANES), BF16),
            pltpu.VMEM((N_KV_HEADS, WINDOW, LANES), BF16),
            pltpu.VMEM((N_HGRN_HEADS, HGRN_D, HGRN_D), F32),
            pltpu.VMEM((tb, D_HGRN), F32),
            pltpu.VMEM((tb, D_HGRN), F32),
        ],
        compiler_params=pltpu.CompilerParams(
            dimension_semantics=("arbitrary", "arbitrary"), vmem_limit_bytes=VMEM_LIMIT),
        name="prompt_mix",
    )(sinks, x, g_pre.reshape(1, D_MODEL), w_in, cos, s_hi, s_lo, lb_raw, g_out.reshape(1, D_HGRN),
      w_out, g_post.reshape(1, D_MODEL), bias, amask)


def _sample_mix_kernel(sink_ref, x_ref, gpre_ref, win_ref, cos_ref, shi_ref, slo_ref, lb_ref, goutc_ref,
                       wout_ref, gpost_ref, ckt_ref, cvt_ref, sin_ref,
                       y_ref, wkt_ref, wvt_ref, sout_ref,
                       q_scr, kn_scr, vn_scr, knt_scr, vnt_scr, ft_scr, hkt_scr, hqt_scr, hvt_scr, hgt_scr,
                       ot_scr, att_scr, *, layer, nb):
    step = pl.program_id(0)
    last = pl.num_programs(0) - 1
    bt = nb // N_HGRN_HEADS

    @pl.when(step == 0)
    def _():
        u = _rms(x_ref[...], gpre_ref[...]).astype(BF16)
        proj = _dot(u, win_ref[...])
        cos, s_hi, s_lo = cos_ref[...], shi_ref[...], slo_ref[...]
        scale = HEAD_DIM ** -0.5
        for j in range(D_ATTN // LANES):
            q_scr[:, LANES * j:LANES * (j + 1)] = _rope(
                proj[:, OFF_Q + LANES * j:OFF_Q + LANES * (j + 1)], cos, s_hi, s_lo) * scale
        k_new = _rope(proj[:, OFF_K:OFF_K + D_KV], cos, s_hi, s_lo)
        v_new = proj[:, OFF_V:OFF_V + D_KV]
        kn_scr[...] = k_new
        vn_scr[...] = v_new
        for scr, val in ((knt_scr, k_new), (vnt_scr, v_new)):
            for i, part in enumerate(_split3(val.T)):
                scr[i] = part
        lb = _lower_bound(lb_ref[...], layer)
        f_t = (lb + (1.0 - lb) * jax.nn.sigmoid(proj[:, OFF_HF:OFF_HF + D_HGRN])).T
        ft_scr[...] = f_t
        hkt_scr[...] = 1.0 - f_t
        hqt_scr[...] = (proj[:, OFF_HQ:OFF_HQ + D_HGRN] * (HGRN_D ** -0.5)).T
        hvt_scr[...] = proj[:, OFF_HI:OFF_HI + D_HGRN].T
        hgt_scr[...] = proj[:, OFF_HG:OFF_HG + D_HGRN].T

    base = pl.multiple_of(step * HGRN_D, HGRN_D)
    hv_t = hvt_scr[pl.ds(base, HGRN_D), :]

    def hgrn_row(k, o_acc):
        f_row = ft_scr[pl.ds(base + k, 1), :]
        s_new = f_row * sin_ref[k] + hkt_scr[pl.ds(base + k, 1), :] * hv_t
        sout_ref[k] = s_new
        return o_acc + hqt_scr[pl.ds(base + k, 1), :] * s_new

    ot_scr[pl.ds(base, HGRN_D), :] = lax.fori_loop(
        0, HGRN_D, hgrn_row, jnp.zeros((HGRN_D, nb), F32), unroll=8)

    lane8 = lax.broadcasted_iota(jnp.int32, (N_Q_HEADS, LANES), 1)
    row8 = lax.broadcasted_iota(jnp.int32, (N_Q_HEADS, LANES), 0)
    keep8 = (lane8 >= HALF) == (row8 >= GQA_GROUP)
    win_lane = lax.broadcasted_iota(jnp.int32, (D_KV, WINDOW), 1)
    sink = sink_ref[...]
    b0 = step * bt
    sel = (lax.broadcasted_iota(jnp.int32, (nb, bt), 0)
           == b0 + lax.broadcasted_iota(jnp.int32, (nb, bt), 1)).astype(BF16)
    k_cols = sum(_dot(knt_scr[i], sel) for i in range(3))
    v_cols = sum(_dot(vnt_scr[i], sel) for i in range(3))
    for bi in range(bt):
        b = b0 + bi
        q_b = jnp.broadcast_to(q_scr[pl.ds(b, 1), :], (N_Q_HEADS, D_ATTN))
        qm = jnp.zeros((N_Q_HEADS, LANES), F32)
        for h in range(N_Q_HEADS):
            c = q_b[:, LANES * (h // 2):LANES * (h // 2 + 1)]
            if h % 2 != h // GQA_GROUP:
                c = pltpu.roll(c, HALF, 1)
            qm = jnp.where(row8 == h, c, qm)
        qm = jnp.where(keep8, qm, 0.0)
        k_new = kn_scr[pl.ds(b, 1), :]
        v_new = vn_scr[pl.ds(b, 1), :]
        k_old, v_old = ckt_ref[bi], cvt_ref[bi]
        s = _dot(qm.astype(BF16), k_old.astype(BF16))
        s = jnp.where(lane8 >= 1, s, NEG_INF)
        s_new = jnp.sum(qm * k_new, axis=-1, keepdims=True)
        m = jnp.maximum(jnp.maximum(jnp.max(s, axis=-1, keepdims=True), s_new), sink)
        p = jnp.exp(s - m)
        p_new = jnp.exp(s_new - m)
        denom = jnp.sum(p, axis=-1, keepdims=True) + p_new + jnp.exp(sink - m)
        o = (_dot_nt(p.astype(BF16), v_old.astype(BF16)) + p_new * v_new) / denom
        att_scr[pl.ds(pl.multiple_of(b * N_Q_HEADS, N_Q_HEADS), N_Q_HEADS), :] = o
        wkt_ref[bi] = jnp.where(win_lane == WINDOW - 1, k_cols[:, bi:bi + 1], pltpu.roll(k_old, WINDOW - 1, 1))
        wvt_ref[bi] = jnp.where(win_lane == WINDOW - 1, v_cols[:, bi:bi + 1], pltpu.roll(v_old, WINDOW - 1, 1))

    @pl.when(step == last)
    def _():
        y = jnp.zeros((nb, D_MODEL), F32)
        for h in range(N_HGRN_HEADS):
            ks = slice(h * HGRN_D, (h + 1) * HGRN_D)
            o_t = ot_scr[ks, :]
            hn_t = (o_t * lax.rsqrt(jnp.mean(o_t * o_t, axis=0, keepdims=True) + EPS)
                    * goutc_ref[...] * _silu(hgt_scr[ks, :]))
            y = y + _dot_tn(hn_t.astype(BF16), wout_ref[D_ATTN + h * HGRN_D:D_ATTN + (h + 1) * HGRN_D, :])
        for h in range(N_Q_HEADS):
            g = h // GQA_GROUP
            a_h = att_scr[pl.ds(h, nb, stride=N_Q_HEADS), :][:, g * HALF:(g + 1) * HALF]
            y = y + _dot(a_h.astype(BF16), wout_ref[h * HEAD_DIM:(h + 1) * HEAD_DIM, :])
        y_ref[...] = x_ref[...] + _rms(y, gpost_ref[...])


def _sample_mix(x, sinks, g_pre, w_in, lb_raw, g_out_head, w_out, g_post, cache_kt, cache_vt, state_t, pos, layer):
    nb = x.shape[0]
    bt = nb // N_HGRN_HEADS
    cos, s_hi, s_lo = _rope_tables(pos)
    lb_rows = lb_raw.shape[0]
    blk3 = pl.BlockSpec((bt, D_KV, WINDOW), lambda i: (i, 0, 0))
    blk_s = pl.BlockSpec((HGRN_D, HGRN_D, nb), lambda i: (i, 0, 0))
    chan_major = pltpu.VMEM((D_HGRN, nb), F32)
    return pl.pallas_call(
        functools.partial(_sample_mix_kernel, layer=layer, nb=nb),
        grid=(N_HGRN_HEADS,),
        in_specs=[
            _const_spec((N_Q_HEADS, 1)),
            _const_spec((nb, D_MODEL)),
            _const_spec((1, D_MODEL)),
            _const_spec((D_MODEL, IN_COLS)),
            _const_spec((1, LANES)),
            _const_spec((1, LANES)),
            _const_spec((1, LANES)),
            _const_spec((lb_rows, D_HGRN)),
            _const_spec((HGRN_D, 1)),
            _const_spec((D_MODEL, D_MODEL)),
            _const_spec((1, D_MODEL)),
            blk3, blk3, blk_s,
        ],
        out_specs=[pl.BlockSpec((nb, D_MODEL), lambda i: (0, 0)), blk3, blk3, blk_s],
        out_shape=[
            jax.ShapeDtypeStruct((nb, D_MODEL), F32),
            jax.ShapeDtypeStruct(cache_kt.shape, F32),
            jax.ShapeDtypeStruct(cache_vt.shape, F32),
            jax.ShapeDtypeStruct(state_t.shape, F32),
        ],
        scratch_shapes=[
            pltpu.VMEM((nb, D_ATTN), F32),
            pltpu.VMEM((nb, D_KV), F32),
            pltpu.VMEM((nb, D_KV), F32),
            pltpu.VMEM((3, D_KV, nb), BF16),
            pltpu.VMEM((3, D_KV, nb), BF16),
            chan_major, chan_major, chan_major, chan_major, chan_major, chan_major,
            pltpu.VMEM((nb * N_Q_HEADS, LANES), F32),
        ],
        compiler_params=pltpu.CompilerParams(
            dimension_semantics=("arbitrary",), vmem_limit_bytes=VMEM_LIMIT),
        name="sample_mix",
    )(sinks.reshape(N_Q_HEADS, 1), x, g_pre.reshape(1, D_MODEL), w_in, cos, s_hi, s_lo, lb_raw,
      g_out_head.reshape(HGRN_D, 1), w_out, g_post.reshape(1, D_MODEL), cache_kt, cache_vt, state_t)


def kernel(x_prompt, x_sample, cache_win_k, cache_win_v, state_hgrn, ffn1_pre_g, ffn1_post_g, ffn1_w_gu,
           ffn1_w_down, mix_pre_g, mix_post_g, w_in, attn_sinks, hgrn_lb, hgrn_out_g, w_out, ffn2_pre_g,
           ffn2_post_g, ffn2_w_gu, ffn2_w_down):
    batch, seq, _ = x_prompt.shape
    nb, t_s, _ = x_sample.shape
    depth = w_in.shape[0]
    assert t_s == 1 and seq % MIX_TILE == 0 and (batch * seq) % FFN_TILE == 0 and nb == LANES
    assert cache_win_k.shape[2:] == (WINDOW, N_KV_HEADS, HEAD_DIM)

    xp = x_prompt.reshape(batch * seq, D_MODEL)
    xs = x_sample.reshape(nb, D_MODEL)
    pos_s = PAST_LEN + jnp.arange(t_s, dtype=F32)
    outs = [[] for _ in range(6)]
    for l in range(depth):
        w_gu1, w_d1 = ffn1_w_gu[l].astype(BF16), ffn1_w_down[l].astype(BF16)
        w_gu2, w_d2 = ffn2_w_gu[l].astype(BF16), ffn2_w_down[l].astype(BF16)
        w_in_l, w_out_l = w_in[l].astype(BF16), w_out[l].astype(BF16)
        g_out = jnp.tile(hgrn_out_g[l], N_HGRN_HEADS)

        xp, xs = _ffn(xp, xs, ffn1_pre_g[l], ffn1_post_g[l], w_gu1, w_d1)

        xp, wk_p, wv_p, s_p = _prompt_mix(xp, attn_sinks[l], mix_pre_g[l], w_in_l, hgrn_lb, g_out, w_out_l,
                                          mix_post_g[l], batch, seq, l)
        to_kt = lambda c: jnp.transpose(c, (0, 2, 3, 1)).reshape(nb, D_KV, WINDOW)
        from_kt = lambda c: jnp.transpose(c.reshape(nb, N_KV_HEADS, HEAD_DIM, WINDOW), (0, 3, 1, 2))
        state_t = jnp.transpose(state_hgrn[l], (1, 2, 3, 0)).reshape(D_HGRN, HGRN_D, nb)
        xs, wkt_s, wvt_s, st_s = _sample_mix(
            xs, attn_sinks[l], mix_pre_g[l], w_in_l, hgrn_lb, hgrn_out_g[l], w_out_l, mix_post_g[l],
            to_kt(cache_win_k[l]), to_kt(cache_win_v[l]), state_t, pos_s, l)
        s_s = jnp.transpose(st_s.reshape(N_HGRN_HEADS, HGRN_D, HGRN_D, nb), (3, 0, 1, 2))

        xp, xs = _ffn(xp, xs, ffn2_pre_g[l], ffn2_post_g[l], w_gu2, w_d2)

        kv_shape = (WINDOW, N_KV_HEADS, HEAD_DIM)
        for lst, val in zip(outs, (wk_p.reshape(batch, *kv_shape), wv_p.reshape(batch, *kv_shape), s_p,
                                   from_kt(wkt_s), from_kt(wvt_s), s_s)):
            lst.append(val)

    return (xp.reshape(batch, seq, D_MODEL), xs.reshape(nb, t_s, D_MODEL)) + tuple(jnp.stack(o) for o in outs)
```

```python
import functools

import jax
import jax.numpy as jnp
import numpy as np
from jax import lax
from jax.experimental import pallas as pl
from jax.experimental.pallas import tpu as pltpu

F32 = jnp.float32
BF16 = jnp.bfloat16

D_MODEL = 1024
D_FF = 2816
HEAD_DIM = 64
N_Q_HEADS = 8
N_KV_HEADS = 2
GQA_GROUP = N_Q_HEADS // N_KV_HEADS
WINDOW = 128
PAST_LEN = 8192
ROT_DIM = HEAD_DIM // 4
ROPE_THETA = 500000.0
N_HGRN_HEADS = 8
HGRN_D = 64
D_ATTN = N_Q_HEADS * HEAD_DIM
D_KV = N_KV_HEADS * HEAD_DIM
D_HGRN = N_HGRN_HEADS * HGRN_D
IN_COLS = D_ATTN + 2 * D_KV + 4 * D_HGRN
OFF_Q, OFF_K, OFF_V = 0, D_ATTN, D_ATTN + D_KV
OFF_HQ = D_ATTN + 2 * D_KV
OFF_HF, OFF_HI, OFF_HG = OFF_HQ + D_HGRN, OFF_HQ + 2 * D_HGRN, OFF_HQ + 3 * D_HGRN
EPS = 1e-6
NEG_INF = -1e30
LOG2E = 1.4426950408889634
LANES = 128
HALF = LANES // 2

FFN_TILE = 1024
FFN_PARTS = 4
FFN_W_STEPS = 8
MIX_TILE = 256
CHUNK = 64
SUB = 16
N_SUB = CHUNK // SUB
VMEM_LIMIT = 56 * 1024 * 1024


def _rms(x, g):
    return x * lax.rsqrt(jnp.mean(x * x, axis=-1, keepdims=True) + EPS) * g


def _silu(x):
    return x * jax.nn.sigmoid(x)


def _dot(a, b):
    return jnp.dot(a, b, preferred_element_type=F32)


def _dot_nt(a, b):
    return lax.dot_general(a, b, (((1,), (1,)), ((), ())), preferred_element_type=F32)


def _dot_tn(a, b):
    return lax.dot_general(a, b, (((0,), (0,)), ((), ())), preferred_element_type=F32)


def _split3(x):
    hi = x.astype(BF16)
    r = x - hi.astype(F32)
    mid = r.astype(BF16)
    lo = (r - mid.astype(F32)).astype(BF16)
    return hi, mid, lo


def _const_spec(shape):
    nd = len(shape)
    return pl.BlockSpec(shape, lambda *_: (0,) * nd, pipeline_mode=pl.Buffered(1))


def _ffn_kernel(x_ref, xs_ref, gpre_ref, gpost_ref, wgu32_ref, wd32_ref, o_ref, os_ref, wgu_ref, wd_ref):
    step = pl.program_id(0)

    @pl.when(step < FFN_W_STEPS)
    def _():
        gu_rows, d_rows = wgu32_ref.shape[0], wd32_ref.shape[0]
        wgu_ref[pl.ds(pl.multiple_of(step * gu_rows, gu_rows), gu_rows), :] = wgu32_ref[...].astype(BF16)
        wd_ref[pl.ds(pl.multiple_of(step * d_rows, d_rows), d_rows), :] = wd32_ref[...].astype(BF16)

    def half_step(x):
        h = _rms(x, gpre_ref[...]).astype(BF16)
        gate = _dot(h, wgu_ref[:, :D_FF])
        up = _dot(h, wgu_ref[:, D_FF:])
        act = (_silu(gate) * up).astype(BF16)
        y = _dot(act, wd_ref[...])
        return x + 0.5 * _rms(y, gpost_ref[...])

    @pl.when(step >= FFN_W_STEPS)
    def _():
        rows = x_ref.shape[0] // FFN_PARTS
        for r in range(FFN_PARTS):
            rs = slice(r * rows, (r + 1) * rows)
            o_ref[rs, :] = half_step(x_ref[rs, :])

    @pl.when(step == pl.num_programs(0) - 1)
    def _():
        os_ref[...] = half_step(xs_ref[...])


def _ffn(x, xs, g_pre, g_post, w_gu, w_down):
    n, nb = x.shape[0], xs.shape[0]
    tile_map = lambda i: (jnp.maximum(i - FFN_W_STEPS, 0), 0)
    chunk_map = lambda i: (jnp.minimum(i, FFN_W_STEPS - 1), 0)
    return pl.pallas_call(
        _ffn_kernel,
        grid=(FFN_W_STEPS + n // FFN_TILE,),
        in_specs=[
            pl.BlockSpec((FFN_TILE, D_MODEL), tile_map),
            _const_spec((nb, D_MODEL)),
            _const_spec((1, D_MODEL)),
            _const_spec((1, D_MODEL)),
            pl.BlockSpec((D_MODEL // FFN_W_STEPS, 2 * D_FF), chunk_map),
            pl.BlockSpec((D_FF // FFN_W_STEPS, D_MODEL), chunk_map),
        ],
        out_specs=[pl.BlockSpec((FFN_TILE, D_MODEL), tile_map),
                   pl.BlockSpec((nb, D_MODEL), lambda i: (0, 0))],
        out_shape=[jax.ShapeDtypeStruct((n, D_MODEL), F32), jax.ShapeDtypeStruct((nb, D_MODEL), F32)],
        scratch_shapes=[pltpu.VMEM((D_MODEL, 2 * D_FF), BF16), pltpu.VMEM((D_FF, D_MODEL), BF16)],
        compiler_params=pltpu.CompilerParams(
            dimension_semantics=("arbitrary",), vmem_limit_bytes=VMEM_LIMIT),
        name="ffn",
    )(x, xs, g_pre.reshape(1, D_MODEL), g_post.reshape(1, D_MODEL), w_gu, w_down)


def _lower_bound(lb_raw, layer):
    m = jnp.max(lb_raw, axis=0, keepdims=True)
    e = jnp.exp(lb_raw - m)
    return jnp.sum(e[: layer + 1], axis=0, keepdims=True) / jnp.sum(e, axis=0, keepdims=True)


def _rope(x, cos, sin_hi, sin_lo):
    return x * cos + pltpu.roll(x, ROT_DIM // 2, 1) * sin_hi + pltpu.roll(x, LANES - ROT_DIM // 2, 1) * sin_lo


def _rope_tables(pos):
    half = ROT_DIM // 2
    inv = ROPE_THETA ** (-jnp.arange(half, dtype=F32) / half)
    ang = pos[:, None] * inv[None, :]
    cos, sin = jnp.cos(ang), jnp.sin(ang)
    t = pos.shape[0]
    one = jnp.ones((t, HEAD_DIM - ROT_DIM), F32)
    zero = jnp.zeros((t, HEAD_DIM - ROT_DIM), F32)
    zh = jnp.zeros((t, half), F32)
    c = jnp.concatenate([cos, cos, one], axis=1)
    s_hi = jnp.concatenate([zh, sin, zero], axis=1)
    s_lo = jnp.concatenate([-sin, zh, zero], axis=1)
    return tuple(jnp.tile(a, (1, LANES // HEAD_DIM)) for a in (c, s_hi, s_lo))


def _dup_half(x, g, lo_half):
    xr = pltpu.roll(x, HALF, 1)
    return jnp.where(lo_half, x, xr) if g == 0 else jnp.where(lo_half, xr, x)


def _prompt_mix_kernel(sink_ref, x_ref, gpre_ref, win_ref, cos_ref, shi_ref, slo_ref, lb_ref, gout_ref,
                       wout_ref, gpost_ref, bias_ref, amask_ref,
                       y_ref, wk_ref, wv_ref, s_ref,
                       kk_scr, vv_scr, st_scr, g_scr, ho_scr, *, layer):
    tb = MIX_TILE
    n = pl.program_id(1)
    last = pl.num_programs(1) - 1

    @pl.when(n == 0)
    def _():
        kk_scr[...] = jnp.zeros_like(kk_scr)
        vv_scr[...] = jnp.zeros_like(vv_scr)
        st_scr[...] = jnp.zeros_like(st_scr)

    x = x_ref[...]
    xg = (x * gpre_ref[...]).astype(BF16)
    r_in = lax.rsqrt(jnp.mean(x * x, axis=-1, keepdims=True) + EPS)

    def proj(off, width):
        return _dot(xg, win_ref[:, off:off + width]) * r_in

    p_attn = proj(OFF_Q, D_ATTN + 2 * D_KV)
    p_hf = proj(OFF_HF, D_HGRN)
    cos, s_hi, s_lo = cos_ref[...], shi_ref[...], slo_ref[...]
    scale = HEAD_DIM ** -0.5 * LOG2E
    q_cols = [_rope(p_attn[:, LANES * j: LANES * (j + 1)], cos, s_hi, s_lo) * scale
              for j in range(D_ATTN // LANES)]
    k_rot = _rope(p_attn[:, OFF_K:OFF_K + D_KV], cos, s_hi, s_lo)
    v_new = p_attn[:, OFF_V:OFF_V + D_KV]

    @pl.when(n == last)
    def _():
        wk_ref[0] = k_rot[tb - WINDOW:]
        wv_ref[0] = v_new[tb - WINDOW:]

    lane = lax.broadcasted_iota(jnp.int32, (WINDOW, LANES), 1)
    lo_half = lane < HALF
    bias = bias_ref[...]
    no_prev = jnp.where(n > 0, 0.0, NEG_INF)
    bias_first = jnp.concatenate([bias[:WINDOW] + no_prev, bias[WINDOW:]], axis=0)

    k_prev = [kk_scr[g] for g in range(N_KV_HEADS)]
    v_prev = [vv_scr[g] for g in range(N_KV_HEADS)]
    scores = []
    for i in range(tb // WINDOW):
        r0 = i * WINDOW
        kc, vc = k_rot[r0:r0 + WINDOW], v_new[r0:r0 + WINDOW]
        k_cur = [_dup_half(kc, g, lo_half).astype(BF16) for g in range(N_KV_HEADS)]
        v_cur = [_dup_half(vc, g, lo_half).astype(BF16) for g in range(N_KV_HEADS)]
        for g in range(N_KV_HEADS):
            keys = jnp.concatenate([k_prev[g], k_cur[g]], axis=0)
            vals = jnp.concatenate([v_prev[g], v_cur[g]], axis=0)
            heads = range(g * GQA_GROUP, (g + 1) * GQA_GROUP)
            qg = jnp.concatenate(
                [jnp.where(lo_half if h % 2 == 0 else ~lo_half, q_cols[h // 2][r0:r0 + WINDOW], 0.0)
                 for h in heads], axis=0).astype(BF16)
            s = _dot_nt(keys, qg) + (bias_first if i == 0 else bias)
            scores.append((s, vals, heads))
        k_prev, v_prev = k_cur, v_cur
    for g in range(N_KV_HEADS):
        kk_scr[g] = k_prev[g]
        vv_scr[g] = v_prev[g]

    hq = proj(OFF_HQ, D_HGRN) * (HGRN_D ** -0.5)
    hv_bf = proj(OFF_HI, D_HGRN).astype(BF16)
    hg_act = _silu(proj(OFF_HG, D_HGRN))

    att_cols = []
    for s, vals, heads in scores:
        sink = jnp.concatenate(
            [jnp.full((1, WINDOW), sink_ref[h] * LOG2E, F32) for h in heads], axis=1)
        m = jnp.maximum(jnp.max(s, axis=0, keepdims=True), sink)
        p = jnp.exp2(s - m)
        denom = jnp.sum(p, axis=0, keepdims=True) + jnp.exp2(sink - m)
        p = (p * (1.0 / denom)).astype(BF16)
        o = _dot_tn(p, vals)
        for jj in range(GQA_GROUP // 2):
            o_even = o[(2 * jj) * WINDOW:(2 * jj + 1) * WINDOW]
            o_odd = o[(2 * jj + 1) * WINDOW:(2 * jj + 2) * WINDOW]
            att_cols.append(jnp.where(lo_half, o_even, o_odd))
    n_cols = D_ATTN // LANES
    att = jnp.concatenate(
        [jnp.concatenate(att_cols[i * n_cols:(i + 1) * n_cols], axis=1) for i in range(tb // WINDOW)], axis=0)
    y_att = _dot(att.astype(BF16), wout_ref[:D_ATTN, :])

    lb = _lower_bound(lb_ref[...], layer)
    f = lb + (1.0 - lb) * jax.nn.sigmoid(p_hf)
    logf = jnp.log(f)
    hk = 1.0 - f

    tr = lax.broadcasted_iota(jnp.int32, (tb, tb), 0)
    tc = lax.broadcasted_iota(jnp.int32, (tb, tb), 1)
    tri = ((tr // CHUNK == tc // CHUNK) & (tc <= tr)).astype(BF16)
    g_cum = sum(_dot(tri, part) for part in _split3(logf))
    g_scr[...] = g_cum

    def bcast_row(r, rows):
        return jnp.broadcast_to(g_scr[r:r + 1, :], (rows, D_HGRN))

    zeros_sub = jnp.zeros((SUB, D_HGRN), F32)
    g_ref_q = jnp.concatenate(
        [zeros_sub if sb % N_SUB == 0 else bcast_row(sb * SUB - 1, SUB) for sb in range(tb // SUB)], axis=0)
    q_loc = (hq * jnp.exp(g_cum - g_ref_q)).astype(BF16)
    g_end = jnp.concatenate(
        [bcast_row(c * CHUNK + CHUNK - 1, CHUNK) for c in range(tb // CHUNK)], axis=0)
    k_sub = []
    for i in range(N_SUB):
        live = (i + 1) * SUB
        pieces = []
        for c in range(tb // CHUNK):
            r0 = c * CHUNK
            g_ref_i = 0.0 if i == 0 else bcast_row(r0 + i * SUB - 1, live)
            pieces.append(hk[r0:r0 + live] * jnp.exp(g_ref_i - g_cum[r0:r0 + live]))
            if live < CHUNK:
                pieces.append(jnp.zeros((CHUNK - live, D_HGRN), F32))
        k_sub.append(jnp.concatenate(pieces, axis=0).astype(BF16))

    n_ch = tb // CHUNK
    gc = [g_scr[c * CHUNK + CHUNK - 1:c * CHUNK + CHUNK, :] for c in range(n_ch)]

    def span(lo, hi):
        if hi <= lo:
            return jnp.ones((CHUNK, D_HGRN), F32)
        return jnp.broadcast_to(jnp.exp(sum(gc[lo:hi])), (CHUNK, D_HGRN))

    zeros_chunk = jnp.zeros((CHUNK, D_HGRN), F32)
    k_end32 = hk * jnp.exp(g_end - g_cum)
    q_glob32 = hq * jnp.exp(g_cum)
    k_cross = [(k_end32 * jnp.concatenate(
        [span(cp + 1, c) if cp < c else zeros_chunk for cp in range(n_ch)], axis=0)).astype(BF16)
        for c in range(1, n_ch)]
    q_tile = (q_glob32 * jnp.concatenate([span(0, c) for c in range(n_ch)], axis=0)).astype(BF16)
    k_tile_end = (k_end32 * jnp.concatenate([span(cp + 1, n_ch) for cp in range(n_ch)], axis=0)).astype(BF16)
    decay_tile = jnp.exp(sum(gc))
    q_glob = q_glob32.astype(BF16)

    def keep_rows(a, block, wanted):
        zero = jnp.zeros((block, a.shape[1]), a.dtype)
        return jnp.concatenate(
            [a[r * block:(r + 1) * block] if wanted(r) else zero for r in range(a.shape[0] // block)], axis=0)

    q_sub = [keep_rows(q_loc, SUB, lambda r, i=i: r % N_SUB == i) for i in range(N_SUB)]
    q_cross = [keep_rows(q_glob, CHUNK, lambda r, c=c: r == c) for c in range(1, n_ch)]
    amask = amask_ref[...]
    y = y_att
    out_group = 2 * LANES // HGRN_D
    for h in range(N_HGRN_HEADS):
        ls = slice(h * HGRN_D, (h + 1) * HGRN_D)
        q_stack = jnp.concatenate([q_sub[i][:, ls] for i in range(N_SUB)], axis=1)
        k_stack = jnp.concatenate([k_sub[i][:, ls] for i in range(N_SUB)], axis=1)
        qc_stack = jnp.concatenate([q_cross[c - 1][:, ls] for c in range(1, n_ch)], axis=1)
        kc_stack = jnp.concatenate([k_cross[c - 1][:, ls] for c in range(1, n_ch)], axis=1)
        a = (_dot_nt(q_stack, k_stack) * amask + _dot_nt(qc_stack, kc_stack)).astype(BF16)
        v_h = hv_bf[:, ls]
        state = st_scr[h]
        o = _dot(a, v_h) + _dot_nt(q_tile[:, ls], state.astype(BF16))
        st_scr[h] = state * decay_tile[:, ls] + _dot_tn(v_h, k_tile_end[:, ls])
        o = o * lax.rsqrt(jnp.mean(o * o, axis=-1, keepdims=True) + EPS)
        ho_scr[:, ls] = o
        if (h + 1) % out_group == 0:
            gs = slice((h + 1 - out_group) * HGRN_D, (h + 1) * HGRN_D)
            hn = ho_scr[:, gs] * gout_ref[:, gs] * hg_act[:, gs]
            y = y + _dot(hn.astype(BF16), wout_ref[D_ATTN + gs.start:D_ATTN + gs.stop, :])

    @pl.when(n == last)
    def _():
        for h in range(N_HGRN_HEADS):
            s_ref[0, h] = st_scr[h].T

    y_ref[...] = x + _rms(y, gpost_ref[...])


def _prompt_mix(x, sinks, g_pre, w_in, lb_raw, g_out, w_out, g_post, batch, seq, layer):
    tb = MIX_TILE
    nt = seq // tb
    cos, s_hi, s_lo = _rope_tables(jnp.arange(seq, dtype=F32))
    tok = lambda b, n: (b * nt + n, 0)
    tab = lambda b, n: (n, 0)
    per_b3 = lambda b, n: (b, 0, 0)
    lb_rows = lb_raw.shape[0]
    key_i = jnp.arange(2 * WINDOW)[:, None]
    rel = jnp.arange(WINDOW)[None, :] + WINDOW - key_i
    bias = jnp.tile(jnp.where((rel >= 0) & (rel < WINDOW), 0.0, NEG_INF).astype(F32), (1, GQA_GROUP))
    t_i = jnp.arange(tb)
    amask = ((t_i[:, None] // CHUNK == t_i[None, :] // CHUNK) & (t_i[None, :] <= t_i[:, None])).astype(F32)
    return pl.pallas_call(
        functools.partial(_prompt_mix_kernel, layer=layer),
        grid=(batch, nt),
        in_specs=[
            pl.BlockSpec(memory_space=pltpu.SMEM),
            pl.BlockSpec((tb, D_MODEL), tok),
            _const_spec((1, D_MODEL)),
            _const_spec((D_MODEL, IN_COLS)),
            pl.BlockSpec((tb, LANES), tab),
            pl.BlockSpec((tb, LANES), tab),
            pl.BlockSpec((tb, LANES), tab),
            _const_spec((lb_rows, D_HGRN)),
            _const_spec((1, D_HGRN)),
            _const_spec((D_MODEL, D_MODEL)),
            _const_spec((1, D_MODEL)),
            _const_spec((2 * WINDOW, GQA_GROUP * WINDOW)),
            _const_spec((tb, tb)),
        ],
        out_specs=[
            pl.BlockSpec((tb, D_MODEL), tok),
            pl.BlockSpec((1, WINDOW, D_KV), per_b3),
            pl.BlockSpec((1, WINDOW, D_KV), per_b3),
            pl.BlockSpec((1, N_HGRN_HEADS, HGRN_D, HGRN_D), lambda b, n: (b, 0, 0, 0)),
        ],
        out_shape=[
            jax.ShapeDtypeStruct((batch * seq, D_MODEL), F32),
            jax.ShapeDtypeStruct((batch, WINDOW, D_KV), F32),
            jax.ShapeDtypeStruct((batch, WINDOW, D_KV), F32),
            jax.ShapeDtypeStruct((batch, N_HGRN_HEADS, HGRN_D, HGRN_D), F32),
        ],
        scratch_shapes=[
            pltpu.VMEM((N_KV_HEADS, WINDOW, LANES), BF16),
            pltpu.VMEM((N_KV_HEADS, WINDOW, LANES), BF16),
            pltpu.VMEM((N_HGRN_HEADS, HGRN_D, HGRN_D), F32),
            pltpu.VMEM((tb, D_HGRN), F32),
            pltpu.VMEM((tb, D_HGRN), F32),
        ],
        compiler_params=pltpu.CompilerParams(
            dimension_semantics=("arbitrary", "arbitrary"), vmem_limit_bytes=VMEM_LIMIT),
        name="prompt_mix",
    )(sinks, x, g_pre.reshape(1, D_MODEL), w_in, cos, s_hi, s_lo, lb_raw, g_out.reshape(1, D_HGRN),
      w_out, g_post.reshape(1, D_MODEL), bias, amask)


def _sample_mix_kernel(sink_ref, x_ref, gpre_ref, win_ref, cos_ref, shi_ref, slo_ref, lb_ref, goutc_ref,
                       wout_ref, gpost_ref, ckt_ref, cvt_ref, sin_ref,
                       y_ref, wkt_ref, wvt_ref, sout_ref,
                       q_scr, kn_scr, vn_scr, knt_scr, vnt_scr, ft_scr, hkt_scr, hqt_scr, hvt_scr, hgt_scr,
                       ot_scr, att_scr, *, layer, nb):
    step = pl.program_id(0)
    last = pl.num_programs(0) - 1
    bt = nb // N_HGRN_HEADS

    @pl.when(step == 0)
    def _():
        u = _rms(x_ref[...], gpre_ref[...]).astype(BF16)
        proj = _dot(u, win_ref[...])
        cos, s_hi, s_lo = cos_ref[...], shi_ref[...], slo_ref[...]
        scale = HEAD_DIM ** -0.5
        for j in range(D_ATTN // LANES):
            q_scr[:, LANES * j:LANES * (j + 1)] = _rope(
                proj[:, OFF_Q + LANES * j:OFF_Q + LANES * (j + 1)], cos, s_hi, s_lo) * scale
        k_new = _rope(proj[:, OFF_K:OFF_K + D_KV], cos, s_hi, s_lo)
        v_new = proj[:, OFF_V:OFF_V + D_KV]
        kn_scr[...] = k_new
        vn_scr[...] = v_new
        for scr, val in ((knt_scr, k_new), (vnt_scr, v_new)):
            for i, part in enumerate(_split3(val.T)):
                scr[i] = part
        lb = _lower_bound(lb_ref[...], layer)
        f_t = (lb + (1.0 - lb) * jax.nn.sigmoid(proj[:, OFF_HF:OFF_HF + D_HGRN])).T
        ft_scr[...] = f_t
        hkt_scr[...] = 1.0 - f_t
        hqt_scr[...] = (proj[:, OFF_HQ:OFF_HQ + D_HGRN] * (HGRN_D ** -0.5)).T
        hvt_scr[...] = proj[:, OFF_HI:OFF_HI + D_HGRN].T
        hgt_scr[...] = proj[:, OFF_HG:OFF_HG + D_HGRN].T

    base = pl.multiple_of(step * HGRN_D, HGRN_D)
    hv_t = hvt_scr[pl.ds(base, HGRN_D), :]

    def hgrn_row(k, o_acc):
        f_row = ft_scr[pl.ds(base + k, 1), :]
        s_new = f_row * sin_ref[k] + hkt_scr[pl.ds(base + k, 1), :] * hv_t
        sout_ref[k] = s_new
        return o_acc + hqt_scr[pl.ds(base + k, 1), :] * s_new

    ot_scr[pl.ds(base, HGRN_D), :] = lax.fori_loop(
        0, HGRN_D, hgrn_row, jnp.zeros((HGRN_D, nb), F32), unroll=8)

    lane8 = lax.broadcasted_iota(jnp.int32, (N_Q_HEADS, LANES), 1)
    row8 = lax.broadcasted_iota(jnp.int32, (N_Q_HEADS, LANES), 0)
    keep8 = (lane8 >= HALF) == (row8 >= GQA_GROUP)
    win_lane = lax.broadcasted_iota(jnp.int32, (D_KV, WINDOW), 1)
    sink = sink_ref[...]
    b0 = step * bt
    sel = (lax.broadcasted_iota(jnp.int32, (nb, bt), 0)
           == b0 + lax.broadcasted_iota(jnp.int32, (nb, bt), 1)).astype(BF16)
    k_cols = sum(_dot(knt_scr[i], sel) for i in range(3))
    v_cols = sum(_dot(vnt_scr[i], sel) for i in range(3))
    for bi in range(bt):
        b = b0 + bi
        q_b = jnp.broadcast_to(q_scr[pl.ds(b, 1), :], (N_Q_HEADS, D_ATTN))
        qm = jnp.zeros((N_Q_HEADS, LANES), F32)
        for h in range(N_Q_HEADS):
            c = q_b[:, LANES * (h // 2):LANES * (h // 2 + 1)]
            if h % 2 != h // GQA_GROUP:
                c = pltpu.roll(c, HALF, 1)
            qm = jnp.where(row8 == h, c, qm)
        qm = jnp.where(keep8, qm, 0.0)
        k_new = kn_scr[pl.ds(b, 1), :]
        v_new = vn_scr[pl.ds(b, 1), :]
        k_old, v_old = ckt_ref[bi], cvt_ref[bi]
        s = _dot(qm.astype(BF16), k_old.astype(BF16))
        s = jnp.where(lane8 >= 1, s, NEG_INF)
        s_new = jnp.sum(qm * k_new, axis=-1, keepdims=True)
        m = jnp.maximum(jnp.maximum(jnp.max(s, axis=-1, keepdims=True), s_new), sink)
        p = jnp.exp(s - m)
        p_new = jnp.exp(s_new - m)
        denom = jnp.sum(p, axis=-1, keepdims=True) + p_new + jnp.exp(sink - m)
        o = (_dot_nt(p.astype(BF16), v_old.astype(BF16)) + p_new * v_new) / denom
        att_scr[pl.ds(pl.multiple_of(b * N_Q_HEADS, N_Q_HEADS), N_Q_HEADS), :] = o
        wkt_ref[bi] = jnp.where(win_lane == WINDOW - 1, k_cols[:, bi:bi + 1], pltpu.roll(k_old, WINDOW - 1, 1))
        wvt_ref[bi] = jnp.where(win_lane == WINDOW - 1, v_cols[:, bi:bi + 1], pltpu.roll(v_old, WINDOW - 1, 1))

    @pl.when(step == last)
    def _():
        y = jnp.zeros((nb, D_MODEL), F32)
        for h in range(N_HGRN_HEADS):
            ks = slice(h * HGRN_D, (h + 1) * HGRN_D)
            o_t = ot_scr[ks, :]
            hn_t = (o_t * lax.rsqrt(jnp.mean(o_t * o_t, axis=0, keepdims=True) + EPS)
                    * goutc_ref[...] * _silu(hgt_scr[ks, :]))
            y = y + _dot_tn(hn_t.astype(BF16), wout_ref[D_ATTN + h * HGRN_D:D_ATTN + (h + 1) * HGRN_D, :])
        for h in range(N_Q_HEADS):
            g = h // GQA_GROUP
            a_h = att_scr[pl.ds(h, nb, stride=N_Q_HEADS), :][:, g * HALF:(g + 1) * HALF]
            y = y + _dot(a_h.astype(BF16), wout_ref[h * HEAD_DIM:(h + 1) * HEAD_DIM, :])
        y_ref[...] = x_ref[...] + _rms(y, gpost_ref[...])


def _sample_mix(x, sinks, g_pre, w_in, lb_raw, g_out_head, w_out, g_post, cache_kt, cache_vt, state_t, pos, layer):
    nb = x.shape[0]
    bt = nb // N_HGRN_HEADS
    cos, s_hi, s_lo = _rope_tables(pos)
    lb_rows = lb_raw.shape[0]
    blk3 = pl.BlockSpec((bt, D_KV, WINDOW), lambda i: (i, 0, 0))
    blk_s = pl.BlockSpec((HGRN_D, HGRN_D, nb), lambda i: (i, 0, 0))
    chan_major = pltpu.VMEM((D_HGRN, nb), F32)
    return pl.pallas_call(
        functools.partial(_sample_mix_kernel, layer=layer, nb=nb),
        grid=(N_HGRN_HEADS,),
        in_specs=[
            _const_spec((N_Q_HEADS, 1)),
            _const_spec((nb, D_MODEL)),
            _const_spec((1, D_MODEL)),
            _const_spec((D_MODEL, IN_COLS)),
            _const_spec((1, LANES)),
            _const_spec((1, LANES)),
            _const_spec((1, LANES)),
            _const_spec((lb_rows, D_HGRN)),
            _const_spec((HGRN_D, 1)),
            _const_spec((D_MODEL, D_MODEL)),
            _const_spec((1, D_MODEL)),
            blk3, blk3, blk_s,
        ],
        out_specs=[pl.BlockSpec((nb, D_MODEL), lambda i: (0, 0)), blk3, blk3, blk_s],
        out_shape=[
            jax.ShapeDtypeStruct((nb, D_MODEL), F32),
            jax.ShapeDtypeStruct(cache_kt.shape, F32),
            jax.ShapeDtypeStruct(cache_vt.shape, F32),
            jax.ShapeDtypeStruct(state_t.shape, F32),
        ],
        scratch_shapes=[
            pltpu.VMEM((nb, D_ATTN), F32),
            pltpu.VMEM((nb, D_KV), F32),
            pltpu.VMEM((nb, D_KV), F32),
            pltpu.VMEM((3, D_KV, nb), BF16),
            pltpu.VMEM((3, D_KV, nb), BF16),
            chan_major, chan_major, chan_major, chan_major, chan_major, chan_major,
            pltpu.VMEM((nb * N_Q_HEADS, LANES), F32),
        ],
        compiler_params=pltpu.CompilerParams(
            dimension_semantics=("arbitrary",), vmem_limit_bytes=VMEM_LIMIT),
        name="sample_mix",
    )(sinks.reshape(N_Q_HEADS, 1), x, g_pre.reshape(1, D_MODEL), w_in, cos, s_hi, s_lo, lb_raw,
      g_out_head.reshape(HGRN_D, 1), w_out, g_post.reshape(1, D_MODEL), cache_kt, cache_vt, state_t)


def kernel(x_prompt, x_sample, cache_win_k, cache_win_v, state_hgrn, ffn1_pre_g, ffn1_post_g, ffn1_w_gu,
           ffn1_w_down, mix_pre_g, mix_post_g, w_in, attn_sinks, hgrn_lb, hgrn_out_g, w_out, ffn2_pre_g,
           ffn2_post_g, ffn2_w_gu, ffn2_w_down):
    batch, seq, _ = x_prompt.shape
    nb, t_s, _ = x_sample.shape
    depth = w_in.shape[0]
    assert t_s == 1 and seq % MIX_TILE == 0 and (batch * seq) % FFN_TILE == 0 and nb == LANES
    assert cache_win_k.shape[2:] == (WINDOW, N_KV_HEADS, HEAD_DIM)

    xp = x_prompt.reshape(batch * seq, D_MODEL)
    xs = x_sample.reshape(nb, D_MODEL)
    pos_s = PAST_LEN + jnp.arange(t_s, dtype=F32)
    outs = [[] for _ in range(6)]
    for l in range(depth):
        w_in_l, w_out_l = w_in[l].astype(BF16), w_out[l].astype(BF16)
        g_out = jnp.tile(hgrn_out_g[l], N_HGRN_HEADS)

        xp, xs = _ffn(xp, xs, ffn1_pre_g[l], ffn1_post_g[l], ffn1_w_gu[l], ffn1_w_down[l])

        xp, wk_p, wv_p, s_p = _prompt_mix(xp, attn_sinks[l], mix_pre_g[l], w_in_l, hgrn_lb, g_out, w_out_l,
                                          mix_post_g[l], batch, seq, l)
        to_kt = lambda c: jnp.transpose(c, (0, 2, 3, 1)).reshape(nb, D_KV, WINDOW)
        from_kt = lambda c: jnp.transpose(c.reshape(nb, N_KV_HEADS, HEAD_DIM, WINDOW), (0, 3, 1, 2))
        state_t = jnp.transpose(state_hgrn[l], (1, 2, 3, 0)).reshape(D_HGRN, HGRN_D, nb)
        xs, wkt_s, wvt_s, st_s = _sample_mix(
            xs, attn_sinks[l], mix_pre_g[l], w_in_l, hgrn_lb, hgrn_out_g[l], w_out_l, mix_post_g[l],
            to_kt(cache_win_k[l]), to_kt(cache_win_v[l]), state_t, pos_s, l)
        s_s = jnp.transpose(st_s.reshape(N_HGRN_HEADS, HGRN_D, HGRN_D, nb), (3, 0, 1, 2))

        xp, xs = _ffn(xp, xs, ffn2_pre_g[l], ffn2_post_g[l], ffn2_w_gu[l], ffn2_w_down[l])

        kv_shape = (WINDOW, N_KV_HEADS, HEAD_DIM)
        for lst, val in zip(outs, (wk_p.reshape(batch, *kv_shape), wv_p.reshape(batch, *kv_shape), s_p,
                                   from_kt(wkt_s), from_kt(wvt_s), s_s)):
            lst.append(val)

    return (xp.reshape(batch, seq, D_MODEL), xs.reshape(nb, t_s, D_MODEL)) + tuple(jnp.stack(o) for o in outs)
```

```python
import functools

import jax
import jax.numpy as jnp
import numpy as np
from jax import lax
from jax.experimental import pallas as pl
from jax.experimental.pallas import tpu as pltpu

F32 = jnp.float32
BF16 = jnp.bfloat16

D_MODEL = 1024
D_FF = 2816
HEAD_DIM = 64
N_Q_HEADS = 8
N_KV_HEADS = 2
GQA_GROUP = N_Q_HEADS // N_KV_HEADS
WINDOW = 128
PAST_LEN = 8192
ROT_DIM = HEAD_DIM // 4
ROPE_THETA = 500000.0
N_HGRN_HEADS = 8
HGRN_D = 64
D_ATTN = N_Q_HEADS * HEAD_DIM
D_KV = N_KV_HEADS * HEAD_DIM
D_HGRN = N_HGRN_HEADS * HGRN_D
IN_COLS = D_ATTN + 2 * D_KV + 4 * D_HGRN
OFF_Q, OFF_K, OFF_V = 0, D_ATTN, D_ATTN + D_KV
OFF_HQ = D_ATTN + 2 * D_KV
OFF_HF, OFF_HI, OFF_HG = OFF_HQ + D_HGRN, OFF_HQ + 2 * D_HGRN, OFF_HQ + 3 * D_HGRN
EPS = 1e-6
NEG_INF = -1e30
LOG2E = 1.4426950408889634
LANES = 128
HALF = LANES // 2

FFN_TILE = 1024
FFN_PARTS = 4
FFN_W_STEPS = 8
MIX_TILE = 256
MIX_PARTS = 2
CHUNK = 64
SUB = 16
N_SUB = CHUNK // SUB
VMEM_LIMIT = 56 * 1024 * 1024


def _rms(x, g):
    return x * lax.rsqrt(jnp.mean(x * x, axis=-1, keepdims=True) + EPS) * g


def _silu(x):
    return x * jax.nn.sigmoid(x)


def _dot(a, b):
    return jnp.dot(a, b, preferred_element_type=F32)


def _dot_nt(a, b):
    return lax.dot_general(a, b, (((1,), (1,)), ((), ())), preferred_element_type=F32)


def _dot_tn(a, b):
    return lax.dot_general(a, b, (((0,), (0,)), ((), ())), preferred_element_type=F32)


def _split3(x):
    hi = x.astype(BF16)
    r = x - hi.astype(F32)
    mid = r.astype(BF16)
    lo = (r - mid.astype(F32)).astype(BF16)
    return hi, mid, lo


def _const_spec(shape):
    nd = len(shape)
    return pl.BlockSpec(shape, lambda *_: (0,) * nd, pipeline_mode=pl.Buffered(1))


def _ffn_kernel(x_ref, xs_ref, gpre_ref, gpost_ref, wgu32_ref, wd32_ref, o_ref, os_ref, wgu_ref, wd_ref):
    step = pl.program_id(0)

    @pl.when(step < FFN_W_STEPS)
    def _():
        gu_rows, d_rows = wgu32_ref.shape[0], wd32_ref.shape[0]
        wgu_ref[pl.ds(pl.multiple_of(step * gu_rows, gu_rows), gu_rows), :] = wgu32_ref[...].astype(BF16)
        wd_ref[pl.ds(pl.multiple_of(step * d_rows, d_rows), d_rows), :] = wd32_ref[...].astype(BF16)

    def half_step(x):
        h = _rms(x, gpre_ref[...]).astype(BF16)
        gate = _dot(h, wgu_ref[:, :D_FF])
        up = _dot(h, wgu_ref[:, D_FF:])
        act = (_silu(gate) * up).astype(BF16)
        y = _dot(act, wd_ref[...])
        return x + 0.5 * _rms(y, gpost_ref[...])

    @pl.when(step >= FFN_W_STEPS)
    def _():
        rows = x_ref.shape[0] // FFN_PARTS
        for r in range(FFN_PARTS):
            rs = slice(r * rows, (r + 1) * rows)
            o_ref[rs, :] = half_step(x_ref[rs, :])

    @pl.when(step == pl.num_programs(0) - 1)
    def _():
        os_ref[...] = half_step(xs_ref[...])


def _ffn(x, xs, g_pre, g_post, w_gu, w_down):
    n, nb = x.shape[0], xs.shape[0]
    tile_map = lambda i: (jnp.maximum(i - FFN_W_STEPS, 0), 0)
    chunk_map = lambda i: (jnp.minimum(i, FFN_W_STEPS - 1), 0)
    return pl.pallas_call(
        _ffn_kernel,
        grid=(FFN_W_STEPS + n // FFN_TILE,),
        in_specs=[
            pl.BlockSpec((FFN_TILE, D_MODEL), tile_map),
            _const_spec((nb, D_MODEL)),
            _const_spec((1, D_MODEL)),
            _const_spec((1, D_MODEL)),
            pl.BlockSpec((D_MODEL // FFN_W_STEPS, 2 * D_FF), chunk_map),
            pl.BlockSpec((D_FF // FFN_W_STEPS, D_MODEL), chunk_map),
        ],
        out_specs=[pl.BlockSpec((FFN_TILE, D_MODEL), tile_map),
                   pl.BlockSpec((nb, D_MODEL), lambda i: (0, 0))],
        out_shape=[jax.ShapeDtypeStruct((n, D_MODEL), F32), jax.ShapeDtypeStruct((nb, D_MODEL), F32)],
        scratch_shapes=[pltpu.VMEM((D_MODEL, 2 * D_FF), BF16), pltpu.VMEM((D_FF, D_MODEL), BF16)],
        compiler_params=pltpu.CompilerParams(
            dimension_semantics=("arbitrary",), vmem_limit_bytes=VMEM_LIMIT),
        name="ffn",
    )(x, xs, g_pre.reshape(1, D_MODEL), g_post.reshape(1, D_MODEL), w_gu, w_down)


def _lower_bound(lb_raw, layer):
    m = jnp.max(lb_raw, axis=0, keepdims=True)
    e = jnp.exp(lb_raw - m)
    return jnp.sum(e[: layer + 1], axis=0, keepdims=True) / jnp.sum(e, axis=0, keepdims=True)


def _rope(x, cos, sin_hi, sin_lo):
    return x * cos + pltpu.roll(x, ROT_DIM // 2, 1) * sin_hi + pltpu.roll(x, LANES - ROT_DIM // 2, 1) * sin_lo


def _rope_tables(pos):
    half = ROT_DIM // 2
    inv = ROPE_THETA ** (-jnp.arange(half, dtype=F32) / half)
    ang = pos[:, None] * inv[None, :]
    cos, sin = jnp.cos(ang), jnp.sin(ang)
    t = pos.shape[0]
    one = jnp.ones((t, HEAD_DIM - ROT_DIM), F32)
    zero = jnp.zeros((t, HEAD_DIM - ROT_DIM), F32)
    zh = jnp.zeros((t, half), F32)
    c = jnp.concatenate([cos, cos, one], axis=1)
    s_hi = jnp.concatenate([zh, sin, zero], axis=1)
    s_lo = jnp.concatenate([-sin, zh, zero], axis=1)
    return tuple(jnp.tile(a, (1, LANES // HEAD_DIM)) for a in (c, s_hi, s_lo))


def _dup_half(x, g, lo_half):
    xr = pltpu.roll(x, HALF, 1)
    return jnp.where(lo_half, x, xr) if g == 0 else jnp.where(lo_half, xr, x)


def _mix_tile(x, rope_tab, seq_start, k_prev, v_prev, state, g_scr, ho_scr,
              sink_ref, gpre_ref, win_ref, lb_ref, gout_ref, wout_ref, gpost_ref, bias_ref, amask_ref, layer):
    tb = MIX_TILE
    u = _rms(x, gpre_ref[...]).astype(BF16)

    def proj(off, width):
        return _dot(u, win_ref[:, off:off + width])

    p_attn = proj(OFF_Q, D_ATTN + 2 * D_KV)
    p_hf = proj(OFF_HF, D_HGRN)
    cos, s_hi, s_lo = rope_tab
    scale = HEAD_DIM ** -0.5 * LOG2E
    q_cols = [_rope(p_attn[:, LANES * j: LANES * (j + 1)], cos, s_hi, s_lo) * scale
              for j in range(D_ATTN // LANES)]
    k_rot = _rope(p_attn[:, OFF_K:OFF_K + D_KV], cos, s_hi, s_lo)
    v_new = p_attn[:, OFF_V:OFF_V + D_KV]

    lane = lax.broadcasted_iota(jnp.int32, (WINDOW, LANES), 1)
    lo_half = lane < HALF
    bias = bias_ref[...]
    if seq_start is None:
        bias_first = bias
    else:
        no_prev = jnp.where(seq_start, NEG_INF, 0.0)
        bias_first = jnp.concatenate([bias[:WINDOW] + no_prev, bias[WINDOW:]], axis=0)

    scores = []
    for i in range(tb // WINDOW):
        r0 = i * WINDOW
        kc, vc = k_rot[r0:r0 + WINDOW], v_new[r0:r0 + WINDOW]
        k_cur = [_dup_half(kc, g, lo_half).astype(BF16) for g in range(N_KV_HEADS)]
        v_cur = [_dup_half(vc, g, lo_half).astype(BF16) for g in range(N_KV_HEADS)]
        for g in range(N_KV_HEADS):
            keys = jnp.concatenate([k_prev[g], k_cur[g]], axis=0)
            vals = jnp.concatenate([v_prev[g], v_cur[g]], axis=0)
            heads = range(g * GQA_GROUP, (g + 1) * GQA_GROUP)
            qg = jnp.concatenate(
                [jnp.where(lo_half if h % 2 == 0 else ~lo_half, q_cols[h // 2][r0:r0 + WINDOW], 0.0)
                 for h in heads], axis=0).astype(BF16)
            s = _dot_nt(keys, qg) + (bias_first if i == 0 else bias)
            scores.append((s, vals, heads))
        k_prev, v_prev = k_cur, v_cur

    hq = proj(OFF_HQ, D_HGRN) * (HGRN_D ** -0.5)
    hv_bf = proj(OFF_HI, D_HGRN).astype(BF16)
    hg_act = _silu(proj(OFF_HG, D_HGRN))

    att_cols = []
    for s, vals, heads in scores:
        sink = jnp.concatenate(
            [jnp.full((1, WINDOW), sink_ref[h] * LOG2E, F32) for h in heads], axis=1)
        m = jnp.maximum(jnp.max(s, axis=0, keepdims=True), sink)
        p = jnp.exp2(s - m)
        denom = jnp.sum(p, axis=0, keepdims=True) + jnp.exp2(sink - m)
        p = (p * (1.0 / denom)).astype(BF16)
        o = _dot_tn(p, vals)
        for jj in range(GQA_GROUP // 2):
            o_even = o[(2 * jj) * WINDOW:(2 * jj + 1) * WINDOW]
            o_odd = o[(2 * jj + 1) * WINDOW:(2 * jj + 2) * WINDOW]
            att_cols.append(jnp.where(lo_half, o_even, o_odd))
    n_cols = D_ATTN // LANES
    att = jnp.concatenate(
        [jnp.concatenate(att_cols[i * n_cols:(i + 1) * n_cols], axis=1) for i in range(tb // WINDOW)], axis=0)
    y_att = _dot(att.astype(BF16), wout_ref[:D_ATTN, :])

    lb = _lower_bound(lb_ref[...], layer)
    f = lb + (1.0 - lb) * jax.nn.sigmoid(p_hf)
    logf = jnp.log(f)
    hk = 1.0 - f

    tr = lax.broadcasted_iota(jnp.int32, (tb, tb), 0)
    tc = lax.broadcasted_iota(jnp.int32, (tb, tb), 1)
    tri = ((tr // CHUNK == tc // CHUNK) & (tc <= tr)).astype(BF16)
    g_cum = sum(_dot(tri, part) for part in _split3(logf))
    g_scr[...] = g_cum

    def bcast_row(r, rows):
        return jnp.broadcast_to(g_scr[r:r + 1, :], (rows, D_HGRN))

    zeros_sub = jnp.zeros((SUB, D_HGRN), F32)
    g_ref_q = jnp.concatenate(
        [zeros_sub if sb % N_SUB == 0 else bcast_row(sb * SUB - 1, SUB) for sb in range(tb // SUB)], axis=0)
    q_loc = (hq * jnp.exp(g_cum - g_ref_q)).astype(BF16)
    g_end = jnp.concatenate(
        [bcast_row(c * CHUNK + CHUNK - 1, CHUNK) for c in range(tb // CHUNK)], axis=0)
    k_sub = []
    for i in range(N_SUB):
        live = (i + 1) * SUB
        pieces = []
        for c in range(tb // CHUNK):
            r0 = c * CHUNK
            g_ref_i = 0.0 if i == 0 else bcast_row(r0 + i * SUB - 1, live)
            pieces.append(hk[r0:r0 + live] * jnp.exp(g_ref_i - g_cum[r0:r0 + live]))
            if live < CHUNK:
                pieces.append(jnp.zeros((CHUNK - live, D_HGRN), F32))
        k_sub.append(jnp.concatenate(pieces, axis=0).astype(BF16))

    n_ch = tb // CHUNK
    gc = [g_scr[c * CHUNK + CHUNK - 1:c * CHUNK + CHUNK, :] for c in range(n_ch)]

    def span(lo, hi):
        if hi <= lo:
            return jnp.ones((CHUNK, D_HGRN), F32)
        return jnp.broadcast_to(jnp.exp(sum(gc[lo:hi])), (CHUNK, D_HGRN))

    zeros_chunk = jnp.zeros((CHUNK, D_HGRN), F32)
    k_end32 = hk * jnp.exp(g_end - g_cum)
    q_glob32 = hq * jnp.exp(g_cum)
    k_cross = [(k_end32 * jnp.concatenate(
        [span(cp + 1, c) if cp < c else zeros_chunk for cp in range(n_ch)], axis=0)).astype(BF16)
        for c in range(1, n_ch)]
    q_tile = (q_glob32 * jnp.concatenate([span(0, c) for c in range(n_ch)], axis=0)).astype(BF16)
    k_tile_end = (k_end32 * jnp.concatenate([span(cp + 1, n_ch) for cp in range(n_ch)], axis=0)).astype(BF16)
    decay_tile = jnp.exp(sum(gc))
    q_glob = q_glob32.astype(BF16)

    def keep_rows(a, block, wanted):
        zero = jnp.zeros((block, a.shape[1]), a.dtype)
        return jnp.concatenate(
            [a[r * block:(r + 1) * block] if wanted(r) else zero for r in range(a.shape[0] // block)], axis=0)

    q_sub = [keep_rows(q_loc, SUB, lambda r, i=i: r % N_SUB == i) for i in range(N_SUB)]
    q_cross = [keep_rows(q_glob, CHUNK, lambda r, c=c: r == c) for c in range(1, n_ch)]
    amask = amask_ref[...]
    y = y_att
    out_group = 2 * LANES // HGRN_D
    new_state = []
    for h in range(N_HGRN_HEADS):
        ls = slice(h * HGRN_D, (h + 1) * HGRN_D)
        q_stack = jnp.concatenate([q_sub[i][:, ls] for i in range(N_SUB)], axis=1)
        k_stack = jnp.concatenate([k_sub[i][:, ls] for i in range(N_SUB)], axis=1)
        qc_stack = jnp.concatenate([q_cross[c - 1][:, ls] for c in range(1, n_ch)], axis=1)
        kc_stack = jnp.concatenate([k_cross[c - 1][:, ls] for c in range(1, n_ch)], axis=1)
        a = (_dot_nt(q_stack, k_stack) * amask + _dot_nt(qc_stack, kc_stack)).astype(BF16)
        v_h = hv_bf[:, ls]
        o = _dot(a, v_h) + _dot_nt(q_tile[:, ls], state[h].astype(BF16))
        new_state.append(state[h] * decay_tile[:, ls] + _dot_tn(v_h, k_tile_end[:, ls]))
        o = o * lax.rsqrt(jnp.mean(o * o, axis=-1, keepdims=True) + EPS)
        ho_scr[:, ls] = o
        if (h + 1) % out_group == 0:
            gs = slice((h + 1 - out_group) * HGRN_D, (h + 1) * HGRN_D)
            hn = ho_scr[:, gs] * gout_ref[:, gs] * hg_act[:, gs]
            y = y + _dot(hn.astype(BF16), wout_ref[D_ATTN + gs.start:D_ATTN + gs.stop, :])

    return x + _rms(y, gpost_ref[...]), k_rot, v_new, k_prev, v_prev, new_state


def _prompt_mix_kernel(sink_ref, x_ref, gpre_ref, win_ref, cos_ref, shi_ref, slo_ref, lb_ref, gout_ref,
                       wout_ref, gpost_ref, bias_ref, amask_ref,
                       y_ref, wk_ref, wv_ref, s_ref,
                       kk_scr, vv_scr, st_scr, g_scr, ho_scr, *, layer):
    tb = MIX_TILE
    step = pl.program_id(1)
    last = pl.num_programs(1) - 1

    @pl.when(step == 0)
    def _():
        kk_scr[...] = jnp.zeros_like(kk_scr)
        vv_scr[...] = jnp.zeros_like(vv_scr)
        st_scr[...] = jnp.zeros_like(st_scr)

    k_prev = [kk_scr[g] for g in range(N_KV_HEADS)]
    v_prev = [vv_scr[g] for g in range(N_KV_HEADS)]
    state = [st_scr[h] for h in range(N_HGRN_HEADS)]
    for part in range(MIX_PARTS):
        rs = slice(part * tb, (part + 1) * tb)
        y, k_rot, v_new, k_prev, v_prev, state = _mix_tile(
            x_ref[rs, :], (cos_ref[rs, :], shi_ref[rs, :], slo_ref[rs, :]), step == 0 if part == 0 else None,
            k_prev, v_prev, state, g_scr.at[part], ho_scr.at[part],
            sink_ref, gpre_ref, win_ref, lb_ref, gout_ref, wout_ref, gpost_ref, bias_ref, amask_ref, layer)
        y_ref[rs, :] = y
    for g in range(N_KV_HEADS):
        kk_scr[g] = k_prev[g]
        vv_scr[g] = v_prev[g]
    for h in range(N_HGRN_HEADS):
        st_scr[h] = state[h]

    @pl.when(step == last)
    def _():
        wk_ref[0] = k_rot[tb - WINDOW:]
        wv_ref[0] = v_new[tb - WINDOW:]
        for h in range(N_HGRN_HEADS):
            s_ref[0, h] = state[h].T


def _prompt_mix(x, sinks, g_pre, w_in, lb_raw, g_out, w_out, g_post, batch, seq, layer):
    tb = MIX_TILE
    rows = MIX_PARTS * tb
    nt = seq // rows
    cos, s_hi, s_lo = _rope_tables(jnp.arange(seq, dtype=F32))
    tok = lambda b, n: (b * nt + n, 0)
    tab = lambda b, n: (n, 0)
    per_b3 = lambda b, n: (b, 0, 0)
    lb_rows = lb_raw.shape[0]
    key_i = jnp.arange(2 * WINDOW)[:, None]
    rel = jnp.arange(WINDOW)[None, :] + WINDOW - key_i
    bias = jnp.tile(jnp.where((rel >= 0) & (rel < WINDOW), 0.0, NEG_INF).astype(F32), (1, GQA_GROUP))
    t_i = jnp.arange(tb)
    amask = ((t_i[:, None] // CHUNK == t_i[None, :] // CHUNK) & (t_i[None, :] <= t_i[:, None])).astype(F32)
    return pl.pallas_call(
        functools.partial(_prompt_mix_kernel, layer=layer),
        grid=(batch, nt),
        in_specs=[
            pl.BlockSpec(memory_space=pltpu.SMEM),
            pl.BlockSpec((rows, D_MODEL), tok),
            _const_spec((1, D_MODEL)),
            _const_spec((D_MODEL, IN_COLS)),
            pl.BlockSpec((rows, LANES), tab),
            pl.BlockSpec((rows, LANES), tab),
            pl.BlockSpec((rows, LANES), tab),
            _const_spec((lb_rows, D_HGRN)),
            _const_spec((1, D_HGRN)),
            _const_spec((D_MODEL, D_MODEL)),
            _const_spec((1, D_MODEL)),
            _const_spec((2 * WINDOW, GQA_GROUP * WINDOW)),
            _const_spec((tb, tb)),
        ],
        out_specs=[
            pl.BlockSpec((rows, D_MODEL), tok),
            pl.BlockSpec((1, WINDOW, D_KV), per_b3),
            pl.BlockSpec((1, WINDOW, D_KV), per_b3),
            pl.BlockSpec((1, N_HGRN_HEADS, HGRN_D, HGRN_D), lambda b, n: (b, 0, 0, 0)),
        ],
        out_shape=[
            jax.ShapeDtypeStruct((batch * seq, D_MODEL), F32),
            jax.ShapeDtypeStruct((batch, WINDOW, D_KV), F32),
            jax.ShapeDtypeStruct((batch, WINDOW, D_KV), F32),
            jax.ShapeDtypeStruct((batch, N_HGRN_HEADS, HGRN_D, HGRN_D), F32),
        ],
        scratch_shapes=[
            pltpu.VMEM((N_KV_HEADS, WINDOW, LANES), BF16),
            pltpu.VMEM((N_KV_HEADS, WINDOW, LANES), BF16),
            pltpu.VMEM((N_HGRN_HEADS, HGRN_D, HGRN_D), F32),
            pltpu.VMEM((MIX_PARTS, tb, D_HGRN), F32),
            pltpu.VMEM((MIX_PARTS, tb, D_HGRN), F32),
        ],
        compiler_params=pltpu.CompilerParams(
            dimension_semantics=("arbitrary", "arbitrary"), vmem_limit_bytes=VMEM_LIMIT),
        name="prompt_mix",
    )(sinks, x, g_pre.reshape(1, D_MODEL), w_in, cos, s_hi, s_lo, lb_raw, g_out.reshape(1, D_HGRN),
      w_out, g_post.reshape(1, D_MODEL), bias, amask)


def _sample_mix_kernel(sink_ref, x_ref, gpre_ref, win_ref, cos_ref, shi_ref, slo_ref, lb_ref, goutc_ref,
                       wout_ref, gpost_ref, ckt_ref, cvt_ref, sin_ref,
                       y_ref, wkt_ref, wvt_ref, sout_ref,
                       q_scr, kn_scr, vn_scr, knt_scr, vnt_scr, ft_scr, hkt_scr, hqt_scr, hvt_scr, hgt_scr,
                       ot_scr, att_scr, *, layer, nb):
    step = pl.program_id(0)
    last = pl.num_programs(0) - 1
    bt = nb // N_HGRN_HEADS

    @pl.when(step == 0)
    def _():
        u = _rms(x_ref[...], gpre_ref[...]).astype(BF16)
        proj = _dot(u, win_ref[...])
        cos, s_hi, s_lo = cos_ref[...], shi_ref[...], slo_ref[...]
        scale = HEAD_DIM ** -0.5
        for j in range(D_ATTN // LANES):
            q_scr[:, LANES * j:LANES * (j + 1)] = _rope(
                proj[:, OFF_Q + LANES * j:OFF_Q + LANES * (j + 1)], cos, s_hi, s_lo) * scale
        k_new = _rope(proj[:, OFF_K:OFF_K + D_KV], cos, s_hi, s_lo)
        v_new = proj[:, OFF_V:OFF_V + D_KV]
        kn_scr[...] = k_new
        vn_scr[...] = v_new
        for scr, val in ((knt_scr, k_new), (vnt_scr, v_new)):
            for i, part in enumerate(_split3(val.T)):
                scr[i] = part
        lb = _lower_bound(lb_ref[...], layer)
        f_t = (lb + (1.0 - lb) * jax.nn.sigmoid(proj[:, OFF_HF:OFF_HF + D_HGRN])).T
        ft_scr[...] = f_t
        hkt_scr[...] = 1.0 - f_t
        hqt_scr[...] = (proj[:, OFF_HQ:OFF_HQ + D_HGRN] * (HGRN_D ** -0.5)).T
        hvt_scr[...] = proj[:, OFF_HI:OFF_HI + D_HGRN].T
        hgt_scr[...] = proj[:, OFF_HG:OFF_HG + D_HGRN].T

    base = pl.multiple_of(step * HGRN_D, HGRN_D)
    hv_t = hvt_scr[pl.ds(base, HGRN_D), :]

    def hgrn_row(k, o_acc):
        f_row = ft_scr[pl.ds(base + k, 1), :]
        s_new = f_row * sin_ref[k] + hkt_scr[pl.ds(base + k, 1), :] * hv_t
        sout_ref[k] = s_new
        return o_acc + hqt_scr[pl.ds(base + k, 1), :] * s_new

    ot_scr[pl.ds(base, HGRN_D), :] = lax.fori_loop(
        0, HGRN_D, hgrn_row, jnp.zeros((HGRN_D, nb), F32), unroll=8)

    lane8 = lax.broadcasted_iota(jnp.int32, (N_Q_HEADS, LANES), 1)
    row8 = lax.broadcasted_iota(jnp.int32, (N_Q_HEADS, LANES), 0)
    keep8 = (lane8 >= HALF) == (row8 >= GQA_GROUP)
    win_lane = lax.broadcasted_iota(jnp.int32, (D_KV, WINDOW), 1)
    sink = sink_ref[...]
    b0 = step * bt
    sel = (lax.broadcasted_iota(jnp.int32, (nb, bt), 0)
           == b0 + lax.broadcasted_iota(jnp.int32, (nb, bt), 1)).astype(BF16)
    k_cols = sum(_dot(knt_scr[i], sel) for i in range(3))
    v_cols = sum(_dot(vnt_scr[i], sel) for i in range(3))
    for bi in range(bt):
        b = b0 + bi
        q_b = jnp.broadcast_to(q_scr[pl.ds(b, 1), :], (N_Q_HEADS, D_ATTN))
        qm = jnp.zeros((N_Q_HEADS, LANES), F32)
        for h in range(N_Q_HEADS):
            c = q_b[:, LANES * (h // 2):LANES * (h // 2 + 1)]
            if h % 2 != h // GQA_GROUP:
                c = pltpu.roll(c, HALF, 1)
            qm = jnp.where(row8 == h, c, qm)
        qm = jnp.where(keep8, qm, 0.0)
        k_new = kn_scr[pl.ds(b, 1), :]
        v_new = vn_scr[pl.ds(b, 1), :]
        k_old, v_old = ckt_ref[bi], cvt_ref[bi]
        s = _dot(qm.astype(BF16), k_old.astype(BF16))
        s = jnp.where(lane8 >= 1, s, NEG_INF)
        s_new = jnp.sum(qm * k_new, axis=-1, keepdims=True)
        m = jnp.maximum(jnp.maximum(jnp.max(s, axis=-1, keepdims=True), s_new), sink)
        p = jnp.exp(s - m)
        p_new = jnp.exp(s_new - m)
        denom = jnp.sum(p, axis=-1, keepdims=True) + p_new + jnp.exp(sink - m)
        o = (_dot_nt(p.astype(BF16), v_old.astype(BF16)) + p_new * v_new) / denom
        att_scr[pl.ds(pl.multiple_of(b * N_Q_HEADS, N_Q_HEADS), N_Q_HEADS), :] = o
        wkt_ref[bi] = jnp.where(win_lane == WINDOW - 1, k_cols[:, bi:bi + 1], pltpu.roll(k_old, WINDOW - 1, 1))
        wvt_ref[bi] = jnp.where(win_lane == WINDOW - 1, v_cols[:, bi:bi + 1], pltpu.roll(v_old, WINDOW - 1, 1))

    @pl.when(step == last)
    def _():
        y = jnp.zeros((nb, D_MODEL), F32)
        for h in range(N_HGRN_HEADS):
            ks = slice(h * HGRN_D, (h + 1) * HGRN_D)
            o_t = ot_scr[ks, :]
            hn_t = (o_t * lax.rsqrt(jnp.mean(o_t * o_t, axis=0, keepdims=True) + EPS)
                    * goutc_ref[...] * _silu(hgt_scr[ks, :]))
            y = y + _dot_tn(hn_t.astype(BF16), wout_ref[D_ATTN + h * HGRN_D:D_ATTN + (h + 1) * HGRN_D, :])
        for h in range(N_Q_HEADS):
            g = h // GQA_GROUP
            a_h = att_scr[pl.ds(h, nb, stride=N_Q_HEADS), :][:, g * HALF:(g + 1) * HALF]
            y = y + _dot(a_h.astype(BF16), wout_ref[h * HEAD_DIM:(h + 1) * HEAD_DIM, :])
        y_ref[...] = x_ref[...] + _rms(y, gpost_ref[...])


def _sample_mix(x, sinks, g_pre, w_in, lb_raw, g_out_head, w_out, g_post, cache_kt, cache_vt, state_t, pos, layer):
    nb = x.shape[0]
    bt = nb // N_HGRN_HEADS
    cos, s_hi, s_lo = _rope_tables(pos)
    lb_rows = lb_raw.shape[0]
    blk3 = pl.BlockSpec((bt, D_KV, WINDOW), lambda i: (i, 0, 0))
    blk_s = pl.BlockSpec((HGRN_D, HGRN_D, nb), lambda i: (i, 0, 0))
    chan_major = pltpu.VMEM((D_HGRN, nb), F32)
    return pl.pallas_call(
        functools.partial(_sample_mix_kernel, layer=layer, nb=nb),
        grid=(N_HGRN_HEADS,),
        in_specs=[
            _const_spec((N_Q_HEADS, 1)),
            _const_spec((nb, D_MODEL)),
            _const_spec((1, D_MODEL)),
            _const_spec((D_MODEL, IN_COLS)),
            _const_spec((1, LANES)),
            _const_spec((1, LANES)),
            _const_spec((1, LANES)),
            _const_spec((lb_rows, D_HGRN)),
            _const_spec((HGRN_D, 1)),
            _const_spec((D_MODEL, D_MODEL)),
            _const_spec((1, D_MODEL)),
            blk3, blk3, blk_s,
        ],
        out_specs=[pl.BlockSpec((nb, D_MODEL), lambda i: (0, 0)), blk3, blk3, blk_s],
        out_shape=[
            jax.ShapeDtypeStruct((nb, D_MODEL), F32),
            jax.ShapeDtypeStruct(cache_kt.shape, F32),
            jax.ShapeDtypeStruct(cache_vt.shape, F32),
            jax.ShapeDtypeStruct(state_t.shape, F32),
        ],
        scratch_shapes=[
            pltpu.VMEM((nb, D_ATTN), F32),
            pltpu.VMEM((nb, D_KV), F32),
            pltpu.VMEM((nb, D_KV), F32),
            pltpu.VMEM((3, D_KV, nb), BF16),
            pltpu.VMEM((3, D_KV, nb), BF16),
            chan_major, chan_major, chan_major, chan_major, chan_major, chan_major,
            pltpu.VMEM((nb * N_Q_HEADS, LANES), F32),
        ],
        compiler_params=pltpu.CompilerParams(
            dimension_semantics=("arbitrary",), vmem_limit_bytes=VMEM_LIMIT),
        name="sample_mix",
    )(sinks.reshape(N_Q_HEADS, 1), x, g_pre.reshape(1, D_MODEL), w_in, cos, s_hi, s_lo, lb_raw,
      g_out_head.reshape(HGRN_D, 1), w_out, g_post.reshape(1, D_MODEL), cache_kt, cache_vt, state_t)


def kernel(x_prompt, x_sample, cache_win_k, cache_win_v, state_hgrn, ffn1_pre_g, ffn1_post_g, ffn1_w_gu,
           ffn1_w_down, mix_pre_g, mix_post_g, w_in, attn_sinks, hgrn_lb, hgrn_out_g, w_out, ffn2_pre_g,
           ffn2_post_g, ffn2_w_gu, ffn2_w_down):
    batch, seq, _ = x_prompt.shape
    nb, t_s, _ = x_sample.shape
    depth = w_in.shape[0]
    assert t_s == 1 and seq % (MIX_PARTS * MIX_TILE) == 0 and (batch * seq) % FFN_TILE == 0 and nb == LANES
    assert cache_win_k.shape[2:] == (WINDOW, N_KV_HEADS, HEAD_DIM)

    xp = x_prompt.reshape(batch * seq, D_MODEL)
    xs = x_sample.reshape(nb, D_MODEL)
    pos_s = PAST_LEN + jnp.arange(t_s, dtype=F32)
    outs = [[] for _ in range(6)]
    for l in range(depth):
        w_in_l, w_out_l = w_in[l].astype(BF16), w_out[l].astype(BF16)
        g_out = jnp.tile(hgrn_out_g[l], N_HGRN_HEADS)

        xp, xs = _ffn(xp, xs, ffn1_pre_g[l], ffn1_post_g[l], ffn1_w_gu[l], ffn1_w_down[l])

        xp, wk_p, wv_p, s_p = _prompt_mix(xp, attn_sinks[l], mix_pre_g[l], w_in_l, hgrn_lb, g_out, w_out_l,
                                          mix_post_g[l], batch, seq, l)
        to_kt = lambda c: jnp.transpose(c, (0, 2, 3, 1)).reshape(nb, D_KV, WINDOW)
        from_kt = lambda c: jnp.transpose(c.reshape(nb, N_KV_HEADS, HEAD_DIM, WINDOW), (0, 3, 1, 2))
        state_t = jnp.transpose(state_hgrn[l], (1, 2, 3, 0)).reshape(D_HGRN, HGRN_D, nb)
        xs, wkt_s, wvt_s, st_s = _sample_mix(
            xs, attn_sinks[l], mix_pre_g[l], w_in_l, hgrn_lb, hgrn_out_g[l], w_out_l, mix_post_g[l],
            to_kt(cache_win_k[l]), to_kt(cache_win_v[l]), state_t, pos_s, l)
        s_s = jnp.transpose(st_s.reshape(N_HGRN_HEADS, HGRN_D, HGRN_D, nb), (3, 0, 1, 2))

        xp, xs = _ffn(xp, xs, ffn2_pre_g[l], ffn2_post_g[l], ffn2_w_gu[l], ffn2_w_down[l])

        kv_shape = (WINDOW, N_KV_HEADS, HEAD_DIM)
        for lst, val in zip(outs, (wk_p.reshape(batch, *kv_shape), wv_p.reshape(batch, *kv_shape), s_p,
                                   from_kt(wkt_s), from_kt(wvt_s), s_s)):
            lst.append(val)

    return (xp.reshape(batch, seq, D_MODEL), xs.reshape(nb, t_s, D_MODEL)) + tuple(jnp.stack(o) for o in outs)
```

```python
import functools

import jax
import jax.numpy as jnp
import numpy as np
from jax import lax
from jax.experimental import pallas as pl
from jax.experimental.pallas import tpu as pltpu

F32 = jnp.float32
BF16 = jnp.bfloat16

D_MODEL = 1024
D_FF = 2816
HEAD_DIM = 64
N_Q_HEADS = 8
N_KV_HEADS = 2
GQA_GROUP = N_Q_HEADS // N_KV_HEADS
WINDOW = 128
PAST_LEN = 8192
ROT_DIM = HEAD_DIM // 4
ROPE_THETA = 500000.0
N_HGRN_HEADS = 8
HGRN_D = 64
D_ATTN = N_Q_HEADS * HEAD_DIM
D_KV = N_KV_HEADS * HEAD_DIM
D_HGRN = N_HGRN_HEADS * HGRN_D
IN_COLS = D_ATTN + 2 * D_KV + 4 * D_HGRN
OFF_Q, OFF_K, OFF_V = 0, D_ATTN, D_ATTN + D_KV
OFF_HQ = D_ATTN + 2 * D_KV
OFF_HF, OFF_HI, OFF_HG = OFF_HQ + D_HGRN, OFF_HQ + 2 * D_HGRN, OFF_HQ + 3 * D_HGRN
EPS = 1e-6
NEG_INF = -1e30
LOG2E = 1.4426950408889634
LANES = 128
HALF = LANES // 2

FFN_TILE = 1024
FFN_PARTS = 4
FFN_W_STEPS = 8
MIX_TILE = 256
MIX_PARTS = 2
CHUNK = 64
SUB = 16
N_SUB = CHUNK // SUB
VMEM_LIMIT = 56 * 1024 * 1024


def _rms(x, g):
    return x * lax.rsqrt(jnp.mean(x * x, axis=-1, keepdims=True) + EPS) * g


def _silu(x):
    return x * jax.nn.sigmoid(x)


def _dot(a, b):
    return jnp.dot(a, b, preferred_element_type=F32)


def _dot_nt(a, b):
    return lax.dot_general(a, b, (((1,), (1,)), ((), ())), preferred_element_type=F32)


def _dot_tn(a, b):
    return lax.dot_general(a, b, (((0,), (0,)), ((), ())), preferred_element_type=F32)


def _split3(x):
    hi = x.astype(BF16)
    r = x - hi.astype(F32)
    mid = r.astype(BF16)
    lo = (r - mid.astype(F32)).astype(BF16)
    return hi, mid, lo


def _const_spec(shape):
    nd = len(shape)
    return pl.BlockSpec(shape, lambda *_: (0,) * nd, pipeline_mode=pl.Buffered(1))


def _ffn_kernel(x_ref, xs_ref, gpre_ref, gpost_ref, wgu32_ref, wd32_ref, o_ref, os_ref, wgu_ref, wd_ref):
    step = pl.program_id(0)

    @pl.when(step < FFN_W_STEPS)
    def _():
        gu_rows, d_rows = wgu32_ref.shape[0], wd32_ref.shape[0]
        wgu_ref[pl.ds(pl.multiple_of(step * gu_rows, gu_rows), gu_rows), :] = wgu32_ref[...].astype(BF16)
        wd_ref[pl.ds(pl.multiple_of(step * d_rows, d_rows), d_rows), :] = wd32_ref[...].astype(BF16)

    def half_step(x):
        h = _rms(x, gpre_ref[...]).astype(BF16)
        gate = _dot(h, wgu_ref[:, :D_FF])
        up = _dot(h, wgu_ref[:, D_FF:])
        act = (_silu(gate) * up).astype(BF16)
        y = _dot(act, wd_ref[...])
        return x + 0.5 * _rms(y, gpost_ref[...])

    @pl.when(step >= FFN_W_STEPS)
    def _():
        rows = x_ref.shape[0] // FFN_PARTS
        for r in range(FFN_PARTS):
            rs = slice(r * rows, (r + 1) * rows)
            o_ref[rs, :] = half_step(x_ref[rs, :])

    @pl.when(step == pl.num_programs(0) - 1)
    def _():
        os_ref[...] = half_step(xs_ref[...])


def _ffn(x, xs, g_pre, g_post, w_gu, w_down):
    n, nb = x.shape[0], xs.shape[0]
    tile_map = lambda i: (jnp.maximum(i - FFN_W_STEPS, 0), 0)
    chunk_map = lambda i: (jnp.minimum(i, FFN_W_STEPS - 1), 0)
    return pl.pallas_call(
        _ffn_kernel,
        grid=(FFN_W_STEPS + n // FFN_TILE,),
        in_specs=[
            pl.BlockSpec((FFN_TILE, D_MODEL), tile_map),
            _const_spec((nb, D_MODEL)),
            _const_spec((1, D_MODEL)),
            _const_spec((1, D_MODEL)),
            pl.BlockSpec((D_MODEL // FFN_W_STEPS, 2 * D_FF), chunk_map),
            pl.BlockSpec((D_FF // FFN_W_STEPS, D_MODEL), chunk_map),
        ],
        out_specs=[pl.BlockSpec((FFN_TILE, D_MODEL), tile_map),
                   pl.BlockSpec((nb, D_MODEL), lambda i: (0, 0))],
        out_shape=[jax.ShapeDtypeStruct((n, D_MODEL), F32), jax.ShapeDtypeStruct((nb, D_MODEL), F32)],
        scratch_shapes=[pltpu.VMEM((D_MODEL, 2 * D_FF), BF16), pltpu.VMEM((D_FF, D_MODEL), BF16)],
        compiler_params=pltpu.CompilerParams(
            dimension_semantics=("arbitrary",), vmem_limit_bytes=VMEM_LIMIT),
        name="ffn",
    )(x, xs, g_pre.reshape(1, D_MODEL), g_post.reshape(1, D_MODEL), w_gu, w_down)


def _lower_bound(lb_raw, layer):
    m = jnp.max(lb_raw, axis=0, keepdims=True)
    e = jnp.exp(lb_raw - m)
    return jnp.sum(e[: layer + 1], axis=0, keepdims=True) / jnp.sum(e, axis=0, keepdims=True)


def _rope(x, cos, sin_hi, sin_lo):
    return x * cos + pltpu.roll(x, ROT_DIM // 2, 1) * sin_hi + pltpu.roll(x, LANES - ROT_DIM // 2, 1) * sin_lo


def _rope_tables(pos):
    half = ROT_DIM // 2
    inv = ROPE_THETA ** (-jnp.arange(half, dtype=F32) / half)
    ang = pos[:, None] * inv[None, :]
    cos, sin = jnp.cos(ang), jnp.sin(ang)
    t = pos.shape[0]
    one = jnp.ones((t, HEAD_DIM - ROT_DIM), F32)
    zero = jnp.zeros((t, HEAD_DIM - ROT_DIM), F32)
    zh = jnp.zeros((t, half), F32)
    c = jnp.concatenate([cos, cos, one], axis=1)
    s_hi = jnp.concatenate([zh, sin, zero], axis=1)
    s_lo = jnp.concatenate([-sin, zh, zero], axis=1)
    return tuple(jnp.tile(a, (1, LANES // HEAD_DIM)) for a in (c, s_hi, s_lo))


def _dup_half(x, g, lo_half):
    xr = pltpu.roll(x, HALF, 1)
    return jnp.where(lo_half, x, xr) if g == 0 else jnp.where(lo_half, xr, x)


def _mix_tile(x, rope_tab, seq_start, k_prev, vt_prev, state, g_scr,
              sink_ref, gpre_ref, win_ref, wint_ref, lb_ref, goutc_ref, wout_ref, gpost_ref, bias_ref, amask_ref,
              layer):
    tb = MIX_TILE
    u = _rms(x, gpre_ref[...]).astype(BF16)

    def proj(off, width):
        return _dot(u, win_ref[:, off:off + width])

    p_attn = proj(OFF_Q, D_ATTN + 2 * D_KV)
    p_hf = proj(OFF_HF, D_HGRN)
    cos, s_hi, s_lo = rope_tab
    scale = HEAD_DIM ** -0.5 * LOG2E
    q_cols = [_rope(p_attn[:, LANES * j: LANES * (j + 1)], cos, s_hi, s_lo) * scale
              for j in range(D_ATTN // LANES)]
    k_rot = _rope(p_attn[:, OFF_K:OFF_K + D_KV], cos, s_hi, s_lo)
    v_new = p_attn[:, OFF_V:OFF_V + D_KV]

    lane = lax.broadcasted_iota(jnp.int32, (WINDOW, LANES), 1)
    lo_half = lane < HALF
    bias = bias_ref[...]
    if seq_start is None:
        bias_first = bias
    else:
        no_prev = jnp.where(seq_start, NEG_INF, 0.0)
        bias_first = jnp.concatenate([bias[:WINDOW] + no_prev, bias[WINDOW:]], axis=0)

    vt_tile = v_new.T.astype(BF16)
    scores = []
    for i in range(tb // WINDOW):
        r0 = i * WINDOW
        k_cur = [_dup_half(k_rot[r0:r0 + WINDOW], g, lo_half).astype(BF16) for g in range(N_KV_HEADS)]
        vt_keys = jnp.concatenate([vt_prev, vt_tile[:, r0:r0 + WINDOW]], axis=1)
        for g in range(N_KV_HEADS):
            keys = jnp.concatenate([k_prev[g], k_cur[g]], axis=0)
            heads = range(g * GQA_GROUP, (g + 1) * GQA_GROUP)
            qg = jnp.concatenate(
                [jnp.where(lo_half if h % 2 == 0 else ~lo_half, q_cols[h // 2][r0:r0 + WINDOW], 0.0)
                 for h in heads], axis=0).astype(BF16)
            s = _dot_nt(keys, qg) + (bias_first if i == 0 else bias)
            scores.append((s, vt_keys[g * HEAD_DIM:(g + 1) * HEAD_DIM], heads))
        k_prev, vt_prev = k_cur, vt_tile[:, r0:r0 + WINDOW]

    hq = proj(OFF_HQ, D_HGRN) * (HGRN_D ** -0.5)
    hv_t = _dot_nt(wint_ref[:D_HGRN, :], u).astype(BF16)
    hg_act_t = _silu(_dot_nt(wint_ref[D_HGRN:, :], u))

    att_t = [[None] * (tb // WINDOW) for _ in range(N_Q_HEADS)]
    for idx, (s, vt_g, heads) in enumerate(scores):
        sink = jnp.concatenate(
            [jnp.full((1, WINDOW), sink_ref[h] * LOG2E, F32) for h in heads], axis=1)
        m = jnp.maximum(jnp.max(s, axis=0, keepdims=True), sink)
        p = jnp.exp2(s - m)
        denom = jnp.sum(p, axis=0, keepdims=True) + jnp.exp2(sink - m)
        p = (p * (1.0 / denom)).astype(BF16)
        o_t = _dot(vt_g, p)
        for j, h in enumerate(heads):
            att_t[h][idx // N_KV_HEADS] = o_t[:, j * WINDOW:(j + 1) * WINDOW]
    att_t = jnp.concatenate([jnp.concatenate(blocks, axis=1) for blocks in att_t], axis=0)
    y_att = _dot_tn(att_t.astype(BF16), wout_ref[:D_ATTN, :])

    lb = _lower_bound(lb_ref[...], layer)
    f = lb + (1.0 - lb) * jax.nn.sigmoid(p_hf)
    logf = jnp.log(f)
    hk = 1.0 - f

    tr = lax.broadcasted_iota(jnp.int32, (tb, tb), 0)
    tc = lax.broadcasted_iota(jnp.int32, (tb, tb), 1)
    tri = ((tr // CHUNK == tc // CHUNK) & (tc <= tr)).astype(BF16)
    g_cum = sum(_dot(tri, part) for part in _split3(logf))
    g_scr[...] = g_cum

    def bcast_row(r, rows):
        return jnp.broadcast_to(g_scr[r:r + 1, :], (rows, D_HGRN))

    zeros_sub = jnp.zeros((SUB, D_HGRN), F32)
    g_ref_q = jnp.concatenate(
        [zeros_sub if sb % N_SUB == 0 else bcast_row(sb * SUB - 1, SUB) for sb in range(tb // SUB)], axis=0)
    q_loc = (hq * jnp.exp(g_cum - g_ref_q)).astype(BF16)
    g_end = jnp.concatenate(
        [bcast_row(c * CHUNK + CHUNK - 1, CHUNK) for c in range(tb // CHUNK)], axis=0)
    k_sub = []
    for i in range(N_SUB):
        live = (i + 1) * SUB
        pieces = []
        for c in range(tb // CHUNK):
            r0 = c * CHUNK
            g_ref_i = 0.0 if i == 0 else bcast_row(r0 + i * SUB - 1, live)
            pieces.append(hk[r0:r0 + live] * jnp.exp(g_ref_i - g_cum[r0:r0 + live]))
            if live < CHUNK:
                pieces.append(jnp.zeros((CHUNK - live, D_HGRN), F32))
        k_sub.append(jnp.concatenate(pieces, axis=0).astype(BF16))

    n_ch = tb // CHUNK
    gc = [g_scr[c * CHUNK + CHUNK - 1:c * CHUNK + CHUNK, :] for c in range(n_ch)]

    def span(lo, hi):
        if hi <= lo:
            return jnp.ones((CHUNK, D_HGRN), F32)
        return jnp.broadcast_to(jnp.exp(sum(gc[lo:hi])), (CHUNK, D_HGRN))

    zeros_chunk = jnp.zeros((CHUNK, D_HGRN), F32)
    k_end32 = hk * jnp.exp(g_end - g_cum)
    q_glob32 = hq * jnp.exp(g_cum)
    k_cross = [(k_end32 * jnp.concatenate(
        [span(cp + 1, c) if cp < c else zeros_chunk for cp in range(n_ch)], axis=0)).astype(BF16)
        for c in range(1, n_ch)]
    q_tile = (q_glob32 * jnp.concatenate([span(0, c) for c in range(n_ch)], axis=0)).astype(BF16)
    k_tile_end = (k_end32 * jnp.concatenate([span(cp + 1, n_ch) for cp in range(n_ch)], axis=0)).astype(BF16)
    decay_tile = jnp.exp(sum(gc))
    q_glob = q_glob32.astype(BF16)

    def keep_rows(a, block, wanted):
        zero = jnp.zeros((block, a.shape[1]), a.dtype)
        return jnp.concatenate(
            [a[r * block:(r + 1) * block] if wanted(r) else zero for r in range(a.shape[0] // block)], axis=0)

    q_sub = [keep_rows(q_loc, SUB, lambda r, i=i: r % N_SUB == i) for i in range(N_SUB)]
    q_cross = [keep_rows(q_glob, CHUNK, lambda r, c=c: r == c) for c in range(1, n_ch)]
    amask_t = amask_ref[...]
    y = y_att
    out_group = 2 * LANES // HGRN_D
    new_state, hn_t = [], []
    for h in range(N_HGRN_HEADS):
        ls = slice(h * HGRN_D, (h + 1) * HGRN_D)
        q_stack = jnp.concatenate([q_sub[i][:, ls] for i in range(N_SUB)], axis=1)
        k_stack = jnp.concatenate([k_sub[i][:, ls] for i in range(N_SUB)], axis=1)
        qc_stack = jnp.concatenate([q_cross[c - 1][:, ls] for c in range(1, n_ch)], axis=1)
        kc_stack = jnp.concatenate([k_cross[c - 1][:, ls] for c in range(1, n_ch)], axis=1)
        a_t = (_dot_nt(k_stack, q_stack) * amask_t + _dot_nt(kc_stack, qc_stack)).astype(BF16)
        vt_h = hv_t[ls, :]
        o_t = _dot(vt_h, a_t) + _dot_nt(state[h].astype(BF16), q_tile[:, ls])
        new_state.append(state[h] * decay_tile[:, ls] + _dot(vt_h, k_tile_end[:, ls]))
        o_t = o_t * lax.rsqrt(jnp.mean(o_t * o_t, axis=0, keepdims=True) + EPS)
        hn_t.append(o_t * goutc_ref[...] * hg_act_t[ls, :])
        if (h + 1) % out_group == 0:
            grp = jnp.concatenate(hn_t[h + 1 - out_group:h + 1], axis=0).astype(BF16)
            y = y + _dot_tn(grp, wout_ref[D_ATTN + (h + 1 - out_group) * HGRN_D:D_ATTN + (h + 1) * HGRN_D, :])

    return x + _rms(y, gpost_ref[...]), k_rot, v_new, k_prev, vt_prev, new_state


def _prompt_mix_kernel(sink_ref, x_ref, gpre_ref, win_ref, wint_ref, cos_ref, shi_ref, slo_ref, lb_ref, goutc_ref,
                       wout_ref, gpost_ref, bias_ref, amask_ref,
                       y_ref, wk_ref, wv_ref, s_ref,
                       kk_scr, vt_scr, st_scr, g_scr, *, layer):
    tb = MIX_TILE
    step = pl.program_id(1)
    last = pl.num_programs(1) - 1

    @pl.when(step == 0)
    def _():
        kk_scr[...] = jnp.zeros_like(kk_scr)
        vt_scr[...] = jnp.zeros_like(vt_scr)
        st_scr[...] = jnp.zeros_like(st_scr)

    k_prev = [kk_scr[g] for g in range(N_KV_HEADS)]
    vt_prev = vt_scr[...]
    state = [st_scr[h] for h in range(N_HGRN_HEADS)]
    for part in range(MIX_PARTS):
        rs = slice(part * tb, (part + 1) * tb)
        y, k_rot, v_new, k_prev, vt_prev, state = _mix_tile(
            x_ref[rs, :], (cos_ref[rs, :], shi_ref[rs, :], slo_ref[rs, :]), step == 0 if part == 0 else None,
            k_prev, vt_prev, state, g_scr.at[part],
            sink_ref, gpre_ref, win_ref, wint_ref, lb_ref, goutc_ref, wout_ref, gpost_ref, bias_ref, amask_ref,
            layer)
        y_ref[rs, :] = y
    for g in range(N_KV_HEADS):
        kk_scr[g] = k_prev[g]
    vt_scr[...] = vt_prev
    for h in range(N_HGRN_HEADS):
        st_scr[h] = state[h]

    @pl.when(step == last)
    def _():
        wk_ref[0] = k_rot[tb - WINDOW:]
        wv_ref[0] = v_new[tb - WINDOW:]
        for h in range(N_HGRN_HEADS):
            s_ref[0, h] = state[h].T


def _prompt_mix(x, sinks, g_pre, w_in, w_in_t, lb_raw, g_out_head, w_out, g_post, batch, seq, layer):
    tb = MIX_TILE
    rows = MIX_PARTS * tb
    nt = seq // rows
    cos, s_hi, s_lo = _rope_tables(jnp.arange(seq, dtype=F32))
    tok = lambda b, n: (b * nt + n, 0)
    tab = lambda b, n: (n, 0)
    per_b3 = lambda b, n: (b, 0, 0)
    lb_rows = lb_raw.shape[0]
    key_i = jnp.arange(2 * WINDOW)[:, None]
    rel = jnp.arange(WINDOW)[None, :] + WINDOW - key_i
    bias = jnp.tile(jnp.where((rel >= 0) & (rel < WINDOW), 0.0, NEG_INF).astype(F32), (1, GQA_GROUP))
    t_i = jnp.arange(tb)
    amask = ((t_i[:, None] // CHUNK == t_i[None, :] // CHUNK) & (t_i[:, None] <= t_i[None, :])).astype(F32)
    return pl.pallas_call(
        functools.partial(_prompt_mix_kernel, layer=layer),
        grid=(batch, nt),
        in_specs=[
            pl.BlockSpec(memory_space=pltpu.SMEM),
            pl.BlockSpec((rows, D_MODEL), tok),
            _const_spec((1, D_MODEL)),
            _const_spec((D_MODEL, IN_COLS)),
            _const_spec((2 * D_HGRN, D_MODEL)),
            pl.BlockSpec((rows, LANES), tab),
            pl.BlockSpec((rows, LANES), tab),
            pl.BlockSpec((rows, LANES), tab),
            _const_spec((lb_rows, D_HGRN)),
            _const_spec((HGRN_D, 1)),
            _const_spec((D_MODEL, D_MODEL)),
            _const_spec((1, D_MODEL)),
            _const_spec((2 * WINDOW, GQA_GROUP * WINDOW)),
            _const_spec((tb, tb)),
        ],
        out_specs=[
            pl.BlockSpec((rows, D_MODEL), tok),
            pl.BlockSpec((1, WINDOW, D_KV), per_b3),
            pl.BlockSpec((1, WINDOW, D_KV), per_b3),
            pl.BlockSpec((1, N_HGRN_HEADS, HGRN_D, HGRN_D), lambda b, n: (b, 0, 0, 0)),
        ],
        out_shape=[
            jax.ShapeDtypeStruct((batch * seq, D_MODEL), F32),
            jax.ShapeDtypeStruct((batch, WINDOW, D_KV), F32),
            jax.ShapeDtypeStruct((batch, WINDOW, D_KV), F32),
            jax.ShapeDtypeStruct((batch, N_HGRN_HEADS, HGRN_D, HGRN_D), F32),
        ],
        scratch_shapes=[
            pltpu.VMEM((N_KV_HEADS, WINDOW, LANES), BF16),
            pltpu.VMEM((D_KV, WINDOW), BF16),
            pltpu.VMEM((N_HGRN_HEADS, HGRN_D, HGRN_D), F32),
            pltpu.VMEM((MIX_PARTS, tb, D_HGRN), F32),
        ],
        compiler_params=pltpu.CompilerParams(
            dimension_semantics=("arbitrary", "arbitrary"), vmem_limit_bytes=VMEM_LIMIT),
        name="prompt_mix",
    )(sinks, x, g_pre.reshape(1, D_MODEL), w_in, w_in_t, cos, s_hi, s_lo, lb_raw, g_out_head.reshape(HGRN_D, 1),
      w_out, g_post.reshape(1, D_MODEL), bias, amask)


def _sample_mix_kernel(sink_ref, x_ref, gpre_ref, win_ref, cos_ref, shi_ref, slo_ref, lb_ref, goutc_ref,
                       wout_ref, gpost_ref, ckt_ref, cvt_ref, sin_ref,
                       y_ref, wkt_ref, wvt_ref, sout_ref,
                       q_scr, kn_scr, vn_scr, knt_scr, vnt_scr, ft_scr, hkt_scr, hqt_scr, hvt_scr, hgt_scr,
                       ot_scr, att_scr, *, layer, nb):
    step = pl.program_id(0)
    last = pl.num_programs(0) - 1
    bt = nb // N_HGRN_HEADS

    @pl.when(step == 0)
    def _():
        u = _rms(x_ref[...], gpre_ref[...]).astype(BF16)
        proj = _dot(u, win_ref[...])
        cos, s_hi, s_lo = cos_ref[...], shi_ref[...], slo_ref[...]
        scale = HEAD_DIM ** -0.5
        for j in range(D_ATTN // LANES):
            q_scr[:, LANES * j:LANES * (j + 1)] = _rope(
                proj[:, OFF_Q + LANES * j:OFF_Q + LANES * (j + 1)], cos, s_hi, s_lo) * scale
        k_new = _rope(proj[:, OFF_K:OFF_K + D_KV], cos, s_hi, s_lo)
        v_new = proj[:, OFF_V:OFF_V + D_KV]
        kn_scr[...] = k_new
        vn_scr[...] = v_new
        for scr, val in ((knt_scr, k_new), (vnt_scr, v_new)):
            for i, part in enumerate(_split3(val.T)):
                scr[i] = part
        lb = _lower_bound(lb_ref[...], layer)
        f_t = (lb + (1.0 - lb) * jax.nn.sigmoid(proj[:, OFF_HF:OFF_HF + D_HGRN])).T
        ft_scr[...] = f_t
        hkt_scr[...] = 1.0 - f_t
        hqt_scr[...] = (proj[:, OFF_HQ:OFF_HQ + D_HGRN] * (HGRN_D ** -0.5)).T
        hvt_scr[...] = proj[:, OFF_HI:OFF_HI + D_HGRN].T
        hgt_scr[...] = proj[:, OFF_HG:OFF_HG + D_HGRN].T

    base = pl.multiple_of(step * HGRN_D, HGRN_D)
    hv_t = hvt_scr[pl.ds(base, HGRN_D), :]

    def hgrn_row(k, o_acc):
        f_row = ft_scr[pl.ds(base + k, 1), :]
        s_new = f_row * sin_ref[k] + hkt_scr[pl.ds(base + k, 1), :] * hv_t
        sout_ref[k] = s_new
        return o_acc + hqt_scr[pl.ds(base + k, 1), :] * s_new

    ot_scr[pl.ds(base, HGRN_D), :] = lax.fori_loop(
        0, HGRN_D, hgrn_row, jnp.zeros((HGRN_D, nb), F32), unroll=8)

    lane8 = lax.broadcasted_iota(jnp.int32, (N_Q_HEADS, LANES), 1)
    row8 = lax.broadcasted_iota(jnp.int32, (N_Q_HEADS, LANES), 0)
    keep8 = (lane8 >= HALF) == (row8 >= GQA_GROUP)
    win_lane = lax.broadcasted_iota(jnp.int32, (D_KV, WINDOW), 1)
    sink = sink_ref[...]
    b0 = step * bt
    sel = (lax.broadcasted_iota(jnp.int32, (nb, bt), 0)
           == b0 + lax.broadcasted_iota(jnp.int32, (nb, bt), 1)).astype(BF16)
    k_cols = sum(_dot(knt_scr[i], sel) for i in range(3))
    v_cols = sum(_dot(vnt_scr[i], sel) for i in range(3))
    for bi in range(bt):
        b = b0 + bi
        q_b = jnp.broadcast_to(q_scr[pl.ds(b, 1), :], (N_Q_HEADS, D_ATTN))
        qm = jnp.zeros((N_Q_HEADS, LANES), F32)
        for h in range(N_Q_HEADS):
            c = q_b[:, LANES * (h // 2):LANES * (h // 2 + 1)]
            if h % 2 != h // GQA_GROUP:
                c = pltpu.roll(c, HALF, 1)
            qm = jnp.where(row8 == h, c, qm)
        qm = jnp.where(keep8, qm, 0.0)
        k_new = kn_scr[pl.ds(b, 1), :]
        v_new = vn_scr[pl.ds(b, 1), :]
        k_old, v_old = ckt_ref[bi], cvt_ref[bi]
        s = _dot(qm.astype(BF16), k_old.astype(BF16))
        s = jnp.where(lane8 >= 1, s, NEG_INF)
        s_new = jnp.sum(qm * k_new, axis=-1, keepdims=True)
        m = jnp.maximum(jnp.maximum(jnp.max(s, axis=-1, keepdims=True), s_new), sink)
        p = jnp.exp(s - m)
        p_new = jnp.exp(s_new - m)
        denom = jnp.sum(p, axis=-1, keepdims=True) + p_new + jnp.exp(sink - m)
        o = (_dot_nt(p.astype(BF16), v_old.astype(BF16)) + p_new * v_new) / denom
        att_scr[pl.ds(pl.multiple_of(b * N_Q_HEADS, N_Q_HEADS), N_Q_HEADS), :] = o
        wkt_ref[bi] = jnp.where(win_lane == WINDOW - 1, k_cols[:, bi:bi + 1], pltpu.roll(k_old, WINDOW - 1, 1))
        wvt_ref[bi] = jnp.where(win_lane == WINDOW - 1, v_cols[:, bi:bi + 1], pltpu.roll(v_old, WINDOW - 1, 1))

    @pl.when(step == last)
    def _():
        y = jnp.zeros((nb, D_MODEL), F32)
        for h in range(N_HGRN_HEADS):
            ks = slice(h * HGRN_D, (h + 1) * HGRN_D)
            o_t = ot_scr[ks, :]
            hn_t = (o_t * lax.rsqrt(jnp.mean(o_t * o_t, axis=0, keepdims=True) + EPS)
                    * goutc_ref[...] * _silu(hgt_scr[ks, :]))
            y = y + _dot_tn(hn_t.astype(BF16), wout_ref[D_ATTN + h * HGRN_D:D_ATTN + (h + 1) * HGRN_D, :])
        for h in range(N_Q_HEADS):
            g = h // GQA_GROUP
            a_h = att_scr[pl.ds(h, nb, stride=N_Q_HEADS), :][:, g * HALF:(g + 1) * HALF]
            y = y + _dot(a_h.astype(BF16), wout_ref[h * HEAD_DIM:(h + 1) * HEAD_DIM, :])
        y_ref[...] = x_ref[...] + _rms(y, gpost_ref[...])


def _sample_mix(x, sinks, g_pre, w_in, lb_raw, g_out_head, w_out, g_post, cache_kt, cache_vt, state_t, pos, layer):
    nb = x.shape[0]
    bt = nb // N_HGRN_HEADS
    cos, s_hi, s_lo = _rope_tables(pos)
    lb_rows = lb_raw.shape[0]
    blk3 = pl.BlockSpec((bt, D_KV, WINDOW), lambda i: (i, 0, 0))
    blk_s = pl.BlockSpec((HGRN_D, HGRN_D, nb), lambda i: (i, 0, 0))
    chan_major = pltpu.VMEM((D_HGRN, nb), F32)
    return pl.pallas_call(
        functools.partial(_sample_mix_kernel, layer=layer, nb=nb),
        grid=(N_HGRN_HEADS,),
        in_specs=[
            _const_spec((N_Q_HEADS, 1)),
            _const_spec((nb, D_MODEL)),
            _const_spec((1, D_MODEL)),
            _const_spec((D_MODEL, IN_COLS)),
            _const_spec((1, LANES)),
            _const_spec((1, LANES)),
            _const_spec((1, LANES)),
            _const_spec((lb_rows, D_HGRN)),
            _const_spec((HGRN_D, 1)),
            _const_spec((D_MODEL, D_MODEL)),
            _const_spec((1, D_MODEL)),
            blk3, blk3, blk_s,
        ],
        out_specs=[pl.BlockSpec((nb, D_MODEL), lambda i: (0, 0)), blk3, blk3, blk_s],
        out_shape=[
            jax.ShapeDtypeStruct((nb, D_MODEL), F32),
            jax.ShapeDtypeStruct(cache_kt.shape, F32),
            jax.ShapeDtypeStruct(cache_vt.shape, F32),
            jax.ShapeDtypeStruct(state_t.shape, F32),
        ],
        scratch_shapes=[
            pltpu.VMEM((nb, D_ATTN), F32),
            pltpu.VMEM((nb, D_KV), F32),
            pltpu.VMEM((nb, D_KV), F32),
            pltpu.VMEM((3, D_KV, nb), BF16),
            pltpu.VMEM((3, D_KV, nb), BF16),
            chan_major, chan_major, chan_major, chan_major, chan_major, chan_major,
            pltpu.VMEM((nb * N_Q_HEADS, LANES), F32),
        ],
        compiler_params=pltpu.CompilerParams(
            dimension_semantics=("arbitrary",), vmem_limit_bytes=VMEM_LIMIT),
        name="sample_mix",
    )(sinks.reshape(N_Q_HEADS, 1), x, g_pre.reshape(1, D_MODEL), w_in, cos, s_hi, s_lo, lb_raw,
      g_out_head.reshape(HGRN_D, 1), w_out, g_post.reshape(1, D_MODEL), cache_kt, cache_vt, state_t)


def kernel(x_prompt, x_sample, cache_win_k, cache_win_v, state_hgrn, ffn1_pre_g, ffn1_post_g, ffn1_w_gu,
           ffn1_w_down, mix_pre_g, mix_post_g, w_in, attn_sinks, hgrn_lb, hgrn_out_g, w_out, ffn2_pre_g,
           ffn2_post_g, ffn2_w_gu, ffn2_w_down):
    batch, seq, _ = x_prompt.shape
    nb, t_s, _ = x_sample.shape
    depth = w_in.shape[0]
    assert t_s == 1 and seq % (MIX_PARTS * MIX_TILE) == 0 and (batch * seq) % FFN_TILE == 0 and nb == LANES
    assert cache_win_k.shape[2:] == (WINDOW, N_KV_HEADS, HEAD_DIM)

    xp = x_prompt.reshape(batch * seq, D_MODEL)
    xs = x_sample.reshape(nb, D_MODEL)
    pos_s = PAST_LEN + jnp.arange(t_s, dtype=F32)
    outs = [[] for _ in range(6)]
    for l in range(depth):
        w_in_l, w_out_l = w_in[l].astype(BF16), w_out[l].astype(BF16)
        w_in_t = w_in[l][:, OFF_HI:].T.astype(BF16)

        xp, xs = _ffn(xp, xs, ffn1_pre_g[l], ffn1_post_g[l], ffn1_w_gu[l], ffn1_w_down[l])

        xp, wk_p, wv_p, s_p = _prompt_mix(xp, attn_sinks[l], mix_pre_g[l], w_in_l, w_in_t, hgrn_lb, hgrn_out_g[l],
                                          w_out_l, mix_post_g[l], batch, seq, l)
        to_kt = lambda c: jnp.transpose(c, (0, 2, 3, 1)).reshape(nb, D_KV, WINDOW)
        from_kt = lambda c: jnp.transpose(c.reshape(nb, N_KV_HEADS, HEAD_DIM, WINDOW), (0, 3, 1, 2))
        state_t = jnp.transpose(state_hgrn[l], (1, 2, 3, 0)).reshape(D_HGRN, HGRN_D, nb)
        xs, wkt_s, wvt_s, st_s = _sample_mix(
            xs, attn_sinks[l], mix_pre_g[l], w_in_l, hgrn_lb, hgrn_out_g[l], w_out_l, mix_post_g[l],
            to_kt(cache_win_k[l]), to_kt(cache_win_v[l]), state_t, pos_s, l)
        s_s = jnp.transpose(st_s.reshape(N_HGRN_HEADS, HGRN_D, HGRN_D, nb), (3, 0, 1, 2))

        xp, xs = _ffn(xp, xs, ffn2_pre_g[l], ffn2_post_g[l], ffn2_w_gu[l], ffn2_w_down[l])

        kv_shape = (WINDOW, N_KV_HEADS, HEAD_DIM)
        for lst, val in zip(outs, (wk_p.reshape(batch, *kv_shape), wv_p.reshape(batch, *kv_shape), s_p,
                                   from_kt(wkt_s), from_kt(wvt_s), s_s)):
            lst.append(val)

    return (xp.reshape(batch, seq, D_MODEL), xs.reshape(nb, t_s, D_MODEL)) + tuple(jnp.stack(o) for o in outs)
```

```python
import functools

import jax
import jax.numpy as jnp
import numpy as np
from jax import lax
from jax.experimental import pallas as pl
from jax.experimental.pallas import tpu as pltpu

F32 = jnp.float32
BF16 = jnp.bfloat16

D_MODEL = 1024
D_FF = 2816
HEAD_DIM = 64
N_Q_HEADS = 8
N_KV_HEADS = 2
GQA_GROUP = N_Q_HEADS // N_KV_HEADS
WINDOW = 128
PAST_LEN = 8192
ROT_DIM = HEAD_DIM // 4
ROPE_THETA = 500000.0
N_HGRN_HEADS = 8
HGRN_D = 64
D_ATTN = N_Q_HEADS * HEAD_DIM
D_KV = N_KV_HEADS * HEAD_DIM
D_HGRN = N_HGRN_HEADS * HGRN_D
IN_COLS = D_ATTN + 2 * D_KV + 4 * D_HGRN
OFF_Q, OFF_K, OFF_V = 0, D_ATTN, D_ATTN + D_KV
OFF_HQ = D_ATTN + 2 * D_KV
OFF_HF, OFF_HI, OFF_HG = OFF_HQ + D_HGRN, OFF_HQ + 2 * D_HGRN, OFF_HQ + 3 * D_HGRN
EPS = 1e-6
NEG_INF = -1e30
LOG2E = 1.4426950408889634
LANES = 128
HALF = LANES // 2

FFN_TILE = 1024
FFN_PARTS = 4
FFN_W_STEPS = 8
MIX_TILE = 256
MIX_PARTS = 4
MIX_SKEW = 0
CHUNK = 64
SUB = 16
N_SUB = CHUNK // SUB
VMEM_LIMIT = 56 * 1024 * 1024


def _rms(x, g):
    return x * lax.rsqrt(jnp.mean(x * x, axis=-1, keepdims=True) + EPS) * g


def _silu(x):
    return x * jax.nn.sigmoid(x)


def _dot(a, b):
    return jnp.dot(a, b, preferred_element_type=F32)


def _dot_nt(a, b):
    return lax.dot_general(a, b, (((1,), (1,)), ((), ())), preferred_element_type=F32)


def _dot_tn(a, b):
    return lax.dot_general(a, b, (((0,), (0,)), ((), ())), preferred_element_type=F32)


def _split3(x):
    hi = x.astype(BF16)
    r = x - hi.astype(F32)
    mid = r.astype(BF16)
    lo = (r - mid.astype(F32)).astype(BF16)
    return hi, mid, lo


def _const_spec(shape):
    nd = len(shape)
    return pl.BlockSpec(shape, lambda *_: (0,) * nd, pipeline_mode=pl.Buffered(1))


def _ffn_kernel(x_ref, xs_ref, gpre_ref, gpost_ref, wgu32_ref, wd32_ref, o_ref, os_ref, wgu_ref, wd_ref):
    step = pl.program_id(0)

    @pl.when(step < FFN_W_STEPS)
    def _():
        gu_rows, d_rows = wgu32_ref.shape[0], wd32_ref.shape[0]
        wgu_ref[pl.ds(pl.multiple_of(step * gu_rows, gu_rows), gu_rows), :] = wgu32_ref[...].astype(BF16)
        wd_ref[pl.ds(pl.multiple_of(step * d_rows, d_rows), d_rows), :] = wd32_ref[...].astype(BF16)

    def half_step(x):
        h = _rms(x, gpre_ref[...]).astype(BF16)
        gate = _dot(h, wgu_ref[:, :D_FF])
        up = _dot(h, wgu_ref[:, D_FF:])
        act = (_silu(gate) * up).astype(BF16)
        y = _dot(act, wd_ref[...])
        return x + 0.5 * _rms(y, gpost_ref[...])

    @pl.when(step >= FFN_W_STEPS)
    def _():
        rows = x_ref.shape[0] // FFN_PARTS
        for r in range(FFN_PARTS):
            rs = slice(r * rows, (r + 1) * rows)
            o_ref[rs, :] = half_step(x_ref[rs, :])

    @pl.when(step == pl.num_programs(0) - 1)
    def _():
        os_ref[...] = half_step(xs_ref[...])


def _ffn(x, xs, g_pre, g_post, w_gu, w_down):
    n, nb = x.shape[0], xs.shape[0]
    tile_map = lambda i: (jnp.maximum(i - FFN_W_STEPS, 0), 0)
    chunk_map = lambda i: (jnp.minimum(i, FFN_W_STEPS - 1), 0)
    return pl.pallas_call(
        _ffn_kernel,
        grid=(FFN_W_STEPS + n // FFN_TILE,),
        in_specs=[
            pl.BlockSpec((FFN_TILE, D_MODEL), tile_map),
            _const_spec((nb, D_MODEL)),
            _const_spec((1, D_MODEL)),
            _const_spec((1, D_MODEL)),
            pl.BlockSpec((D_MODEL // FFN_W_STEPS, 2 * D_FF), chunk_map),
            pl.BlockSpec((D_FF // FFN_W_STEPS, D_MODEL), chunk_map),
        ],
        out_specs=[pl.BlockSpec((FFN_TILE, D_MODEL), tile_map),
                   pl.BlockSpec((nb, D_MODEL), lambda i: (0, 0))],
        out_shape=[jax.ShapeDtypeStruct((n, D_MODEL), F32), jax.ShapeDtypeStruct((nb, D_MODEL), F32)],
        scratch_shapes=[pltpu.VMEM((D_MODEL, 2 * D_FF), BF16), pltpu.VMEM((D_FF, D_MODEL), BF16)],
        compiler_params=pltpu.CompilerParams(
            dimension_semantics=("arbitrary",), vmem_limit_bytes=VMEM_LIMIT),
        name="ffn",
    )(x, xs, g_pre.reshape(1, D_MODEL), g_post.reshape(1, D_MODEL), w_gu, w_down)


def _lower_bound(lb_raw, layer):
    m = jnp.max(lb_raw, axis=0, keepdims=True)
    e = jnp.exp(lb_raw - m)
    return jnp.sum(e[: layer + 1], axis=0, keepdims=True) / jnp.sum(e, axis=0, keepdims=True)


def _rope(x, cos, sin_hi, sin_lo):
    return x * cos + pltpu.roll(x, ROT_DIM // 2, 1) * sin_hi + pltpu.roll(x, LANES - ROT_DIM // 2, 1) * sin_lo


def _rope_tables(pos):
    half = ROT_DIM // 2
    inv = ROPE_THETA ** (-jnp.arange(half, dtype=F32) / half)
    ang = pos[:, None] * inv[None, :]
    cos, sin = jnp.cos(ang), jnp.sin(ang)
    t = pos.shape[0]
    one = jnp.ones((t, HEAD_DIM - ROT_DIM), F32)
    zero = jnp.zeros((t, HEAD_DIM - ROT_DIM), F32)
    zh = jnp.zeros((t, half), F32)
    c = jnp.concatenate([cos, cos, one], axis=1)
    s_hi = jnp.concatenate([zh, sin, zero], axis=1)
    s_lo = jnp.concatenate([-sin, zh, zero], axis=1)
    return tuple(jnp.tile(a, (1, LANES // HEAD_DIM)) for a in (c, s_hi, s_lo))


def _dup_half(x, g, lo_half):
    xr = pltpu.roll(x, HALF, 1)
    return jnp.where(lo_half, x, xr) if g == 0 else jnp.where(lo_half, xr, x)


def _mix_tile(x, rope_tab, seq_start, src, dst, g_scr,
              sink_ref, gpre_ref, win_ref, wint_ref, lb_ref, goutc_ref, wout_ref, gpost_ref, bias_ref, amask_ref,
              layer):
    tb = MIX_TILE
    u = _rms(x, gpre_ref[...]).astype(BF16)

    def proj(off, width):
        return _dot(u, win_ref[:, off:off + width])

    p_attn = proj(OFF_Q, D_ATTN + 2 * D_KV)
    p_hf = proj(OFF_HF, D_HGRN)
    cos, s_hi, s_lo = rope_tab
    scale = HEAD_DIM ** -0.5 * LOG2E
    q_cols = [_rope(p_attn[:, LANES * j: LANES * (j + 1)], cos, s_hi, s_lo) * scale
              for j in range(D_ATTN // LANES)]
    k_rot = _rope(p_attn[:, OFF_K:OFF_K + D_KV], cos, s_hi, s_lo)
    v_new = p_attn[:, OFF_V:OFF_V + D_KV]

    lane = lax.broadcasted_iota(jnp.int32, (WINDOW, LANES), 1)
    lo_half = lane < HALF
    bias = bias_ref[...]
    if seq_start is None:
        bias_first = bias
    else:
        no_prev = jnp.where(seq_start, NEG_INF, 0.0)
        bias_first = jnp.concatenate([bias[:WINDOW] + no_prev, bias[WINDOW:]], axis=0)

    vt_tile = v_new.T.astype(BF16)
    yield
    k_prev, vt_prev = src["k_prev"], src["vt_prev"]
    scores = []
    for i in range(tb // WINDOW):
        r0 = i * WINDOW
        k_cur = [_dup_half(k_rot[r0:r0 + WINDOW], g, lo_half).astype(BF16) for g in range(N_KV_HEADS)]
        vt_keys = jnp.concatenate([vt_prev, vt_tile[:, r0:r0 + WINDOW]], axis=1)
        for g in range(N_KV_HEADS):
            keys = jnp.concatenate([k_prev[g], k_cur[g]], axis=0)
            heads = range(g * GQA_GROUP, (g + 1) * GQA_GROUP)
            qg = jnp.concatenate(
                [jnp.where(lo_half if h % 2 == 0 else ~lo_half, q_cols[h // 2][r0:r0 + WINDOW], 0.0)
                 for h in heads], axis=0).astype(BF16)
            s = _dot_nt(keys, qg) + (bias_first if i == 0 else bias)
            scores.append((s, vt_keys[g * HEAD_DIM:(g + 1) * HEAD_DIM], heads))
        k_prev, vt_prev = k_cur, vt_tile[:, r0:r0 + WINDOW]
    dst["k_prev"], dst["vt_prev"] = k_prev, vt_prev
    yield

    hq = proj(OFF_HQ, D_HGRN) * (HGRN_D ** -0.5)
    hv_t = _dot_nt(wint_ref[:D_HGRN, :], u).astype(BF16)
    hg_act_t = _silu(_dot_nt(wint_ref[D_HGRN:, :], u))
    yield

    att_t = [[None] * (tb // WINDOW) for _ in range(N_Q_HEADS)]
    for idx, (s, vt_g, heads) in enumerate(scores):
        sink = jnp.concatenate(
            [jnp.full((1, WINDOW), sink_ref[h] * LOG2E, F32) for h in heads], axis=1)
        m = jnp.maximum(jnp.max(s, axis=0, keepdims=True), sink)
        p = jnp.exp2(s - m)
        denom = jnp.sum(p, axis=0, keepdims=True) + jnp.exp2(sink - m)
        o_t = _dot(vt_g, p.astype(BF16)) * (1.0 / denom)
        for j, h in enumerate(heads):
            att_t[h][idx // N_KV_HEADS] = o_t[:, j * WINDOW:(j + 1) * WINDOW]
        yield
    att = jnp.concatenate(
        [jnp.concatenate([jnp.concatenate([att_t[2 * j][i], att_t[2 * j + 1][i]], axis=0).T
                          for j in range(N_Q_HEADS // 2)], axis=1)
         for i in range(tb // WINDOW)], axis=0)
    y_att = _dot(att.astype(BF16), wout_ref[:D_ATTN, :])
    yield

    lb = _lower_bound(lb_ref[...], layer)
    f = lb + (1.0 - lb) * jax.nn.sigmoid(p_hf)
    logf = jnp.log(f)
    hk = 1.0 - f

    tr = lax.broadcasted_iota(jnp.int32, (tb, tb), 0)
    tc = lax.broadcasted_iota(jnp.int32, (tb, tb), 1)
    tri = ((tr // CHUNK == tc // CHUNK) & (tc <= tr)).astype(BF16)
    g_cum = sum(_dot(tri, part) for part in _split3(logf))
    g_scr[...] = g_cum
    yield

    def bcast_row(r, rows):
        return jnp.broadcast_to(g_scr[r:r + 1, :], (rows, D_HGRN))

    zeros_sub = jnp.zeros((SUB, D_HGRN), F32)
    g_ref_q = jnp.concatenate(
        [zeros_sub if sb % N_SUB == 0 else bcast_row(sb * SUB - 1, SUB) for sb in range(tb // SUB)], axis=0)
    q_loc = (hq * jnp.exp(g_cum - g_ref_q)).astype(BF16)
    g_end = jnp.concatenate(
        [bcast_row(c * CHUNK + CHUNK - 1, CHUNK) for c in range(tb // CHUNK)], axis=0)
    k_sub = []
    for i in range(N_SUB):
        live = (i + 1) * SUB
        pieces = []
        for c in range(tb // CHUNK):
            r0 = c * CHUNK
            g_ref_i = 0.0 if i == 0 else bcast_row(r0 + i * SUB - 1, live)
            pieces.append(hk[r0:r0 + live] * jnp.exp(g_ref_i - g_cum[r0:r0 + live]))
            if live < CHUNK:
                pieces.append(jnp.zeros((CHUNK - live, D_HGRN), F32))
        k_sub.append(jnp.concatenate(pieces, axis=0).astype(BF16))

    n_ch = tb // CHUNK
    gc = [g_scr[c * CHUNK + CHUNK - 1:c * CHUNK + CHUNK, :] for c in range(n_ch)]

    def span(lo, hi):
        if hi <= lo:
            return jnp.ones((CHUNK, D_HGRN), F32)
        return jnp.broadcast_to(jnp.exp(sum(gc[lo:hi])), (CHUNK, D_HGRN))

    zeros_chunk = jnp.zeros((CHUNK, D_HGRN), F32)
    k_end32 = hk * jnp.exp(g_end - g_cum)
    q_glob32 = hq * jnp.exp(g_cum)
    k_cross = [(k_end32 * jnp.concatenate(
        [span(cp + 1, c) if cp < c else zeros_chunk for cp in range(n_ch)], axis=0)).astype(BF16)
        for c in range(1, n_ch)]
    q_tile = (q_glob32 * jnp.concatenate([span(0, c) for c in range(n_ch)], axis=0)).astype(BF16)
    k_tile_end = (k_end32 * jnp.concatenate([span(cp + 1, n_ch) for cp in range(n_ch)], axis=0)).astype(BF16)
    decay_tile = jnp.exp(sum(gc))
    q_glob = q_glob32.astype(BF16)

    def keep_rows(a, block, wanted):
        zero = jnp.zeros((block, a.shape[1]), a.dtype)
        return jnp.concatenate(
            [a[r * block:(r + 1) * block] if wanted(r) else zero for r in range(a.shape[0] // block)], axis=0)

    q_sub = [keep_rows(q_loc, SUB, lambda r, i=i: r % N_SUB == i) for i in range(N_SUB)]
    q_cross = [keep_rows(q_glob, CHUNK, lambda r, c=c: r == c) for c in range(1, n_ch)]
    yield
    amask_t = amask_ref[...]
    y = y_att
    out_group = 2 * LANES // HGRN_D
    hn_t = []
    for h in range(N_HGRN_HEADS):
        ls = slice(h * HGRN_D, (h + 1) * HGRN_D)
        q_stack = jnp.concatenate([q_sub[i][:, ls] for i in range(N_SUB)], axis=1)
        k_stack = jnp.concatenate([k_sub[i][:, ls] for i in range(N_SUB)], axis=1)
        qc_stack = jnp.concatenate([q_cross[c - 1][:, ls] for c in range(1, n_ch)], axis=1)
        kc_stack = jnp.concatenate([k_cross[c - 1][:, ls] for c in range(1, n_ch)], axis=1)
        a_t = (_dot_nt(k_stack, q_stack) * amask_t + _dot_nt(kc_stack, qc_stack)).astype(BF16)
        vt_h = hv_t[ls, :]
        state_h = src["state"][h]
        o_t = _dot(vt_h, a_t) + _dot_nt(state_h.astype(BF16), q_tile[:, ls])
        dst["state"][h] = state_h * decay_tile[:, ls] + _dot(vt_h, k_tile_end[:, ls])
        o_t = o_t * lax.rsqrt(jnp.mean(o_t * o_t, axis=0, keepdims=True) + EPS)
        hn_t.append(o_t * goutc_ref[...] * hg_act_t[ls, :])
        if (h + 1) % out_group == 0:
            first = h + 1 - out_group
            grp = jnp.concatenate(
                [jnp.concatenate(
                    [jnp.concatenate(hn_t[j:j + 2], axis=0)[:, r:r + LANES].T for j in range(first, h + 1, 2)],
                    axis=1) for r in range(0, tb, LANES)], axis=0).astype(BF16)
            y = y + _dot(grp, wout_ref[D_ATTN + first * HGRN_D:D_ATTN + (h + 1) * HGRN_D, :])
        yield

    dst["y"], dst["k_rot"], dst["v_new"] = x + _rms(y, gpost_ref[...]), k_rot, v_new


def _prompt_mix_kernel(sink_ref, x_ref, gpre_ref, win_ref, wint_ref, cos_ref, shi_ref, slo_ref, lb_ref, goutc_ref,
                       wout_ref, gpost_ref, bias_ref, amask_ref,
                       y_ref, wk_ref, wv_ref, s_ref,
                       kk_scr, vt_scr, st_scr, g_scr, *, layer):
    tb = MIX_TILE
    step = pl.program_id(1)
    last = pl.num_programs(1) - 1

    @pl.when(step == 0)
    def _():
        kk_scr[...] = jnp.zeros_like(kk_scr)
        vt_scr[...] = jnp.zeros_like(vt_scr)
        st_scr[...] = jnp.zeros_like(st_scr)

    hand = [{"state": [None] * N_HGRN_HEADS} for _ in range(MIX_PARTS + 1)]
    hand[0]["k_prev"] = [kk_scr[g] for g in range(N_KV_HEADS)]
    hand[0]["vt_prev"] = vt_scr[...]
    hand[0]["state"] = [st_scr[h] for h in range(N_HGRN_HEADS)]
    tiles = []
    for part in range(MIX_PARTS):
        rs = slice(part * tb, (part + 1) * tb)
        tiles.append(_mix_tile(
            x_ref[rs, :], (cos_ref[rs, :], shi_ref[rs, :], slo_ref[rs, :]), step == 0 if part == 0 else None,
            hand[part], hand[part + 1], g_scr.at[part],
            sink_ref, gpre_ref, win_ref, wint_ref, lb_ref, goutc_ref, wout_ref, gpost_ref, bias_ref, amask_ref,
            layer))
    for lead, t in enumerate(tiles):
        for _ in range(MIX_SKEW * (len(tiles) - 1 - lead)):
            next(t)
    while tiles:
        tiles = [t for t in tiles if next(t, True) is None]
    for part in range(MIX_PARTS):
        y_ref[part * tb:(part + 1) * tb, :] = hand[part + 1]["y"]
    final = hand[MIX_PARTS]
    for g in range(N_KV_HEADS):
        kk_scr[g] = final["k_prev"][g]
    vt_scr[...] = final["vt_prev"]
    for h in range(N_HGRN_HEADS):
        st_scr[h] = final["state"][h]

    @pl.when(step == last)
    def _():
        wk_ref[0] = final["k_rot"][tb - WINDOW:]
        wv_ref[0] = final["v_new"][tb - WINDOW:]
        for h in range(N_HGRN_HEADS):
            s_ref[0, h] = final["state"][h].T


def _prompt_mix(x, sinks, g_pre, w_in, w_in_t, lb_raw, g_out_head, w_out, g_post, batch, seq, layer):
    tb = MIX_TILE
    rows = MIX_PARTS * tb
    nt = seq // rows
    cos, s_hi, s_lo = _rope_tables(jnp.arange(seq, dtype=F32))
    tok = lambda b, n: (b * nt + n, 0)
    tab = lambda b, n: (n, 0)
    per_b3 = lambda b, n: (b, 0, 0)
    lb_rows = lb_raw.shape[0]
    key_i = jnp.arange(2 * WINDOW)[:, None]
    rel = jnp.arange(WINDOW)[None, :] + WINDOW - key_i
    bias = jnp.tile(jnp.where((rel >= 0) & (rel < WINDOW), 0.0, NEG_INF).astype(F32), (1, GQA_GROUP))
    t_i = jnp.arange(tb)
    amask = ((t_i[:, None] // CHUNK == t_i[None, :] // CHUNK) & (t_i[:, None] <= t_i[None, :])).astype(F32)
    return pl.pallas_call(
        functools.partial(_prompt_mix_kernel, layer=layer),
        grid=(batch, nt),
        in_specs=[
            pl.BlockSpec(memory_space=pltpu.SMEM),
            pl.BlockSpec((rows, D_MODEL), tok),
            _const_spec((1, D_MODEL)),
            _const_spec((D_MODEL, IN_COLS)),
            _const_spec((2 * D_HGRN, D_MODEL)),
            pl.BlockSpec((rows, LANES), tab),
            pl.BlockSpec((rows, LANES), tab),
            pl.BlockSpec((rows, LANES), tab),
            _const_spec((lb_rows, D_HGRN)),
            _const_spec((HGRN_D, 1)),
            _const_spec((D_MODEL, D_MODEL)),
            _const_spec((1, D_MODEL)),
            _const_spec((2 * WINDOW, GQA_GROUP * WINDOW)),
            _const_spec((tb, tb)),
        ],
        out_specs=[
            pl.BlockSpec((rows, D_MODEL), tok),
            pl.BlockSpec((1, WINDOW, D_KV), per_b3),
            pl.BlockSpec((1, WINDOW, D_KV), per_b3),
            pl.BlockSpec((1, N_HGRN_HEADS, HGRN_D, HGRN_D), lambda b, n: (b, 0, 0, 0)),
        ],
        out_shape=[
            jax.ShapeDtypeStruct((batch * seq, D_MODEL), F32),
            jax.ShapeDtypeStruct((batch, WINDOW, D_KV), F32),
            jax.ShapeDtypeStruct((batch, WINDOW, D_KV), F32),
            jax.ShapeDtypeStruct((batch, N_HGRN_HEADS, HGRN_D, HGRN_D), F32),
        ],
        scratch_shapes=[
            pltpu.VMEM((N_KV_HEADS, WINDOW, LANES), BF16),
            pltpu.VMEM((D_KV, WINDOW), BF16),
            pltpu.VMEM((N_HGRN_HEADS, HGRN_D, HGRN_D), F32),
            pltpu.VMEM((MIX_PARTS, tb, D_HGRN), F32),
        ],
        compiler_params=pltpu.CompilerParams(
            dimension_semantics=("arbitrary", "arbitrary"), vmem_limit_bytes=VMEM_LIMIT),
        name="prompt_mix",
    )(sinks, x, g_pre.reshape(1, D_MODEL), w_in, w_in_t, cos, s_hi, s_lo, lb_raw, g_out_head.reshape(HGRN_D, 1),
      w_out, g_post.reshape(1, D_MODEL), bias, amask)


def _sample_mix_kernel(sink_ref, x_ref, gpre_ref, win_ref, cos_ref, shi_ref, slo_ref, lb_ref, goutc_ref,
                       wout_ref, gpost_ref, ckt_ref, cvt_ref, sin_ref,
                       y_ref, wkt_ref, wvt_ref, sout_ref,
                       q_scr, kn_scr, vn_scr, knt_scr, vnt_scr, ft_scr, hkt_scr, hqt_scr, hvt_scr, hgt_scr,
                       ot_scr, att_scr, *, layer, nb):
    step = pl.program_id(0)
    last = pl.num_programs(0) - 1
    bt = nb // N_HGRN_HEADS

    @pl.when(step == 0)
    def _():
        u = _rms(x_ref[...], gpre_ref[...]).astype(BF16)
        proj = _dot(u, win_ref[...])
        cos, s_hi, s_lo = cos_ref[...], shi_ref[...], slo_ref[...]
        scale = HEAD_DIM ** -0.5
        for j in range(D_ATTN // LANES):
            q_scr[:, LANES * j:LANES * (j + 1)] = _rope(
                proj[:, OFF_Q + LANES * j:OFF_Q + LANES * (j + 1)], cos, s_hi, s_lo) * scale
        k_new = _rope(proj[:, OFF_K:OFF_K + D_KV], cos, s_hi, s_lo)
        v_new = proj[:, OFF_V:OFF_V + D_KV]
        kn_scr[...] = k_new
        vn_scr[...] = v_new
        for scr, val in ((knt_scr, k_new), (vnt_scr, v_new)):
            for i, part in enumerate(_split3(val.T)):
                scr[i] = part
        lb = _lower_bound(lb_ref[...], layer)
        f_t = (lb + (1.0 - lb) * jax.nn.sigmoid(proj[:, OFF_HF:OFF_HF + D_HGRN])).T
        ft_scr[...] = f_t
        hkt_scr[...] = 1.0 - f_t
        hqt_scr[...] = (proj[:, OFF_HQ:OFF_HQ + D_HGRN] * (HGRN_D ** -0.5)).T
        hvt_scr[...] = proj[:, OFF_HI:OFF_HI + D_HGRN].T
        hgt_scr[...] = proj[:, OFF_HG:OFF_HG + D_HGRN].T

    base = pl.multiple_of(step * HGRN_D, HGRN_D)
    hv_t = hvt_scr[pl.ds(base, HGRN_D), :]

    def hgrn_row(k, o_acc):
        f_row = ft_scr[pl.ds(base + k, 1), :]
        s_new = f_row * sin_ref[k] + hkt_scr[pl.ds(base + k, 1), :] * hv_t
        sout_ref[k] = s_new
        return o_acc + hqt_scr[pl.ds(base + k, 1), :] * s_new

    ot_scr[pl.ds(base, HGRN_D), :] = lax.fori_loop(
        0, HGRN_D, hgrn_row, jnp.zeros((HGRN_D, nb), F32), unroll=8)

    lane8 = lax.broadcasted_iota(jnp.int32, (N_Q_HEADS, LANES), 1)
    row8 = lax.broadcasted_iota(jnp.int32, (N_Q_HEADS, LANES), 0)
    keep8 = (lane8 >= HALF) == (row8 >= GQA_GROUP)
    win_lane = lax.broadcasted_iota(jnp.int32, (D_KV, WINDOW), 1)
    sink = sink_ref[...]
    b0 = step * bt
    sel = (lax.broadcasted_iota(jnp.int32, (nb, bt), 0)
           == b0 + lax.broadcasted_iota(jnp.int32, (nb, bt), 1)).astype(BF16)
    k_cols = sum(_dot(knt_scr[i], sel) for i in range(3))
    v_cols = sum(_dot(vnt_scr[i], sel) for i in range(3))
    for bi in range(bt):
        b = b0 + bi
        q_b = jnp.broadcast_to(q_scr[pl.ds(b, 1), :], (N_Q_HEADS, D_ATTN))
        qm = jnp.zeros((N_Q_HEADS, LANES), F32)
        for h in range(N_Q_HEADS):
            c = q_b[:, LANES * (h // 2):LANES * (h // 2 + 1)]
            if h % 2 != h // GQA_GROUP:
                c = pltpu.roll(c, HALF, 1)
            qm = jnp.where(row8 == h, c, qm)
        qm = jnp.where(keep8, qm, 0.0)
        k_new = kn_scr[pl.ds(b, 1), :]
        v_new = vn_scr[pl.ds(b, 1), :]
        k_old, v_old = ckt_ref[bi], cvt_ref[bi]
        s = _dot(qm.astype(BF16), k_old.astype(BF16))
        s = jnp.where(lane8 >= 1, s, NEG_INF)
        s_new = jnp.sum(qm * k_new, axis=-1, keepdims=True)
        m = jnp.maximum(jnp.maximum(jnp.max(s, axis=-1, keepdims=True), s_new), sink)
        p = jnp.exp(s - m)
        p_new = jnp.exp(s_new - m)
        denom = jnp.sum(p, axis=-1, keepdims=True) + p_new + jnp.exp(sink - m)
        o = (_dot_nt(p.astype(BF16), v_old.astype(BF16)) + p_new * v_new) / denom
        att_scr[pl.ds(pl.multiple_of(b * N_Q_HEADS, N_Q_HEADS), N_Q_HEADS), :] = o
        wkt_ref[bi] = jnp.where(win_lane == WINDOW - 1, k_cols[:, bi:bi + 1], pltpu.roll(k_old, WINDOW - 1, 1))
        wvt_ref[bi] = jnp.where(win_lane == WINDOW - 1, v_cols[:, bi:bi + 1], pltpu.roll(v_old, WINDOW - 1, 1))

    @pl.when(step == last)
    def _():
        y = jnp.zeros((nb, D_MODEL), F32)
        for h in range(N_HGRN_HEADS):
            ks = slice(h * HGRN_D, (h + 1) * HGRN_D)
            o_t = ot_scr[ks, :]
            hn_t = (o_t * lax.rsqrt(jnp.mean(o_t * o_t, axis=0, keepdims=True) + EPS)
                    * goutc_ref[...] * _silu(hgt_scr[ks, :]))
            y = y + _dot_tn(hn_t.astype(BF16), wout_ref[D_ATTN + h * HGRN_D:D_ATTN + (h + 1) * HGRN_D, :])
        for h in range(N_Q_HEADS):
            g = h // GQA_GROUP
            a_h = att_scr[pl.ds(h, nb, stride=N_Q_HEADS), :][:, g * HALF:(g + 1) * HALF]
            y = y + _dot(a_h.astype(BF16), wout_ref[h * HEAD_DIM:(h + 1) * HEAD_DIM, :])
        y_ref[...] = x_ref[...] + _rms(y, gpost_ref[...])


def _sample_mix(x, sinks, g_pre, w_in, lb_raw, g_out_head, w_out, g_post, cache_kt, cache_vt, state_t, pos, layer):
    nb = x.shape[0]
    bt = nb // N_HGRN_HEADS
    cos, s_hi, s_lo = _rope_tables(pos)
    lb_rows = lb_raw.shape[0]
    blk3 = pl.BlockSpec((bt, D_KV, WINDOW), lambda i: (i, 0, 0))
    blk_s = pl.BlockSpec((HGRN_D, HGRN_D, nb), lambda i: (i, 0, 0))
    chan_major = pltpu.VMEM((D_HGRN, nb), F32)
    return pl.pallas_call(
        functools.partial(_sample_mix_kernel, layer=layer, nb=nb),
        grid=(N_HGRN_HEADS,),
        in_specs=[
            _const_spec((N_Q_HEADS, 1)),
            _const_spec((nb, D_MODEL)),
            _const_spec((1, D_MODEL)),
            _const_spec((D_MODEL, IN_COLS)),
            _const_spec((1, LANES)),
            _const_spec((1, LANES)),
            _const_spec((1, LANES)),
            _const_spec((lb_rows, D_HGRN)),
            _const_spec((HGRN_D, 1)),
            _const_spec((D_MODEL, D_MODEL)),
            _const_spec((1, D_MODEL)),
            blk3, blk3, blk_s,
        ],
        out_specs=[pl.BlockSpec((nb, D_MODEL), lambda i: (0, 0)), blk3, blk3, blk_s],
        out_shape=[
            jax.ShapeDtypeStruct((nb, D_MODEL), F32),
            jax.ShapeDtypeStruct(cache_kt.shape, F32),
            jax.ShapeDtypeStruct(cache_vt.shape, F32),
            jax.ShapeDtypeStruct(state_t.shape, F32),
        ],
        scratch_shapes=[
            pltpu.VMEM((nb, D_ATTN), F32),
            pltpu.VMEM((nb, D_KV), F32),
            pltpu.VMEM((nb, D_KV), F32),
            pltpu.VMEM((3, D_KV, nb), BF16),
            pltpu.VMEM((3, D_KV, nb), BF16),
            chan_major, chan_major, chan_major, chan_major, chan_major, chan_major,
            pltpu.VMEM((nb * N_Q_HEADS, LANES), F32),
        ],
        compiler_params=pltpu.CompilerParams(
            dimension_semantics=("arbitrary",), vmem_limit_bytes=VMEM_LIMIT),
        name="sample_mix",
    )(sinks.reshape(N_Q_HEADS, 1), x, g_pre.reshape(1, D_MODEL), w_in, cos, s_hi, s_lo, lb_raw,
      g_out_head.reshape(HGRN_D, 1), w_out, g_post.reshape(1, D_MODEL), cache_kt, cache_vt, state_t)


def kernel(x_prompt, x_sample, cache_win_k, cache_win_v, state_hgrn, ffn1_pre_g, ffn1_post_g, ffn1_w_gu,
           ffn1_w_down, mix_pre_g, mix_post_g, w_in, attn_sinks, hgrn_lb, hgrn_out_g, w_out, ffn2_pre_g,
           ffn2_post_g, ffn2_w_gu, ffn2_w_down):
    batch, seq, _ = x_prompt.shape
    nb, t_s, _ = x_sample.shape
    depth = w_in.shape[0]
    assert t_s == 1 and seq % (MIX_PARTS * MIX_TILE) == 0 and (batch * seq) % FFN_TILE == 0 and nb == LANES
    assert cache_win_k.shape[2:] == (WINDOW, N_KV_HEADS, HEAD_DIM)

    xp = x_prompt.reshape(batch * seq, D_MODEL)
    xs = x_sample.reshape(nb, D_MODEL)
    pos_s = PAST_LEN + jnp.arange(t_s, dtype=F32)
    outs = [[] for _ in range(6)]
    for l in range(depth):
        w_in_l, w_out_l = w_in[l].astype(BF16), w_out[l].astype(BF16)
        w_in_t = w_in[l][:, OFF_HI:].T.astype(BF16)

        xp, xs = _ffn(xp, xs, ffn1_pre_g[l], ffn1_post_g[l], ffn1_w_gu[l], ffn1_w_down[l])

        xp, wk_p, wv_p, s_p = _prompt_mix(xp, attn_sinks[l], mix_pre_g[l], w_in_l, w_in_t, hgrn_lb, hgrn_out_g[l],
                                          w_out_l, mix_post_g[l], batch, seq, l)
        to_kt = lambda c: jnp.transpose(c, (0, 2, 3, 1)).reshape(nb, D_KV, WINDOW)
        from_kt = lambda c: jnp.transpose(c.reshape(nb, N_KV_HEADS, HEAD_DIM, WINDOW), (0, 3, 1, 2))
        state_t = jnp.transpose(state_hgrn[l], (1, 2, 3, 0)).reshape(D_HGRN, HGRN_D, nb)
        xs, wkt_s, wvt_s, st_s = _sample_mix(
            xs, attn_sinks[l], mix_pre_g[l], w_in_l, hgrn_lb, hgrn_out_g[l], w_out_l, mix_post_g[l],
            to_kt(cache_win_k[l]), to_kt(cache_win_v[l]), state_t, pos_s, l)
        s_s = jnp.transpose(st_s.reshape(N_HGRN_HEADS, HGRN_D, HGRN_D, nb), (3, 0, 1, 2))

        xp, xs = _ffn(xp, xs, ffn2_pre_g[l], ffn2_post_g[l], ffn2_w_gu[l], ffn2_w_down[l])

        kv_shape = (WINDOW, N_KV_HEADS, HEAD_DIM)
        for lst, val in zip(outs, (wk_p.reshape(batch, *kv_shape), wv_p.reshape(batch, *kv_shape), s_p,
                                   from_kt(wkt_s), from_kt(wvt_s), s_s)):
            lst.append(val)

    return (xp.reshape(batch, seq, D_MODEL), xs.reshape(nb, t_s, D_MODEL)) + tuple(jnp.stack(o) for o in outs)
```

```python
import functools

import jax
import jax.numpy as jnp
import numpy as np
from jax import lax
from jax.experimental import pallas as pl
from jax.experimental.pallas import tpu as pltpu

F32 = jnp.float32
BF16 = jnp.bfloat16

D_MODEL = 1024
D_FF = 2816
HEAD_DIM = 64
N_Q_HEADS = 8
N_KV_HEADS = 2
GQA_GROUP = N_Q_HEADS // N_KV_HEADS
WINDOW = 128
PAST_LEN = 8192
ROT_DIM = HEAD_DIM // 4
ROPE_THETA = 500000.0
N_HGRN_HEADS = 8
HGRN_D = 64
D_ATTN = N_Q_HEADS * HEAD_DIM
D_KV = N_KV_HEADS * HEAD_DIM
D_HGRN = N_HGRN_HEADS * HGRN_D
IN_COLS = D_ATTN + 2 * D_KV + 4 * D_HGRN
OFF_Q, OFF_K, OFF_V = 0, D_ATTN, D_ATTN + D_KV
OFF_HQ = D_ATTN + 2 * D_KV
OFF_HF, OFF_HI, OFF_HG = OFF_HQ + D_HGRN, OFF_HQ + 2 * D_HGRN, OFF_HQ + 3 * D_HGRN
EPS = 1e-6
NEG_INF = -1e30
LOG2E = 1.4426950408889634
LANES = 128
HALF = LANES // 2

FFN_TILE = 1024
FFN_PARTS = 4
FFN_W_STEPS = 8
MIX_TILE = 256
MIX_PARTS = 4
MIX_SKEW = 0
CHUNK = 64
SUB = 16
N_SUB = CHUNK // SUB
VMEM_LIMIT = 56 * 1024 * 1024


def _rms(x, g):
    return x * lax.rsqrt(jnp.mean(x * x, axis=-1, keepdims=True) + EPS) * g


def _silu(x):
    return x * jax.nn.sigmoid(x)


def _dot(a, b):
    return jnp.dot(a, b, preferred_element_type=F32)


def _dot_nt(a, b):
    return lax.dot_general(a, b, (((1,), (1,)), ((), ())), preferred_element_type=F32)


def _dot_tn(a, b):
    return lax.dot_general(a, b, (((0,), (0,)), ((), ())), preferred_element_type=F32)


def _split3(x):
    hi = x.astype(BF16)
    r = x - hi.astype(F32)
    mid = r.astype(BF16)
    lo = (r - mid.astype(F32)).astype(BF16)
    return hi, mid, lo


def _const_spec(shape):
    nd = len(shape)
    return pl.BlockSpec(shape, lambda *_: (0,) * nd, pipeline_mode=pl.Buffered(1))


def _ffn_kernel(x_ref, xs_ref, gpre_ref, gpost_ref, wgu32_ref, wd32_ref, o_ref, os_ref, wgu_ref, wd_ref):
    step = pl.program_id(0)

    @pl.when(step < FFN_W_STEPS)
    def _():
        gu_rows, d_rows = wgu32_ref.shape[0], wd32_ref.shape[0]
        wgu_ref[pl.ds(pl.multiple_of(step * gu_rows, gu_rows), gu_rows), :] = wgu32_ref[...].astype(BF16)
        wd_ref[pl.ds(pl.multiple_of(step * d_rows, d_rows), d_rows), :] = wd32_ref[...].astype(BF16)

    def half_step(x):
        h = _rms(x, gpre_ref[...]).astype(BF16)
        gate = _dot(h, wgu_ref[:, :D_FF])
        up = _dot(h, wgu_ref[:, D_FF:])
        act = (_silu(gate) * up).astype(BF16)
        y = _dot(act, wd_ref[...])
        return x + 0.5 * _rms(y, gpost_ref[...])

    @pl.when(step >= FFN_W_STEPS)
    def _():
        rows = x_ref.shape[0] // FFN_PARTS
        for r in range(FFN_PARTS):
            rs = slice(r * rows, (r + 1) * rows)
            o_ref[rs, :] = half_step(x_ref[rs, :])

    @pl.when(step == pl.num_programs(0) - 1)
    def _():
        os_ref[...] = half_step(xs_ref[...])


def _ffn(x, xs, g_pre, g_post, w_gu, w_down):
    n, nb = x.shape[0], xs.shape[0]
    tile_map = lambda i: (jnp.maximum(i - FFN_W_STEPS, 0), 0)
    chunk_map = lambda i: (jnp.minimum(i, FFN_W_STEPS - 1), 0)
    return pl.pallas_call(
        _ffn_kernel,
        grid=(FFN_W_STEPS + n // FFN_TILE,),
        in_specs=[
            pl.BlockSpec((FFN_TILE, D_MODEL), tile_map),
            _const_spec((nb, D_MODEL)),
            _const_spec((1, D_MODEL)),
            _const_spec((1, D_MODEL)),
            pl.BlockSpec((D_MODEL // FFN_W_STEPS, 2 * D_FF), chunk_map),
            pl.BlockSpec((D_FF // FFN_W_STEPS, D_MODEL), chunk_map),
        ],
        out_specs=[pl.BlockSpec((FFN_TILE, D_MODEL), tile_map),
                   pl.BlockSpec((nb, D_MODEL), lambda i: (0, 0))],
        out_shape=[jax.ShapeDtypeStruct((n, D_MODEL), F32), jax.ShapeDtypeStruct((nb, D_MODEL), F32)],
        scratch_shapes=[pltpu.VMEM((D_MODEL, 2 * D_FF), BF16), pltpu.VMEM((D_FF, D_MODEL), BF16)],
        compiler_params=pltpu.CompilerParams(
            dimension_semantics=("arbitrary",), vmem_limit_bytes=VMEM_LIMIT),
        name="ffn",
    )(x, xs, g_pre.reshape(1, D_MODEL), g_post.reshape(1, D_MODEL), w_gu, w_down)


def _lower_bound(lb_raw, layer):
    m = jnp.max(lb_raw, axis=0, keepdims=True)
    e = jnp.exp(lb_raw - m)
    return jnp.sum(e[: layer + 1], axis=0, keepdims=True) / jnp.sum(e, axis=0, keepdims=True)


def _rope(x, cos, sin_hi, sin_lo):
    return x * cos + pltpu.roll(x, ROT_DIM // 2, 1) * sin_hi + pltpu.roll(x, LANES - ROT_DIM // 2, 1) * sin_lo


def _rope_tables(pos):
    half = ROT_DIM // 2
    inv = (np.float32(ROPE_THETA) ** (-np.arange(half, dtype=np.float32) / half)).astype(np.float64)
    ang = np.asarray(pos, np.float64)[:, None] * inv[None, :]
    cos, sin = np.cos(ang), np.sin(ang)
    t = ang.shape[0]
    one = np.ones((t, HEAD_DIM - ROT_DIM))
    zero = np.zeros((t, HEAD_DIM - ROT_DIM))
    zh = np.zeros((t, half))
    c = np.concatenate([cos, cos, one], axis=1)
    s_hi = np.concatenate([zh, sin, zero], axis=1)
    s_lo = np.concatenate([-sin, zh, zero], axis=1)
    return tuple(jnp.asarray(np.tile(a, (1, LANES // HEAD_DIM)), F32) for a in (c, s_hi, s_lo))


def _dup_half(x, g, lo_half):
    xr = pltpu.roll(x, HALF, 1)
    return jnp.where(lo_half, x, xr) if g == 0 else jnp.where(lo_half, xr, x)


def _mix_tile(x, rope_tab, seq_start, src, dst, g_scr,
              sink_ref, gpre_ref, win_ref, wint_ref, lb_ref, goutc_ref, wout_ref, gpost_ref, bias_ref, amask_ref,
              layer):
    tb = MIX_TILE
    u = _rms(x, gpre_ref[...]).astype(BF16)

    def proj(off, width):
        return _dot(u, win_ref[:, off:off + width])

    p_attn = proj(OFF_Q, D_ATTN + 2 * D_KV)
    p_hf = proj(OFF_HF, D_HGRN)
    cos, s_hi, s_lo = rope_tab
    scale = HEAD_DIM ** -0.5 * LOG2E
    q_cols = [_rope(p_attn[:, LANES * j: LANES * (j + 1)], cos, s_hi, s_lo) * scale
              for j in range(D_ATTN // LANES)]
    k_rot = _rope(p_attn[:, OFF_K:OFF_K + D_KV], cos, s_hi, s_lo)
    v_new = p_attn[:, OFF_V:OFF_V + D_KV]

    lane = lax.broadcasted_iota(jnp.int32, (WINDOW, LANES), 1)
    lo_half = lane < HALF
    bias = bias_ref[...]
    if seq_start is None:
        bias_first = bias
    else:
        no_prev = jnp.where(seq_start, NEG_INF, 0.0)
        bias_first = jnp.concatenate([bias[:WINDOW] + no_prev, bias[WINDOW:]], axis=0)

    vt_tile = v_new.T.astype(BF16)
    yield
    k_prev, vt_prev = src["k_prev"], src["vt_prev"]
    scores = []
    for i in range(tb // WINDOW):
        r0 = i * WINDOW
        k_cur = [_dup_half(k_rot[r0:r0 + WINDOW], g, lo_half).astype(BF16) for g in range(N_KV_HEADS)]
        vt_keys = jnp.concatenate([vt_prev, vt_tile[:, r0:r0 + WINDOW]], axis=1)
        for g in range(N_KV_HEADS):
            keys = jnp.concatenate([k_prev[g], k_cur[g]], axis=0)
            heads = range(g * GQA_GROUP, (g + 1) * GQA_GROUP)
            qg = jnp.concatenate(
                [jnp.where(lo_half if h % 2 == 0 else ~lo_half, q_cols[h // 2][r0:r0 + WINDOW], 0.0)
                 for h in heads], axis=0).astype(BF16)
            s = _dot_nt(keys, qg) + (bias_first if i == 0 else bias)
            scores.append((s, vt_keys[g * HEAD_DIM:(g + 1) * HEAD_DIM], heads))
        k_prev, vt_prev = k_cur, vt_tile[:, r0:r0 + WINDOW]
    dst["k_prev"], dst["vt_prev"] = k_prev, vt_prev
    yield

    hq = proj(OFF_HQ, D_HGRN) * (HGRN_D ** -0.5)
    hv_t = _dot_nt(wint_ref[:D_HGRN, :], u).astype(BF16)
    hg_act_t = _silu(_dot_nt(wint_ref[D_HGRN:, :], u))
    yield

    att_t = [[None] * (tb // WINDOW) for _ in range(N_Q_HEADS)]
    for idx, (s, vt_g, heads) in enumerate(scores):
        sink = jnp.concatenate(
            [jnp.full((1, WINDOW), sink_ref[h] * LOG2E, F32) for h in heads], axis=1)
        m = jnp.maximum(jnp.max(s, axis=0, keepdims=True), sink)
        p = jnp.exp2(s - m)
        denom = jnp.sum(p, axis=0, keepdims=True) + jnp.exp2(sink - m)
        o_t = _dot(vt_g, p.astype(BF16)) * (1.0 / denom)
        for j, h in enumerate(heads):
            att_t[h][idx // N_KV_HEADS] = o_t[:, j * WINDOW:(j + 1) * WINDOW]
        yield
    att = jnp.concatenate(
        [jnp.concatenate([jnp.concatenate([att_t[2 * j][i], att_t[2 * j + 1][i]], axis=0).T
                          for j in range(N_Q_HEADS // 2)], axis=1)
         for i in range(tb // WINDOW)], axis=0)
    y_att = _dot(att.astype(BF16), wout_ref[:D_ATTN, :])
    yield

    lb = _lower_bound(lb_ref[...], layer)
    f = lb + (1.0 - lb) * jax.nn.sigmoid(p_hf)
    logf = jnp.log(f)
    hk = 1.0 - f

    tr = lax.broadcasted_iota(jnp.int32, (tb, tb), 0)
    tc = lax.broadcasted_iota(jnp.int32, (tb, tb), 1)
    tri = ((tr // CHUNK == tc // CHUNK) & (tc <= tr)).astype(BF16)
    g_cum = sum(_dot(tri, part) for part in _split3(logf))
    g_scr[...] = g_cum
    yield

    def bcast_row(r, rows):
        return jnp.broadcast_to(g_scr[r:r + 1, :], (rows, D_HGRN))

    zeros_sub = jnp.zeros((SUB, D_HGRN), F32)
    g_ref_q = jnp.concatenate(
        [zeros_sub if sb % N_SUB == 0 else bcast_row(sb * SUB - 1, SUB) for sb in range(tb // SUB)], axis=0)
    q_loc = (hq * jnp.exp(g_cum - g_ref_q)).astype(BF16)
    g_end = jnp.concatenate(
        [bcast_row(c * CHUNK + CHUNK - 1, CHUNK) for c in range(tb // CHUNK)], axis=0)
    k_sub = []
    for i in range(N_SUB):
        live = (i + 1) * SUB
        pieces = []
        for c in range(tb // CHUNK):
            r0 = c * CHUNK
            g_ref_i = 0.0 if i == 0 else bcast_row(r0 + i * SUB - 1, live)
            pieces.append(hk[r0:r0 + live] * jnp.exp(g_ref_i - g_cum[r0:r0 + live]))
            if live < CHUNK:
                pieces.append(jnp.zeros((CHUNK - live, D_HGRN), F32))
        k_sub.append(jnp.concatenate(pieces, axis=0).astype(BF16))

    n_ch = tb // CHUNK
    gc = [g_scr[c * CHUNK + CHUNK - 1:c * CHUNK + CHUNK, :] for c in range(n_ch)]

    def span(lo, hi):
        if hi <= lo:
            return jnp.ones((CHUNK, D_HGRN), F32)
        return jnp.broadcast_to(jnp.exp(sum(gc[lo:hi])), (CHUNK, D_HGRN))

    zeros_chunk = jnp.zeros((CHUNK, D_HGRN), F32)
    k_end32 = hk * jnp.exp(g_end - g_cum)
    q_glob32 = hq * jnp.exp(g_cum)
    k_cross = [(k_end32 * jnp.concatenate(
        [span(cp + 1, c) if cp < c else zeros_chunk for cp in range(n_ch)], axis=0)).astype(BF16)
        for c in range(1, n_ch)]
    q_tile = (q_glob32 * jnp.concatenate([span(0, c) for c in range(n_ch)], axis=0)).astype(BF16)
    k_tile_end = (k_end32 * jnp.concatenate([span(cp + 1, n_ch) for cp in range(n_ch)], axis=0)).astype(BF16)
    decay_tile = jnp.exp(sum(gc))
    q_glob = q_glob32.astype(BF16)

    def keep_rows(a, block, wanted):
        zero = jnp.zeros((block, a.shape[1]), a.dtype)
        return jnp.concatenate(
            [a[r * block:(r + 1) * block] if wanted(r) else zero for r in range(a.shape[0] // block)], axis=0)

    q_sub = [keep_rows(q_loc, SUB, lambda r, i=i: r % N_SUB == i) for i in range(N_SUB)]
    q_cross = [keep_rows(q_glob, CHUNK, lambda r, c=c: r == c) for c in range(1, n_ch)]
    yield
    amask_t = amask_ref[...]
    y = y_att
    out_group = 2 * LANES // HGRN_D
    hn_t = []
    for h in range(N_HGRN_HEADS):
        ls = slice(h * HGRN_D, (h + 1) * HGRN_D)
        q_stack = jnp.concatenate([q_sub[i][:, ls] for i in range(N_SUB)], axis=1)
        k_stack = jnp.concatenate([k_sub[i][:, ls] for i in range(N_SUB)], axis=1)
        qc_stack = jnp.concatenate([q_cross[c - 1][:, ls] for c in range(1, n_ch)], axis=1)
        kc_stack = jnp.concatenate([k_cross[c - 1][:, ls] for c in range(1, n_ch)], axis=1)
        a_t = (_dot_nt(k_stack, q_stack) * amask_t + _dot_nt(kc_stack, qc_stack)).astype(BF16)
        vt_h = hv_t[ls, :]
        state_h = src["state"][h]
        o_t = _dot(vt_h, a_t) + _dot_nt(state_h.astype(BF16), q_tile[:, ls])
        dst["state"][h] = state_h * decay_tile[:, ls] + _dot(vt_h, k_tile_end[:, ls])
        o_t = o_t * lax.rsqrt(jnp.mean(o_t * o_t, axis=0, keepdims=True) + EPS)
        hn_t.append(o_t * goutc_ref[...] * hg_act_t[ls, :])
        if (h + 1) % out_group == 0:
            first = h + 1 - out_group
            grp = jnp.concatenate(
                [jnp.concatenate(
                    [jnp.concatenate(hn_t[j:j + 2], axis=0)[:, r:r + LANES].T for j in range(first, h + 1, 2)],
                    axis=1) for r in range(0, tb, LANES)], axis=0).astype(BF16)
            y = y + _dot(grp, wout_ref[D_ATTN + first * HGRN_D:D_ATTN + (h + 1) * HGRN_D, :])
        yield

    dst["y"], dst["k_rot"], dst["v_new"] = x + _rms(y, gpost_ref[...]), k_rot, v_new


def _prompt_mix_kernel(sink_ref, x_ref, gpre_ref, win_ref, wint_ref, cos_ref, shi_ref, slo_ref, lb_ref, goutc_ref,
                       wout_ref, gpost_ref, bias_ref, amask_ref,
                       y_ref, wk_ref, wv_ref, s_ref,
                       kk_scr, vt_scr, st_scr, g_scr, *, layer):
    tb = MIX_TILE
    step = pl.program_id(1)
    last = pl.num_programs(1) - 1

    @pl.when(step == 0)
    def _():
        kk_scr[...] = jnp.zeros_like(kk_scr)
        vt_scr[...] = jnp.zeros_like(vt_scr)
        st_scr[...] = jnp.zeros_like(st_scr)

    hand = [{"state": [None] * N_HGRN_HEADS} for _ in range(MIX_PARTS + 1)]
    hand[0]["k_prev"] = [kk_scr[g] for g in range(N_KV_HEADS)]
    hand[0]["vt_prev"] = vt_scr[...]
    hand[0]["state"] = [st_scr[h] for h in range(N_HGRN_HEADS)]
    tiles = []
    for part in range(MIX_PARTS):
        rs = slice(part * tb, (part + 1) * tb)
        tiles.append(_mix_tile(
            x_ref[rs, :], (cos_ref[rs, :], shi_ref[rs, :], slo_ref[rs, :]), step == 0 if part == 0 else None,
            hand[part], hand[part + 1], g_scr.at[part],
            sink_ref, gpre_ref, win_ref, wint_ref, lb_ref, goutc_ref, wout_ref, gpost_ref, bias_ref, amask_ref,
            layer))
    for lead, t in enumerate(tiles):
        for _ in range(MIX_SKEW * (len(tiles) - 1 - lead)):
            next(t)
    while tiles:
        tiles = [t for t in tiles if next(t, True) is None]
    for part in range(MIX_PARTS):
        y_ref[part * tb:(part + 1) * tb, :] = hand[part + 1]["y"]
    final = hand[MIX_PARTS]
    for g in range(N_KV_HEADS):
        kk_scr[g] = final["k_prev"][g]
    vt_scr[...] = final["vt_prev"]
    for h in range(N_HGRN_HEADS):
        st_scr[h] = final["state"][h]

    @pl.when(step == last)
    def _():
        wk_ref[0] = final["k_rot"][tb - WINDOW:].T
        wv_ref[0] = final["v_new"][tb - WINDOW:].T
        for h in range(N_HGRN_HEADS):
            s_ref[0, h] = final["state"][h].T


def _prompt_mix(x, sinks, g_pre, w_in, w_in_t, lb_raw, g_out_head, w_out, g_post, batch, seq, layer):
    tb = MIX_TILE
    rows = MIX_PARTS * tb
    nt = seq // rows
    cos, s_hi, s_lo = _rope_tables(np.arange(seq))
    tok = lambda b, n: (b * nt + n, 0)
    tab = lambda b, n: (n, 0)
    per_b3 = lambda b, n: (b, 0, 0)
    lb_rows = lb_raw.shape[0]
    key_i = np.arange(2 * WINDOW)[:, None]
    rel = np.arange(WINDOW)[None, :] + WINDOW - key_i
    bias = jnp.asarray(np.tile(np.where((rel >= 0) & (rel < WINDOW), 0.0, NEG_INF), (1, GQA_GROUP)), F32)
    t_i = np.arange(tb)
    amask = jnp.asarray((t_i[:, None] // CHUNK == t_i[None, :] // CHUNK) & (t_i[:, None] <= t_i[None, :]), F32)
    return pl.pallas_call(
        functools.partial(_prompt_mix_kernel, layer=layer),
        grid=(batch, nt),
        in_specs=[
            pl.BlockSpec(memory_space=pltpu.SMEM),
            pl.BlockSpec((rows, D_MODEL), tok),
            _const_spec((1, D_MODEL)),
            _const_spec((D_MODEL, IN_COLS)),
            _const_spec((2 * D_HGRN, D_MODEL)),
            pl.BlockSpec((rows, LANES), tab),
            pl.BlockSpec((rows, LANES), tab),
            pl.BlockSpec((rows, LANES), tab),
            _const_spec((lb_rows, D_HGRN)),
            _const_spec((HGRN_D, 1)),
            _const_spec((D_MODEL, D_MODEL)),
            _const_spec((1, D_MODEL)),
            _const_spec((2 * WINDOW, GQA_GROUP * WINDOW)),
            _const_spec((tb, tb)),
        ],
        out_specs=[
            pl.BlockSpec((rows, D_MODEL), tok),
            pl.BlockSpec((1, D_KV, WINDOW), per_b3),
            pl.BlockSpec((1, D_KV, WINDOW), per_b3),
            pl.BlockSpec((1, N_HGRN_HEADS, HGRN_D, HGRN_D), lambda b, n: (b, 0, 0, 0)),
        ],
        out_shape=[
            jax.ShapeDtypeStruct((batch * seq, D_MODEL), F32),
            jax.ShapeDtypeStruct((batch, D_KV, WINDOW), F32),
            jax.ShapeDtypeStruct((batch, D_KV, WINDOW), F32),
            jax.ShapeDtypeStruct((batch, N_HGRN_HEADS, HGRN_D, HGRN_D), F32),
        ],
        scratch_shapes=[
            pltpu.VMEM((N_KV_HEADS, WINDOW, LANES), BF16),
            pltpu.VMEM((D_KV, WINDOW), BF16),
            pltpu.VMEM((N_HGRN_HEADS, HGRN_D, HGRN_D), F32),
            pltpu.VMEM((MIX_PARTS, tb, D_HGRN), F32),
        ],
        compiler_params=pltpu.CompilerParams(
            dimension_semantics=("arbitrary", "arbitrary"), vmem_limit_bytes=VMEM_LIMIT),
        name="prompt_mix",
    )(sinks, x, g_pre.reshape(1, D_MODEL), w_in, w_in_t, cos, s_hi, s_lo, lb_raw, g_out_head.reshape(HGRN_D, 1),
      w_out, g_post.reshape(1, D_MODEL), bias, amask)


def _sample_mix_kernel(sink_ref, x_ref, gpre_ref, win_ref, cos_ref, shi_ref, slo_ref, lb_ref, goutc_ref,
                       wout_ref, gpost_ref, ckt_ref, cvt_ref, sin_ref,
                       y_ref, wkt_ref, wvt_ref, sout_ref,
                       q_scr, kn_scr, vn_scr, knt_scr, vnt_scr, ft_scr, hkt_scr, hqt_scr, hvt_scr, hgt_scr,
                       ot_scr, att_scr, *, layer, nb):
    step = pl.program_id(0)
    last = pl.num_programs(0) - 1
    bt = nb // N_HGRN_HEADS

    @pl.when(step == 0)
    def _():
        u = _rms(x_ref[...], gpre_ref[...]).astype(BF16)
        proj = _dot(u, win_ref[...])
        cos, s_hi, s_lo = cos_ref[...], shi_ref[...], slo_ref[...]
        scale = HEAD_DIM ** -0.5
        for j in range(D_ATTN // LANES):
            q_scr[:, LANES * j:LANES * (j + 1)] = _rope(
                proj[:, OFF_Q + LANES * j:OFF_Q + LANES * (j + 1)], cos, s_hi, s_lo) * scale
        k_new = _rope(proj[:, OFF_K:OFF_K + D_KV], cos, s_hi, s_lo)
        v_new = proj[:, OFF_V:OFF_V + D_KV]
        kn_scr[...] = k_new
        vn_scr[...] = v_new
        for scr, val in ((knt_scr, k_new), (vnt_scr, v_new)):
            for i, part in enumerate(_split3(val.T)):
                scr[i] = part
        lb = _lower_bound(lb_ref[...], layer)
        f_t = (lb + (1.0 - lb) * jax.nn.sigmoid(proj[:, OFF_HF:OFF_HF + D_HGRN])).T
        ft_scr[...] = f_t
        hkt_scr[...] = 1.0 - f_t
        hqt_scr[...] = (proj[:, OFF_HQ:OFF_HQ + D_HGRN] * (HGRN_D ** -0.5)).T
        hvt_scr[...] = proj[:, OFF_HI:OFF_HI + D_HGRN].T
        hgt_scr[...] = proj[:, OFF_HG:OFF_HG + D_HGRN].T

    base = pl.multiple_of(step * HGRN_D, HGRN_D)
    hv_t = hvt_scr[pl.ds(base, HGRN_D), :]

    def hgrn_row(k, o_acc):
        f_row = ft_scr[pl.ds(base + k, 1), :]
        s_new = f_row * sin_ref[k] + hkt_scr[pl.ds(base + k, 1), :] * hv_t
        sout_ref[k] = s_new
        return o_acc + hqt_scr[pl.ds(base + k, 1), :] * s_new

    ot_scr[pl.ds(base, HGRN_D), :] = lax.fori_loop(
        0, HGRN_D, hgrn_row, jnp.zeros((HGRN_D, nb), F32), unroll=8)

    lane8 = lax.broadcasted_iota(jnp.int32, (N_Q_HEADS, LANES), 1)
    row8 = lax.broadcasted_iota(jnp.int32, (N_Q_HEADS, LANES), 0)
    keep8 = (lane8 >= HALF) == (row8 >= GQA_GROUP)
    win_lane = lax.broadcasted_iota(jnp.int32, (D_KV, WINDOW), 1)
    sink = sink_ref[...]
    b0 = step * bt
    sel = (lax.broadcasted_iota(jnp.int32, (nb, bt), 0)
           == b0 + lax.broadcasted_iota(jnp.int32, (nb, bt), 1)).astype(BF16)
    k_cols = sum(_dot(knt_scr[i], sel) for i in range(3))
    v_cols = sum(_dot(vnt_scr[i], sel) for i in range(3))
    def one_row(bi):
        b = b0 + bi
        q_b = jnp.broadcast_to(q_scr[pl.ds(b, 1), :], (N_Q_HEADS, D_ATTN))
        qm = jnp.zeros((N_Q_HEADS, LANES), F32)
        for h in range(N_Q_HEADS):
            c = q_b[:, LANES * (h // 2):LANES * (h // 2 + 1)]
            if h % 2 != h // GQA_GROUP:
                c = pltpu.roll(c, HALF, 1)
            qm = jnp.where(row8 == h, c, qm)
        qm = jnp.where(keep8, qm, 0.0)
        k_new = kn_scr[pl.ds(b, 1), :]
        v_new = vn_scr[pl.ds(b, 1), :]
        k_old = ckt_ref[bi]
        s = _dot(qm.astype(BF16), k_old.astype(BF16))
        wkt_ref[bi] = jnp.where(win_lane == WINDOW - 1, k_cols[:, bi:bi + 1], pltpu.roll(k_old, WINDOW - 1, 1))
        yield
        s = jnp.where(lane8 >= 1, s, NEG_INF)
        s_new = jnp.sum(qm * k_new, axis=-1, keepdims=True)
        m = jnp.maximum(jnp.maximum(jnp.max(s, axis=-1, keepdims=True), s_new), sink)
        p = jnp.exp(s - m)
        p_new = jnp.exp(s_new - m)
        denom = jnp.sum(p, axis=-1, keepdims=True) + p_new + jnp.exp(sink - m)
        yield
        v_old = cvt_ref[bi]
        o = (_dot_nt(p.astype(BF16), v_old.astype(BF16)) + p_new * v_new) / denom
        att_scr[pl.ds(pl.multiple_of(b * N_Q_HEADS, N_Q_HEADS), N_Q_HEADS), :] = o
        wvt_ref[bi] = jnp.where(win_lane == WINDOW - 1, v_cols[:, bi:bi + 1], pltpu.roll(v_old, WINDOW - 1, 1))

    rows = [one_row(bi) for bi in range(bt)]
    while rows:
        rows = [r for r in rows if next(r, True) is None]

    @pl.when(step == last)
    def _():
        y = jnp.zeros((nb, D_MODEL), F32)
        for h in range(N_HGRN_HEADS):
            ks = slice(h * HGRN_D, (h + 1) * HGRN_D)
            o_t = ot_scr[ks, :]
            hn_t = (o_t * lax.rsqrt(jnp.mean(o_t * o_t, axis=0, keepdims=True) + EPS)
                    * goutc_ref[...] * _silu(hgt_scr[ks, :]))
            y = y + _dot_tn(hn_t.astype(BF16), wout_ref[D_ATTN + h * HGRN_D:D_ATTN + (h + 1) * HGRN_D, :])
        for h in range(N_Q_HEADS):
            g = h // GQA_GROUP
            a_h = att_scr[pl.ds(h, nb, stride=N_Q_HEADS), :][:, g * HALF:(g + 1) * HALF]
            y = y + _dot(a_h.astype(BF16), wout_ref[h * HEAD_DIM:(h + 1) * HEAD_DIM, :])
        y_ref[...] = x_ref[...] + _rms(y, gpost_ref[...])


def _sample_mix(x, sinks, g_pre, w_in, lb_raw, g_out_head, w_out, g_post, cache_kt, cache_vt, state_t, pos, layer):
    nb = x.shape[0]
    bt = nb // N_HGRN_HEADS
    cos, s_hi, s_lo = _rope_tables(pos)
    lb_rows = lb_raw.shape[0]
    blk3 = pl.BlockSpec((bt, D_KV, WINDOW), lambda i: (i, 0, 0))
    blk_s = pl.BlockSpec((HGRN_D, HGRN_D, nb), lambda i: (i, 0, 0))
    chan_major = pltpu.VMEM((D_HGRN, nb), F32)
    return pl.pallas_call(
        functools.partial(_sample_mix_kernel, layer=layer, nb=nb),
        grid=(N_HGRN_HEADS,),
        in_specs=[
            _const_spec((N_Q_HEADS, 1)),
            _const_spec((nb, D_MODEL)),
            _const_spec((1, D_MODEL)),
            _const_spec((D_MODEL, IN_COLS)),
            _const_spec((1, LANES)),
            _const_spec((1, LANES)),
            _const_spec((1, LANES)),
            _const_spec((lb_rows, D_HGRN)),
            _const_spec((HGRN_D, 1)),
            _const_spec((D_MODEL, D_MODEL)),
            _const_spec((1, D_MODEL)),
            blk3, blk3, blk_s,
        ],
        out_specs=[pl.BlockSpec((nb, D_MODEL), lambda i: (0, 0)), blk3, blk3, blk_s],
        out_shape=[
            jax.ShapeDtypeStruct((nb, D_MODEL), F32),
            jax.ShapeDtypeStruct(cache_kt.shape, F32),
            jax.ShapeDtypeStruct(cache_vt.shape, F32),
            jax.ShapeDtypeStruct(state_t.shape, F32),
        ],
        scratch_shapes=[
            pltpu.VMEM((nb, D_ATTN), F32),
            pltpu.VMEM((nb, D_KV), F32),
            pltpu.VMEM((nb, D_KV), F32),
            pltpu.VMEM((3, D_KV, nb), BF16),
            pltpu.VMEM((3, D_KV, nb), BF16),
            chan_major, chan_major, chan_major, chan_major, chan_major, chan_major,
            pltpu.VMEM((nb * N_Q_HEADS, LANES), F32),
        ],
        compiler_params=pltpu.CompilerParams(
            dimension_semantics=("arbitrary",), vmem_limit_bytes=VMEM_LIMIT),
        name="sample_mix",
    )(sinks.reshape(N_Q_HEADS, 1), x, g_pre.reshape(1, D_MODEL), w_in, cos, s_hi, s_lo, lb_raw,
      g_out_head.reshape(HGRN_D, 1), w_out, g_post.reshape(1, D_MODEL), cache_kt, cache_vt, state_t)


def kernel(x_prompt, x_sample, cache_win_k, cache_win_v, state_hgrn, ffn1_pre_g, ffn1_post_g, ffn1_w_gu,
           ffn1_w_down, mix_pre_g, mix_post_g, w_in, attn_sinks, hgrn_lb, hgrn_out_g, w_out, ffn2_pre_g,
           ffn2_post_g, ffn2_w_gu, ffn2_w_down):
    batch, seq, _ = x_prompt.shape
    nb, t_s, _ = x_sample.shape
    depth = w_in.shape[0]
    assert t_s == 1 and seq % (MIX_PARTS * MIX_TILE) == 0 and (batch * seq) % FFN_TILE == 0 and nb == LANES
    assert cache_win_k.shape[2:] == (WINDOW, N_KV_HEADS, HEAD_DIM)

    xp = x_prompt.reshape(batch * seq, D_MODEL)
    xs = x_sample.reshape(nb, D_MODEL)
    pos_s = PAST_LEN + np.arange(t_s)
    outs = [[] for _ in range(6)]
    for l in range(depth):
        w_in_l, w_out_l = w_in[l].astype(BF16), w_out[l].astype(BF16)
        w_in_t = w_in[l][:, OFF_HI:].T.astype(BF16)

        xp, xs = _ffn(xp, xs, ffn1_pre_g[l], ffn1_post_g[l], ffn1_w_gu[l], ffn1_w_down[l])

        xp, wkt_p, wvt_p, s_p = _prompt_mix(xp, attn_sinks[l], mix_pre_g[l], w_in_l, w_in_t, hgrn_lb,
                                            hgrn_out_g[l], w_out_l, mix_post_g[l], batch, seq, l)
        to_kt = lambda c: jnp.transpose(c, (0, 2, 3, 1)).reshape(-1, D_KV, WINDOW)
        from_kt = lambda c: jnp.transpose(c.reshape(-1, N_KV_HEADS, HEAD_DIM, WINDOW), (0, 3, 1, 2))
        state_t = jnp.transpose(state_hgrn[l], (1, 2, 3, 0)).reshape(D_HGRN, HGRN_D, nb)
        xs, wkt_s, wvt_s, st_s = _sample_mix(
            xs, attn_sinks[l], mix_pre_g[l], w_in_l, hgrn_lb, hgrn_out_g[l], w_out_l, mix_post_g[l],
            to_kt(cache_win_k[l]), to_kt(cache_win_v[l]), state_t, pos_s, l)
        s_s = jnp.transpose(st_s.reshape(N_HGRN_HEADS, HGRN_D, HGRN_D, nb), (3, 0, 1, 2))

        xp, xs = _ffn(xp, xs, ffn2_pre_g[l], ffn2_post_g[l], ffn2_w_gu[l], ffn2_w_down[l])

        for lst, val in zip(outs, (from_kt(wkt_p), from_kt(wvt_p), s_p, from_kt(wkt_s), from_kt(wvt_s), s_s)):
            lst.append(val)

    return (xp.reshape(batch, seq, D_MODEL), xs.reshape(nb, t_s, D_MODEL)) + tuple(jnp.stack(o) for o in outs)
```

```python
import functools

import jax
import jax.numpy as jnp
import numpy as np
from jax import lax
from jax.experimental import pallas as pl
from jax.experimental.pallas import tpu as pltpu

F32 = jnp.float32
BF16 = jnp.bfloat16

D_MODEL = 1024
D_FF = 2816
HEAD_DIM = 64
N_Q_HEADS = 8
N_KV_HEADS = 2
GQA_GROUP = N_Q_HEADS // N_KV_HEADS
WINDOW = 128
PAST_LEN = 8192
ROT_DIM = HEAD_DIM // 4
ROPE_THETA = 500000.0
N_HGRN_HEADS = 8
HGRN_D = 64
D_ATTN = N_Q_HEADS * HEAD_DIM
D_KV = N_KV_HEADS * HEAD_DIM
D_HGRN = N_HGRN_HEADS * HGRN_D
IN_COLS = D_ATTN + 2 * D_KV + 4 * D_HGRN
OFF_Q, OFF_K, OFF_V = 0, D_ATTN, D_ATTN + D_KV
OFF_HQ = D_ATTN + 2 * D_KV
OFF_HF, OFF_HI, OFF_HG = OFF_HQ + D_HGRN, OFF_HQ + 2 * D_HGRN, OFF_HQ + 3 * D_HGRN
EPS = 1e-6
NEG_INF = -1e30
LOG2E = 1.4426950408889634
LANES = 128
HALF = LANES // 2

FFN_TILE = 1024
FFN_PARTS = 4
FFN_W_STEPS = 8
MIX_TILE = 256
MIX_PARTS = 4
MIX_SKEW = 0
CHUNK = 64
SUB = 16
N_SUB = CHUNK // SUB
VMEM_LIMIT = 56 * 1024 * 1024


def _rms(x, g):
    return x * lax.rsqrt(jnp.mean(x * x, axis=-1, keepdims=True) + EPS) * g


def _silu(x):
    return x * jax.nn.sigmoid(x)


def _dot(a, b):
    return jnp.dot(a, b, preferred_element_type=F32)


def _dot_nt(a, b):
    return lax.dot_general(a, b, (((1,), (1,)), ((), ())), preferred_element_type=F32)


def _dot_tn(a, b):
    return lax.dot_general(a, b, (((0,), (0,)), ((), ())), preferred_element_type=F32)


def _split3(x):
    hi = x.astype(BF16)
    r = x - hi.astype(F32)
    mid = r.astype(BF16)
    lo = (r - mid.astype(F32)).astype(BF16)
    return hi, mid, lo


def _const_spec(shape):
    nd = len(shape)
    return pl.BlockSpec(shape, lambda *_: (0,) * nd, pipeline_mode=pl.Buffered(1))


def _ffn_kernel(x_ref, xs_ref, gpre_ref, gpost_ref, wgu32_ref, wd32_ref, o_ref, os_ref, wgu_ref, wd_ref):
    step = pl.program_id(0)

    @pl.when(step < FFN_W_STEPS)
    def _():
        gu_rows, d_rows = wgu32_ref.shape[0], wd32_ref.shape[0]
        wgu_ref[pl.ds(pl.multiple_of(step * gu_rows, gu_rows), gu_rows), :] = wgu32_ref[...].astype(BF16)
        wd_ref[pl.ds(pl.multiple_of(step * d_rows, d_rows), d_rows), :] = wd32_ref[...].astype(BF16)

    def half_step(src_ref, dst_ref, rs):
        x = src_ref[rs, :]
        h = _rms(x, gpre_ref[...]).astype(BF16)
        yield
        gate = _dot(h, wgu_ref[:, :D_FF])
        up = _dot(h, wgu_ref[:, D_FF:])
        yield
        act = (_silu(gate) * up).astype(BF16)
        yield
        y = _dot(act, wd_ref[...])
        yield
        dst_ref[rs, :] = x + 0.5 * _rms(y, gpost_ref[...])

    @pl.when(step >= FFN_W_STEPS)
    def _():
        rows = x_ref.shape[0] // FFN_PARTS
        parts = [half_step(x_ref, o_ref, slice(r * rows, (r + 1) * rows)) for r in range(FFN_PARTS)]
        for lead, part in enumerate(parts):
            for _ in range(FFN_PARTS - 1 - lead):
                next(part)
        while parts:
            parts = [part for part in parts if next(part, True) is None]

    @pl.when(step == pl.num_programs(0) - 1)
    def _():
        for _ in half_step(xs_ref, os_ref, slice(None)):
            pass


def _ffn(x, xs, g_pre, g_post, w_gu, w_down):
    n, nb = x.shape[0], xs.shape[0]
    tile_map = lambda i: (jnp.maximum(i - FFN_W_STEPS, 0), 0)
    chunk_map = lambda i: (jnp.minimum(i, FFN_W_STEPS - 1), 0)
    return pl.pallas_call(
        _ffn_kernel,
        grid=(FFN_W_STEPS + n // FFN_TILE,),
        in_specs=[
            pl.BlockSpec((FFN_TILE, D_MODEL), tile_map),
            _const_spec((nb, D_MODEL)),
            _const_spec((1, D_MODEL)),
            _const_spec((1, D_MODEL)),
            pl.BlockSpec((D_MODEL // FFN_W_STEPS, 2 * D_FF), chunk_map),
            pl.BlockSpec((D_FF // FFN_W_STEPS, D_MODEL), chunk_map),
        ],
        out_specs=[pl.BlockSpec((FFN_TILE, D_MODEL), tile_map),
                   pl.BlockSpec((nb, D_MODEL), lambda i: (0, 0))],
        out_shape=[jax.ShapeDtypeStruct((n, D_MODEL), F32), jax.ShapeDtypeStruct((nb, D_MODEL), F32)],
        scratch_shapes=[pltpu.VMEM((D_MODEL, 2 * D_FF), BF16), pltpu.VMEM((D_FF, D_MODEL), BF16)],
        compiler_params=pltpu.CompilerParams(
            dimension_semantics=("arbitrary",), vmem_limit_bytes=VMEM_LIMIT),
        name="ffn",
    )(x, xs, g_pre.reshape(1, D_MODEL), g_post.reshape(1, D_MODEL), w_gu, w_down)


def _lower_bound(lb_raw, layer):
    m = jnp.max(lb_raw, axis=0, keepdims=True)
    e = jnp.exp(lb_raw - m)
    return jnp.sum(e[: layer + 1], axis=0, keepdims=True) / jnp.sum(e, axis=0, keepdims=True)


def _rope(x, cos, sin_hi, sin_lo):
    return x * cos + pltpu.roll(x, ROT_DIM // 2, 1) * sin_hi + pltpu.roll(x, LANES - ROT_DIM // 2, 1) * sin_lo


def _rope_tables(pos):
    half = ROT_DIM // 2
    inv = (np.float32(ROPE_THETA) ** (-np.arange(half, dtype=np.float32) / half)).astype(np.float64)
    ang = np.asarray(pos, np.float64)[:, None] * inv[None, :]
    cos, sin = np.cos(ang), np.sin(ang)
    t = ang.shape[0]
    one = np.ones((t, HEAD_DIM - ROT_DIM))
    zero = np.zeros((t, HEAD_DIM - ROT_DIM))
    zh = np.zeros((t, half))
    c = np.concatenate([cos, cos, one], axis=1)
    s_hi = np.concatenate([zh, sin, zero], axis=1)
    s_lo = np.concatenate([-sin, zh, zero], axis=1)
    return tuple(jnp.asarray(np.tile(a, (1, LANES // HEAD_DIM)), F32) for a in (c, s_hi, s_lo))


def _dup_half(x, g, lo_half):
    xr = pltpu.roll(x, HALF, 1)
    return jnp.where(lo_half, x, xr) if g == 0 else jnp.where(lo_half, xr, x)


def _mix_tile(x, rope_tab, seq_start, src, dst, g_scr,
              sink_ref, gpre_ref, win_ref, wint_ref, lb_ref, goutc_ref, wout_ref, gpost_ref, bias_ref, amask_ref,
              layer):
    tb = MIX_TILE
    u = _rms(x, gpre_ref[...]).astype(BF16)

    def proj(off, width):
        return _dot(u, win_ref[:, off:off + width])

    p_attn = proj(OFF_Q, D_ATTN + 2 * D_KV)
    p_hf = proj(OFF_HF, D_HGRN)
    cos, s_hi, s_lo = rope_tab
    scale = HEAD_DIM ** -0.5 * LOG2E
    q_cols = [_rope(p_attn[:, LANES * j: LANES * (j + 1)], cos, s_hi, s_lo) * scale
              for j in range(D_ATTN // LANES)]
    k_rot = _rope(p_attn[:, OFF_K:OFF_K + D_KV], cos, s_hi, s_lo)
    v_new = p_attn[:, OFF_V:OFF_V + D_KV]

    lane = lax.broadcasted_iota(jnp.int32, (WINDOW, LANES), 1)
    lo_half = lane < HALF
    bias = bias_ref[...]
    if seq_start is None:
        bias_first = bias
    else:
        no_prev = jnp.where(seq_start, NEG_INF, 0.0)
        bias_first = jnp.concatenate([bias[:WINDOW] + no_prev, bias[WINDOW:]], axis=0)

    vt_tile = v_new.T.astype(BF16)
    yield
    k_prev, vt_prev = src["k_prev"], src["vt_prev"]
    scores = []
    for i in range(tb // WINDOW):
        r0 = i * WINDOW
        k_cur = [_dup_half(k_rot[r0:r0 + WINDOW], g, lo_half).astype(BF16) for g in range(N_KV_HEADS)]
        vt_keys = jnp.concatenate([vt_prev, vt_tile[:, r0:r0 + WINDOW]], axis=1)
        for g in range(N_KV_HEADS):
            keys = jnp.concatenate([k_prev[g], k_cur[g]], axis=0)
            heads = range(g * GQA_GROUP, (g + 1) * GQA_GROUP)
            qg = jnp.concatenate(
                [jnp.where(lo_half if h % 2 == 0 else ~lo_half, q_cols[h // 2][r0:r0 + WINDOW], 0.0)
                 for h in heads], axis=0).astype(BF16)
            s = _dot_nt(keys, qg) + (bias_first if i == 0 else bias)
            scores.append((s, vt_keys[g * HEAD_DIM:(g + 1) * HEAD_DIM], heads))
        k_prev, vt_prev = k_cur, vt_tile[:, r0:r0 + WINDOW]
    dst["k_prev"], dst["vt_prev"] = k_prev, vt_prev
    yield

    hq = proj(OFF_HQ, D_HGRN) * (HGRN_D ** -0.5)
    hv_t = _dot_nt(wint_ref[:D_HGRN, :], u).astype(BF16)
    hg_act_t = _silu(_dot_nt(wint_ref[D_HGRN:, :], u))
    yield

    att_t = [[None] * (tb // WINDOW) for _ in range(N_Q_HEADS)]
    for idx, (s, vt_g, heads) in enumerate(scores):
        sink = jnp.concatenate(
            [jnp.full((1, WINDOW), sink_ref[h] * LOG2E, F32) for h in heads], axis=1)
        m = jnp.maximum(jnp.max(s, axis=0, keepdims=True), sink)
        p = jnp.exp2(s - m)
        denom = jnp.sum(p, axis=0, keepdims=True) + jnp.exp2(sink - m)
        o_t = _dot(vt_g, p.astype(BF16)) * (1.0 / denom)
        for j, h in enumerate(heads):
            att_t[h][idx // N_KV_HEADS] = o_t[:, j * WINDOW:(j + 1) * WINDOW]
        yield
    att = jnp.concatenate(
        [jnp.concatenate([jnp.concatenate([att_t[2 * j][i], att_t[2 * j + 1][i]], axis=0).T
                          for j in range(N_Q_HEADS // 2)], axis=1)
         for i in range(tb // WINDOW)], axis=0)
    y_att = _dot(att.astype(BF16), wout_ref[:D_ATTN, :])
    yield

    lb = _lower_bound(lb_ref[...], layer)
    f = lb + (1.0 - lb) * jax.nn.sigmoid(p_hf)
    logf = jnp.log(f)
    hk = 1.0 - f

    tr = lax.broadcasted_iota(jnp.int32, (tb, tb), 0)
    tc = lax.broadcasted_iota(jnp.int32, (tb, tb), 1)
    tri = ((tr // CHUNK == tc // CHUNK) & (tc <= tr)).astype(BF16)
    g_cum = sum(_dot(tri, part) for part in _split3(logf))
    g_scr[...] = g_cum
    yield

    def bcast_row(r, rows):
        return jnp.broadcast_to(g_scr[r:r + 1, :], (rows, D_HGRN))

    zeros_sub = jnp.zeros((SUB, D_HGRN), F32)
    g_ref_q = jnp.concatenate(
        [zeros_sub if sb % N_SUB == 0 else bcast_row(sb * SUB - 1, SUB) for sb in range(tb // SUB)], axis=0)
    q_loc = (hq * jnp.exp(g_cum - g_ref_q)).astype(BF16)
    g_end = jnp.concatenate(
        [bcast_row(c * CHUNK + CHUNK - 1, CHUNK) for c in range(tb // CHUNK)], axis=0)
    k_sub = []
    for i in range(N_SUB):
        live = (i + 1) * SUB
        pieces = []
        for c in range(tb // CHUNK):
            r0 = c * CHUNK
            g_ref_i = 0.0 if i == 0 else bcast_row(r0 + i * SUB - 1, live)
            pieces.append(hk[r0:r0 + live] * jnp.exp(g_ref_i - g_cum[r0:r0 + live]))
            if live < CHUNK:
                pieces.append(jnp.zeros((CHUNK - live, D_HGRN), F32))
        k_sub.append(jnp.concatenate(pieces, axis=0).astype(BF16))

    n_ch = tb // CHUNK
    gc = [g_scr[c * CHUNK + CHUNK - 1:c * CHUNK + CHUNK, :] for c in range(n_ch)]

    def span(lo, hi):
        if hi <= lo:
            return jnp.ones((CHUNK, D_HGRN), F32)
        return jnp.broadcast_to(jnp.exp(sum(gc[lo:hi])), (CHUNK, D_HGRN))

    zeros_chunk = jnp.zeros((CHUNK, D_HGRN), F32)
    k_end32 = hk * jnp.exp(g_end - g_cum)
    q_glob32 = hq * jnp.exp(g_cum)
    k_cross = [(k_end32 * jnp.concatenate(
        [span(cp + 1, c) if cp < c else zeros_chunk for cp in range(n_ch)], axis=0)).astype(BF16)
        for c in range(1, n_ch)]
    q_tile = (q_glob32 * jnp.concatenate([span(0, c) for c in range(n_ch)], axis=0)).astype(BF16)
    k_tile_end = (k_end32 * jnp.concatenate([span(cp + 1, n_ch) for cp in range(n_ch)], axis=0)).astype(BF16)
    decay_tile = jnp.exp(sum(gc))
    q_glob = q_glob32.astype(BF16)

    def keep_rows(a, block, wanted):
        zero = jnp.zeros((block, a.shape[1]), a.dtype)
        return jnp.concatenate(
            [a[r * block:(r + 1) * block] if wanted(r) else zero for r in range(a.shape[0] // block)], axis=0)

    q_sub = [keep_rows(q_loc, SUB, lambda r, i=i: r % N_SUB == i) for i in range(N_SUB)]
    q_cross = [keep_rows(q_glob, CHUNK, lambda r, c=c: r == c) for c in range(1, n_ch)]
    yield
    amask_t = amask_ref[...]
    y = y_att
    out_group = 2 * LANES // HGRN_D
    for first in range(0, N_HGRN_HEADS, out_group):
        heads = range(first, first + out_group)
        lanes = {h: slice(h * HGRN_D, (h + 1) * HGRN_D) for h in heads}
        pair = {}
        for h, ls in lanes.items():
            q_stack = jnp.concatenate([q_sub[i][:, ls] for i in range(N_SUB)], axis=1)
            k_stack = jnp.concatenate([k_sub[i][:, ls] for i in range(N_SUB)], axis=1)
            qc_stack = jnp.concatenate([q_cross[c - 1][:, ls] for c in range(1, n_ch)], axis=1)
            kc_stack = jnp.concatenate([k_cross[c - 1][:, ls] for c in range(1, n_ch)], axis=1)
            pair[h] = (_dot_nt(k_stack, q_stack), _dot_nt(kc_stack, qc_stack))
        yield
        o_t = {}
        for h, ls in lanes.items():
            a_t = (pair[h][0] * amask_t + pair[h][1]).astype(BF16)
            vt_h = hv_t[ls, :]
            state_h = src["state"][h]
            o_t[h] = _dot(vt_h, a_t) + _dot_nt(state_h.astype(BF16), q_tile[:, ls])
            dst["state"][h] = state_h * decay_tile[:, ls] + _dot(vt_h, k_tile_end[:, ls])
        yield
        hn_t = {h: o_t[h] * lax.rsqrt(jnp.mean(o_t[h] * o_t[h], axis=0, keepdims=True) + EPS)
                * goutc_ref[...] * hg_act_t[ls, :] for h, ls in lanes.items()}
        grp = jnp.concatenate(
            [jnp.concatenate(
                [jnp.concatenate([hn_t[j], hn_t[j + 1]], axis=0)[:, r:r + LANES].T for j in heads[::2]],
                axis=1) for r in range(0, tb, LANES)], axis=0).astype(BF16)
        y = y + _dot(grp, wout_ref[D_ATTN + first * HGRN_D:D_ATTN + (first + out_group) * HGRN_D, :])
        yield

    dst["y"], dst["k_rot"], dst["v_new"] = x + _rms(y, gpost_ref[...]), k_rot, v_new


def _prompt_mix_kernel(sink_ref, x_ref, gpre_ref, win_ref, wint_ref, cos_ref, shi_ref, slo_ref, lb_ref, goutc_ref,
                       wout_ref, gpost_ref, bias_ref, amask_ref,
                       y_ref, wk_ref, wv_ref, s_ref,
                       kk_scr, vt_scr, st_scr, g_scr, *, layer):
    tb = MIX_TILE
    step = pl.program_id(1)
    last = pl.num_programs(1) - 1

    @pl.when(step == 0)
    def _():
        kk_scr[...] = jnp.zeros_like(kk_scr)
        vt_scr[...] = jnp.zeros_like(vt_scr)
        st_scr[...] = jnp.zeros_like(st_scr)

    hand = [{"state": [None] * N_HGRN_HEADS} for _ in range(MIX_PARTS + 1)]
    hand[0]["k_prev"] = [kk_scr[g] for g in range(N_KV_HEADS)]
    hand[0]["vt_prev"] = vt_scr[...]
    hand[0]["state"] = [st_scr[h] for h in range(N_HGRN_HEADS)]
    tiles = []
    for part in range(MIX_PARTS):
        rs = slice(part * tb, (part + 1) * tb)
        tiles.append(_mix_tile(
            x_ref[rs, :], (cos_ref[rs, :], shi_ref[rs, :], slo_ref[rs, :]), step == 0 if part == 0 else None,
            hand[part], hand[part + 1], g_scr.at[part],
            sink_ref, gpre_ref, win_ref, wint_ref, lb_ref, goutc_ref, wout_ref, gpost_ref, bias_ref, amask_ref,
            layer))
    for lead, t in enumerate(tiles):
        for _ in range(MIX_SKEW * (len(tiles) - 1 - lead)):
            next(t)
    while tiles:
        tiles = [t for t in tiles if next(t, True) is None]
    for part in range(MIX_PARTS):
        y_ref[part * tb:(part + 1) * tb, :] = hand[part + 1]["y"]
    final = hand[MIX_PARTS]
    for g in range(N_KV_HEADS):
        kk_scr[g] = final["k_prev"][g]
    vt_scr[...] = final["vt_prev"]
    for h in range(N_HGRN_HEADS):
        st_scr[h] = final["state"][h]

    @pl.when(step == last)
    def _():
        wk_ref[0] = final["k_rot"][tb - WINDOW:].T
        wv_ref[0] = final["v_new"][tb - WINDOW:].T
        for h in range(N_HGRN_HEADS):
            s_ref[0, h] = final["state"][h].T


def _prompt_mix(x, sinks, g_pre, w_in, w_in_t, lb_raw, g_out_head, w_out, g_post, batch, seq, layer):
    tb = MIX_TILE
    rows = MIX_PARTS * tb
    nt = seq // rows
    cos, s_hi, s_lo = _rope_tables(np.arange(seq))
    tok = lambda b, n: (b * nt + n, 0)
    tab = lambda b, n: (n, 0)
    per_b3 = lambda b, n: (b, 0, 0)
    lb_rows = lb_raw.shape[0]
    key_i = np.arange(2 * WINDOW)[:, None]
    rel = np.arange(WINDOW)[None, :] + WINDOW - key_i
    bias = jnp.asarray(np.tile(np.where((rel >= 0) & (rel < WINDOW), 0.0, NEG_INF), (1, GQA_GROUP)), F32)
    t_i = np.arange(tb)
    amask = jnp.asarray((t_i[:, None] // CHUNK == t_i[None, :] // CHUNK) & (t_i[:, None] <= t_i[None, :]), F32)
    return pl.pallas_call(
        functools.partial(_prompt_mix_kernel, layer=layer),
        grid=(batch, nt),
        in_specs=[
            pl.BlockSpec(memory_space=pltpu.SMEM),
            pl.BlockSpec((rows, D_MODEL), tok),
            _const_spec((1, D_MODEL)),
            _const_spec((D_MODEL, IN_COLS)),
            _const_spec((2 * D_HGRN, D_MODEL)),
            pl.BlockSpec((rows, LANES), tab),
            pl.BlockSpec((rows, LANES), tab),
            pl.BlockSpec((rows, LANES), tab),
            _const_spec((lb_rows, D_HGRN)),
            _const_spec((HGRN_D, 1)),
            _const_spec((D_MODEL, D_MODEL)),
            _const_spec((1, D_MODEL)),
            _const_spec((2 * WINDOW, GQA_GROUP * WINDOW)),
            _const_spec((tb, tb)),
        ],
        out_specs=[
            pl.BlockSpec((rows, D_MODEL), tok),
            pl.BlockSpec((1, D_KV, WINDOW), per_b3),
            pl.BlockSpec((1, D_KV, WINDOW), per_b3),
            pl.BlockSpec((1, N_HGRN_HEADS, HGRN_D, HGRN_D), lambda b, n: (b, 0, 0, 0)),
        ],
        out_shape=[
            jax.ShapeDtypeStruct((batch * seq, D_MODEL), F32),
            jax.ShapeDtypeStruct((batch, D_KV, WINDOW), F32),
            jax.ShapeDtypeStruct((batch, D_KV, WINDOW), F32),
            jax.ShapeDtypeStruct((batch, N_HGRN_HEADS, HGRN_D, HGRN_D), F32),
        ],
        scratch_shapes=[
            pltpu.VMEM((N_KV_HEADS, WINDOW, LANES), BF16),
            pltpu.VMEM((D_KV, WINDOW), BF16),
            pltpu.VMEM((N_HGRN_HEADS, HGRN_D, HGRN_D), F32),
            pltpu.VMEM((MIX_PARTS, tb, D_HGRN), F32),
        ],
        compiler_params=pltpu.CompilerParams(
            dimension_semantics=("arbitrary", "arbitrary"), vmem_limit_bytes=VMEM_LIMIT),
        name="prompt_mix",
    )(sinks, x, g_pre.reshape(1, D_MODEL), w_in, w_in_t, cos, s_hi, s_lo, lb_raw, g_out_head.reshape(HGRN_D, 1),
      w_out, g_post.reshape(1, D_MODEL), bias, amask)


def _sample_mix_kernel(sink_ref, x_ref, gpre_ref, win_ref, cos_ref, shi_ref, slo_ref, lb_ref, goutc_ref,
                       wout_ref, gpost_ref, ckt_ref, cvt_ref, sin_ref,
                       y_ref, wkt_ref, wvt_ref, sout_ref,
                       q_scr, kn_scr, vn_scr, knt_scr, vnt_scr, ft_scr, hkt_scr, hqt_scr, hvt_scr, hgt_scr,
                       ot_scr, att_scr, *, layer, nb):
    step = pl.program_id(0)
    last = pl.num_programs(0) - 1
    bt = nb // N_HGRN_HEADS

    @pl.when(step == 0)
    def _():
        u = _rms(x_ref[...], gpre_ref[...]).astype(BF16)
        proj = _dot(u, win_ref[...])
        cos, s_hi, s_lo = cos_ref[...], shi_ref[...], slo_ref[...]
        scale = HEAD_DIM ** -0.5
        for j in range(D_ATTN // LANES):
            q_scr[:, LANES * j:LANES * (j + 1)] = _rope(
                proj[:, OFF_Q + LANES * j:OFF_Q + LANES * (j + 1)], cos, s_hi, s_lo) * scale
        k_new = _rope(proj[:, OFF_K:OFF_K + D_KV], cos, s_hi, s_lo)
        v_new = proj[:, OFF_V:OFF_V + D_KV]
        kn_scr[...] = k_new
        vn_scr[...] = v_new
        for scr, val in ((knt_scr, k_new), (vnt_scr, v_new)):
            for i, part in enumerate(_split3(val.T)):
                scr[i] = part
        lb = _lower_bound(lb_ref[...], layer)
        f_t = (lb + (1.0 - lb) * jax.nn.sigmoid(proj[:, OFF_HF:OFF_HF + D_HGRN])).T
        ft_scr[...] = f_t
        hkt_scr[...] = 1.0 - f_t
        hqt_scr[...] = (proj[:, OFF_HQ:OFF_HQ + D_HGRN] * (HGRN_D ** -0.5)).T
        hvt_scr[...] = proj[:, OFF_HI:OFF_HI + D_HGRN].T
        hgt_scr[...] = proj[:, OFF_HG:OFF_HG + D_HGRN].T

    base = pl.multiple_of(step * HGRN_D, HGRN_D)
    hv_t = hvt_scr[pl.ds(base, HGRN_D), :]

    def hgrn_row(k, o_acc):
        f_row = ft_scr[pl.ds(base + k, 1), :]
        s_new = f_row * sin_ref[k] + hkt_scr[pl.ds(base + k, 1), :] * hv_t
        sout_ref[k] = s_new
        return o_acc + hqt_scr[pl.ds(base + k, 1), :] * s_new

    ot_scr[pl.ds(base, HGRN_D), :] = lax.fori_loop(
        0, HGRN_D, hgrn_row, jnp.zeros((HGRN_D, nb), F32), unroll=8)

    lane8 = lax.broadcasted_iota(jnp.int32, (N_Q_HEADS, LANES), 1)
    row8 = lax.broadcasted_iota(jnp.int32, (N_Q_HEADS, LANES), 0)
    keep8 = (lane8 >= HALF) == (row8 >= GQA_GROUP)
    win_lane = lax.broadcasted_iota(jnp.int32, (D_KV, WINDOW), 1)
    sink = sink_ref[...]
    b0 = step * bt
    sel = (lax.broadcasted_iota(jnp.int32, (nb, bt), 0)
           == b0 + lax.broadcasted_iota(jnp.int32, (nb, bt), 1)).astype(BF16)
    k_cols = sum(_dot(knt_scr[i], sel) for i in range(3))
    v_cols = sum(_dot(vnt_scr[i], sel) for i in range(3))
    def one_row(bi):
        b = b0 + bi
        q_b = jnp.broadcast_to(q_scr[pl.ds(b, 1), :], (N_Q_HEADS, D_ATTN))
        qm = jnp.zeros((N_Q_HEADS, LANES), F32)
        for h in range(N_Q_HEADS):
            c = q_b[:, LANES * (h // 2):LANES * (h // 2 + 1)]
            if h % 2 != h // GQA_GROUP:
                c = pltpu.roll(c, HALF, 1)
            qm = jnp.where(row8 == h, c, qm)
        qm = jnp.where(keep8, qm, 0.0)
        k_new = kn_scr[pl.ds(b, 1), :]
        v_new = vn_scr[pl.ds(b, 1), :]
        k_old = ckt_ref[bi]
        s = _dot(qm.astype(BF16), k_old.astype(BF16))
        wkt_ref[bi] = jnp.where(win_lane == WINDOW - 1, k_cols[:, bi:bi + 1], pltpu.roll(k_old, WINDOW - 1, 1))
        yield
        s = jnp.where(lane8 >= 1, s, NEG_INF)
        s_new = jnp.sum(qm * k_new, axis=-1, keepdims=True)
        m = jnp.maximum(jnp.maximum(jnp.max(s, axis=-1, keepdims=True), s_new), sink)
        p = jnp.exp(s - m)
        p_new = jnp.exp(s_new - m)
        denom = jnp.sum(p, axis=-1, keepdims=True) + p_new + jnp.exp(sink - m)
        yield
        v_old = cvt_ref[bi]
        o = (_dot_nt(p.astype(BF16), v_old.astype(BF16)) + p_new * v_new) / denom
        att_scr[pl.ds(pl.multiple_of(b * N_Q_HEADS, N_Q_HEADS), N_Q_HEADS), :] = o
        wvt_ref[bi] = jnp.where(win_lane == WINDOW - 1, v_cols[:, bi:bi + 1], pltpu.roll(v_old, WINDOW - 1, 1))

    rows = [one_row(bi) for bi in range(bt)]
    while rows:
        rows = [r for r in rows if next(r, True) is None]

    @pl.when(step == last)
    def _():
        y = jnp.zeros((nb, D_MODEL), F32)
        for h in range(N_HGRN_HEADS):
            ks = slice(h * HGRN_D, (h + 1) * HGRN_D)
            o_t = ot_scr[ks, :]
            hn_t = (o_t * lax.rsqrt(jnp.mean(o_t * o_t, axis=0, keepdims=True) + EPS)
                    * goutc_ref[...] * _silu(hgt_scr[ks, :]))
            y = y + _dot_tn(hn_t.astype(BF16), wout_ref[D_ATTN + h * HGRN_D:D_ATTN + (h + 1) * HGRN_D, :])
        for h in range(N_Q_HEADS):
            g = h // GQA_GROUP
            a_h = att_scr[pl.ds(h, nb, stride=N_Q_HEADS), :][:, g * HALF:(g + 1) * HALF]
            y = y + _dot(a_h.astype(BF16), wout_ref[h * HEAD_DIM:(h + 1) * HEAD_DIM, :])
        y_ref[...] = x_ref[...] + _rms(y, gpost_ref[...])


def _sample_mix(x, sinks, g_pre, w_in, lb_raw, g_out_head, w_out, g_post, cache_kt, cache_vt, state_t, pos, layer):
    nb = x.shape[0]
    bt = nb // N_HGRN_HEADS
    cos, s_hi, s_lo = _rope_tables(pos)
    lb_rows = lb_raw.shape[0]
    blk3 = pl.BlockSpec((bt, D_KV, WINDOW), lambda i: (i, 0, 0))
    blk_s = pl.BlockSpec((HGRN_D, HGRN_D, nb), lambda i: (i, 0, 0))
    chan_major = pltpu.VMEM((D_HGRN, nb), F32)
    return pl.pallas_call(
        functools.partial(_sample_mix_kernel, layer=layer, nb=nb),
        grid=(N_HGRN_HEADS,),
        in_specs=[
            _const_spec((N_Q_HEADS, 1)),
            _const_spec((nb, D_MODEL)),
            _const_spec((1, D_MODEL)),
            _const_spec((D_MODEL, IN_COLS)),
            _const_spec((1, LANES)),
            _const_spec((1, LANES)),
            _const_spec((1, LANES)),
            _const_spec((lb_rows, D_HGRN)),
            _const_spec((HGRN_D, 1)),
            _const_spec((D_MODEL, D_MODEL)),
            _const_spec((1, D_MODEL)),
            blk3, blk3, blk_s,
        ],
        out_specs=[pl.BlockSpec((nb, D_MODEL), lambda i: (0, 0)), blk3, blk3, blk_s],
        out_shape=[
            jax.ShapeDtypeStruct((nb, D_MODEL), F32),
            jax.ShapeDtypeStruct(cache_kt.shape, F32),
            jax.ShapeDtypeStruct(cache_vt.shape, F32),
            jax.ShapeDtypeStruct(state_t.shape, F32),
        ],
        scratch_shapes=[
            pltpu.VMEM((nb, D_ATTN), F32),
            pltpu.VMEM((nb, D_KV), F32),
            pltpu.VMEM((nb, D_KV), F32),
            pltpu.VMEM((3, D_KV, nb), BF16),
            pltpu.VMEM((3, D_KV, nb), BF16),
            chan_major, chan_major, chan_major, chan_major, chan_major, chan_major,
            pltpu.VMEM((nb * N_Q_HEADS, LANES), F32),
        ],
        compiler_params=pltpu.CompilerParams(
            dimension_semantics=("arbitrary",), vmem_limit_bytes=VMEM_LIMIT),
        name="sample_mix",
    )(sinks.reshape(N_Q_HEADS, 1), x, g_pre.reshape(1, D_MODEL), w_in, cos, s_hi, s_lo, lb_raw,
      g_out_head.reshape(HGRN_D, 1), w_out, g_post.reshape(1, D_MODEL), cache_kt, cache_vt, state_t)


def kernel(x_prompt, x_sample, cache_win_k, cache_win_v, state_hgrn, ffn1_pre_g, ffn1_post_g, ffn1_w_gu,
           ffn1_w_down, mix_pre_g, mix_post_g, w_in, attn_sinks, hgrn_lb, hgrn_out_g, w_out, ffn2_pre_g,
           ffn2_post_g, ffn2_w_gu, ffn2_w_down):
    batch, seq, _ = x_prompt.shape
    nb, t_s, _ = x_sample.shape
    depth = w_in.shape[0]
    assert t_s == 1 and seq % (MIX_PARTS * MIX_TILE) == 0 and (batch * seq) % FFN_TILE == 0 and nb == LANES
    assert cache_win_k.shape[2:] == (WINDOW, N_KV_HEADS, HEAD_DIM)

    xp = x_prompt.reshape(batch * seq, D_MODEL)
    xs = x_sample.reshape(nb, D_MODEL)
    pos_s = PAST_LEN + np.arange(t_s)
    outs = [[] for _ in range(6)]
    for l in range(depth):
        w_in_l, w_out_l = w_in[l].astype(BF16), w_out[l].astype(BF16)
        w_in_t = w_in[l][:, OFF_HI:].T.astype(BF16)

        xp, xs = _ffn(xp, xs, ffn1_pre_g[l], ffn1_post_g[l], ffn1_w_gu[l], ffn1_w_down[l])

        xp, wkt_p, wvt_p, s_p = _prompt_mix(xp, attn_sinks[l], mix_pre_g[l], w_in_l, w_in_t, hgrn_lb,
                                            hgrn_out_g[l], w_out_l, mix_post_g[l], batch, seq, l)
        to_kt = lambda c: jnp.transpose(c, (0, 2, 3, 1)).reshape(-1, D_KV, WINDOW)
        from_kt = lambda c: jnp.transpose(c.reshape(-1, N_KV_HEADS, HEAD_DIM, WINDOW), (0, 3, 1, 2))
        state_t = jnp.transpose(state_hgrn[l], (1, 2, 3, 0)).reshape(D_HGRN, HGRN_D, nb)
        xs, wkt_s, wvt_s, st_s = _sample_mix(
            xs, attn_sinks[l], mix_pre_g[l], w_in_l, hgrn_lb, hgrn_out_g[l], w_out_l, mix_post_g[l],
            to_kt(cache_win_k[l]), to_kt(cache_win_v[l]), state_t, pos_s, l)
        s_s = jnp.transpose(st_s.reshape(N_HGRN_HEADS, HGRN_D, HGRN_D, nb), (3, 0, 1, 2))

        xp, xs = _ffn(xp, xs, ffn2_pre_g[l], ffn2_post_g[l], ffn2_w_gu[l], ffn2_w_down[l])

        for lst, val in zip(outs, (from_kt(wkt_p), from_kt(wvt_p), s_p, from_kt(wkt_s), from_kt(wvt_s), s_s)):
            lst.append(val)

    return (xp.reshape(batch, seq, D_MODEL), xs.reshape(nb, t_s, D_MODEL)) + tuple(jnp.stack(o) for o in outs)
```

```python
import functools

import jax
import jax.numpy as jnp
import numpy as np
from jax import lax
from jax.experimental import pallas as pl
from jax.experimental.pallas import tpu as pltpu

F32 = jnp.float32
BF16 = jnp.bfloat16

D_MODEL = 1024
D_FF = 2816
HEAD_DIM = 64
N_Q_HEADS = 8
N_KV_HEADS = 2
GQA_GROUP = N_Q_HEADS // N_KV_HEADS
WINDOW = 128
PAST_LEN = 8192
ROT_DIM = HEAD_DIM // 4
ROPE_THETA = 500000.0
N_HGRN_HEADS = 8
HGRN_D = 64
D_ATTN = N_Q_HEADS * HEAD_DIM
D_KV = N_KV_HEADS * HEAD_DIM
D_HGRN = N_HGRN_HEADS * HGRN_D
IN_COLS = D_ATTN + 2 * D_KV + 4 * D_HGRN
OFF_Q, OFF_K, OFF_V = 0, D_ATTN, D_ATTN + D_KV
OFF_HQ = D_ATTN + 2 * D_KV
OFF_HF, OFF_HI, OFF_HG = OFF_HQ + D_HGRN, OFF_HQ + 2 * D_HGRN, OFF_HQ + 3 * D_HGRN
EPS = 1e-6
NEG_INF = -1e30
LOG2E = 1.4426950408889634
LANES = 128
HALF = LANES // 2

FFN_TILE = 1024
FFN_PARTS = 4
FFN_W_STEPS = 8
MIX_TILE = 256
MIX_PARTS = 4
MIX_SKEW = 0
CHUNK = 64
SUB = 16
N_SUB = CHUNK // SUB
VMEM_LIMIT = 56 * 1024 * 1024


def _rms(x, g):
    return x * lax.rsqrt(jnp.mean(x * x, axis=-1, keepdims=True) + EPS) * g


def _silu(x):
    return x * jax.nn.sigmoid(x)


def _dot(a, b):
    return jnp.dot(a, b, preferred_element_type=F32)


def _dot_nt(a, b):
    return lax.dot_general(a, b, (((1,), (1,)), ((), ())), preferred_element_type=F32)


def _dot_tn(a, b):
    return lax.dot_general(a, b, (((0,), (0,)), ((), ())), preferred_element_type=F32)


def _split3(x):
    hi = x.astype(BF16)
    r = x - hi.astype(F32)
    mid = r.astype(BF16)
    lo = (r - mid.astype(F32)).astype(BF16)
    return hi, mid, lo


def _const_spec(shape):
    nd = len(shape)
    return pl.BlockSpec(shape, lambda *_: (0,) * nd, pipeline_mode=pl.Buffered(1))


def _ffn_kernel(x_ref, xs_ref, gpre_ref, gpost_ref, wgu32_ref, wd32_ref, o_ref, os_ref, wgu_ref, wd_ref):
    step = pl.program_id(0)

    @pl.when(step < FFN_W_STEPS)
    def _():
        gu_rows, d_rows = wgu32_ref.shape[0], wd32_ref.shape[0]
        wgu_ref[pl.ds(pl.multiple_of(step * gu_rows, gu_rows), gu_rows), :] = wgu32_ref[...].astype(BF16)
        wd_ref[pl.ds(pl.multiple_of(step * d_rows, d_rows), d_rows), :] = wd32_ref[...].astype(BF16)

    def half_step(src_ref, dst_ref, rs):
        x = src_ref[rs, :]
        h = _rms(x, gpre_ref[...]).astype(BF16)
        yield
        gate = _dot(h, wgu_ref[:, :D_FF])
        up = _dot(h, wgu_ref[:, D_FF:])
        yield
        act = (_silu(gate) * up).astype(BF16)
        yield
        y = _dot(act, wd_ref[...])
        yield
        dst_ref[rs, :] = x + 0.5 * _rms(y, gpost_ref[...])

    @pl.when(step >= FFN_W_STEPS)
    def _():
        rows = x_ref.shape[0] // FFN_PARTS
        parts = [half_step(x_ref, o_ref, slice(r * rows, (r + 1) * rows)) for r in range(FFN_PARTS)]
        for lead, part in enumerate(parts):
            for _ in range(FFN_PARTS - 1 - lead):
                next(part)
        while parts:
            parts = [part for part in parts if next(part, True) is None]

    @pl.when(step == pl.num_programs(0) - 1)
    def _():
        for _ in half_step(xs_ref, os_ref, slice(None)):
            pass


def _ffn(x, xs, g_pre, g_post, w_gu, w_down):
    n, nb = x.shape[0], xs.shape[0]
    tile_map = lambda i: (jnp.maximum(i - FFN_W_STEPS, 0), 0)
    chunk_map = lambda i: (jnp.minimum(i, FFN_W_STEPS - 1), 0)
    return pl.pallas_call(
        _ffn_kernel,
        grid=(FFN_W_STEPS + n // FFN_TILE,),
        in_specs=[
            pl.BlockSpec((FFN_TILE, D_MODEL), tile_map),
            _const_spec((nb, D_MODEL)),
            _const_spec((1, D_MODEL)),
            _const_spec((1, D_MODEL)),
            pl.BlockSpec((D_MODEL // FFN_W_STEPS, 2 * D_FF), chunk_map),
            pl.BlockSpec((D_FF // FFN_W_STEPS, D_MODEL), chunk_map),
        ],
        out_specs=[pl.BlockSpec((FFN_TILE, D_MODEL), tile_map),
                   pl.BlockSpec((nb, D_MODEL), lambda i: (0, 0))],
        out_shape=[jax.ShapeDtypeStruct((n, D_MODEL), F32), jax.ShapeDtypeStruct((nb, D_MODEL), F32)],
        scratch_shapes=[pltpu.VMEM((D_MODEL, 2 * D_FF), BF16), pltpu.VMEM((D_FF, D_MODEL), BF16)],
        compiler_params=pltpu.CompilerParams(
            dimension_semantics=("arbitrary",), vmem_limit_bytes=VMEM_LIMIT),
        name="ffn",
    )(x, xs, g_pre.reshape(1, D_MODEL), g_post.reshape(1, D_MODEL), w_gu, w_down)


def _lower_bound(lb_raw, layer):
    m = jnp.max(lb_raw, axis=0, keepdims=True)
    e = jnp.exp(lb_raw - m)
    return jnp.sum(e[: layer + 1], axis=0, keepdims=True) / jnp.sum(e, axis=0, keepdims=True)


def _rope(x, cos, sin_hi, sin_lo):
    return x * cos + pltpu.roll(x, ROT_DIM // 2, 1) * sin_hi + pltpu.roll(x, LANES - ROT_DIM // 2, 1) * sin_lo


def _rope_tables(pos):
    half = ROT_DIM // 2
    inv = (np.float32(ROPE_THETA) ** (-np.arange(half, dtype=np.float32) / half)).astype(np.float64)
    ang = np.asarray(pos, np.float64)[:, None] * inv[None, :]
    cos, sin = np.cos(ang), np.sin(ang)
    t = ang.shape[0]
    one = np.ones((t, HEAD_DIM - ROT_DIM))
    zero = np.zeros((t, HEAD_DIM - ROT_DIM))
    zh = np.zeros((t, half))
    c = np.concatenate([cos, cos, one], axis=1)
    s_hi = np.concatenate([zh, sin, zero], axis=1)
    s_lo = np.concatenate([-sin, zh, zero], axis=1)
    return tuple(jnp.asarray(np.tile(a, (1, LANES // HEAD_DIM)), F32) for a in (c, s_hi, s_lo))


def _dup_half(x, g, lo_half):
    xr = pltpu.roll(x, HALF, 1)
    return jnp.where(lo_half, x, xr) if g == 0 else jnp.where(lo_half, xr, x)


def _mix_tile(x, rope_tab, seq_start, src, dst, g_scr,
              sink_ref, gpre_ref, win_ref, wint_ref, lb_ref, goutc_ref, wout_ref, gpost_ref, bias_ref, amask_ref,
              layer):
    tb = MIX_TILE
    u = _rms(x, gpre_ref[...]).astype(BF16)

    def proj(off, width):
        return _dot(u, win_ref[:, off:off + width])

    p_attn = proj(OFF_Q, D_ATTN + 2 * D_KV)
    p_hf = proj(OFF_HF, D_HGRN)
    cos, s_hi, s_lo = rope_tab
    scale = HEAD_DIM ** -0.5 * LOG2E
    q_cols = [_rope(p_attn[:, LANES * j: LANES * (j + 1)], cos, s_hi, s_lo) * scale
              for j in range(D_ATTN // LANES)]
    k_rot = _rope(p_attn[:, OFF_K:OFF_K + D_KV], cos, s_hi, s_lo)
    v_new = p_attn[:, OFF_V:OFF_V + D_KV]

    lane = lax.broadcasted_iota(jnp.int32, (WINDOW, LANES), 1)
    lo_half = lane < HALF
    bias = bias_ref[...]
    if seq_start is None:
        bias_first = bias
    else:
        no_prev = jnp.where(seq_start, NEG_INF, 0.0)
        bias_first = jnp.concatenate([bias[:WINDOW] + no_prev, bias[WINDOW:]], axis=0)

    vt_tile = v_new.T.astype(BF16)
    yield
    k_prev, vt_prev = src["k_prev"], src["vt_prev"]
    scores = []
    for i in range(tb // WINDOW):
        r0 = i * WINDOW
        k_cur = [_dup_half(k_rot[r0:r0 + WINDOW], g, lo_half).astype(BF16) for g in range(N_KV_HEADS)]
        vt_keys = jnp.concatenate([vt_prev, vt_tile[:, r0:r0 + WINDOW]], axis=1)
        for g in range(N_KV_HEADS):
            keys = jnp.concatenate([k_prev[g], k_cur[g]], axis=0)
            heads = range(g * GQA_GROUP, (g + 1) * GQA_GROUP)
            qg = jnp.concatenate(
                [jnp.where(lo_half if h % 2 == 0 else ~lo_half, q_cols[h // 2][r0:r0 + WINDOW], 0.0)
                 for h in heads], axis=0).astype(BF16)
            s = _dot_nt(keys, qg) + (bias_first if i == 0 else bias)
            scores.append((s, vt_keys[g * HEAD_DIM:(g + 1) * HEAD_DIM], heads))
        k_prev, vt_prev = k_cur, vt_tile[:, r0:r0 + WINDOW]
    dst["k_prev"], dst["vt_prev"] = k_prev, vt_prev
    yield

    hq = proj(OFF_HQ, D_HGRN) * (HGRN_D ** -0.5)
    hv_t = _dot_nt(wint_ref[:D_HGRN, :], u).astype(BF16)
    hg_act_t = _silu(_dot_nt(wint_ref[D_HGRN:, :], u))
    yield

    att_t = [[None] * (tb // WINDOW) for _ in range(N_Q_HEADS)]
    probs = []
    for s, _, heads in scores:
        sink = jnp.concatenate(
            [jnp.full((1, WINDOW), sink_ref[h] * LOG2E, F32) for h in heads], axis=1)
        m = jnp.maximum(jnp.max(s, axis=0, keepdims=True), sink)
        p = jnp.exp2(s - m)
        denom = jnp.sum(p, axis=0, keepdims=True) + jnp.exp2(sink - m)
        probs.append((p.astype(BF16), 1.0 / denom))
    yield
    for idx, ((p, r_denom), (_, vt_g, heads)) in enumerate(zip(probs, scores)):
        o_t = _dot(vt_g, p) * r_denom
        for j, h in enumerate(heads):
            att_t[h][idx // N_KV_HEADS] = o_t[:, j * WINDOW:(j + 1) * WINDOW]
    att = jnp.concatenate(
        [jnp.concatenate([jnp.concatenate([att_t[2 * j][i], att_t[2 * j + 1][i]], axis=0).T
                          for j in range(N_Q_HEADS // 2)], axis=1)
         for i in range(tb // WINDOW)], axis=0)
    y_att = _dot(att.astype(BF16), wout_ref[:D_ATTN, :])
    yield

    lb = _lower_bound(lb_ref[...], layer)
    f = lb + (1.0 - lb) * jax.nn.sigmoid(p_hf)
    logf = jnp.log(f) * LOG2E
    hk = 1.0 - f

    tr = lax.broadcasted_iota(jnp.int32, (tb, tb), 0)
    tc = lax.broadcasted_iota(jnp.int32, (tb, tb), 1)
    tri = ((tr // CHUNK == tc // CHUNK) & (tc <= tr)).astype(BF16)
    g_cum = sum(_dot(tri, part) for part in _split3(logf))
    g_scr[...] = g_cum
    yield

    def bcast_row(r, rows):
        return jnp.broadcast_to(g_scr[r:r + 1, :], (rows, D_HGRN))

    zeros_sub = jnp.zeros((SUB, D_HGRN), F32)
    g_ref_q = jnp.concatenate(
        [zeros_sub if sb % N_SUB == 0 else bcast_row(sb * SUB - 1, SUB) for sb in range(tb // SUB)], axis=0)
    q_loc = (hq * jnp.exp2(g_cum - g_ref_q)).astype(BF16)
    g_end = jnp.concatenate(
        [bcast_row(c * CHUNK + CHUNK - 1, CHUNK) for c in range(tb // CHUNK)], axis=0)
    k_sub = []
    for i in range(N_SUB):
        live = (i + 1) * SUB
        pieces = []
        for c in range(tb // CHUNK):
            r0 = c * CHUNK
            g_ref_i = 0.0 if i == 0 else bcast_row(r0 + i * SUB - 1, live)
            pieces.append(hk[r0:r0 + live] * jnp.exp2(g_ref_i - g_cum[r0:r0 + live]))
            if live < CHUNK:
                pieces.append(jnp.zeros((CHUNK - live, D_HGRN), F32))
        k_sub.append(jnp.concatenate(pieces, axis=0).astype(BF16))

    n_ch = tb // CHUNK
    gc = [g_scr[c * CHUNK + CHUNK - 1:c * CHUNK + CHUNK, :] for c in range(n_ch)]

    def span(lo, hi):
        if hi <= lo:
            return jnp.ones((CHUNK, D_HGRN), F32)
        return jnp.broadcast_to(jnp.exp2(sum(gc[lo:hi])), (CHUNK, D_HGRN))

    zeros_chunk = jnp.zeros((CHUNK, D_HGRN), F32)
    k_end32 = hk * jnp.exp2(g_end - g_cum)
    q_glob32 = hq * jnp.exp2(g_cum)
    k_cross = [(k_end32 * jnp.concatenate(
        [span(cp + 1, c) if cp < c else zeros_chunk for cp in range(n_ch)], axis=0)).astype(BF16)
        for c in range(1, n_ch)]
    q_tile = (q_glob32 * jnp.concatenate([span(0, c) for c in range(n_ch)], axis=0)).astype(BF16)
    k_tile_end = (k_end32 * jnp.concatenate([span(cp + 1, n_ch) for cp in range(n_ch)], axis=0)).astype(BF16)
    decay_tile = jnp.exp2(sum(gc))
    q_glob = q_glob32.astype(BF16)

    def keep_rows(a, block, wanted):
        zero = jnp.zeros((block, a.shape[1]), a.dtype)
        return jnp.concatenate(
            [a[r * block:(r + 1) * block] if wanted(r) else zero for r in range(a.shape[0] // block)], axis=0)

    q_sub = [keep_rows(q_loc, SUB, lambda r, i=i: r % N_SUB == i) for i in range(N_SUB)]
    q_cross = [keep_rows(q_glob, CHUNK, lambda r, c=c: r == c) for c in range(1, n_ch)]
    yield
    amask_t = amask_ref[...]
    y = y_att
    out_group = 2 * LANES // HGRN_D
    for first in range(0, N_HGRN_HEADS, out_group):
        heads = range(first, first + out_group)
        lanes = {h: slice(h * HGRN_D, (h + 1) * HGRN_D) for h in heads}
        pair = {}
        for h, ls in lanes.items():
            q_stack = jnp.concatenate([q_sub[i][:, ls] for i in range(N_SUB)], axis=1)
            k_stack = jnp.concatenate([k_sub[i][:, ls] for i in range(N_SUB)], axis=1)
            qc_stack = jnp.concatenate([q_cross[c - 1][:, ls] for c in range(1, n_ch)], axis=1)
            kc_stack = jnp.concatenate([k_cross[c - 1][:, ls] for c in range(1, n_ch)], axis=1)
            pair[h] = (_dot_nt(k_stack, q_stack), _dot_nt(kc_stack, qc_stack))
        yield
        o_t = {}
        for h, ls in lanes.items():
            same, cross = pair[h]
            half = tb // 2
            a_t = jnp.concatenate([
                jnp.concatenate([same[:half, :half] * amask_t[:half, :half] + cross[:half, :half],
                                 cross[:half, half:]], axis=1),
                jnp.concatenate([jnp.zeros((half, half), F32),
                                 same[half:, half:] * amask_t[half:, half:] + cross[half:, half:]], axis=1),
            ], axis=0).astype(BF16)
            vt_h = hv_t[ls, :]
            state_h = src["state"][h]
            o_t[h] = _dot(vt_h, a_t) + _dot_nt(state_h.astype(BF16), q_tile[:, ls])
            dst["state"][h] = state_h * decay_tile[:, ls] + _dot(vt_h, k_tile_end[:, ls])
        yield
        hn_t = {h: o_t[h] * lax.rsqrt(jnp.mean(o_t[h] * o_t[h], axis=0, keepdims=True) + EPS)
                * goutc_ref[...] * hg_act_t[ls, :] for h, ls in lanes.items()}
        grp = jnp.concatenate(
            [jnp.concatenate(
                [jnp.concatenate([hn_t[j], hn_t[j + 1]], axis=0)[:, r:r + LANES].T for j in heads[::2]],
                axis=1) for r in range(0, tb, LANES)], axis=0).astype(BF16)
        y = y + _dot(grp, wout_ref[D_ATTN + first * HGRN_D:D_ATTN + (first + out_group) * HGRN_D, :])
        yield

    dst["y"], dst["k_rot"], dst["v_new"] = x + _rms(y, gpost_ref[...]), k_rot, v_new


def _prompt_mix_kernel(sink_ref, x_ref, gpre_ref, win_ref, wint_ref, cos_ref, shi_ref, slo_ref, lb_ref, goutc_ref,
                       wout_ref, gpost_ref, bias_ref, amask_ref,
                       y_ref, wk_ref, wv_ref, s_ref,
                       kk_scr, vt_scr, st_scr, g_scr, *, layer):
    tb = MIX_TILE
    step = pl.program_id(1)
    last = pl.num_programs(1) - 1

    @pl.when(step == 0)
    def _():
        kk_scr[...] = jnp.zeros_like(kk_scr)
        vt_scr[...] = jnp.zeros_like(vt_scr)
        st_scr[...] = jnp.zeros_like(st_scr)

    hand = [{"state": [None] * N_HGRN_HEADS} for _ in range(MIX_PARTS + 1)]
    hand[0]["k_prev"] = [kk_scr[g] for g in range(N_KV_HEADS)]
    hand[0]["vt_prev"] = vt_scr[...]
    hand[0]["state"] = [st_scr[h] for h in range(N_HGRN_HEADS)]
    tiles = []
    for part in range(MIX_PARTS):
        rs = slice(part * tb, (part + 1) * tb)
        tiles.append(_mix_tile(
            x_ref[rs, :], (cos_ref[rs, :], shi_ref[rs, :], slo_ref[rs, :]), step == 0 if part == 0 else None,
            hand[part], hand[part + 1], g_scr.at[part],
            sink_ref, gpre_ref, win_ref, wint_ref, lb_ref, goutc_ref, wout_ref, gpost_ref, bias_ref, amask_ref,
            layer))
    for lead, t in enumerate(tiles):
        for _ in range(MIX_SKEW * (len(tiles) - 1 - lead)):
            next(t)
    while tiles:
        tiles = [t for t in tiles if next(t, True) is None]
    for part in range(MIX_PARTS):
        y_ref[part * tb:(part + 1) * tb, :] = hand[part + 1]["y"]
    final = hand[MIX_PARTS]
    for g in range(N_KV_HEADS):
        kk_scr[g] = final["k_prev"][g]
    vt_scr[...] = final["vt_prev"]
    for h in range(N_HGRN_HEADS):
        st_scr[h] = final["state"][h]

    @pl.when(step == last)
    def _():
        wk_ref[0] = final["k_rot"][tb - WINDOW:].T
        wv_ref[0] = final["v_new"][tb - WINDOW:].T
        for h in range(N_HGRN_HEADS):
            s_ref[0, h] = final["state"][h].T


def _prompt_mix(x, sinks, g_pre, w_in, w_in_t, lb_raw, g_out_head, w_out, g_post, batch, seq, layer):
    tb = MIX_TILE
    rows = MIX_PARTS * tb
    nt = seq // rows
    cos, s_hi, s_lo = _rope_tables(np.arange(seq))
    tok = lambda b, n: (b * nt + n, 0)
    tab = lambda b, n: (n, 0)
    per_b3 = lambda b, n: (b, 0, 0)
    lb_rows = lb_raw.shape[0]
    key_i = np.arange(2 * WINDOW)[:, None]
    rel = np.arange(WINDOW)[None, :] + WINDOW - key_i
    bias = jnp.asarray(np.tile(np.where((rel >= 0) & (rel < WINDOW), 0.0, NEG_INF), (1, GQA_GROUP)), F32)
    t_i = np.arange(tb)
    amask = jnp.asarray((t_i[:, None] // CHUNK == t_i[None, :] // CHUNK) & (t_i[:, None] <= t_i[None, :]), F32)
    return pl.pallas_call(
        functools.partial(_prompt_mix_kernel, layer=layer),
        grid=(batch, nt),
        in_specs=[
            pl.BlockSpec(memory_space=pltpu.SMEM),
            pl.BlockSpec((rows, D_MODEL), tok),
            _const_spec((1, D_MODEL)),
            _const_spec((D_MODEL, IN_COLS)),
            _const_spec((2 * D_HGRN, D_MODEL)),
            pl.BlockSpec((rows, LANES), tab),
            pl.BlockSpec((rows, LANES), tab),
            pl.BlockSpec((rows, LANES), tab),
            _const_spec((lb_rows, D_HGRN)),
            _const_spec((HGRN_D, 1)),
            _const_spec((D_MODEL, D_MODEL)),
            _const_spec((1, D_MODEL)),
            _const_spec((2 * WINDOW, GQA_GROUP * WINDOW)),
            _const_spec((tb, tb)),
        ],
        out_specs=[
            pl.BlockSpec((rows, D_MODEL), tok),
            pl.BlockSpec((1, D_KV, WINDOW), per_b3),
            pl.BlockSpec((1, D_KV, WINDOW), per_b3),
            pl.BlockSpec((1, N_HGRN_HEADS, HGRN_D, HGRN_D), lambda b, n: (b, 0, 0, 0)),
        ],
        out_shape=[
            jax.ShapeDtypeStruct((batch * seq, D_MODEL), F32),
            jax.ShapeDtypeStruct((batch, D_KV, WINDOW), F32),
            jax.ShapeDtypeStruct((batch, D_KV, WINDOW), F32),
            jax.ShapeDtypeStruct((batch, N_HGRN_HEADS, HGRN_D, HGRN_D), F32),
        ],
        scratch_shapes=[
            pltpu.VMEM((N_KV_HEADS, WINDOW, LANES), BF16),
            pltpu.VMEM((D_KV, WINDOW), BF16),
            pltpu.VMEM((N_HGRN_HEADS, HGRN_D, HGRN_D), F32),
            pltpu.VMEM((MIX_PARTS, tb, D_HGRN), F32),
        ],
        compiler_params=pltpu.CompilerParams(
            dimension_semantics=("arbitrary", "arbitrary"), vmem_limit_bytes=VMEM_LIMIT),
        name="prompt_mix",
    )(sinks, x, g_pre.reshape(1, D_MODEL), w_in, w_in_t, cos, s_hi, s_lo, lb_raw, g_out_head.reshape(HGRN_D, 1),
      w_out, g_post.reshape(1, D_MODEL), bias, amask)


def _sample_mix_kernel(sink_ref, x_ref, gpre_ref, win_ref, cos_ref, shi_ref, slo_ref, lb_ref, goutc_ref,
                       wout_ref, gpost_ref, ckt_ref, cvt_ref, sin_ref,
                       y_ref, wkt_ref, wvt_ref, sout_ref,
                       q_scr, kn_scr, vn_scr, knt_scr, vnt_scr, ft_scr, hkt_scr, hqt_scr, hvt_scr, hgt_scr,
                       ot_scr, att_scr, *, layer, nb):
    step = pl.program_id(0)
    last = pl.num_programs(0) - 1
    bt = nb // N_HGRN_HEADS

    @pl.when(step == 0)
    def _():
        u = _rms(x_ref[...], gpre_ref[...]).astype(BF16)
        proj = _dot(u, win_ref[...])
        cos, s_hi, s_lo = cos_ref[...], shi_ref[...], slo_ref[...]
        scale = HEAD_DIM ** -0.5
        for j in range(D_ATTN // LANES):
            q_scr[:, LANES * j:LANES * (j + 1)] = _rope(
                proj[:, OFF_Q + LANES * j:OFF_Q + LANES * (j + 1)], cos, s_hi, s_lo) * scale
        k_new = _rope(proj[:, OFF_K:OFF_K + D_KV], cos, s_hi, s_lo)
        v_new = proj[:, OFF_V:OFF_V + D_KV]
        kn_scr[...] = k_new
        vn_scr[...] = v_new
        for scr, val in ((knt_scr, k_new), (vnt_scr, v_new)):
            for i, part in enumerate(_split3(val.T)):
                scr[i] = part
        lb = _lower_bound(lb_ref[...], layer)
        f_t = (lb + (1.0 - lb) * jax.nn.sigmoid(proj[:, OFF_HF:OFF_HF + D_HGRN])).T
        ft_scr[...] = f_t
        hkt_scr[...] = 1.0 - f_t
        hqt_scr[...] = (proj[:, OFF_HQ:OFF_HQ + D_HGRN] * (HGRN_D ** -0.5)).T
        hvt_scr[...] = proj[:, OFF_HI:OFF_HI + D_HGRN].T
        hgt_scr[...] = proj[:, OFF_HG:OFF_HG + D_HGRN].T

    base = pl.multiple_of(step * HGRN_D, HGRN_D)
    hv_t = hvt_scr[pl.ds(base, HGRN_D), :]

    def hgrn_row(k, o_acc):
        f_row = ft_scr[pl.ds(base + k, 1), :]
        s_new = f_row * sin_ref[k] + hkt_scr[pl.ds(base + k, 1), :] * hv_t
        sout_ref[k] = s_new
        return o_acc + hqt_scr[pl.ds(base + k, 1), :] * s_new

    ot_scr[pl.ds(base, HGRN_D), :] = lax.fori_loop(
        0, HGRN_D, hgrn_row, jnp.zeros((HGRN_D, nb), F32), unroll=8)

    lane8 = lax.broadcasted_iota(jnp.int32, (N_Q_HEADS, LANES), 1)
    row8 = lax.broadcasted_iota(jnp.int32, (N_Q_HEADS, LANES), 0)
    keep8 = (lane8 >= HALF) == (row8 >= GQA_GROUP)
    win_lane = lax.broadcasted_iota(jnp.int32, (D_KV, WINDOW), 1)
    sink = sink_ref[...]
    b0 = step * bt
    sel = (lax.broadcasted_iota(jnp.int32, (nb, bt), 0)
           == b0 + lax.broadcasted_iota(jnp.int32, (nb, bt), 1)).astype(BF16)
    k_cols = sum(_dot(knt_scr[i], sel) for i in range(3))
    v_cols = sum(_dot(vnt_scr[i], sel) for i in range(3))
    def one_row(bi):
        b = b0 + bi
        q_b = jnp.broadcast_to(q_scr[pl.ds(b, 1), :], (N_Q_HEADS, D_ATTN))
        qm = jnp.zeros((N_Q_HEADS, LANES), F32)
        for h in range(N_Q_HEADS):
            c = q_b[:, LANES * (h // 2):LANES * (h // 2 + 1)]
            if h % 2 != h // GQA_GROUP:
                c = pltpu.roll(c, HALF, 1)
            qm = jnp.where(row8 == h, c, qm)
        qm = jnp.where(keep8, qm, 0.0)
        k_new = kn_scr[pl.ds(b, 1), :]
        v_new = vn_scr[pl.ds(b, 1), :]
        k_old = ckt_ref[bi]
        s = _dot(qm.astype(BF16), k_old.astype(BF16))
        wkt_ref[bi] = jnp.where(win_lane == WINDOW - 1, k_cols[:, bi:bi + 1], pltpu.roll(k_old, WINDOW - 1, 1))
        yield
        s = jnp.where(lane8 >= 1, s, NEG_INF)
        s_new = jnp.sum(qm * k_new, axis=-1, keepdims=True)
        m = jnp.maximum(jnp.maximum(jnp.max(s, axis=-1, keepdims=True), s_new), sink)
        p = jnp.exp(s - m)
        p_new = jnp.exp(s_new - m)
        denom = jnp.sum(p, axis=-1, keepdims=True) + p_new + jnp.exp(sink - m)
        yield
        v_old = cvt_ref[bi]
        o = (_dot_nt(p.astype(BF16), v_old.astype(BF16)) + p_new * v_new) / denom
        att_scr[pl.ds(pl.multiple_of(b * N_Q_HEADS, N_Q_HEADS), N_Q_HEADS), :] = o
        wvt_ref[bi] = jnp.where(win_lane == WINDOW - 1, v_cols[:, bi:bi + 1], pltpu.roll(v_old, WINDOW - 1, 1))

    rows = [one_row(bi) for bi in range(bt)]
    while rows:
        rows = [r for r in rows if next(r, True) is None]

    @pl.when(step == last)
    def _():
        y = jnp.zeros((nb, D_MODEL), F32)
        for h in range(N_HGRN_HEADS):
            ks = slice(h * HGRN_D, (h + 1) * HGRN_D)
            o_t = ot_scr[ks, :]
            hn_t = (o_t * lax.rsqrt(jnp.mean(o_t * o_t, axis=0, keepdims=True) + EPS)
                    * goutc_ref[...] * _silu(hgt_scr[ks, :]))
            y = y + _dot_tn(hn_t.astype(BF16), wout_ref[D_ATTN + h * HGRN_D:D_ATTN + (h + 1) * HGRN_D, :])
        for h in range(N_Q_HEADS):
            g = h // GQA_GROUP
            a_h = att_scr[pl.ds(h, nb, stride=N_Q_HEADS), :][:, g * HALF:(g + 1) * HALF]
            y = y + _dot(a_h.astype(BF16), wout_ref[h * HEAD_DIM:(h + 1) * HEAD_DIM, :])
        y_ref[...] = x_ref[...] + _rms(y, gpost_ref[...])


def _sample_mix(x, sinks, g_pre, w_in, lb_raw, g_out_head, w_out, g_post, cache_kt, cache_vt, state_t, pos, layer):
    nb = x.shape[0]
    bt = nb // N_HGRN_HEADS
    cos, s_hi, s_lo = _rope_tables(pos)
    lb_rows = lb_raw.shape[0]
    blk3 = pl.BlockSpec((bt, D_KV, WINDOW), lambda i: (i, 0, 0))
    blk_s = pl.BlockSpec((HGRN_D, HGRN_D, nb), lambda i: (i, 0, 0))
    chan_major = pltpu.VMEM((D_HGRN, nb), F32)
    return pl.pallas_call(
        functools.partial(_sample_mix_kernel, layer=layer, nb=nb),
        grid=(N_HGRN_HEADS,),
        in_specs=[
            _const_spec((N_Q_HEADS, 1)),
            _const_spec((nb, D_MODEL)),
            _const_spec((1, D_MODEL)),
            _const_spec((D_MODEL, IN_COLS)),
            _const_spec((1, LANES)),
            _const_spec((1, LANES)),
            _const_spec((1, LANES)),
            _const_spec((lb_rows, D_HGRN)),
            _const_spec((HGRN_D, 1)),
            _const_spec((D_MODEL, D_MODEL)),
            _const_spec((1, D_MODEL)),
            blk3, blk3, blk_s,
        ],
        out_specs=[pl.BlockSpec((nb, D_MODEL), lambda i: (0, 0)), blk3, blk3, blk_s],
        out_shape=[
            jax.ShapeDtypeStruct((nb, D_MODEL), F32),
            jax.ShapeDtypeStruct(cache_kt.shape, F32),
            jax.ShapeDtypeStruct(cache_vt.shape, F32),
            jax.ShapeDtypeStruct(state_t.shape, F32),
        ],
        scratch_shapes=[
            pltpu.VMEM((nb, D_ATTN), F32),
            pltpu.VMEM((nb, D_KV), F32),
            pltpu.VMEM((nb, D_KV), F32),
            pltpu.VMEM((3, D_KV, nb), BF16),
            pltpu.VMEM((3, D_KV, nb), BF16),
            chan_major, chan_major, chan_major, chan_major, chan_major, chan_major,
            pltpu.VMEM((nb * N_Q_HEADS, LANES), F32),
        ],
        compiler_params=pltpu.CompilerParams(
            dimension_semantics=("arbitrary",), vmem_limit_bytes=VMEM_LIMIT),
        name="sample_mix",
    )(sinks.reshape(N_Q_HEADS, 1), x, g_pre.reshape(1, D_MODEL), w_in, cos, s_hi, s_lo, lb_raw,
      g_out_head.reshape(HGRN_D, 1), w_out, g_post.reshape(1, D_MODEL), cache_kt, cache_vt, state_t)


def kernel(x_prompt, x_sample, cache_win_k, cache_win_v, state_hgrn, ffn1_pre_g, ffn1_post_g, ffn1_w_gu,
           ffn1_w_down, mix_pre_g, mix_post_g, w_in, attn_sinks, hgrn_lb, hgrn_out_g, w_out, ffn2_pre_g,
           ffn2_post_g, ffn2_w_gu, ffn2_w_down):
    batch, seq, _ = x_prompt.shape
    nb, t_s, _ = x_sample.shape
    depth = w_in.shape[0]
    assert t_s == 1 and seq % (MIX_PARTS * MIX_TILE) == 0 and (batch * seq) % FFN_TILE == 0 and nb == LANES
    assert cache_win_k.shape[2:] == (WINDOW, N_KV_HEADS, HEAD_DIM)

    xp = x_prompt.reshape(batch * seq, D_MODEL)
    xs = x_sample.reshape(nb, D_MODEL)
    pos_s = PAST_LEN + np.arange(t_s)
    outs = [[] for _ in range(6)]
    for l in range(depth):
        w_in_l, w_out_l = w_in[l].astype(BF16), w_out[l].astype(BF16)
        w_in_t = w_in[l][:, OFF_HI:].T.astype(BF16)

        xp, xs = _ffn(xp, xs, ffn1_pre_g[l], ffn1_post_g[l], ffn1_w_gu[l], ffn1_w_down[l])

        xp, wkt_p, wvt_p, s_p = _prompt_mix(xp, attn_sinks[l], mix_pre_g[l], w_in_l, w_in_t, hgrn_lb,
                                            hgrn_out_g[l], w_out_l, mix_post_g[l], batch, seq, l)
        to_kt = lambda c: jnp.transpose(c, (0, 2, 3, 1)).reshape(-1, D_KV, WINDOW)
        from_kt = lambda c: jnp.transpose(c.reshape(-1, N_KV_HEADS, HEAD_DIM, WINDOW), (0, 3, 1, 2))
        state_t = jnp.transpose(state_hgrn[l], (1, 2, 3, 0)).reshape(D_HGRN, HGRN_D, nb)
        xs, wkt_s, wvt_s, st_s = _sample_mix(
            xs, attn_sinks[l], mix_pre_g[l], w_in_l, hgrn_lb, hgrn_out_g[l], w_out_l, mix_post_g[l],
            to_kt(cache_win_k[l]), to_kt(cache_win_v[l]), state_t, pos_s, l)
        s_s = jnp.transpose(st_s.reshape(N_HGRN_HEADS, HGRN_D, HGRN_D, nb), (3, 0, 1, 2))

        xp, xs = _ffn(xp, xs, ffn2_pre_g[l], ffn2_post_g[l], ffn2_w_gu[l], ffn2_w_down[l])

        for lst, val in zip(outs, (from_kt(wkt_p), from_kt(wvt_p), s_p, from_kt(wkt_s), from_kt(wvt_s), s_s)):
            lst.append(val)

    return (xp.reshape(batch, seq, D_MODEL), xs.reshape(nb, t_s, D_MODEL)) + tuple(jnp.stack(o) for o in outs)
```

```python
import functools

import jax
import jax.numpy as jnp
import numpy as np
from jax import lax
from jax.experimental import pallas as pl
from jax.experimental.pallas import tpu as pltpu

F32 = jnp.float32
BF16 = jnp.bfloat16

D_MODEL = 1024
D_FF = 2816
HEAD_DIM = 64
N_Q_HEADS = 8
N_KV_HEADS = 2
GQA_GROUP = N_Q_HEADS // N_KV_HEADS
WINDOW = 128
PAST_LEN = 8192
ROT_DIM = HEAD_DIM // 4
ROPE_THETA = 500000.0
N_HGRN_HEADS = 8
HGRN_D = 64
D_ATTN = N_Q_HEADS * HEAD_DIM
D_KV = N_KV_HEADS * HEAD_DIM
D_HGRN = N_HGRN_HEADS * HGRN_D
IN_COLS = D_ATTN + 2 * D_KV + 4 * D_HGRN
OFF_Q, OFF_K, OFF_V = 0, D_ATTN, D_ATTN + D_KV
OFF_HQ = D_ATTN + 2 * D_KV
OFF_HF, OFF_HI, OFF_HG = OFF_HQ + D_HGRN, OFF_HQ + 2 * D_HGRN, OFF_HQ + 3 * D_HGRN
EPS = 1e-6
NEG_INF = -1e30
LOG2E = 1.4426950408889634
LANES = 128
HALF = LANES // 2

FFN_TILE = 1024
FFN_PARTS = 4
FFN_W_STEPS = 8
MIX_TILE = 256
MIX_PARTS = 4
CHUNK = 64
SUB = 16
N_SUB = CHUNK // SUB
VMEM_LIMIT = 56 * 1024 * 1024


def _rms(x, g):
    return x * lax.rsqrt(jnp.mean(x * x, axis=-1, keepdims=True) + EPS) * g


def _silu(x):
    return x * jax.nn.sigmoid(x)


def _dot(a, b):
    return jnp.dot(a, b, preferred_element_type=F32)


def _dot_nt(a, b):
    return lax.dot_general(a, b, (((1,), (1,)), ((), ())), preferred_element_type=F32)


def _dot_tn(a, b):
    return lax.dot_general(a, b, (((0,), (0,)), ((), ())), preferred_element_type=F32)


def _split3(x):
    hi = x.astype(BF16)
    r = x - hi.astype(F32)
    mid = r.astype(BF16)
    lo = (r - mid.astype(F32)).astype(BF16)
    return hi, mid, lo


def _interleave(gens, skew):
    pending, active, rnd = list(gens), [], 0
    while pending or active:
        while pending and (skew == 0 or rnd % skew == 0):
            active.append(pending.pop(0))
            if skew:
                break
        active = [g for g in active if next(g, True) is None]
        rnd += 1


def _const_spec(shape):
    nd = len(shape)
    return pl.BlockSpec(shape, lambda *_: (0,) * nd, pipeline_mode=pl.Buffered(1))


def _ffn_kernel(x_ref, xs_ref, gpre_ref, gpost_ref, wgu32_ref, wd32_ref, o_ref, os_ref, wgu_ref, wd_ref):
    step = pl.program_id(0)

    @pl.when(step < FFN_W_STEPS)
    def _():
        gu_rows, d_rows = wgu32_ref.shape[0], wd32_ref.shape[0]
        wgu_ref[pl.ds(pl.multiple_of(step * gu_rows, gu_rows), gu_rows), :] = wgu32_ref[...].astype(BF16)
        wd_ref[pl.ds(pl.multiple_of(step * d_rows, d_rows), d_rows), :] = wd32_ref[...].astype(BF16)

    def half_step(src_ref, dst_ref, rs):
        x = src_ref[rs, :]
        h = _rms(x, gpre_ref[...]).astype(BF16)
        yield
        gate = _dot(h, wgu_ref[:, :D_FF])
        up = _dot(h, wgu_ref[:, D_FF:])
        yield
        act = (_silu(gate) * up).astype(BF16)
        yield
        y = _dot(act, wd_ref[...])
        yield
        dst_ref[rs, :] = x + 0.5 * _rms(y, gpost_ref[...])

    @pl.when(step >= FFN_W_STEPS)
    def _():
        rows = x_ref.shape[0] // FFN_PARTS
        _interleave([half_step(x_ref, o_ref, slice(r * rows, (r + 1) * rows)) for r in range(FFN_PARTS)], skew=1)

    @pl.when(step == pl.num_programs(0) - 1)
    def _():
        for _ in half_step(xs_ref, os_ref, slice(None)):
            pass


def _ffn(x, xs, g_pre, g_post, w_gu, w_down):
    n, nb = x.shape[0], xs.shape[0]
    tile_map = lambda i: (jnp.maximum(i - FFN_W_STEPS, 0), 0)
    chunk_map = lambda i: (jnp.minimum(i, FFN_W_STEPS - 1), 0)
    return pl.pallas_call(
        _ffn_kernel,
        grid=(FFN_W_STEPS + n // FFN_TILE,),
        in_specs=[
            pl.BlockSpec((FFN_TILE, D_MODEL), tile_map),
            _const_spec((nb, D_MODEL)),
            _const_spec((1, D_MODEL)),
            _const_spec((1, D_MODEL)),
            pl.BlockSpec((D_MODEL // FFN_W_STEPS, 2 * D_FF), chunk_map),
            pl.BlockSpec((D_FF // FFN_W_STEPS, D_MODEL), chunk_map),
        ],
        out_specs=[pl.BlockSpec((FFN_TILE, D_MODEL), tile_map),
                   pl.BlockSpec((nb, D_MODEL), lambda i: (0, 0))],
        out_shape=[jax.ShapeDtypeStruct((n, D_MODEL), F32), jax.ShapeDtypeStruct((nb, D_MODEL), F32)],
        scratch_shapes=[pltpu.VMEM((D_MODEL, 2 * D_FF), BF16), pltpu.VMEM((D_FF, D_MODEL), BF16)],
        compiler_params=pltpu.CompilerParams(
            dimension_semantics=("arbitrary",), vmem_limit_bytes=VMEM_LIMIT),
        name="ffn",
    )(x, xs, g_pre.reshape(1, D_MODEL), g_post.reshape(1, D_MODEL), w_gu, w_down)


def _lower_bound(lb_raw, layer):
    m = jnp.max(lb_raw, axis=0, keepdims=True)
    e = jnp.exp(lb_raw - m)
    return jnp.sum(e[: layer + 1], axis=0, keepdims=True) / jnp.sum(e, axis=0, keepdims=True)


def _rope(x, cos, sin_hi, sin_lo):
    return x * cos + pltpu.roll(x, ROT_DIM // 2, 1) * sin_hi + pltpu.roll(x, LANES - ROT_DIM // 2, 1) * sin_lo


def _rope_tables(pos):
    half = ROT_DIM // 2
    inv = (np.float32(ROPE_THETA) ** (-np.arange(half, dtype=np.float32) / half)).astype(np.float64)
    ang = np.asarray(pos, np.float64)[:, None] * inv[None, :]
    cos, sin = np.cos(ang), np.sin(ang)
    t = ang.shape[0]
    one = np.ones((t, HEAD_DIM - ROT_DIM))
    zero = np.zeros((t, HEAD_DIM - ROT_DIM))
    zh = np.zeros((t, half))
    c = np.concatenate([cos, cos, one], axis=1)
    s_hi = np.concatenate([zh, sin, zero], axis=1)
    s_lo = np.concatenate([-sin, zh, zero], axis=1)
    return tuple(jnp.asarray(np.tile(a, (1, LANES // HEAD_DIM)), F32) for a in (c, s_hi, s_lo))


def _dup_half(x, g, lo_half):
    xr = pltpu.roll(x, HALF, 1)
    return jnp.where(lo_half, x, xr) if g == 0 else jnp.where(lo_half, xr, x)


def _mix_tile(x, rope_tab, seq_start, src, dst, g_scr,
              sink_ref, gpre_ref, win_ref, wint_ref, lb_ref, goutc_ref, wout_ref, gpost_ref, bias_ref, amask_ref,
              layer):
    tb = MIX_TILE
    u = _rms(x, gpre_ref[...]).astype(BF16)

    def proj(off, width):
        return _dot(u, win_ref[:, off:off + width])

    p_attn = proj(OFF_Q, D_ATTN + 2 * D_KV)
    p_hf = proj(OFF_HF, D_HGRN)
    cos, s_hi, s_lo = rope_tab
    scale = HEAD_DIM ** -0.5 * LOG2E
    q_cols = [_rope(p_attn[:, LANES * j: LANES * (j + 1)], cos, s_hi, s_lo) * scale
              for j in range(D_ATTN // LANES)]
    k_rot = _rope(p_attn[:, OFF_K:OFF_K + D_KV], cos, s_hi, s_lo)
    v_new = p_attn[:, OFF_V:OFF_V + D_KV]

    lane = lax.broadcasted_iota(jnp.int32, (WINDOW, LANES), 1)
    lo_half = lane < HALF
    bias = bias_ref[...]
    if seq_start is None:
        bias_first = bias
    else:
        no_prev = jnp.where(seq_start, NEG_INF, 0.0)
        bias_first = jnp.concatenate([bias[:WINDOW] + no_prev, bias[WINDOW:]], axis=0)

    vt_tile = v_new.T.astype(BF16)
    yield
    k_prev, vt_prev = src["k_prev"], src["vt_prev"]
    scores = []
    for i in range(tb // WINDOW):
        r0 = i * WINDOW
        k_cur = [_dup_half(k_rot[r0:r0 + WINDOW], g, lo_half).astype(BF16) for g in range(N_KV_HEADS)]
        vt_keys = jnp.concatenate([vt_prev, vt_tile[:, r0:r0 + WINDOW]], axis=1)
        for g in range(N_KV_HEADS):
            keys = jnp.concatenate([k_prev[g], k_cur[g]], axis=0)
            heads = range(g * GQA_GROUP, (g + 1) * GQA_GROUP)
            qg = jnp.concatenate(
                [jnp.where(lo_half if h % 2 == 0 else ~lo_half, q_cols[h // 2][r0:r0 + WINDOW], 0.0)
                 for h in heads], axis=0).astype(BF16)
            s = _dot_nt(keys, qg) + (bias_first if i == 0 else bias)
            scores.append((s, vt_keys[g * HEAD_DIM:(g + 1) * HEAD_DIM], heads))
        k_prev, vt_prev = k_cur, vt_tile[:, r0:r0 + WINDOW]
    dst["k_prev"], dst["vt_prev"] = k_prev, vt_prev
    yield

    hq = proj(OFF_HQ, D_HGRN) * (HGRN_D ** -0.5)
    hv_t = _dot_nt(wint_ref[:D_HGRN, :], u).astype(BF16)
    hg_act_t = _silu(_dot_nt(wint_ref[D_HGRN:, :], u))
    yield

    att_t = [[None] * (tb // WINDOW) for _ in range(N_Q_HEADS)]
    probs = []
    for s, _, heads in scores:
        sink = jnp.concatenate(
            [jnp.full((1, WINDOW), sink_ref[h] * LOG2E, F32) for h in heads], axis=1)
        m = jnp.maximum(jnp.max(s, axis=0, keepdims=True), sink)
        p = jnp.exp2(s - m)
        denom = jnp.sum(p, axis=0, keepdims=True) + jnp.exp2(sink - m)
        probs.append((p.astype(BF16), 1.0 / denom))
    yield
    for idx, ((p, r_denom), (_, vt_g, heads)) in enumerate(zip(probs, scores)):
        o_t = _dot(vt_g, p) * r_denom
        for j, h in enumerate(heads):
            att_t[h][idx // N_KV_HEADS] = o_t[:, j * WINDOW:(j + 1) * WINDOW]
    att = jnp.concatenate(
        [jnp.concatenate([jnp.concatenate([att_t[2 * j][i], att_t[2 * j + 1][i]], axis=0).T
                          for j in range(N_Q_HEADS // 2)], axis=1)
         for i in range(tb // WINDOW)], axis=0)
    y_att = _dot(att.astype(BF16), wout_ref[:D_ATTN, :])
    yield

    lb = _lower_bound(lb_ref[...], layer)
    f = lb + (1.0 - lb) * jax.nn.sigmoid(p_hf)
    logf = jnp.log(f) * LOG2E
    hk = 1.0 - f

    tr = lax.broadcasted_iota(jnp.int32, (tb, tb), 0)
    tc = lax.broadcasted_iota(jnp.int32, (tb, tb), 1)
    tri = ((tr // CHUNK == tc // CHUNK) & (tc <= tr)).astype(BF16)
    g_cum = sum(_dot(tri, part) for part in _split3(logf))
    g_scr[...] = g_cum
    yield

    def bcast_row(r, rows):
        return jnp.broadcast_to(g_scr[r:r + 1, :], (rows, D_HGRN))

    zeros_sub = jnp.zeros((SUB, D_HGRN), F32)
    g_ref_q = jnp.concatenate(
        [zeros_sub if sb % N_SUB == 0 else bcast_row(sb * SUB - 1, SUB) for sb in range(tb // SUB)], axis=0)
    q_loc = (hq * jnp.exp2(g_cum - g_ref_q)).astype(BF16)
    g_end = jnp.concatenate(
        [bcast_row(c * CHUNK + CHUNK - 1, CHUNK) for c in range(tb // CHUNK)], axis=0)
    k_sub = []
    for i in range(N_SUB):
        live = (i + 1) * SUB
        pieces = []
        for c in range(tb // CHUNK):
            r0 = c * CHUNK
            g_ref_i = 0.0 if i == 0 else bcast_row(r0 + i * SUB - 1, live)
            pieces.append(hk[r0:r0 + live] * jnp.exp2(g_ref_i - g_cum[r0:r0 + live]))
            if live < CHUNK:
                pieces.append(jnp.zeros((CHUNK - live, D_HGRN), F32))
        k_sub.append(jnp.concatenate(pieces, axis=0).astype(BF16))

    n_ch = tb // CHUNK
    gc = [g_scr[c * CHUNK + CHUNK - 1:c * CHUNK + CHUNK, :] for c in range(n_ch)]

    def span(lo, hi):
        if hi <= lo:
            return jnp.ones((CHUNK, D_HGRN), F32)
        return jnp.broadcast_to(jnp.exp2(sum(gc[lo:hi])), (CHUNK, D_HGRN))

    zeros_chunk = jnp.zeros((CHUNK, D_HGRN), F32)
    k_end32 = hk * jnp.exp2(g_end - g_cum)
    q_glob32 = hq * jnp.exp2(g_cum)
    k_cross = [(k_end32 * jnp.concatenate(
        [span(cp + 1, c) if cp < c else zeros_chunk for cp in range(n_ch)], axis=0)).astype(BF16)
        for c in range(1, n_ch)]
    q_tile = (q_glob32 * jnp.concatenate([span(0, c) for c in range(n_ch)], axis=0)).astype(BF16)
    k_tile_end = (k_end32 * jnp.concatenate([span(cp + 1, n_ch) for cp in range(n_ch)], axis=0)).astype(BF16)
    decay_tile = jnp.exp2(sum(gc))
    q_glob = q_glob32.astype(BF16)

    def keep_rows(a, block, wanted):
        zero = jnp.zeros((block, a.shape[1]), a.dtype)
        return jnp.concatenate(
            [a[r * block:(r + 1) * block] if wanted(r) else zero for r in range(a.shape[0] // block)], axis=0)

    q_sub = [keep_rows(q_loc, SUB, lambda r, i=i: r % N_SUB == i) for i in range(N_SUB)]
    q_cross = [keep_rows(q_glob, CHUNK, lambda r, c=c: r == c) for c in range(1, n_ch)]
    yield
    amask_t = amask_ref[...]
    y = y_att
    out_group = 2 * LANES // HGRN_D
    for first in range(0, N_HGRN_HEADS, out_group):
        heads = range(first, first + out_group)
        lanes = {h: slice(h * HGRN_D, (h + 1) * HGRN_D) for h in heads}
        pair = {}
        for h, ls in lanes.items():
            q_stack = jnp.concatenate([q_sub[i][:, ls] for i in range(N_SUB)], axis=1)
            k_stack = jnp.concatenate([k_sub[i][:, ls] for i in range(N_SUB)], axis=1)
            qc_stack = jnp.concatenate([q_cross[c - 1][:, ls] for c in range(1, n_ch)], axis=1)
            kc_stack = jnp.concatenate([k_cross[c - 1][:, ls] for c in range(1, n_ch)], axis=1)
            pair[h] = (_dot_nt(k_stack, q_stack), _dot_nt(kc_stack, qc_stack))
        yield
        o_t = {}
        for h, ls in lanes.items():
            same, cross = pair[h]
            half = tb // 2
            a_t = jnp.concatenate([
                jnp.concatenate([same[:half, :half] * amask_t[:half, :half] + cross[:half, :half],
                                 cross[:half, half:]], axis=1),
                jnp.concatenate([jnp.zeros((half, half), F32),
                                 same[half:, half:] * amask_t[half:, half:] + cross[half:, half:]], axis=1),
            ], axis=0).astype(BF16)
            vt_h = hv_t[ls, :]
            state_h = src["state"][h]
            o_t[h] = _dot(vt_h, a_t) + _dot_nt(state_h.astype(BF16), q_tile[:, ls])
            dst["state"][h] = state_h * decay_tile[:, ls] + _dot(vt_h, k_tile_end[:, ls])
        yield
        hn_t = {h: o_t[h] * lax.rsqrt(jnp.mean(o_t[h] * o_t[h], axis=0, keepdims=True) + EPS)
                * goutc_ref[...] * hg_act_t[ls, :] for h, ls in lanes.items()}
        grp = jnp.concatenate(
            [jnp.concatenate(
                [jnp.concatenate([hn_t[j], hn_t[j + 1]], axis=0)[:, r:r + LANES].T for j in heads[::2]],
                axis=1) for r in range(0, tb, LANES)], axis=0).astype(BF16)
        y = y + _dot(grp, wout_ref[D_ATTN + first * HGRN_D:D_ATTN + (first + out_group) * HGRN_D, :])
        yield

    dst["y"], dst["k_rot"], dst["v_new"] = x + _rms(y, gpost_ref[...]), k_rot, v_new


def _prompt_mix_kernel(sink_ref, x_ref, gpre_ref, win_ref, wint_ref, cos_ref, shi_ref, slo_ref, lb_ref, goutc_ref,
                       wout_ref, gpost_ref, bias_ref, amask_ref,
                       y_ref, wk_ref, wv_ref, s_ref,
                       kk_scr, vt_scr, st_scr, g_scr, *, layer):
    tb = MIX_TILE
    step = pl.program_id(1)
    last = pl.num_programs(1) - 1

    @pl.when(step == 0)
    def _():
        kk_scr[...] = jnp.zeros_like(kk_scr)
        vt_scr[...] = jnp.zeros_like(vt_scr)
        st_scr[...] = jnp.zeros_like(st_scr)

    hand = [{"state": [None] * N_HGRN_HEADS} for _ in range(MIX_PARTS + 1)]
    hand[0]["k_prev"] = [kk_scr[g] for g in range(N_KV_HEADS)]
    hand[0]["vt_prev"] = vt_scr[...]
    hand[0]["state"] = [st_scr[h] for h in range(N_HGRN_HEADS)]
    tiles = []
    for part in range(MIX_PARTS):
        rs = slice(part * tb, (part + 1) * tb)
        tiles.append(_mix_tile(
            x_ref[rs, :], (cos_ref[rs, :], shi_ref[rs, :], slo_ref[rs, :]), step == 0 if part == 0 else None,
            hand[part], hand[part + 1], g_scr.at[part],
            sink_ref, gpre_ref, win_ref, wint_ref, lb_ref, goutc_ref, wout_ref, gpost_ref, bias_ref, amask_ref,
            layer))
    _interleave(tiles, skew=0)
    for part in range(MIX_PARTS):
        y_ref[part * tb:(part + 1) * tb, :] = hand[part + 1]["y"]
    final = hand[MIX_PARTS]
    for g in range(N_KV_HEADS):
        kk_scr[g] = final["k_prev"][g]
    vt_scr[...] = final["vt_prev"]
    for h in range(N_HGRN_HEADS):
        st_scr[h] = final["state"][h]

    @pl.when(step == last)
    def _():
        wk_ref[0] = final["k_rot"][tb - WINDOW:].T
        wv_ref[0] = final["v_new"][tb - WINDOW:].T
        for h in range(N_HGRN_HEADS):
            s_ref[0, h] = final["state"][h].T


def _prompt_mix(x, sinks, g_pre, w_in, w_in_t, lb_raw, g_out_head, w_out, g_post, batch, seq, layer):
    tb = MIX_TILE
    rows = MIX_PARTS * tb
    nt = seq // rows
    cos, s_hi, s_lo = _rope_tables(np.arange(seq))
    tok = lambda b, n: (b * nt + n, 0)
    tab = lambda b, n: (n, 0)
    per_b3 = lambda b, n: (b, 0, 0)
    lb_rows = lb_raw.shape[0]
    key_i = np.arange(2 * WINDOW)[:, None]
    rel = np.arange(WINDOW)[None, :] + WINDOW - key_i
    bias = jnp.asarray(np.tile(np.where((rel >= 0) & (rel < WINDOW), 0.0, NEG_INF), (1, GQA_GROUP)), F32)
    t_i = np.arange(tb)
    amask = jnp.asarray((t_i[:, None] // CHUNK == t_i[None, :] // CHUNK) & (t_i[:, None] <= t_i[None, :]), F32)
    return pl.pallas_call(
        functools.partial(_prompt_mix_kernel, layer=layer),
        grid=(batch, nt),
        in_specs=[
            pl.BlockSpec(memory_space=pltpu.SMEM),
            pl.BlockSpec((rows, D_MODEL), tok),
            _const_spec((1, D_MODEL)),
            _const_spec((D_MODEL, IN_COLS)),
            _const_spec((2 * D_HGRN, D_MODEL)),
            pl.BlockSpec((rows, LANES), tab),
            pl.BlockSpec((rows, LANES), tab),
            pl.BlockSpec((rows, LANES), tab),
            _const_spec((lb_rows, D_HGRN)),
            _const_spec((HGRN_D, 1)),
            _const_spec((D_MODEL, D_MODEL)),
            _const_spec((1, D_MODEL)),
            _const_spec((2 * WINDOW, GQA_GROUP * WINDOW)),
            _const_spec((tb, tb)),
        ],
        out_specs=[
            pl.BlockSpec((rows, D_MODEL), tok),
            pl.BlockSpec((1, D_KV, WINDOW), per_b3),
            pl.BlockSpec((1, D_KV, WINDOW), per_b3),
            pl.BlockSpec((1, N_HGRN_HEADS, HGRN_D, HGRN_D), lambda b, n: (b, 0, 0, 0)),
        ],
        out_shape=[
            jax.ShapeDtypeStruct((batch * seq, D_MODEL), F32),
            jax.ShapeDtypeStruct((batch, D_KV, WINDOW), F32),
            jax.ShapeDtypeStruct((batch, D_KV, WINDOW), F32),
            jax.ShapeDtypeStruct((batch, N_HGRN_HEADS, HGRN_D, HGRN_D), F32),
        ],
        scratch_shapes=[
            pltpu.VMEM((N_KV_HEADS, WINDOW, LANES), BF16),
            pltpu.VMEM((D_KV, WINDOW), BF16),
            pltpu.VMEM((N_HGRN_HEADS, HGRN_D, HGRN_D), F32),
            pltpu.VMEM((MIX_PARTS, tb, D_HGRN), F32),
        ],
        compiler_params=pltpu.CompilerParams(
            dimension_semantics=("arbitrary", "arbitrary"), vmem_limit_bytes=VMEM_LIMIT),
        name="prompt_mix",
    )(sinks, x, g_pre.reshape(1, D_MODEL), w_in, w_in_t, cos, s_hi, s_lo, lb_raw, g_out_head.reshape(HGRN_D, 1),
      w_out, g_post.reshape(1, D_MODEL), bias, amask)


def _sample_mix_kernel(sink_ref, x_ref, gpre_ref, win_ref, cos_ref, shi_ref, slo_ref, lb_ref, goutc_ref,
                       wout_ref, gpost_ref, ckt_ref, cvt_ref, sin_ref,
                       y_ref, wkt_ref, wvt_ref, sout_ref,
                       q_scr, kn_scr, vn_scr, knt_scr, vnt_scr, ft_scr, hkt_scr, hqt_scr, hvt_scr, hgt_scr,
                       ot_scr, att_scr, *, layer, nb):
    step = pl.program_id(0)
    last = pl.num_programs(0) - 1
    bt = nb // N_HGRN_HEADS

    @pl.when(step == 0)
    def _():
        u = _rms(x_ref[...], gpre_ref[...]).astype(BF16)
        proj = _dot(u, win_ref[...])
        cos, s_hi, s_lo = cos_ref[...], shi_ref[...], slo_ref[...]
        scale = HEAD_DIM ** -0.5
        for j in range(D_ATTN // LANES):
            q_scr[:, LANES * j:LANES * (j + 1)] = _rope(
                proj[:, OFF_Q + LANES * j:OFF_Q + LANES * (j + 1)], cos, s_hi, s_lo) * scale
        k_new = _rope(proj[:, OFF_K:OFF_K + D_KV], cos, s_hi, s_lo)
        v_new = proj[:, OFF_V:OFF_V + D_KV]
        kn_scr[...] = k_new
        vn_scr[...] = v_new
        for scr, val in ((knt_scr, k_new), (vnt_scr, v_new)):
            for i, part in enumerate(_split3(val.T)):
                scr[i] = part
        lb = _lower_bound(lb_ref[...], layer)
        f_t = (lb + (1.0 - lb) * jax.nn.sigmoid(proj[:, OFF_HF:OFF_HF + D_HGRN])).T
        ft_scr[...] = f_t
        hkt_scr[...] = 1.0 - f_t
        hqt_scr[...] = (proj[:, OFF_HQ:OFF_HQ + D_HGRN] * (HGRN_D ** -0.5)).T
        hvt_scr[...] = proj[:, OFF_HI:OFF_HI + D_HGRN].T
        hgt_scr[...] = proj[:, OFF_HG:OFF_HG + D_HGRN].T

    base = pl.multiple_of(step * HGRN_D, HGRN_D)
    hv_t = hvt_scr[pl.ds(base, HGRN_D), :]

    def hgrn_row(k, o_acc):
        f_row = ft_scr[pl.ds(base + k, 1), :]
        s_new = f_row * sin_ref[k] + hkt_scr[pl.ds(base + k, 1), :] * hv_t
        sout_ref[k] = s_new
        return o_acc + hqt_scr[pl.ds(base + k, 1), :] * s_new

    ot_scr[pl.ds(base, HGRN_D), :] = lax.fori_loop(
        0, HGRN_D, hgrn_row, jnp.zeros((HGRN_D, nb), F32), unroll=8)

    lane8 = lax.broadcasted_iota(jnp.int32, (N_Q_HEADS, LANES), 1)
    row8 = lax.broadcasted_iota(jnp.int32, (N_Q_HEADS, LANES), 0)
    keep8 = (lane8 >= HALF) == (row8 >= GQA_GROUP)
    win_lane = lax.broadcasted_iota(jnp.int32, (D_KV, WINDOW), 1)
    sink = sink_ref[...]
    b0 = step * bt
    sel = (lax.broadcasted_iota(jnp.int32, (nb, bt), 0)
           == b0 + lax.broadcasted_iota(jnp.int32, (nb, bt), 1)).astype(BF16)
    k_cols = sum(_dot(knt_scr[i], sel) for i in range(3))
    v_cols = sum(_dot(vnt_scr[i], sel) for i in range(3))
    def one_row(bi):
        b = b0 + bi
        q_b = jnp.broadcast_to(q_scr[pl.ds(b, 1), :], (N_Q_HEADS, D_ATTN))
        qm = jnp.zeros((N_Q_HEADS, LANES), F32)
        for h in range(N_Q_HEADS):
            c = q_b[:, LANES * (h // 2):LANES * (h // 2 + 1)]
            if h % 2 != h // GQA_GROUP:
                c = pltpu.roll(c, HALF, 1)
            qm = jnp.where(row8 == h, c, qm)
        qm = jnp.where(keep8, qm, 0.0)
        k_new = kn_scr[pl.ds(b, 1), :]
        v_new = vn_scr[pl.ds(b, 1), :]
        k_old = ckt_ref[bi]
        s = _dot(qm.astype(BF16), k_old.astype(BF16))
        wkt_ref[bi] = jnp.where(win_lane == WINDOW - 1, k_cols[:, bi:bi + 1], pltpu.roll(k_old, WINDOW - 1, 1))
        yield
        s = jnp.where(lane8 >= 1, s, NEG_INF)
        s_new = jnp.sum(qm * k_new, axis=-1, keepdims=True)
        m = jnp.maximum(jnp.maximum(jnp.max(s, axis=-1, keepdims=True), s_new), sink)
        p = jnp.exp(s - m)
        p_new = jnp.exp(s_new - m)
        denom = jnp.sum(p, axis=-1, keepdims=True) + p_new + jnp.exp(sink - m)
        yield
        v_old = cvt_ref[bi]
        o = (_dot_nt(p.astype(BF16), v_old.astype(BF16)) + p_new * v_new) / denom
        att_scr[pl.ds(pl.multiple_of(b * N_Q_HEADS, N_Q_HEADS), N_Q_HEADS), :] = o
        wvt_ref[bi] = jnp.where(win_lane == WINDOW - 1, v_cols[:, bi:bi + 1], pltpu.roll(v_old, WINDOW - 1, 1))

    _interleave([one_row(bi) for bi in range(bt)], skew=0)

    @pl.when(step == last)
    def _():
        y = jnp.zeros((nb, D_MODEL), F32)
        for h in range(N_HGRN_HEADS):
            ks = slice(h * HGRN_D, (h + 1) * HGRN_D)
            o_t = ot_scr[ks, :]
            hn_t = (o_t * lax.rsqrt(jnp.mean(o_t * o_t, axis=0, keepdims=True) + EPS)
                    * goutc_ref[...] * _silu(hgt_scr[ks, :]))
            y = y + _dot_tn(hn_t.astype(BF16), wout_ref[D_ATTN + h * HGRN_D:D_ATTN + (h + 1) * HGRN_D, :])
        for h in range(N_Q_HEADS):
            g = h // GQA_GROUP
            a_h = att_scr[pl.ds(h, nb, stride=N_Q_HEADS), :][:, g * HALF:(g + 1) * HALF]
            y = y + _dot(a_h.astype(BF16), wout_ref[h * HEAD_DIM:(h + 1) * HEAD_DIM, :])
        y_ref[...] = x_ref[...] + _rms(y, gpost_ref[...])


def _sample_mix(x, sinks, g_pre, w_in, lb_raw, g_out_head, w_out, g_post, cache_kt, cache_vt, state_t, pos, layer):
    nb = x.shape[0]
    bt = nb // N_HGRN_HEADS
    cos, s_hi, s_lo = _rope_tables(pos)
    lb_rows = lb_raw.shape[0]
    blk3 = pl.BlockSpec((bt, D_KV, WINDOW), lambda i: (i, 0, 0))
    blk_s = pl.BlockSpec((HGRN_D, HGRN_D, nb), lambda i: (i, 0, 0))
    chan_major = pltpu.VMEM((D_HGRN, nb), F32)
    return pl.pallas_call(
        functools.partial(_sample_mix_kernel, layer=layer, nb=nb),
        grid=(N_HGRN_HEADS,),
        in_specs=[
            _const_spec((N_Q_HEADS, 1)),
            _const_spec((nb, D_MODEL)),
            _const_spec((1, D_MODEL)),
            _const_spec((D_MODEL, IN_COLS)),
            _const_spec((1, LANES)),
            _const_spec((1, LANES)),
            _const_spec((1, LANES)),
            _const_spec((lb_rows, D_HGRN)),
            _const_spec((HGRN_D, 1)),
            _const_spec((D_MODEL, D_MODEL)),
            _const_spec((1, D_MODEL)),
            blk3, blk3, blk_s,
        ],
        out_specs=[pl.BlockSpec((nb, D_MODEL), lambda i: (0, 0)), blk3, blk3, blk_s],
        out_shape=[
            jax.ShapeDtypeStruct((nb, D_MODEL), F32),
            jax.ShapeDtypeStruct(cache_kt.shape, F32),
            jax.ShapeDtypeStruct(cache_vt.shape, F32),
            jax.ShapeDtypeStruct(state_t.shape, F32),
        ],
        scratch_shapes=[
            pltpu.VMEM((nb, D_ATTN), F32),
            pltpu.VMEM((nb, D_KV), F32),
            pltpu.VMEM((nb, D_KV), F32),
            pltpu.VMEM((3, D_KV, nb), BF16),
            pltpu.VMEM((3, D_KV, nb), BF16),
            chan_major, chan_major, chan_major, chan_major, chan_major, chan_major,
            pltpu.VMEM((nb * N_Q_HEADS, LANES), F32),
        ],
        compiler_params=pltpu.CompilerParams(
            dimension_semantics=("arbitrary",), vmem_limit_bytes=VMEM_LIMIT),
        name="sample_mix",
    )(sinks.reshape(N_Q_HEADS, 1), x, g_pre.reshape(1, D_MODEL), w_in, cos, s_hi, s_lo, lb_raw,
      g_out_head.reshape(HGRN_D, 1), w_out, g_post.reshape(1, D_MODEL), cache_kt, cache_vt, state_t)


def kernel(x_prompt, x_sample, cache_win_k, cache_win_v, state_hgrn, ffn1_pre_g, ffn1_post_g, ffn1_w_gu,
           ffn1_w_down, mix_pre_g, mix_post_g, w_in, attn_sinks, hgrn_lb, hgrn_out_g, w_out, ffn2_pre_g,
           ffn2_post_g, ffn2_w_gu, ffn2_w_down):
    batch, seq, _ = x_prompt.shape
    nb, t_s, _ = x_sample.shape
    depth = w_in.shape[0]
    assert t_s == 1 and seq % (MIX_PARTS * MIX_TILE) == 0 and (batch * seq) % FFN_TILE == 0 and nb == LANES
    assert cache_win_k.shape[2:] == (WINDOW, N_KV_HEADS, HEAD_DIM)

    xp = x_prompt.reshape(batch * seq, D_MODEL)
    xs = x_sample.reshape(nb, D_MODEL)
    pos_s = PAST_LEN + np.arange(t_s)
    outs = [[] for _ in range(6)]
    for l in range(depth):
        w_in_l, w_out_l = w_in[l].astype(BF16), w_out[l].astype(BF16)
        w_in_t = w_in[l][:, OFF_HI:].T.astype(BF16)

        xp, xs = _ffn(xp, xs, ffn1_pre_g[l], ffn1_post_g[l], ffn1_w_gu[l], ffn1_w_down[l])

        xp, wkt_p, wvt_p, s_p = _prompt_mix(xp, attn_sinks[l], mix_pre_g[l], w_in_l, w_in_t, hgrn_lb,
                                            hgrn_out_g[l], w_out_l, mix_post_g[l], batch, seq, l)
        to_kt = lambda c: jnp.transpose(c, (0, 2, 3, 1)).reshape(-1, D_KV, WINDOW)
        from_kt = lambda c: jnp.transpose(c.reshape(-1, N_KV_HEADS, HEAD_DIM, WINDOW), (0, 3, 1, 2))
        state_t = jnp.transpose(state_hgrn[l], (1, 2, 3, 0)).reshape(D_HGRN, HGRN_D, nb)
        xs, wkt_s, wvt_s, st_s = _sample_mix(
            xs, attn_sinks[l], mix_pre_g[l], w_in_l, hgrn_lb, hgrn_out_g[l], w_out_l, mix_post_g[l],
            to_kt(cache_win_k[l]), to_kt(cache_win_v[l]), state_t, pos_s, l)
        s_s = jnp.transpose(st_s.reshape(N_HGRN_HEADS, HGRN_D, HGRN_D, nb), (3, 0, 1, 2))

        xp, xs = _ffn(xp, xs, ffn2_pre_g[l], ffn2_post_g[l], ffn2_w_gu[l], ffn2_w_down[l])

        for lst, val in zip(outs, (from_kt(wkt_p), from_kt(wvt_p), s_p, from_kt(wkt_s), from_kt(wvt_s), s_s)):
            lst.append(val)

    return (xp.reshape(batch, seq, D_MODEL), xs.reshape(nb, t_s, D_MODEL)) + tuple(jnp.stack(o) for o in outs)
```

```python
import functools

import jax
import jax.numpy as jnp
import numpy as np
from jax import lax
from jax.experimental import pallas as pl
from jax.experimental.pallas import tpu as pltpu

F32 = jnp.float32
BF16 = jnp.bfloat16

D_MODEL = 1024
D_FF = 2816
HEAD_DIM = 64
N_Q_HEADS = 8
N_KV_HEADS = 2
GQA_GROUP = N_Q_HEADS // N_KV_HEADS
WINDOW = 128
PAST_LEN = 8192
ROT_DIM = HEAD_DIM // 4
ROPE_THETA = 500000.0
N_HGRN_HEADS = 8
HGRN_D = 64
D_ATTN = N_Q_HEADS * HEAD_DIM
D_KV = N_KV_HEADS * HEAD_DIM
D_HGRN = N_HGRN_HEADS * HGRN_D
IN_COLS = D_ATTN + 2 * D_KV + 4 * D_HGRN
OFF_Q, OFF_K, OFF_V = 0, D_ATTN, D_ATTN + D_KV
OFF_HQ = D_ATTN + 2 * D_KV
OFF_HF, OFF_HI, OFF_HG = OFF_HQ + D_HGRN, OFF_HQ + 2 * D_HGRN, OFF_HQ + 3 * D_HGRN
EPS = 1e-6
NEG_INF = -1e30
LOG2E = 1.4426950408889634
LANES = 128
HALF = LANES // 2

FFN_TILE = 1024
FFN_PARTS = 4
FFN_W_STEPS = 8
MIX_TILE = 256
MIX_PARTS = 2
CHUNK = 64
SUB = 16
N_SUB = CHUNK // SUB
VMEM_LIMIT = 56 * 1024 * 1024


def _rms(x, g):
    return x * lax.rsqrt(jnp.mean(x * x, axis=-1, keepdims=True) + EPS) * g


def _silu(x):
    return x * jax.nn.sigmoid(x)


def _dot(a, b):
    return jnp.dot(a, b, preferred_element_type=F32)


def _dot_nt(a, b):
    return lax.dot_general(a, b, (((1,), (1,)), ((), ())), preferred_element_type=F32)


def _dot_tn(a, b):
    return lax.dot_general(a, b, (((0,), (0,)), ((), ())), preferred_element_type=F32)


def _split3(x):
    hi = x.astype(BF16)
    r = x - hi.astype(F32)
    mid = r.astype(BF16)
    lo = (r - mid.astype(F32)).astype(BF16)
    return hi, mid, lo


def _interleave(gens, skew):
    pending, active, rnd = list(gens), [], 0
    while pending or active:
        while pending and (skew == 0 or rnd % skew == 0):
            active.append(pending.pop(0))
            if skew:
                break
        active = [g for g in active if next(g, True) is None]
        rnd += 1


def _const_spec(shape):
    nd = len(shape)
    return pl.BlockSpec(shape, lambda *_: (0,) * nd, pipeline_mode=pl.Buffered(1))


def _ffn_kernel(x_ref, xs_ref, gpre_ref, gpost_ref, wgu32_ref, wd32_ref, o_ref, os_ref, wgu_ref, wd_ref):
    step = pl.program_id(0)

    @pl.when(step < FFN_W_STEPS)
    def _():
        gu_rows, d_rows = wgu32_ref.shape[0], wd32_ref.shape[0]
        wgu_ref[pl.ds(pl.multiple_of(step * gu_rows, gu_rows), gu_rows), :] = wgu32_ref[...].astype(BF16)
        wd_ref[pl.ds(pl.multiple_of(step * d_rows, d_rows), d_rows), :] = wd32_ref[...].astype(BF16)

    def half_step(src_ref, dst_ref, rs):
        x = src_ref[rs, :]
        h = _rms(x, gpre_ref[...]).astype(BF16)
        yield
        gate = _dot(h, wgu_ref[:, :D_FF])
        up = _dot(h, wgu_ref[:, D_FF:])
        yield
        act = (_silu(gate) * up).astype(BF16)
        yield
        y = _dot(act, wd_ref[...])
        yield
        dst_ref[rs, :] = x + 0.5 * _rms(y, gpost_ref[...])

    @pl.when(step >= FFN_W_STEPS)
    def _():
        rows = x_ref.shape[0] // FFN_PARTS
        _interleave([half_step(x_ref, o_ref, slice(r * rows, (r + 1) * rows)) for r in range(FFN_PARTS)], skew=1)

    @pl.when(step == pl.num_programs(0) - 1)
    def _():
        for _ in half_step(xs_ref, os_ref, slice(None)):
            pass


def _ffn(x, xs, g_pre, g_post, w_gu, w_down):
    n, nb = x.shape[0], xs.shape[0]
    tile_map = lambda i: (jnp.maximum(i - FFN_W_STEPS, 0), 0)
    chunk_map = lambda i: (jnp.minimum(i, FFN_W_STEPS - 1), 0)
    return pl.pallas_call(
        _ffn_kernel,
        grid=(FFN_W_STEPS + n // FFN_TILE,),
        in_specs=[
            pl.BlockSpec((FFN_TILE, D_MODEL), tile_map),
            _const_spec((nb, D_MODEL)),
            _const_spec((1, D_MODEL)),
            _const_spec((1, D_MODEL)),
            pl.BlockSpec((D_MODEL // FFN_W_STEPS, 2 * D_FF), chunk_map),
            pl.BlockSpec((D_FF // FFN_W_STEPS, D_MODEL), chunk_map),
        ],
        out_specs=[pl.BlockSpec((FFN_TILE, D_MODEL), tile_map),
                   pl.BlockSpec((nb, D_MODEL), lambda i: (0, 0))],
        out_shape=[jax.ShapeDtypeStruct((n, D_MODEL), F32), jax.ShapeDtypeStruct((nb, D_MODEL), F32)],
        scratch_shapes=[pltpu.VMEM((D_MODEL, 2 * D_FF), BF16), pltpu.VMEM((D_FF, D_MODEL), BF16)],
        compiler_params=pltpu.CompilerParams(
            dimension_semantics=("arbitrary",), vmem_limit_bytes=VMEM_LIMIT),
        name="ffn",
    )(x, xs, g_pre.reshape(1, D_MODEL), g_post.reshape(1, D_MODEL), w_gu, w_down)


def _lower_bound(lb_raw, layer):
    m = jnp.max(lb_raw, axis=0, keepdims=True)
    e = jnp.exp(lb_raw - m)
    return jnp.sum(e[: layer + 1], axis=0, keepdims=True) / jnp.sum(e, axis=0, keepdims=True)


def _rope(x, cos, sin_hi, sin_lo):
    return x * cos + pltpu.roll(x, ROT_DIM // 2, 1) * sin_hi + pltpu.roll(x, LANES - ROT_DIM // 2, 1) * sin_lo


def _rope_tables(pos):
    half = ROT_DIM // 2
    inv = (np.float32(ROPE_THETA) ** (-np.arange(half, dtype=np.float32) / half)).astype(np.float64)
    ang = np.asarray(pos, np.float64)[:, None] * inv[None, :]
    cos, sin = np.cos(ang), np.sin(ang)
    t = ang.shape[0]
    one = np.ones((t, HEAD_DIM - ROT_DIM))
    zero = np.zeros((t, HEAD_DIM - ROT_DIM))
    zh = np.zeros((t, half))
    c = np.concatenate([cos, cos, one], axis=1)
    s_hi = np.concatenate([zh, sin, zero], axis=1)
    s_lo = np.concatenate([-sin, zh, zero], axis=1)
    return tuple(jnp.asarray(np.tile(a, (1, LANES // HEAD_DIM)), F32) for a in (c, s_hi, s_lo))


def _dup_half(x, g, lo_half):
    xr = pltpu.roll(x, HALF, 1)
    return jnp.where(lo_half, x, xr) if g == 0 else jnp.where(lo_half, xr, x)


def _mix_tile(x, rope_tab, seq_start, src, dst, g_scr,
              sink_ref, gpre_ref, win_ref, wint_ref, lb_ref, goutc_ref, wout_ref, gpost_ref, bias_ref, amask_ref,
              layer):
    tb = MIX_TILE
    u = _rms(x, gpre_ref[...]).astype(BF16)

    def proj(off, width):
        return _dot(u, win_ref[:, off:off + width])

    p_attn = proj(OFF_Q, D_ATTN + 2 * D_KV)
    p_hf = proj(OFF_HF, D_HGRN)
    cos, s_hi, s_lo = rope_tab
    scale = HEAD_DIM ** -0.5 * LOG2E
    q_cols = [_rope(p_attn[:, LANES * j: LANES * (j + 1)], cos, s_hi, s_lo) * scale
              for j in range(D_ATTN // LANES)]
    k_rot = _rope(p_attn[:, OFF_K:OFF_K + D_KV], cos, s_hi, s_lo)
    v_new = p_attn[:, OFF_V:OFF_V + D_KV]

    lane = lax.broadcasted_iota(jnp.int32, (WINDOW, LANES), 1)
    lo_half = lane < HALF
    bias = bias_ref[...]
    if seq_start is None:
        bias_first = bias
    else:
        no_prev = jnp.where(seq_start, NEG_INF, 0.0)
        bias_first = jnp.concatenate([bias[:WINDOW] + no_prev, bias[WINDOW:]], axis=0)

    vt_tile = v_new.T.astype(BF16)
    yield
    k_prev, vt_prev = src["k_prev"], src["vt_prev"]
    scores = []
    for i in range(tb // WINDOW):
        r0 = i * WINDOW
        k_cur = [_dup_half(k_rot[r0:r0 + WINDOW], g, lo_half).astype(BF16) for g in range(N_KV_HEADS)]
        vt_keys = jnp.concatenate([vt_prev, vt_tile[:, r0:r0 + WINDOW]], axis=1)
        for g in range(N_KV_HEADS):
            keys = jnp.concatenate([k_prev[g], k_cur[g]], axis=0)
            heads = range(g * GQA_GROUP, (g + 1) * GQA_GROUP)
            qg = jnp.concatenate(
                [jnp.where(lo_half if h % 2 == 0 else ~lo_half, q_cols[h // 2][r0:r0 + WINDOW], 0.0)
                 for h in heads], axis=0).astype(BF16)
            s = _dot_nt(keys, qg) + (bias_first if i == 0 else bias)
            scores.append((s, vt_keys[g * HEAD_DIM:(g + 1) * HEAD_DIM], heads))
        k_prev, vt_prev = k_cur, vt_tile[:, r0:r0 + WINDOW]
    dst["k_prev"], dst["vt_prev"] = k_prev, vt_prev
    yield

    hq = proj(OFF_HQ, D_HGRN) * (HGRN_D ** -0.5)
    hv_t = _dot_nt(wint_ref[:D_HGRN, :], u).astype(BF16)
    hg_act_t = _silu(_dot_nt(wint_ref[D_HGRN:, :], u))
    yield

    att_t = [[None] * (tb // WINDOW) for _ in range(N_Q_HEADS)]
    probs = []
    for s, _, heads in scores:
        sink = jnp.concatenate(
            [jnp.full((1, WINDOW), sink_ref[h] * LOG2E, F32) for h in heads], axis=1)
        m = jnp.maximum(jnp.max(s, axis=0, keepdims=True), sink)
        p = jnp.exp2(s - m)
        denom = jnp.sum(p, axis=0, keepdims=True) + jnp.exp2(sink - m)
        probs.append((p.astype(BF16), 1.0 / denom))
    yield
    for idx, ((p, r_denom), (_, vt_g, heads)) in enumerate(zip(probs, scores)):
        o_t = _dot(vt_g, p) * r_denom
        for j, h in enumerate(heads):
            att_t[h][idx // N_KV_HEADS] = o_t[:, j * WINDOW:(j + 1) * WINDOW]
    att = jnp.concatenate(
        [jnp.concatenate([jnp.concatenate([att_t[2 * j][i], att_t[2 * j + 1][i]], axis=0).T
                          for j in range(N_Q_HEADS // 2)], axis=1)
         for i in range(tb // WINDOW)], axis=0)
    y_att = _dot(att.astype(BF16), wout_ref[:D_ATTN, :])
    yield

    lb = _lower_bound(lb_ref[...], layer)
    f = lb + (1.0 - lb) * jax.nn.sigmoid(p_hf)
    logf = jnp.log(f) * LOG2E
    hk = 1.0 - f

    tr = lax.broadcasted_iota(jnp.int32, (tb, tb), 0)
    tc = lax.broadcasted_iota(jnp.int32, (tb, tb), 1)
    tri = ((tr // CHUNK == tc // CHUNK) & (tc <= tr)).astype(BF16)
    g_cum = sum(_dot(tri, part) for part in _split3(logf))
    g_scr[...] = g_cum
    yield

    def bcast_row(r, rows):
        return jnp.broadcast_to(g_scr[r:r + 1, :], (rows, D_HGRN))

    zeros_sub = jnp.zeros((SUB, D_HGRN), F32)
    g_ref_q = jnp.concatenate(
        [zeros_sub if sb % N_SUB == 0 else bcast_row(sb * SUB - 1, SUB) for sb in range(tb // SUB)], axis=0)
    q_loc = (hq * jnp.exp2(g_cum - g_ref_q)).astype(BF16)
    g_end = jnp.concatenate(
        [bcast_row(c * CHUNK + CHUNK - 1, CHUNK) for c in range(tb // CHUNK)], axis=0)
    k_sub = []
    for i in range(N_SUB):
        live = (i + 1) * SUB
        pieces = []
        for c in range(tb // CHUNK):
            r0 = c * CHUNK
            g_ref_i = 0.0 if i == 0 else bcast_row(r0 + i * SUB - 1, live)
            pieces.append(hk[r0:r0 + live] * jnp.exp2(g_ref_i - g_cum[r0:r0 + live]))
            if live < CHUNK:
                pieces.append(jnp.zeros((CHUNK - live, D_HGRN), F32))
        k_sub.append(jnp.concatenate(pieces, axis=0).astype(BF16))

    n_ch = tb // CHUNK
    gc = [g_scr[c * CHUNK + CHUNK - 1:c * CHUNK + CHUNK, :] for c in range(n_ch)]

    def span(lo, hi):
        if hi <= lo:
            return jnp.ones((CHUNK, D_HGRN), F32)
        return jnp.broadcast_to(jnp.exp2(sum(gc[lo:hi])), (CHUNK, D_HGRN))

    zeros_chunk = jnp.zeros((CHUNK, D_HGRN), F32)
    k_end32 = hk * jnp.exp2(g_end - g_cum)
    q_glob32 = hq * jnp.exp2(g_cum)
    k_cross = [(k_end32 * jnp.concatenate(
        [span(cp + 1, c) if cp < c else zeros_chunk for cp in range(n_ch)], axis=0)).astype(BF16)
        for c in range(1, n_ch)]
    q_tile = (q_glob32 * jnp.concatenate([span(0, c) for c in range(n_ch)], axis=0)).astype(BF16)
    k_tile_end = (k_end32 * jnp.concatenate([span(cp + 1, n_ch) for cp in range(n_ch)], axis=0)).astype(BF16)
    decay_tile = jnp.exp2(sum(gc))
    q_glob = q_glob32.astype(BF16)

    def keep_rows(a, block, wanted):
        zero = jnp.zeros((block, a.shape[1]), a.dtype)
        return jnp.concatenate(
            [a[r * block:(r + 1) * block] if wanted(r) else zero for r in range(a.shape[0] // block)], axis=0)

    q_sub = [keep_rows(q_loc, SUB, lambda r, i=i: r % N_SUB == i) for i in range(N_SUB)]
    q_cross = [keep_rows(q_glob, CHUNK, lambda r, c=c: r == c) for c in range(1, n_ch)]
    yield
    amask_t = amask_ref[...]
    y = y_att
    out_group = 2 * LANES // HGRN_D
    for first in range(0, N_HGRN_HEADS, out_group):
        heads = range(first, first + out_group)
        lanes = {h: slice(h * HGRN_D, (h + 1) * HGRN_D) for h in heads}
        pair = {}
        for h, ls in lanes.items():
            q_stack = jnp.concatenate([q_sub[i][:, ls] for i in range(N_SUB)], axis=1)
            k_stack = jnp.concatenate([k_sub[i][:, ls] for i in range(N_SUB)], axis=1)
            qc_stack = jnp.concatenate([q_cross[c - 1][:, ls] for c in range(1, n_ch)], axis=1)
            kc_stack = jnp.concatenate([k_cross[c - 1][:, ls] for c in range(1, n_ch)], axis=1)
            pair[h] = (_dot_nt(k_stack, q_stack), _dot_nt(kc_stack, qc_stack))
        yield
        o_t = {}
        for h, ls in lanes.items():
            same, cross = pair[h]
            half = tb // 2
            a_t = jnp.concatenate([
                jnp.concatenate([same[:half, :half] * amask_t[:half, :half] + cross[:half, :half],
                                 cross[:half, half:]], axis=1),
                jnp.concatenate([jnp.zeros((half, half), F32),
                                 same[half:, half:] * amask_t[half:, half:] + cross[half:, half:]], axis=1),
            ], axis=0).astype(BF16)
            vt_h = hv_t[ls, :]
            state_h = src["state"][h]
            o_t[h] = _dot(vt_h, a_t) + _dot_nt(state_h.astype(BF16), q_tile[:, ls])
            dst["state"][h] = state_h * decay_tile[:, ls] + _dot(vt_h, k_tile_end[:, ls])
        yield
        hn_t = {h: o_t[h] * lax.rsqrt(jnp.mean(o_t[h] * o_t[h], axis=0, keepdims=True) + EPS)
                * goutc_ref[...] * hg_act_t[ls, :] for h, ls in lanes.items()}
        grp = jnp.concatenate(
            [jnp.concatenate(
                [jnp.concatenate([hn_t[j], hn_t[j + 1]], axis=0)[:, r:r + LANES].T for j in heads[::2]],
                axis=1) for r in range(0, tb, LANES)], axis=0).astype(BF16)
        y = y + _dot(grp, wout_ref[D_ATTN + first * HGRN_D:D_ATTN + (first + out_group) * HGRN_D, :])
        yield

    dst["y"], dst["k_rot"], dst["v_new"] = x + _rms(y, gpost_ref[...]), k_rot, v_new


def _prompt_mix_kernel(sink_ref, x_ref, gpre_ref, win_ref, wint_ref, cos_ref, shi_ref, slo_ref, lb_ref, goutc_ref,
                       wout_ref, gpost_ref, bias_ref, amask_ref,
                       y_ref, wk_ref, wv_ref, s_ref,
                       kk_scr, vt_scr, st_scr, g_scr, *, layer):
    tb = MIX_TILE
    step = pl.program_id(1)
    last = pl.num_programs(1) - 1

    @pl.when(step == 0)
    def _():
        kk_scr[...] = jnp.zeros_like(kk_scr)
        vt_scr[...] = jnp.zeros_like(vt_scr)
        st_scr[...] = jnp.zeros_like(st_scr)

    hand = [{"state": [None] * N_HGRN_HEADS} for _ in range(MIX_PARTS + 1)]
    hand[0]["k_prev"] = [kk_scr[g] for g in range(N_KV_HEADS)]
    hand[0]["vt_prev"] = vt_scr[...]
    hand[0]["state"] = [st_scr[h] for h in range(N_HGRN_HEADS)]
    tiles = []
    for part in range(MIX_PARTS):
        rs = slice(part * tb, (part + 1) * tb)
        tiles.append(_mix_tile(
            x_ref[rs, :], (cos_ref[rs, :], shi_ref[rs, :], slo_ref[rs, :]), step == 0 if part == 0 else None,
            hand[part], hand[part + 1], g_scr.at[part],
            sink_ref, gpre_ref, win_ref, wint_ref, lb_ref, goutc_ref, wout_ref, gpost_ref, bias_ref, amask_ref,
            layer))
    _interleave(tiles, skew=0)
    for part in range(MIX_PARTS):
        y_ref[part * tb:(part + 1) * tb, :] = hand[part + 1]["y"]
    final = hand[MIX_PARTS]
    for g in range(N_KV_HEADS):
        kk_scr[g] = final["k_prev"][g]
    vt_scr[...] = final["vt_prev"]
    for h in range(N_HGRN_HEADS):
        st_scr[h] = final["state"][h]

    @pl.when(step == last)
    def _():
        wk_ref[0] = final["k_rot"][tb - WINDOW:].T
        wv_ref[0] = final["v_new"][tb - WINDOW:].T
        for h in range(N_HGRN_HEADS):
            s_ref[0, h] = final["state"][h].T


def _prompt_mix(x, sinks, g_pre, w_in, w_in_t, lb_raw, g_out_head, w_out, g_post, batch, seq, layer):
    tb = MIX_TILE
    rows = MIX_PARTS * tb
    nt = seq // rows
    cos, s_hi, s_lo = _rope_tables(np.arange(seq))
    tok = lambda b, n: (b * nt + n, 0)
    tab = lambda b, n: (n, 0)
    per_b3 = lambda b, n: (b, 0, 0)
    lb_rows = lb_raw.shape[0]
    key_i = np.arange(2 * WINDOW)[:, None]
    rel = np.arange(WINDOW)[None, :] + WINDOW - key_i
    bias = jnp.asarray(np.tile(np.where((rel >= 0) & (rel < WINDOW), 0.0, NEG_INF), (1, GQA_GROUP)), F32)
    t_i = np.arange(tb)
    amask = jnp.asarray((t_i[:, None] // CHUNK == t_i[None, :] // CHUNK) & (t_i[:, None] <= t_i[None, :]), F32)
    return pl.pallas_call(
        functools.partial(_prompt_mix_kernel, layer=layer),
        grid=(batch, nt),
        in_specs=[
            pl.BlockSpec(memory_space=pltpu.SMEM),
            pl.BlockSpec((rows, D_MODEL), tok),
            _const_spec((1, D_MODEL)),
            _const_spec((D_MODEL, IN_COLS)),
            _const_spec((2 * D_HGRN, D_MODEL)),
            pl.BlockSpec((rows, LANES), tab),
            pl.BlockSpec((rows, LANES), tab),
            pl.BlockSpec((rows, LANES), tab),
            _const_spec((lb_rows, D_HGRN)),
            _const_spec((HGRN_D, 1)),
            _const_spec((D_MODEL, D_MODEL)),
            _const_spec((1, D_MODEL)),
            _const_spec((2 * WINDOW, GQA_GROUP * WINDOW)),
            _const_spec((tb, tb)),
        ],
        out_specs=[
            pl.BlockSpec((rows, D_MODEL), tok),
            pl.BlockSpec((1, D_KV, WINDOW), per_b3),
            pl.BlockSpec((1, D_KV, WINDOW), per_b3),
            pl.BlockSpec((1, N_HGRN_HEADS, HGRN_D, HGRN_D), lambda b, n: (b, 0, 0, 0)),
        ],
        out_shape=[
            jax.ShapeDtypeStruct((batch * seq, D_MODEL), F32),
            jax.ShapeDtypeStruct((batch, D_KV, WINDOW), F32),
            jax.ShapeDtypeStruct((batch, D_KV, WINDOW), F32),
            jax.ShapeDtypeStruct((batch, N_HGRN_HEADS, HGRN_D, HGRN_D), F32),
        ],
        scratch_shapes=[
            pltpu.VMEM((N_KV_HEADS, WINDOW, LANES), BF16),
            pltpu.VMEM((D_KV, WINDOW), BF16),
            pltpu.VMEM((N_HGRN_HEADS, HGRN_D, HGRN_D), F32),
            pltpu.VMEM((MIX_PARTS, tb, D_HGRN), F32),
        ],
        compiler_params=pltpu.CompilerParams(
            dimension_semantics=("arbitrary", "arbitrary"), vmem_limit_bytes=VMEM_LIMIT),
        name="prompt_mix",
    )(sinks, x, g_pre.reshape(1, D_MODEL), w_in, w_in_t, cos, s_hi, s_lo, lb_raw, g_out_head.reshape(HGRN_D, 1),
      w_out, g_post.reshape(1, D_MODEL), bias, amask)


def _sample_mix_kernel(sink_ref, x_ref, gpre_ref, win_ref, cos_ref, shi_ref, slo_ref, lb_ref, goutc_ref,
                       wout_ref, gpost_ref, ckt_ref, cvt_ref, sin_ref,
                       y_ref, wkt_ref, wvt_ref, sout_ref,
                       q_scr, kn_scr, vn_scr, knt_scr, vnt_scr, ft_scr, hkt_scr, hqt_scr, hvt_scr, hgt_scr,
                       ot_scr, att_scr, *, layer, nb):
    step = pl.program_id(0)
    last = pl.num_programs(0) - 1
    bt = nb // N_HGRN_HEADS

    @pl.when(step == 0)
    def _():
        u = _rms(x_ref[...], gpre_ref[...]).astype(BF16)
        proj = _dot(u, win_ref[...])
        cos, s_hi, s_lo = cos_ref[...], shi_ref[...], slo_ref[...]
        scale = HEAD_DIM ** -0.5
        for j in range(D_ATTN // LANES):
            q_scr[:, LANES * j:LANES * (j + 1)] = _rope(
                proj[:, OFF_Q + LANES * j:OFF_Q + LANES * (j + 1)], cos, s_hi, s_lo) * scale
        k_new = _rope(proj[:, OFF_K:OFF_K + D_KV], cos, s_hi, s_lo)
        v_new = proj[:, OFF_V:OFF_V + D_KV]
        kn_scr[...] = k_new
        vn_scr[...] = v_new
        for scr, val in ((knt_scr, k_new), (vnt_scr, v_new)):
            for i, part in enumerate(_split3(val.T)):
                scr[i] = part
        lb = _lower_bound(lb_ref[...], layer)
        f_t = (lb + (1.0 - lb) * jax.nn.sigmoid(proj[:, OFF_HF:OFF_HF + D_HGRN])).T
        ft_scr[...] = f_t
        hkt_scr[...] = 1.0 - f_t
        hqt_scr[...] = (proj[:, OFF_HQ:OFF_HQ + D_HGRN] * (HGRN_D ** -0.5)).T
        hvt_scr[...] = proj[:, OFF_HI:OFF_HI + D_HGRN].T
        hgt_scr[...] = proj[:, OFF_HG:OFF_HG + D_HGRN].T

    base = pl.multiple_of(step * HGRN_D, HGRN_D)
    hv_t = hvt_scr[pl.ds(base, HGRN_D), :]

    def hgrn_row(k, o_acc):
        f_row = ft_scr[pl.ds(base + k, 1), :]
        s_new = f_row * sin_ref[k] + hkt_scr[pl.ds(base + k, 1), :] * hv_t
        sout_ref[k] = s_new
        return o_acc + hqt_scr[pl.ds(base + k, 1), :] * s_new

    ot_scr[pl.ds(base, HGRN_D), :] = lax.fori_loop(
        0, HGRN_D, hgrn_row, jnp.zeros((HGRN_D, nb), F32), unroll=8)

    lane8 = lax.broadcasted_iota(jnp.int32, (N_Q_HEADS, LANES), 1)
    row8 = lax.broadcasted_iota(jnp.int32, (N_Q_HEADS, LANES), 0)
    keep8 = (lane8 >= HALF) == (row8 >= GQA_GROUP)
    win_lane = lax.broadcasted_iota(jnp.int32, (D_KV, WINDOW), 1)
    sink = sink_ref[...]
    b0 = step * bt
    sel = (lax.broadcasted_iota(jnp.int32, (nb, bt), 0)
           == b0 + lax.broadcasted_iota(jnp.int32, (nb, bt), 1)).astype(BF16)
    k_cols = sum(_dot(knt_scr[i], sel) for i in range(3))
    v_cols = sum(_dot(vnt_scr[i], sel) for i in range(3))
    def one_row(bi):
        b = b0 + bi
        q_b = jnp.broadcast_to(q_scr[pl.ds(b, 1), :], (N_Q_HEADS, D_ATTN))
        qm = jnp.zeros((N_Q_HEADS, LANES), F32)
        for h in range(N_Q_HEADS):
            c = q_b[:, LANES * (h // 2):LANES * (h // 2 + 1)]
            if h % 2 != h // GQA_GROUP:
                c = pltpu.roll(c, HALF, 1)
            qm = jnp.where(row8 == h, c, qm)
        qm = jnp.where(keep8, qm, 0.0)
        k_new = kn_scr[pl.ds(b, 1), :]
        v_new = vn_scr[pl.ds(b, 1), :]
        k_old = ckt_ref[bi]
        s = _dot(qm.astype(BF16), k_old.astype(BF16))
        wkt_ref[bi] = jnp.where(win_lane == WINDOW - 1, k_cols[:, bi:bi + 1], pltpu.roll(k_old, WINDOW - 1, 1))
        yield
        s = jnp.where(lane8 >= 1, s, NEG_INF)
        s_new = jnp.sum(qm * k_new, axis=-1, keepdims=True)
        m = jnp.maximum(jnp.maximum(jnp.max(s, axis=-1, keepdims=True), s_new), sink)
        p = jnp.exp(s - m)
        p_new = jnp.exp(s_new - m)
        denom = jnp.sum(p, axis=-1, keepdims=True) + p_new + jnp.exp(sink - m)
        yield
        v_old = cvt_ref[bi]
        o = (_dot_nt(p.astype(BF16), v_old.astype(BF16)) + p_new * v_new) / denom
        att_scr[pl.ds(pl.multiple_of(b * N_Q_HEADS, N_Q_HEADS), N_Q_HEADS), :] = o
        wvt_ref[bi] = jnp.where(win_lane == WINDOW - 1, v_cols[:, bi:bi + 1], pltpu.roll(v_old, WINDOW - 1, 1))

    _interleave([one_row(bi) for bi in range(bt)], skew=0)

    @pl.when(step == last)
    def _():
        y = jnp.zeros((nb, D_MODEL), F32)
        for h in range(N_HGRN_HEADS):
            ks = slice(h * HGRN_D, (h + 1) * HGRN_D)
            o_t = ot_scr[ks, :]
            hn_t = (o_t * lax.rsqrt(jnp.mean(o_t * o_t, axis=0, keepdims=True) + EPS)
                    * goutc_ref[...] * _silu(hgt_scr[ks, :]))
            y = y + _dot_tn(hn_t.astype(BF16), wout_ref[D_ATTN + h * HGRN_D:D_ATTN + (h + 1) * HGRN_D, :])
        for h in range(N_Q_HEADS):
            g = h // GQA_GROUP
            a_h = att_scr[pl.ds(h, nb, stride=N_Q_HEADS), :][:, g * HALF:(g + 1) * HALF]
            y = y + _dot(a_h.astype(BF16), wout_ref[h * HEAD_DIM:(h + 1) * HEAD_DIM, :])
        y_ref[...] = x_ref[...] + _rms(y, gpost_ref[...])


def _sample_mix(x, sinks, g_pre, w_in, lb_raw, g_out_head, w_out, g_post, cache_kt, cache_vt, state_t, pos, layer):
    nb = x.shape[0]
    bt = nb // N_HGRN_HEADS
    cos, s_hi, s_lo = _rope_tables(pos)
    lb_rows = lb_raw.shape[0]
    blk3 = pl.BlockSpec((bt, D_KV, WINDOW), lambda i: (i, 0, 0))
    blk_s = pl.BlockSpec((HGRN_D, HGRN_D, nb), lambda i: (i, 0, 0))
    chan_major = pltpu.VMEM((D_HGRN, nb), F32)
    return pl.pallas_call(
        functools.partial(_sample_mix_kernel, layer=layer, nb=nb),
        grid=(N_HGRN_HEADS,),
        in_specs=[
            _const_spec((N_Q_HEADS, 1)),
            _const_spec((nb, D_MODEL)),
            _const_spec((1, D_MODEL)),
            _const_spec((D_MODEL, IN_COLS)),
            _const_spec((1, LANES)),
            _const_spec((1, LANES)),
            _const_spec((1, LANES)),
            _const_spec((lb_rows, D_HGRN)),
            _const_spec((HGRN_D, 1)),
            _const_spec((D_MODEL, D_MODEL)),
            _const_spec((1, D_MODEL)),
            blk3, blk3, blk_s,
        ],
        out_specs=[pl.BlockSpec((nb, D_MODEL), lambda i: (0, 0)), blk3, blk3, blk_s],
        out_shape=[
            jax.ShapeDtypeStruct((nb, D_MODEL), F32),
            jax.ShapeDtypeStruct(cache_kt.shape, F32),
            jax.ShapeDtypeStruct(cache_vt.shape, F32),
            jax.ShapeDtypeStruct(state_t.shape, F32),
        ],
        scratch_shapes=[
            pltpu.VMEM((nb, D_ATTN), F32),
            pltpu.VMEM((nb, D_KV), F32),
            pltpu.VMEM((nb, D_KV), F32),
            pltpu.VMEM((3, D_KV, nb), BF16),
            pltpu.VMEM((3, D_KV, nb), BF16),
            chan_major, chan_major, chan_major, chan_major, chan_major, chan_major,
            pltpu.VMEM((nb * N_Q_HEADS, LANES), F32),
        ],
        compiler_params=pltpu.CompilerParams(
            dimension_semantics=("arbitrary",), vmem_limit_bytes=VMEM_LIMIT),
        name="sample_mix",
    )(sinks.reshape(N_Q_HEADS, 1), x, g_pre.reshape(1, D_MODEL), w_in, cos, s_hi, s_lo, lb_raw,
      g_out_head.reshape(HGRN_D, 1), w_out, g_post.reshape(1, D_MODEL), cache_kt, cache_vt, state_t)


def kernel(x_prompt, x_sample, cache_win_k, cache_win_v, state_hgrn, ffn1_pre_g, ffn1_post_g, ffn1_w_gu,
           ffn1_w_down, mix_pre_g, mix_post_g, w_in, attn_sinks, hgrn_lb, hgrn_out_g, w_out, ffn2_pre_g,
           ffn2_post_g, ffn2_w_gu, ffn2_w_down):
    batch, seq, _ = x_prompt.shape
    nb, t_s, _ = x_sample.shape
    depth = w_in.shape[0]
    assert t_s == 1 and seq % (MIX_PARTS * MIX_TILE) == 0 and (batch * seq) % FFN_TILE == 0 and nb == LANES
    assert cache_win_k.shape[2:] == (WINDOW, N_KV_HEADS, HEAD_DIM)

    xp = x_prompt.reshape(batch * seq, D_MODEL)
    xs = x_sample.reshape(nb, D_MODEL)
    pos_s = PAST_LEN + np.arange(t_s)
    outs = [[] for _ in range(6)]
    for l in range(depth):
        w_in_l, w_out_l = w_in[l].astype(BF16), w_out[l].astype(BF16)
        w_in_t = w_in[l][:, OFF_HI:].T.astype(BF16)

        xp, xs = _ffn(xp, xs, ffn1_pre_g[l], ffn1_post_g[l], ffn1_w_gu[l], ffn1_w_down[l])

        xp, wkt_p, wvt_p, s_p = _prompt_mix(xp, attn_sinks[l], mix_pre_g[l], w_in_l, w_in_t, hgrn_lb,
                                            hgrn_out_g[l], w_out_l, mix_post_g[l], batch, seq, l)
        to_kt = lambda c: jnp.transpose(c, (0, 2, 3, 1)).reshape(-1, D_KV, WINDOW)
        from_kt = lambda c: jnp.transpose(c.reshape(-1, N_KV_HEADS, HEAD_DIM, WINDOW), (0, 3, 1, 2))
        state_t = jnp.transpose(state_hgrn[l], (1, 2, 3, 0)).reshape(D_HGRN, HGRN_D, nb)
        xs, wkt_s, wvt_s, st_s = _sample_mix(
            xs, attn_sinks[l], mix_pre_g[l], w_in_l, hgrn_lb, hgrn_out_g[l], w_out_l, mix_post_g[l],
            to_kt(cache_win_k[l]), to_kt(cache_win_v[l]), state_t, pos_s, l)
        s_s = jnp.transpose(st_s.reshape(N_HGRN_HEADS, HGRN_D, HGRN_D, nb), (3, 0, 1, 2))

        xp, xs = _ffn(xp, xs, ffn2_pre_g[l], ffn2_post_g[l], ffn2_w_gu[l], ffn2_w_down[l])

        for lst, val in zip(outs, (from_kt(wkt_p), from_kt(wvt_p), s_p, from_kt(wkt_s), from_kt(wvt_s), s_s)):
            lst.append(val)

    return (xp.reshape(batch, seq, D_MODEL), xs.reshape(nb, t_s, D_MODEL)) + tuple(jnp.stack(o) for o in outs)
```

```python
import functools

import jax
import jax.numpy as jnp
import numpy as np
from jax import lax
from jax.experimental import pallas as pl
from jax.experimental.pallas import tpu as pltpu

F32 = jnp.float32
BF16 = jnp.bfloat16

D_MODEL = 1024
D_FF = 2816
HEAD_DIM = 64
N_Q_HEADS = 8
N_KV_HEADS = 2
GQA_GROUP = N_Q_HEADS // N_KV_HEADS
WINDOW = 128
PAST_LEN = 8192
ROT_DIM = HEAD_DIM // 4
ROPE_THETA = 500000.0
N_HGRN_HEADS = 8
HGRN_D = 64
D_ATTN = N_Q_HEADS * HEAD_DIM
D_KV = N_KV_HEADS * HEAD_DIM
D_HGRN = N_HGRN_HEADS * HGRN_D
IN_COLS = D_ATTN + 2 * D_KV + 4 * D_HGRN
OFF_Q, OFF_K, OFF_V = 0, D_ATTN, D_ATTN + D_KV
OFF_HQ = D_ATTN + 2 * D_KV
OFF_HF, OFF_HI, OFF_HG = OFF_HQ + D_HGRN, OFF_HQ + 2 * D_HGRN, OFF_HQ + 3 * D_HGRN
EPS = 1e-6
NEG_INF = -1e30
LOG2E = 1.4426950408889634
LANES = 128
HALF = LANES // 2

FFN_TILE = 1024
FFN_PARTS = 4
FFN_W_STEPS = 8
MIX_TILE = 256
MIX_PARTS = 2
CHUNK = 64
SUB = 16
N_SUB = CHUNK // SUB
VMEM_LIMIT = 56 * 1024 * 1024


def _rms(x, g):
    return x * lax.rsqrt(jnp.mean(x * x, axis=-1, keepdims=True) + EPS) * g


def _silu(x):
    return x * jax.nn.sigmoid(x)


def _dot(a, b):
    return jnp.dot(a, b, preferred_element_type=F32)


def _dot_nt(a, b):
    return lax.dot_general(a, b, (((1,), (1,)), ((), ())), preferred_element_type=F32)


def _dot_tn(a, b):
    return lax.dot_general(a, b, (((0,), (0,)), ((), ())), preferred_element_type=F32)


def _split3(x):
    hi = x.astype(BF16)
    r = x - hi.astype(F32)
    mid = r.astype(BF16)
    lo = (r - mid.astype(F32)).astype(BF16)
    return hi, mid, lo


def _interleave(gens, skew):
    pending, active, rnd = list(gens), [], 0
    while pending or active:
        while pending and (skew == 0 or rnd % skew == 0):
            active.append(pending.pop(0))
            if skew:
                break
        active = [g for g in active if next(g, True) is None]
        rnd += 1


def _const_spec(shape):
    nd = len(shape)
    return pl.BlockSpec(shape, lambda *_: (0,) * nd, pipeline_mode=pl.Buffered(1))


def _ffn_kernel(x_ref, xs_ref, gpre_ref, gpost_ref, wgu32_ref, wd32_ref, o_ref, os_ref, wgu_ref, wd_ref):
    step = pl.program_id(0)

    @pl.when(step < FFN_W_STEPS)
    def _():
        gu_rows, d_rows = wgu32_ref.shape[0], wd32_ref.shape[0]
        wgu_ref[pl.ds(pl.multiple_of(step * gu_rows, gu_rows), gu_rows), :] = wgu32_ref[...].astype(BF16)
        wd_ref[pl.ds(pl.multiple_of(step * d_rows, d_rows), d_rows), :] = wd32_ref[...].astype(BF16)

    def half_step(src_ref, dst_ref, rs):
        x = src_ref[rs, :]
        h = _rms(x, gpre_ref[...]).astype(BF16)
        yield
        gate = _dot(h, wgu_ref[:, :D_FF])
        up = _dot(h, wgu_ref[:, D_FF:])
        yield
        act = (_silu(gate) * up).astype(BF16)
        yield
        y = _dot(act, wd_ref[...])
        yield
        dst_ref[rs, :] = x + 0.5 * _rms(y, gpost_ref[...])

    @pl.when(step >= FFN_W_STEPS)
    def _():
        rows = x_ref.shape[0] // FFN_PARTS
        _interleave([half_step(x_ref, o_ref, slice(r * rows, (r + 1) * rows)) for r in range(FFN_PARTS)], skew=1)

    @pl.when(step == pl.num_programs(0) - 1)
    def _():
        for _ in half_step(xs_ref, os_ref, slice(None)):
            pass


def _ffn(x, xs, g_pre, g_post, w_gu, w_down):
    n, nb = x.shape[0], xs.shape[0]
    tile_map = lambda i: (jnp.maximum(i - FFN_W_STEPS, 0), 0)
    chunk_map = lambda i: (jnp.minimum(i, FFN_W_STEPS - 1), 0)
    return pl.pallas_call(
        _ffn_kernel,
        grid=(FFN_W_STEPS + n // FFN_TILE,),
        in_specs=[
            pl.BlockSpec((FFN_TILE, D_MODEL), tile_map),
            _const_spec((nb, D_MODEL)),
            _const_spec((1, D_MODEL)),
            _const_spec((1, D_MODEL)),
            pl.BlockSpec((D_MODEL // FFN_W_STEPS, 2 * D_FF), chunk_map),
            pl.BlockSpec((D_FF // FFN_W_STEPS, D_MODEL), chunk_map),
        ],
        out_specs=[pl.BlockSpec((FFN_TILE, D_MODEL), tile_map),
                   pl.BlockSpec((nb, D_MODEL), lambda i: (0, 0))],
        out_shape=[jax.ShapeDtypeStruct((n, D_MODEL), F32), jax.ShapeDtypeStruct((nb, D_MODEL), F32)],
        scratch_shapes=[pltpu.VMEM((D_MODEL, 2 * D_FF), BF16), pltpu.VMEM((D_FF, D_MODEL), BF16)],
        compiler_params=pltpu.CompilerParams(
            dimension_semantics=("arbitrary",), vmem_limit_bytes=VMEM_LIMIT),
        name="ffn",
    )(x, xs, g_pre.reshape(1, D_MODEL), g_post.reshape(1, D_MODEL), w_gu, w_down)


def _lower_bound(lb_raw, layer):
    m = jnp.max(lb_raw, axis=0, keepdims=True)
    e = jnp.exp(lb_raw - m)
    return jnp.sum(e[: layer + 1], axis=0, keepdims=True) / jnp.sum(e, axis=0, keepdims=True)


def _rope(x, cos, sin_hi, sin_lo):
    return x * cos + pltpu.roll(x, ROT_DIM // 2, 1) * sin_hi + pltpu.roll(x, LANES - ROT_DIM // 2, 1) * sin_lo


def _rope_tables(pos):
    half = ROT_DIM // 2
    inv = (np.float32(ROPE_THETA) ** (-np.arange(half, dtype=np.float32) / half)).astype(np.float64)
    ang = np.asarray(pos, np.float64)[:, None] * inv[None, :]
    cos, sin = np.cos(ang), np.sin(ang)
    t = ang.shape[0]
    one = np.ones((t, HEAD_DIM - ROT_DIM))
    zero = np.zeros((t, HEAD_DIM - ROT_DIM))
    zh = np.zeros((t, half))
    c = np.concatenate([cos, cos, one], axis=1)
    s_hi = np.concatenate([zh, sin, zero], axis=1)
    s_lo = np.concatenate([-sin, zh, zero], axis=1)
    return tuple(jnp.asarray(np.tile(a, (1, LANES // HEAD_DIM)), F32) for a in (c, s_hi, s_lo))


def _dup_half(x, g, lo_half):
    xr = pltpu.roll(x, HALF, 1)
    return jnp.where(lo_half, x, xr) if g == 0 else jnp.where(lo_half, xr, x)


def _mix_tile(x, rope_tab, seq_start, src, dst, g_scr,
              sink_ref, gpre_ref, win_ref, wint_ref, lb_ref, goutc_ref, wout_ref, gpost_ref, bias_ref, amask_ref,
              layer):
    tb = MIX_TILE
    u = _rms(x, gpre_ref[...]).astype(BF16)

    def proj(off, width):
        return _dot(u, win_ref[:, off:off + width])

    p_attn = proj(OFF_Q, D_ATTN + 2 * D_KV)
    p_hf = proj(OFF_HF, D_HGRN)
    cos, s_hi, s_lo = rope_tab
    scale = HEAD_DIM ** -0.5 * LOG2E
    q_cols = [_rope(p_attn[:, LANES * j: LANES * (j + 1)], cos, s_hi, s_lo) * scale
              for j in range(D_ATTN // LANES)]
    k_rot = _rope(p_attn[:, OFF_K:OFF_K + D_KV], cos, s_hi, s_lo)
    v_new = p_attn[:, OFF_V:OFF_V + D_KV]

    lane = lax.broadcasted_iota(jnp.int32, (WINDOW, LANES), 1)
    lo_half = lane < HALF
    bias = bias_ref[...]
    if seq_start is None:
        bias_first = bias
    else:
        no_prev = jnp.where(seq_start, NEG_INF, 0.0)
        bias_first = jnp.concatenate([bias[:WINDOW] + no_prev, bias[WINDOW:]], axis=0)

    vt_tile = v_new.T.astype(BF16)
    yield
    k_prev, vt_prev = src["k_prev"], src["vt_prev"]
    scores = []
    for i in range(tb // WINDOW):
        r0 = i * WINDOW
        k_cur = [_dup_half(k_rot[r0:r0 + WINDOW], g, lo_half).astype(BF16) for g in range(N_KV_HEADS)]
        vt_keys = jnp.concatenate([vt_prev, vt_tile[:, r0:r0 + WINDOW]], axis=1)
        for g in range(N_KV_HEADS):
            keys = jnp.concatenate([k_prev[g], k_cur[g]], axis=0)
            heads = range(g * GQA_GROUP, (g + 1) * GQA_GROUP)
            qg = jnp.concatenate(
                [jnp.where(lo_half if h % 2 == 0 else ~lo_half, q_cols[h // 2][r0:r0 + WINDOW], 0.0)
                 for h in heads], axis=0).astype(BF16)
            s = _dot_nt(keys, qg) + (bias_first if i == 0 else bias)
            scores.append((s, vt_keys[g * HEAD_DIM:(g + 1) * HEAD_DIM], heads))
        k_prev, vt_prev = k_cur, vt_tile[:, r0:r0 + WINDOW]
    dst["k_prev"], dst["vt_prev"] = k_prev, vt_prev
    yield

    hq = proj(OFF_HQ, D_HGRN) * (HGRN_D ** -0.5)
    hv_t = _dot_nt(wint_ref[:D_HGRN, :], u).astype(BF16)
    hg_act_t = _silu(_dot_nt(wint_ref[D_HGRN:, :], u))
    yield

    att_t = [[None] * (tb // WINDOW) for _ in range(N_Q_HEADS)]
    probs = []
    for s, _, heads in scores:
        sink = jnp.concatenate(
            [jnp.full((1, WINDOW), sink_ref[h] * LOG2E, F32) for h in heads], axis=1)
        m = jnp.maximum(jnp.max(s, axis=0, keepdims=True), sink)
        p = jnp.exp2(s - m)
        denom = jnp.sum(p, axis=0, keepdims=True) + jnp.exp2(sink - m)
        probs.append((p.astype(BF16), 1.0 / denom))
    yield
    for idx, ((p, r_denom), (_, vt_g, heads)) in enumerate(zip(probs, scores)):
        o_t = _dot(vt_g, p) * r_denom
        for j, h in enumerate(heads):
            att_t[h][idx // N_KV_HEADS] = o_t[:, j * WINDOW:(j + 1) * WINDOW]
    att = jnp.concatenate(
        [jnp.concatenate([jnp.concatenate([att_t[2 * j][i], att_t[2 * j + 1][i]], axis=0).T
                          for j in range(N_Q_HEADS // 2)], axis=1)
         for i in range(tb // WINDOW)], axis=0)
    y_att = _dot(att.astype(BF16), wout_ref[:D_ATTN, :])
    yield

    lb = _lower_bound(lb_ref[...], layer)
    f = lb + (1.0 - lb) * jax.nn.sigmoid(p_hf)
    logf = jnp.log(f) * LOG2E
    hk = 1.0 - f

    tr = lax.broadcasted_iota(jnp.int32, (tb, tb), 0)
    tc = lax.broadcasted_iota(jnp.int32, (tb, tb), 1)
    tri = ((tr // CHUNK == tc // CHUNK) & (tc <= tr)).astype(BF16)
    g_cum = sum(_dot(tri, part) for part in _split3(logf))
    g_scr[...] = g_cum
    yield

    def bcast_row(r, rows):
        return jnp.broadcast_to(g_scr[r:r + 1, :], (rows, D_HGRN))

    zeros_sub = jnp.zeros((SUB, D_HGRN), F32)
    g_ref_q = jnp.concatenate(
        [zeros_sub if sb % N_SUB == 0 else bcast_row(sb * SUB - 1, SUB) for sb in range(tb // SUB)], axis=0)
    q_loc = (hq * jnp.exp2(g_cum - g_ref_q)).astype(BF16)
    g_end = jnp.concatenate(
        [bcast_row(c * CHUNK + CHUNK - 1, CHUNK) for c in range(tb // CHUNK)], axis=0)
    k_sub = []
    for i in range(N_SUB):
        live = (i + 1) * SUB
        pieces = []
        for c in range(tb // CHUNK):
            r0 = c * CHUNK
            g_ref_i = 0.0 if i == 0 else bcast_row(r0 + i * SUB - 1, live)
            pieces.append(hk[r0:r0 + live] * jnp.exp2(g_ref_i - g_cum[r0:r0 + live]))
            if live < CHUNK:
                pieces.append(jnp.zeros((CHUNK - live, D_HGRN), F32))
        k_sub.append(jnp.concatenate(pieces, axis=0).astype(BF16))

    n_ch = tb // CHUNK
    gc = [g_scr[c * CHUNK + CHUNK - 1:c * CHUNK + CHUNK, :] for c in range(n_ch)]

    def span(lo, hi):
        if hi <= lo:
            return jnp.ones((CHUNK, D_HGRN), F32)
        return jnp.broadcast_to(jnp.exp2(sum(gc[lo:hi])), (CHUNK, D_HGRN))

    zeros_chunk = jnp.zeros((CHUNK, D_HGRN), F32)
    k_end32 = hk * jnp.exp2(g_end - g_cum)
    q_glob32 = hq * jnp.exp2(g_cum)
    k_cross = [(k_end32 * jnp.concatenate(
        [span(cp + 1, c) if cp < c else zeros_chunk for cp in range(n_ch)], axis=0)).astype(BF16)
        for c in range(1, n_ch)]
    q_tile = (q_glob32 * jnp.concatenate([span(0, c) for c in range(n_ch)], axis=0)).astype(BF16)
    k_tile_end = (k_end32 * jnp.concatenate([span(cp + 1, n_ch) for cp in range(n_ch)], axis=0)).astype(BF16)
    decay_tile = jnp.exp2(sum(gc))
    q_glob = q_glob32.astype(BF16)

    def keep_rows(a, block, wanted):
        zero = jnp.zeros((block, a.shape[1]), a.dtype)
        return jnp.concatenate(
            [a[r * block:(r + 1) * block] if wanted(r) else zero for r in range(a.shape[0] // block)], axis=0)

    q_sub = [keep_rows(q_loc, SUB, lambda r, i=i: r % N_SUB == i) for i in range(N_SUB)]
    q_cross = [keep_rows(q_glob, CHUNK, lambda r, c=c: r == c) for c in range(1, n_ch)]
    yield
    amask_t = amask_ref[...]
    y = y_att
    out_group = N_HGRN_HEADS
    for first in range(0, N_HGRN_HEADS, out_group):
        heads = range(first, first + out_group)
        lanes = {h: slice(h * HGRN_D, (h + 1) * HGRN_D) for h in heads}
        pair = {}
        for h, ls in lanes.items():
            q_stack = jnp.concatenate([q_sub[i][:, ls] for i in range(N_SUB)], axis=1)
            k_stack = jnp.concatenate([k_sub[i][:, ls] for i in range(N_SUB)], axis=1)
            qc_stack = jnp.concatenate([q_cross[c - 1][:, ls] for c in range(1, n_ch)], axis=1)
            kc_stack = jnp.concatenate([k_cross[c - 1][:, ls] for c in range(1, n_ch)], axis=1)
            pair[h] = (_dot_nt(k_stack, q_stack), _dot_nt(kc_stack, qc_stack))
        yield
        o_t = {}
        for h, ls in lanes.items():
            same, cross = pair[h]
            half = tb // 2
            a_t = jnp.concatenate([
                jnp.concatenate([same[:half, :half] * amask_t[:half, :half] + cross[:half, :half],
                                 cross[:half, half:]], axis=1),
                jnp.concatenate([jnp.zeros((half, half), F32),
                                 same[half:, half:] * amask_t[half:, half:] + cross[half:, half:]], axis=1),
            ], axis=0).astype(BF16)
            vt_h = hv_t[ls, :]
            state_h = src["state"][h]
            o_t[h] = _dot(vt_h, a_t) + _dot_nt(state_h.astype(BF16), q_tile[:, ls])
            dst["state"][h] = state_h * decay_tile[:, ls] + _dot(vt_h, k_tile_end[:, ls])
        yield
        hn_t = {h: o_t[h] * lax.rsqrt(jnp.mean(o_t[h] * o_t[h], axis=0, keepdims=True) + EPS)
                * goutc_ref[...] * hg_act_t[ls, :] for h, ls in lanes.items()}
        grp = jnp.concatenate(
            [jnp.concatenate(
                [jnp.concatenate([hn_t[j], hn_t[j + 1]], axis=0)[:, r:r + LANES].T for j in heads[::2]],
                axis=1) for r in range(0, tb, LANES)], axis=0).astype(BF16)
        y = y + _dot(grp, wout_ref[D_ATTN + first * HGRN_D:D_ATTN + (first + out_group) * HGRN_D, :])
        yield

    dst["y"], dst["k_rot"], dst["v_new"] = x + _rms(y, gpost_ref[...]), k_rot, v_new


def _prompt_mix_kernel(sink_ref, x_ref, gpre_ref, win_ref, wint_ref, cos_ref, shi_ref, slo_ref, lb_ref, goutc_ref,
                       wout_ref, gpost_ref, bias_ref, amask_ref,
                       y_ref, wk_ref, wv_ref, s_ref,
                       kk_scr, vt_scr, st_scr, g_scr, *, layer):
    tb = MIX_TILE
    step = pl.program_id(1)
    last = pl.num_programs(1) - 1

    @pl.when(step == 0)
    def _():
        kk_scr[...] = jnp.zeros_like(kk_scr)
        vt_scr[...] = jnp.zeros_like(vt_scr)
        st_scr[...] = jnp.zeros_like(st_scr)

    hand = [{"state": [None] * N_HGRN_HEADS} for _ in range(MIX_PARTS + 1)]
    hand[0]["k_prev"] = [kk_scr[g] for g in range(N_KV_HEADS)]
    hand[0]["vt_prev"] = vt_scr[...]
    hand[0]["state"] = [st_scr[h] for h in range(N_HGRN_HEADS)]
    tiles = []
    for part in range(MIX_PARTS):
        rs = slice(part * tb, (part + 1) * tb)
        tiles.append(_mix_tile(
            x_ref[rs, :], (cos_ref[rs, :], shi_ref[rs, :], slo_ref[rs, :]), step == 0 if part == 0 else None,
            hand[part], hand[part + 1], g_scr.at[part],
            sink_ref, gpre_ref, win_ref, wint_ref, lb_ref, goutc_ref, wout_ref, gpost_ref, bias_ref, amask_ref,
            layer))
    _interleave(tiles, skew=0)
    for part in range(MIX_PARTS):
        y_ref[part * tb:(part + 1) * tb, :] = hand[part + 1]["y"]
    final = hand[MIX_PARTS]
    for g in range(N_KV_HEADS):
        kk_scr[g] = final["k_prev"][g]
    vt_scr[...] = final["vt_prev"]
    for h in range(N_HGRN_HEADS):
        st_scr[h] = final["state"][h]

    @pl.when(step == last)
    def _():
        wk_ref[0] = final["k_rot"][tb - WINDOW:].T
        wv_ref[0] = final["v_new"][tb - WINDOW:].T
        for h in range(N_HGRN_HEADS):
            s_ref[0, h] = final["state"][h].T


def _prompt_mix(x, sinks, g_pre, w_in, w_in_t, lb_raw, g_out_head, w_out, g_post, batch, seq, layer):
    tb = MIX_TILE
    rows = MIX_PARTS * tb
    nt = seq // rows
    cos, s_hi, s_lo = _rope_tables(np.arange(seq))
    tok = lambda b, n: (b * nt + n, 0)
    tab = lambda b, n: (n, 0)
    per_b3 = lambda b, n: (b, 0, 0)
    lb_rows = lb_raw.shape[0]
    key_i = np.arange(2 * WINDOW)[:, None]
    rel = np.arange(WINDOW)[None, :] + WINDOW - key_i
    bias = jnp.asarray(np.tile(np.where((rel >= 0) & (rel < WINDOW), 0.0, NEG_INF), (1, GQA_GROUP)), F32)
    t_i = np.arange(tb)
    amask = jnp.asarray((t_i[:, None] // CHUNK == t_i[None, :] // CHUNK) & (t_i[:, None] <= t_i[None, :]), F32)
    return pl.pallas_call(
        functools.partial(_prompt_mix_kernel, layer=layer),
        grid=(batch, nt),
        in_specs=[
            pl.BlockSpec(memory_space=pltpu.SMEM),
            pl.BlockSpec((rows, D_MODEL), tok),
            _const_spec((1, D_MODEL)),
            _const_spec((D_MODEL, IN_COLS)),
            _const_spec((2 * D_HGRN, D_MODEL)),
            pl.BlockSpec((rows, LANES), tab),
            pl.BlockSpec((rows, LANES), tab),
            pl.BlockSpec((rows, LANES), tab),
            _const_spec((lb_rows, D_HGRN)),
            _const_spec((HGRN_D, 1)),
            _const_spec((D_MODEL, D_MODEL)),
            _const_spec((1, D_MODEL)),
            _const_spec((2 * WINDOW, GQA_GROUP * WINDOW)),
            _const_spec((tb, tb)),
        ],
        out_specs=[
            pl.BlockSpec((rows, D_MODEL), tok),
            pl.BlockSpec((1, D_KV, WINDOW), per_b3),
            pl.BlockSpec((1, D_KV, WINDOW), per_b3),
            pl.BlockSpec((1, N_HGRN_HEADS, HGRN_D, HGRN_D), lambda b, n: (b, 0, 0, 0)),
        ],
        out_shape=[
            jax.ShapeDtypeStruct((batch * seq, D_MODEL), F32),
            jax.ShapeDtypeStruct((batch, D_KV, WINDOW), F32),
            jax.ShapeDtypeStruct((batch, D_KV, WINDOW), F32),
            jax.ShapeDtypeStruct((batch, N_HGRN_HEADS, HGRN_D, HGRN_D), F32),
        ],
        scratch_shapes=[
            pltpu.VMEM((N_KV_HEADS, WINDOW, LANES), BF16),
            pltpu.VMEM((D_KV, WINDOW), BF16),
            pltpu.VMEM((N_HGRN_HEADS, HGRN_D, HGRN_D), F32),
            pltpu.VMEM((MIX_PARTS, tb, D_HGRN), F32),
        ],
        compiler_params=pltpu.CompilerParams(
            dimension_semantics=("arbitrary", "arbitrary"), vmem_limit_bytes=VMEM_LIMIT),
        name="prompt_mix",
    )(sinks, x, g_pre.reshape(1, D_MODEL), w_in, w_in_t, cos, s_hi, s_lo, lb_raw, g_out_head.reshape(HGRN_D, 1),
      w_out, g_post.reshape(1, D_MODEL), bias, amask)


def _sample_mix_kernel(sink_ref, x_ref, gpre_ref, win_ref, cos_ref, shi_ref, slo_ref, lb_ref, goutc_ref,
                       wout_ref, gpost_ref, ckt_ref, cvt_ref, sin_ref,
                       y_ref, wkt_ref, wvt_ref, sout_ref,
                       q_scr, kn_scr, vn_scr, knt_scr, vnt_scr, ft_scr, hkt_scr, hqt_scr, hvt_scr, hgt_scr,
                       ot_scr, att_scr, *, layer, nb):
    step = pl.program_id(0)
    last = pl.num_programs(0) - 1
    bt = nb // N_HGRN_HEADS

    @pl.when(step == 0)
    def _():
        u = _rms(x_ref[...], gpre_ref[...]).astype(BF16)
        proj = _dot(u, win_ref[...])
        cos, s_hi, s_lo = cos_ref[...], shi_ref[...], slo_ref[...]
        scale = HEAD_DIM ** -0.5
        for j in range(D_ATTN // LANES):
            q_scr[:, LANES * j:LANES * (j + 1)] = _rope(
                proj[:, OFF_Q + LANES * j:OFF_Q + LANES * (j + 1)], cos, s_hi, s_lo) * scale
        k_new = _rope(proj[:, OFF_K:OFF_K + D_KV], cos, s_hi, s_lo)
        v_new = proj[:, OFF_V:OFF_V + D_KV]
        kn_scr[...] = k_new
        vn_scr[...] = v_new
        for scr, val in ((knt_scr, k_new), (vnt_scr, v_new)):
            for i, part in enumerate(_split3(val.T)):
                scr[i] = part
        lb = _lower_bound(lb_ref[...], layer)
        f_t = (lb + (1.0 - lb) * jax.nn.sigmoid(proj[:, OFF_HF:OFF_HF + D_HGRN])).T
        ft_scr[...] = f_t
        hkt_scr[...] = 1.0 - f_t
        hqt_scr[...] = (proj[:, OFF_HQ:OFF_HQ + D_HGRN] * (HGRN_D ** -0.5)).T
        hvt_scr[...] = proj[:, OFF_HI:OFF_HI + D_HGRN].T
        hgt_scr[...] = proj[:, OFF_HG:OFF_HG + D_HGRN].T

    base = pl.multiple_of(step * HGRN_D, HGRN_D)
    hv_t = hvt_scr[pl.ds(base, HGRN_D), :]

    def hgrn_row(k, o_acc):
        f_row = ft_scr[pl.ds(base + k, 1), :]
        s_new = f_row * sin_ref[k] + hkt_scr[pl.ds(base + k, 1), :] * hv_t
        sout_ref[k] = s_new
        return o_acc + hqt_scr[pl.ds(base + k, 1), :] * s_new

    ot_scr[pl.ds(base, HGRN_D), :] = lax.fori_loop(
        0, HGRN_D, hgrn_row, jnp.zeros((HGRN_D, nb), F32), unroll=8)

    lane8 = lax.broadcasted_iota(jnp.int32, (N_Q_HEADS, LANES), 1)
    row8 = lax.broadcasted_iota(jnp.int32, (N_Q_HEADS, LANES), 0)
    keep8 = (lane8 >= HALF) == (row8 >= GQA_GROUP)
    win_lane = lax.broadcasted_iota(jnp.int32, (D_KV, WINDOW), 1)
    sink = sink_ref[...]
    b0 = step * bt
    sel = (lax.broadcasted_iota(jnp.int32, (nb, bt), 0)
           == b0 + lax.broadcasted_iota(jnp.int32, (nb, bt), 1)).astype(BF16)
    k_cols = sum(_dot(knt_scr[i], sel) for i in range(3))
    v_cols = sum(_dot(vnt_scr[i], sel) for i in range(3))
    def one_row(bi):
        b = b0 + bi
        q_b = jnp.broadcast_to(q_scr[pl.ds(b, 1), :], (N_Q_HEADS, D_ATTN))
        qm = jnp.zeros((N_Q_HEADS, LANES), F32)
        for h in range(N_Q_HEADS):
            c = q_b[:, LANES * (h // 2):LANES * (h // 2 + 1)]
            if h % 2 != h // GQA_GROUP:
                c = pltpu.roll(c, HALF, 1)
            qm = jnp.where(row8 == h, c, qm)
        qm = jnp.where(keep8, qm, 0.0)
        k_new = kn_scr[pl.ds(b, 1), :]
        v_new = vn_scr[pl.ds(b, 1), :]
        k_old = ckt_ref[bi]
        s = _dot(qm.astype(BF16), k_old.astype(BF16))
        wkt_ref[bi] = jnp.where(win_lane == WINDOW - 1, k_cols[:, bi:bi + 1], pltpu.roll(k_old, WINDOW - 1, 1))
        yield
        s = jnp.where(lane8 >= 1, s, NEG_INF)
        s_new = jnp.sum(qm * k_new, axis=-1, keepdims=True)
        m = jnp.maximum(jnp.maximum(jnp.max(s, axis=-1, keepdims=True), s_new), sink)
        p = jnp.exp(s - m)
        p_new = jnp.exp(s_new - m)
        denom = jnp.sum(p, axis=-1, keepdims=True) + p_new + jnp.exp(sink - m)
        yield
        v_old = cvt_ref[bi]
        o = (_dot_nt(p.astype(BF16), v_old.astype(BF16)) + p_new * v_new) / denom
        att_scr[pl.ds(pl.multiple_of(b * N_Q_HEADS, N_Q_HEADS), N_Q_HEADS), :] = o
        wvt_ref[bi] = jnp.where(win_lane == WINDOW - 1, v_cols[:, bi:bi + 1], pltpu.roll(v_old, WINDOW - 1, 1))

    _interleave([one_row(bi) for bi in range(bt)], skew=0)

    @pl.when(step == last)
    def _():
        y = jnp.zeros((nb, D_MODEL), F32)
        for h in range(N_HGRN_HEADS):
            ks = slice(h * HGRN_D, (h + 1) * HGRN_D)
            o_t = ot_scr[ks, :]
            hn_t = (o_t * lax.rsqrt(jnp.mean(o_t * o_t, axis=0, keepdims=True) + EPS)
                    * goutc_ref[...] * _silu(hgt_scr[ks, :]))
            y = y + _dot_tn(hn_t.astype(BF16), wout_ref[D_ATTN + h * HGRN_D:D_ATTN + (h + 1) * HGRN_D, :])
        for h in range(N_Q_HEADS):
            g = h // GQA_GROUP
            a_h = att_scr[pl.ds(h, nb, stride=N_Q_HEADS), :][:, g * HALF:(g + 1) * HALF]
            y = y + _dot(a_h.astype(BF16), wout_ref[h * HEAD_DIM:(h + 1) * HEAD_DIM, :])
        y_ref[...] = x_ref[...] + _rms(y, gpost_ref[...])


def _sample_mix(x, sinks, g_pre, w_in, lb_raw, g_out_head, w_out, g_post, cache_kt, cache_vt, state_t, pos, layer):
    nb = x.shape[0]
    bt = nb // N_HGRN_HEADS
    cos, s_hi, s_lo = _rope_tables(pos)
    lb_rows = lb_raw.shape[0]
    blk3 = pl.BlockSpec((bt, D_KV, WINDOW), lambda i: (i, 0, 0))
    blk_s = pl.BlockSpec((HGRN_D, HGRN_D, nb), lambda i: (i, 0, 0))
    chan_major = pltpu.VMEM((D_HGRN, nb), F32)
    return pl.pallas_call(
        functools.partial(_sample_mix_kernel, layer=layer, nb=nb),
        grid=(N_HGRN_HEADS,),
        in_specs=[
            _const_spec((N_Q_HEADS, 1)),
            _const_spec((nb, D_MODEL)),
            _const_spec((1, D_MODEL)),
            _const_spec((D_MODEL, IN_COLS)),
            _const_spec((1, LANES)),
            _const_spec((1, LANES)),
            _const_spec((1, LANES)),
            _const_spec((lb_rows, D_HGRN)),
            _const_spec((HGRN_D, 1)),
            _const_spec((D_MODEL, D_MODEL)),
            _const_spec((1, D_MODEL)),
            blk3, blk3, blk_s,
        ],
        out_specs=[pl.BlockSpec((nb, D_MODEL), lambda i: (0, 0)), blk3, blk3, blk_s],
        out_shape=[
            jax.ShapeDtypeStruct((nb, D_MODEL), F32),
            jax.ShapeDtypeStruct(cache_kt.shape, F32),
            jax.ShapeDtypeStruct(cache_vt.shape, F32),
            jax.ShapeDtypeStruct(state_t.shape, F32),
        ],
        scratch_shapes=[
            pltpu.VMEM((nb, D_ATTN), F32),
            pltpu.VMEM((nb, D_KV), F32),
            pltpu.VMEM((nb, D_KV), F32),
            pltpu.VMEM((3, D_KV, nb), BF16),
            pltpu.VMEM((3, D_KV, nb), BF16),
            chan_major, chan_major, chan_major, chan_major, chan_major, chan_major,
            pltpu.VMEM((nb * N_Q_HEADS, LANES), F32),
        ],
        compiler_params=pltpu.CompilerParams(
            dimension_semantics=("arbitrary",), vmem_limit_bytes=VMEM_LIMIT),
        name="sample_mix",
    )(sinks.reshape(N_Q_HEADS, 1), x, g_pre.reshape(1, D_MODEL), w_in, cos, s_hi, s_lo, lb_raw,
      g_out_head.reshape(HGRN_D, 1), w_out, g_post.reshape(1, D_MODEL), cache_kt, cache_vt, state_t)


def kernel(x_prompt, x_sample, cache_win_k, cache_win_v, state_hgrn, ffn1_pre_g, ffn1_post_g, ffn1_w_gu,
           ffn1_w_down, mix_pre_g, mix_post_g, w_in, attn_sinks, hgrn_lb, hgrn_out_g, w_out, ffn2_pre_g,
           ffn2_post_g, ffn2_w_gu, ffn2_w_down):
    batch, seq, _ = x_prompt.shape
    nb, t_s, _ = x_sample.shape
    depth = w_in.shape[0]
    assert t_s == 1 and seq % (MIX_PARTS * MIX_TILE) == 0 and (batch * seq) % FFN_TILE == 0 and nb == LANES
    assert cache_win_k.shape[2:] == (WINDOW, N_KV_HEADS, HEAD_DIM)

    xp = x_prompt.reshape(batch * seq, D_MODEL)
    xs = x_sample.reshape(nb, D_MODEL)
    pos_s = PAST_LEN + np.arange(t_s)
    outs = [[] for _ in range(6)]
    for l in range(depth):
        w_in_l, w_out_l = w_in[l].astype(BF16), w_out[l].astype(BF16)
        w_in_t = w_in[l][:, OFF_HI:].T.astype(BF16)

        xp, xs = _ffn(xp, xs, ffn1_pre_g[l], ffn1_post_g[l], ffn1_w_gu[l], ffn1_w_down[l])

        xp, wkt_p, wvt_p, s_p = _prompt_mix(xp, attn_sinks[l], mix_pre_g[l], w_in_l, w_in_t, hgrn_lb,
                                            hgrn_out_g[l], w_out_l, mix_post_g[l], batch, seq, l)
        to_kt = lambda c: jnp.transpose(c, (0, 2, 3, 1)).reshape(-1, D_KV, WINDOW)
        from_kt = lambda c: jnp.transpose(c.reshape(-1, N_KV_HEADS, HEAD_DIM, WINDOW), (0, 3, 1, 2))
        state_t = jnp.transpose(state_hgrn[l], (1, 2, 3, 0)).reshape(D_HGRN, HGRN_D, nb)
        xs, wkt_s, wvt_s, st_s = _sample_mix(
            xs, attn_sinks[l], mix_pre_g[l], w_in_l, hgrn_lb, hgrn_out_g[l], w_out_l, mix_post_g[l],
            to_kt(cache_win_k[l]), to_kt(cache_win_v[l]), state_t, pos_s, l)
        s_s = jnp.transpose(st_s.reshape(N_HGRN_HEADS, HGRN_D, HGRN_D, nb), (3, 0, 1, 2))

        xp, xs = _ffn(xp, xs, ffn2_pre_g[l], ffn2_post_g[l], ffn2_w_gu[l], ffn2_w_down[l])

        for lst, val in zip(outs, (from_kt(wkt_p), from_kt(wvt_p), s_p, from_kt(wkt_s), from_kt(wvt_s), s_s)):
            lst.append(val)

    return (xp.reshape(batch, seq, D_MODEL), xs.reshape(nb, t_s, D_MODEL)) + tuple(jnp.stack(o) for o in outs)
```

```python
import functools

import jax
import jax.numpy as jnp
import numpy as np
from jax import lax
from jax.experimental import pallas as pl
from jax.experimental.pallas import tpu as pltpu

F32 = jnp.float32
BF16 = jnp.bfloat16

D_MODEL = 1024
D_FF = 2816
HEAD_DIM = 64
N_Q_HEADS = 8
N_KV_HEADS = 2
GQA_GROUP = N_Q_HEADS // N_KV_HEADS
WINDOW = 128
PAST_LEN = 8192
ROT_DIM = HEAD_DIM // 4
ROPE_THETA = 500000.0
N_HGRN_HEADS = 8
HGRN_D = 64
D_ATTN = N_Q_HEADS * HEAD_DIM
D_KV = N_KV_HEADS * HEAD_DIM
D_HGRN = N_HGRN_HEADS * HGRN_D
IN_COLS = D_ATTN + 2 * D_KV + 4 * D_HGRN
OFF_Q, OFF_K, OFF_V = 0, D_ATTN, D_ATTN + D_KV
OFF_HQ = D_ATTN + 2 * D_KV
OFF_HF, OFF_HI, OFF_HG = OFF_HQ + D_HGRN, OFF_HQ + 2 * D_HGRN, OFF_HQ + 3 * D_HGRN
EPS = 1e-6
NEG_INF = -1e30
LOG2E = 1.4426950408889634
LANES = 128
HALF = LANES // 2

FFN_TILE = 1024
FFN_PARTS = 4
FFN_W_STEPS = 8
MIX_TILE = 256
MIX_PARTS = 2
CHUNK = 64
SUB = 16
N_SUB = CHUNK // SUB
VMEM_LIMIT = 56 * 1024 * 1024


def _rms(x, g):
    return x * lax.rsqrt(jnp.mean(x * x, axis=-1, keepdims=True) + EPS) * g


def _silu(x):
    return x * jax.nn.sigmoid(x)


def _dot(a, b):
    return jnp.dot(a, b, preferred_element_type=F32)


def _dot_nt(a, b):
    return lax.dot_general(a, b, (((1,), (1,)), ((), ())), preferred_element_type=F32)


def _dot_tn(a, b):
    return lax.dot_general(a, b, (((0,), (0,)), ((), ())), preferred_element_type=F32)


def _split3(x):
    hi = x.astype(BF16)
    r = x - hi.astype(F32)
    mid = r.astype(BF16)
    lo = (r - mid.astype(F32)).astype(BF16)
    return hi, mid, lo


def _interleave(gens, skew):
    pending, active, rnd = list(gens), [], 0
    while pending or active:
        while pending and (skew == 0 or rnd % skew == 0):
            active.append(pending.pop(0))
            if skew:
                break
        active = [g for g in active if next(g, True) is None]
        rnd += 1


def _const_spec(shape):
    nd = len(shape)
    return pl.BlockSpec(shape, lambda *_: (0,) * nd, pipeline_mode=pl.Buffered(1))


def _ffn_kernel(x_ref, xs_ref, gpre_ref, gpost_ref, wgu32_ref, wd32_ref, o_ref, os_ref, wgu_ref, wd_ref):
    step = pl.program_id(0)

    @pl.when(step < FFN_W_STEPS)
    def _():
        gu_rows, d_rows = wgu32_ref.shape[0], wd32_ref.shape[0]
        wgu_ref[pl.ds(pl.multiple_of(step * gu_rows, gu_rows), gu_rows), :] = wgu32_ref[...].astype(BF16)
        wd_ref[pl.ds(pl.multiple_of(step * d_rows, d_rows), d_rows), :] = wd32_ref[...].astype(BF16)

    def half_step(src_ref, dst_ref, rs):
        x = src_ref[rs, :]
        h = _rms(x, gpre_ref[...]).astype(BF16)
        yield
        gate = _dot(h, wgu_ref[:, :D_FF])
        up = _dot(h, wgu_ref[:, D_FF:])
        yield
        act = (_silu(gate) * up).astype(BF16)
        yield
        y = _dot(act, wd_ref[...])
        yield
        dst_ref[rs, :] = x + 0.5 * _rms(y, gpost_ref[...])

    @pl.when(step >= FFN_W_STEPS)
    def _():
        rows = x_ref.shape[0] // FFN_PARTS
        _interleave([half_step(x_ref, o_ref, slice(r * rows, (r + 1) * rows)) for r in range(FFN_PARTS)], skew=1)

    @pl.when(step == pl.num_programs(0) - 1)
    def _():
        for _ in half_step(xs_ref, os_ref, slice(None)):
            pass


def _ffn(x, xs, g_pre, g_post, w_gu, w_down):
    n, nb = x.shape[0], xs.shape[0]
    tile_map = lambda i: (jnp.maximum(i - FFN_W_STEPS, 0), 0)
    chunk_map = lambda i: (jnp.minimum(i, FFN_W_STEPS - 1), 0)
    return pl.pallas_call(
        _ffn_kernel,
        grid=(FFN_W_STEPS + n // FFN_TILE,),
        in_specs=[
            pl.BlockSpec((FFN_TILE, D_MODEL), tile_map),
            _const_spec((nb, D_MODEL)),
            _const_spec((1, D_MODEL)),
            _const_spec((1, D_MODEL)),
            pl.BlockSpec((D_MODEL // FFN_W_STEPS, 2 * D_FF), chunk_map),
            pl.BlockSpec((D_FF // FFN_W_STEPS, D_MODEL), chunk_map),
        ],
        out_specs=[pl.BlockSpec((FFN_TILE, D_MODEL), tile_map),
                   pl.BlockSpec((nb, D_MODEL), lambda i: (0, 0))],
        out_shape=[jax.ShapeDtypeStruct((n, D_MODEL), F32), jax.ShapeDtypeStruct((nb, D_MODEL), F32)],
        scratch_shapes=[pltpu.VMEM((D_MODEL, 2 * D_FF), BF16), pltpu.VMEM((D_FF, D_MODEL), BF16)],
        compiler_params=pltpu.CompilerParams(
            dimension_semantics=("arbitrary",), vmem_limit_bytes=VMEM_LIMIT),
        name="ffn",
    )(x, xs, g_pre.reshape(1, D_MODEL), g_post.reshape(1, D_MODEL), w_gu, w_down)


def _lower_bound(lb_raw, layer):
    m = jnp.max(lb_raw, axis=0, keepdims=True)
    e = jnp.exp(lb_raw - m)
    return jnp.sum(e[: layer + 1], axis=0, keepdims=True) / jnp.sum(e, axis=0, keepdims=True)


def _rope(x, cos, sin_hi, sin_lo):
    return x * cos + pltpu.roll(x, ROT_DIM // 2, 1) * sin_hi + pltpu.roll(x, LANES - ROT_DIM // 2, 1) * sin_lo


def _rope_tables(pos):
    half = ROT_DIM // 2
    inv = (np.float32(ROPE_THETA) ** (-np.arange(half, dtype=np.float32) / half)).astype(np.float64)
    ang = np.asarray(pos, np.float64)[:, None] * inv[None, :]
    cos, sin = np.cos(ang), np.sin(ang)
    t = ang.shape[0]
    one = np.ones((t, HEAD_DIM - ROT_DIM))
    zero = np.zeros((t, HEAD_DIM - ROT_DIM))
    zh = np.zeros((t, half))
    c = np.concatenate([cos, cos, one], axis=1)
    s_hi = np.concatenate([zh, sin, zero], axis=1)
    s_lo = np.concatenate([-sin, zh, zero], axis=1)
    return tuple(jnp.asarray(np.tile(a, (1, LANES // HEAD_DIM)), F32) for a in (c, s_hi, s_lo))


def _dup_half(x, g, lo_half):
    xr = pltpu.roll(x, HALF, 1)
    return jnp.where(lo_half, x, xr) if g == 0 else jnp.where(lo_half, xr, x)


def _mix_tile(x, rope_tab, seq_start, src, dst, g_scr,
              sink_ref, gpre_ref, win_ref, wint_ref, lb_ref, goutc_ref, wout_ref, gpost_ref, bias_ref, amask_ref,
              layer):
    tb = MIX_TILE
    u = _rms(x, gpre_ref[...]).astype(BF16)

    def proj(off, width):
        return _dot(u, win_ref[:, off:off + width])

    p_attn = proj(OFF_Q, D_ATTN + 2 * D_KV)
    p_hf = proj(OFF_HF, D_HGRN)
    cos, s_hi, s_lo = rope_tab
    scale = HEAD_DIM ** -0.5 * LOG2E
    q_cols = [_rope(p_attn[:, LANES * j: LANES * (j + 1)], cos, s_hi, s_lo) * scale
              for j in range(D_ATTN // LANES)]
    k_rot = _rope(p_attn[:, OFF_K:OFF_K + D_KV], cos, s_hi, s_lo)
    v_new = p_attn[:, OFF_V:OFF_V + D_KV]

    lane = lax.broadcasted_iota(jnp.int32, (WINDOW, LANES), 1)
    lo_half = lane < HALF
    bias = bias_ref[...]
    if seq_start is None:
        bias_first = bias
    else:
        no_prev = jnp.where(seq_start, NEG_INF, 0.0)
        bias_first = jnp.concatenate([bias[:WINDOW] + no_prev, bias[WINDOW:]], axis=0)

    vt_tile = v_new.T.astype(BF16)
    yield
    k_prev, vt_prev = src["k_prev"], src["vt_prev"]
    scores = []
    for i in range(tb // WINDOW):
        r0 = i * WINDOW
        k_cur = [_dup_half(k_rot[r0:r0 + WINDOW], g, lo_half).astype(BF16) for g in range(N_KV_HEADS)]
        vt_keys = jnp.concatenate([vt_prev, vt_tile[:, r0:r0 + WINDOW]], axis=1)
        for g in range(N_KV_HEADS):
            keys = jnp.concatenate([k_prev[g], k_cur[g]], axis=0)
            heads = range(g * GQA_GROUP, (g + 1) * GQA_GROUP)
            qg = jnp.concatenate(
                [jnp.where(lo_half if h % 2 == 0 else ~lo_half, q_cols[h // 2][r0:r0 + WINDOW], 0.0)
                 for h in heads], axis=0).astype(BF16)
            s = _dot_nt(keys, qg) + (bias_first if i == 0 else bias)
            scores.append((s, vt_keys[g * HEAD_DIM:(g + 1) * HEAD_DIM], heads))
        k_prev, vt_prev = k_cur, vt_tile[:, r0:r0 + WINDOW]
    dst["k_prev"], dst["vt_prev"] = k_prev, vt_prev
    yield

    hq = proj(OFF_HQ, D_HGRN) * (HGRN_D ** -0.5)
    hv_t = _dot_nt(wint_ref[:D_HGRN, :], u).astype(BF16)
    hg_act_t = _silu(_dot_nt(wint_ref[D_HGRN:, :], u))
    yield

    att_t = [[None] * (tb // WINDOW) for _ in range(N_Q_HEADS)]
    probs = []
    for s, _, heads in scores:
        sink = jnp.concatenate(
            [jnp.full((1, WINDOW), sink_ref[h] * LOG2E, F32) for h in heads], axis=1)
        m = jnp.maximum(jnp.max(s, axis=0, keepdims=True), sink)
        p = jnp.exp2(s - m)
        denom = jnp.sum(p, axis=0, keepdims=True) + jnp.exp2(sink - m)
        probs.append((p.astype(BF16), 1.0 / denom))
    yield
    for idx, ((p, r_denom), (_, vt_g, heads)) in enumerate(zip(probs, scores)):
        o_t = _dot(vt_g, p) * r_denom
        for j, h in enumerate(heads):
            att_t[h][idx // N_KV_HEADS] = o_t[:, j * WINDOW:(j + 1) * WINDOW]
    att = jnp.concatenate(
        [jnp.concatenate([jnp.concatenate([att_t[2 * j][i], att_t[2 * j + 1][i]], axis=0).T
                          for j in range(N_Q_HEADS // 2)], axis=1)
         for i in range(tb // WINDOW)], axis=0)
    y_att = _dot(att.astype(BF16), wout_ref[:D_ATTN, :])
    yield

    lb = _lower_bound(lb_ref[...], layer)
    f = lb + (1.0 - lb) * jax.nn.sigmoid(p_hf)
    logf = jnp.log(f) * LOG2E
    hk = 1.0 - f

    tr = lax.broadcasted_iota(jnp.int32, (tb, tb), 0)
    tc = lax.broadcasted_iota(jnp.int32, (tb, tb), 1)
    tri = ((tr // CHUNK == tc // CHUNK) & (tc <= tr)).astype(BF16)
    g_cum = sum(_dot(tri, part) for part in _split3(logf))
    g_scr[...] = g_cum
    yield

    def bcast_row(r, rows):
        return jnp.broadcast_to(g_scr[r:r + 1, :], (rows, D_HGRN))

    zeros_sub = jnp.zeros((SUB, D_HGRN), F32)
    g_ref_q = jnp.concatenate(
        [zeros_sub if sb % N_SUB == 0 else bcast_row(sb * SUB - 1, SUB) for sb in range(tb // SUB)], axis=0)
    q_loc = (hq * jnp.exp2(g_cum - g_ref_q)).astype(BF16)
    g_end = jnp.concatenate(
        [bcast_row(c * CHUNK + CHUNK - 1, CHUNK) for c in range(tb // CHUNK)], axis=0)
    k_sub = []
    for i in range(N_SUB):
        live = (i + 1) * SUB
        pieces = []
        for c in range(tb // CHUNK):
            r0 = c * CHUNK
            g_ref_i = 0.0 if i == 0 else bcast_row(r0 + i * SUB - 1, live)
            pieces.append(hk[r0:r0 + live] * jnp.exp2(g_ref_i - g_cum[r0:r0 + live]))
            if live < CHUNK:
                pieces.append(jnp.zeros((CHUNK - live, D_HGRN), F32))
        k_sub.append(jnp.concatenate(pieces, axis=0).astype(BF16))

    n_ch = tb // CHUNK
    gc = [g_scr[c * CHUNK + CHUNK - 1:c * CHUNK + CHUNK, :] for c in range(n_ch)]

    def span(lo, hi):
        if hi <= lo:
            return jnp.ones((CHUNK, D_HGRN), F32)
        return jnp.broadcast_to(jnp.exp2(sum(gc[lo:hi])), (CHUNK, D_HGRN))

    zeros_chunk = jnp.zeros((CHUNK, D_HGRN), F32)
    k_end32 = hk * jnp.exp2(g_end - g_cum)
    q_glob32 = hq * jnp.exp2(g_cum)
    k_cross = [(k_end32 * jnp.concatenate(
        [span(cp + 1, c) if cp < c else zeros_chunk for cp in range(n_ch)], axis=0)).astype(BF16)
        for c in range(1, n_ch)]
    q_tile = (q_glob32 * jnp.concatenate([span(0, c) for c in range(n_ch)], axis=0)).astype(BF16)
    k_tile_end = (k_end32 * jnp.concatenate([span(cp + 1, n_ch) for cp in range(n_ch)], axis=0)).astype(BF16)
    decay_tile = jnp.exp2(sum(gc))
    q_glob = q_glob32.astype(BF16)

    def keep_rows(a, block, wanted):
        zero = jnp.zeros((block, a.shape[1]), a.dtype)
        return jnp.concatenate(
            [a[r * block:(r + 1) * block] if wanted(r) else zero for r in range(a.shape[0] // block)], axis=0)

    q_sub = [keep_rows(q_loc, SUB, lambda r, i=i: r % N_SUB == i) for i in range(N_SUB)]
    q_cross = [keep_rows(q_glob, CHUNK, lambda r, c=c: r == c) for c in range(1, n_ch)]
    yield
    amask_t = amask_ref[...]
    y = y_att
    out_group = N_HGRN_HEADS
    for first in range(0, N_HGRN_HEADS, out_group):
        heads = range(first, first + out_group)
        lanes = {h: slice(h * HGRN_D, (h + 1) * HGRN_D) for h in heads}
        half = tb // 2
        pair = {}
        for h, ls in lanes.items():
            q_stack = jnp.concatenate([q_sub[i][:, ls] for i in range(N_SUB)], axis=1)
            k_stack = jnp.concatenate([k_sub[i][:, ls] for i in range(N_SUB)], axis=1)
            qc_stack = jnp.concatenate([q_cross[c - 1][:, ls] for c in range(1, n_ch)], axis=1)
            kc_stack = jnp.concatenate([k_cross[c - 1][:, ls] for c in range(1, n_ch)], axis=1)
            pair[h] = (_dot_nt(k_stack[:half], q_stack[:half]), _dot_nt(k_stack[half:], q_stack[half:]),
                       _dot_nt(kc_stack[:half], qc_stack), _dot_nt(kc_stack[half:], qc_stack[half:]))
        yield
        o_t = {}
        for h, ls in lanes.items():
            same_tl, same_br, cross_top, cross_br = pair[h]
            a_t = jnp.concatenate([
                jnp.concatenate([same_tl * amask_t[:half, :half] + cross_top[:, :half], cross_top[:, half:]], axis=1),
                jnp.concatenate([jnp.zeros((half, half), F32), same_br * amask_t[half:, half:] + cross_br], axis=1),
            ], axis=0).astype(BF16)
            vt_h = hv_t[ls, :]
            state_h = src["state"][h]
            o_t[h] = _dot(vt_h, a_t) + _dot_nt(state_h.astype(BF16), q_tile[:, ls])
            dst["state"][h] = state_h * decay_tile[:, ls] + _dot(vt_h, k_tile_end[:, ls])
        yield
        hn_t = {h: o_t[h] * lax.rsqrt(jnp.mean(o_t[h] * o_t[h], axis=0, keepdims=True) + EPS)
                * goutc_ref[...] * hg_act_t[ls, :] for h, ls in lanes.items()}
        grp = jnp.concatenate(
            [jnp.concatenate(
                [jnp.concatenate([hn_t[j], hn_t[j + 1]], axis=0)[:, r:r + LANES].T for j in heads[::2]],
                axis=1) for r in range(0, tb, LANES)], axis=0).astype(BF16)
        y = y + _dot(grp, wout_ref[D_ATTN + first * HGRN_D:D_ATTN + (first + out_group) * HGRN_D, :])
        yield

    dst["y"], dst["k_rot"], dst["v_new"] = x + _rms(y, gpost_ref[...]), k_rot, v_new


def _prompt_mix_kernel(sink_ref, x_ref, gpre_ref, win_ref, wint_ref, cos_ref, shi_ref, slo_ref, lb_ref, goutc_ref,
                       wout_ref, gpost_ref, bias_ref, amask_ref,
                       y_ref, wk_ref, wv_ref, s_ref,
                       kk_scr, vt_scr, st_scr, g_scr, *, layer):
    tb = MIX_TILE
    step = pl.program_id(1)
    last = pl.num_programs(1) - 1

    @pl.when(step == 0)
    def _():
        kk_scr[...] = jnp.zeros_like(kk_scr)
        vt_scr[...] = jnp.zeros_like(vt_scr)
        st_scr[...] = jnp.zeros_like(st_scr)

    hand = [{"state": [None] * N_HGRN_HEADS} for _ in range(MIX_PARTS + 1)]
    hand[0]["k_prev"] = [kk_scr[g] for g in range(N_KV_HEADS)]
    hand[0]["vt_prev"] = vt_scr[...]
    hand[0]["state"] = [st_scr[h] for h in range(N_HGRN_HEADS)]
    tiles = []
    for part in range(MIX_PARTS):
        rs = slice(part * tb, (part + 1) * tb)
        tiles.append(_mix_tile(
            x_ref[rs, :], (cos_ref[rs, :], shi_ref[rs, :], slo_ref[rs, :]), step == 0 if part == 0 else None,
            hand[part], hand[part + 1], g_scr.at[part],
            sink_ref, gpre_ref, win_ref, wint_ref, lb_ref, goutc_ref, wout_ref, gpost_ref, bias_ref, amask_ref,
            layer))
    _interleave(tiles, skew=0)
    for part in range(MIX_PARTS):
        y_ref[part * tb:(part + 1) * tb, :] = hand[part + 1]["y"]
    final = hand[MIX_PARTS]
    for g in range(N_KV_HEADS):
        kk_scr[g] = final["k_prev"][g]
    vt_scr[...] = final["vt_prev"]
    for h in range(N_HGRN_HEADS):
        st_scr[h] = final["state"][h]

    @pl.when(step == last)
    def _():
        wk_ref[0] = final["k_rot"][tb - WINDOW:].T
        wv_ref[0] = final["v_new"][tb - WINDOW:].T
        for h in range(N_HGRN_HEADS):
            s_ref[0, h] = final["state"][h].T


def _prompt_mix(x, sinks, g_pre, w_in, w_in_t, lb_raw, g_out_head, w_out, g_post, batch, seq, layer):
    tb = MIX_TILE
    rows = MIX_PARTS * tb
    nt = seq // rows
    cos, s_hi, s_lo = _rope_tables(np.arange(seq))
    tok = lambda b, n: (b * nt + n, 0)
    tab = lambda b, n: (n, 0)
    per_b3 = lambda b, n: (b, 0, 0)
    lb_rows = lb_raw.shape[0]
    key_i = np.arange(2 * WINDOW)[:, None]
    rel = np.arange(WINDOW)[None, :] + WINDOW - key_i
    bias = jnp.asarray(np.tile(np.where((rel >= 0) & (rel < WINDOW), 0.0, NEG_INF), (1, GQA_GROUP)), F32)
    t_i = np.arange(tb)
    amask = jnp.asarray((t_i[:, None] // CHUNK == t_i[None, :] // CHUNK) & (t_i[:, None] <= t_i[None, :]), F32)
    return pl.pallas_call(
        functools.partial(_prompt_mix_kernel, layer=layer),
        grid=(batch, nt),
        in_specs=[
            pl.BlockSpec(memory_space=pltpu.SMEM),
            pl.BlockSpec((rows, D_MODEL), tok),
            _const_spec((1, D_MODEL)),
            _const_spec((D_MODEL, IN_COLS)),
            _const_spec((2 * D_HGRN, D_MODEL)),
            pl.BlockSpec((rows, LANES), tab),
            pl.BlockSpec((rows, LANES), tab),
            pl.BlockSpec((rows, LANES), tab),
            _const_spec((lb_rows, D_HGRN)),
            _const_spec((HGRN_D, 1)),
            _const_spec((D_MODEL, D_MODEL)),
            _const_spec((1, D_MODEL)),
            _const_spec((2 * WINDOW, GQA_GROUP * WINDOW)),
            _const_spec((tb, tb)),
        ],
        out_specs=[
            pl.BlockSpec((rows, D_MODEL), tok),
            pl.BlockSpec((1, D_KV, WINDOW), per_b3),
            pl.BlockSpec((1, D_KV, WINDOW), per_b3),
            pl.BlockSpec((1, N_HGRN_HEADS, HGRN_D, HGRN_D), lambda b, n: (b, 0, 0, 0)),
        ],
        out_shape=[
            jax.ShapeDtypeStruct((batch * seq, D_MODEL), F32),
            jax.ShapeDtypeStruct((batch, D_KV, WINDOW), F32),
            jax.ShapeDtypeStruct((batch, D_KV, WINDOW), F32),
            jax.ShapeDtypeStruct((batch, N_HGRN_HEADS, HGRN_D, HGRN_D), F32),
        ],
        scratch_shapes=[
            pltpu.VMEM((N_KV_HEADS, WINDOW, LANES), BF16),
            pltpu.VMEM((D_KV, WINDOW), BF16),
            pltpu.VMEM((N_HGRN_HEADS, HGRN_D, HGRN_D), F32),
            pltpu.VMEM((MIX_PARTS, tb, D_HGRN), F32),
        ],
        compiler_params=pltpu.CompilerParams(
            dimension_semantics=("arbitrary", "arbitrary"), vmem_limit_bytes=VMEM_LIMIT),
        name="prompt_mix",
    )(sinks, x, g_pre.reshape(1, D_MODEL), w_in, w_in_t, cos, s_hi, s_lo, lb_raw, g_out_head.reshape(HGRN_D, 1),
      w_out, g_post.reshape(1, D_MODEL), bias, amask)


def _sample_mix_kernel(sink_ref, x_ref, gpre_ref, win_ref, cos_ref, shi_ref, slo_ref, lb_ref, goutc_ref,
                       wout_ref, gpost_ref, ckt_ref, cvt_ref, sin_ref,
                       y_ref, wkt_ref, wvt_ref, sout_ref,
                       q_scr, kn_scr, vn_scr, knt_scr, vnt_scr, ft_scr, hkt_scr, hqt_scr, hvt_scr, hgt_scr,
                       ot_scr, att_scr, *, layer, nb):
    step = pl.program_id(0)
    last = pl.num_programs(0) - 1
    bt = nb // N_HGRN_HEADS

    @pl.when(step == 0)
    def _():
        u = _rms(x_ref[...], gpre_ref[...]).astype(BF16)
        proj = _dot(u, win_ref[...])
        cos, s_hi, s_lo = cos_ref[...], shi_ref[...], slo_ref[...]
        scale = HEAD_DIM ** -0.5
        for j in range(D_ATTN // LANES):
            q_scr[:, LANES * j:LANES * (j + 1)] = _rope(
                proj[:, OFF_Q + LANES * j:OFF_Q + LANES * (j + 1)], cos, s_hi, s_lo) * scale
        k_new = _rope(proj[:, OFF_K:OFF_K + D_KV], cos, s_hi, s_lo)
        v_new = proj[:, OFF_V:OFF_V + D_KV]
        kn_scr[...] = k_new
        vn_scr[...] = v_new
        for scr, val in ((knt_scr, k_new), (vnt_scr, v_new)):
            for i, part in enumerate(_split3(val.T)):
                scr[i] = part
        lb = _lower_bound(lb_ref[...], layer)
        f_t = (lb + (1.0 - lb) * jax.nn.sigmoid(proj[:, OFF_HF:OFF_HF + D_HGRN])).T
        ft_scr[...] = f_t
        hkt_scr[...] = 1.0 - f_t
        hqt_scr[...] = (proj[:, OFF_HQ:OFF_HQ + D_HGRN] * (HGRN_D ** -0.5)).T
        hvt_scr[...] = proj[:, OFF_HI:OFF_HI + D_HGRN].T
        hgt_scr[...] = proj[:, OFF_HG:OFF_HG + D_HGRN].T

    base = pl.multiple_of(step * HGRN_D, HGRN_D)
    hv_t = hvt_scr[pl.ds(base, HGRN_D), :]

    def hgrn_row(k, o_acc):
        f_row = ft_scr[pl.ds(base + k, 1), :]
        s_new = f_row * sin_ref[k] + hkt_scr[pl.ds(base + k, 1), :] * hv_t
        sout_ref[k] = s_new
        return o_acc + hqt_scr[pl.ds(base + k, 1), :] * s_new

    ot_scr[pl.ds(base, HGRN_D), :] = lax.fori_loop(
        0, HGRN_D, hgrn_row, jnp.zeros((HGRN_D, nb), F32), unroll=8)

    lane8 = lax.broadcasted_iota(jnp.int32, (N_Q_HEADS, LANES), 1)
    row8 = lax.broadcasted_iota(jnp.int32, (N_Q_HEADS, LANES), 0)
    keep8 = (lane8 >= HALF) == (row8 >= GQA_GROUP)
    win_lane = lax.broadcasted_iota(jnp.int32, (D_KV, WINDOW), 1)
    sink = sink_ref[...]
    b0 = step * bt
    sel = (lax.broadcasted_iota(jnp.int32, (nb, bt), 0)
           == b0 + lax.broadcasted_iota(jnp.int32, (nb, bt), 1)).astype(BF16)
    k_cols = sum(_dot(knt_scr[i], sel) for i in range(3))
    v_cols = sum(_dot(vnt_scr[i], sel) for i in range(3))
    def one_row(bi):
        b = b0 + bi
        q_b = jnp.broadcast_to(q_scr[pl.ds(b, 1), :], (N_Q_HEADS, D_ATTN))
        qm = jnp.zeros((N_Q_HEADS, LANES), F32)
        for h in range(N_Q_HEADS):
            c = q_b[:, LANES * (h // 2):LANES * (h // 2 + 1)]
            if h % 2 != h // GQA_GROUP:
                c = pltpu.roll(c, HALF, 1)
            qm = jnp.where(row8 == h, c, qm)
        qm = jnp.where(keep8, qm, 0.0)
        k_new = kn_scr[pl.ds(b, 1), :]
        v_new = vn_scr[pl.ds(b, 1), :]
        k_old = ckt_ref[bi]
        s = _dot(qm.astype(BF16), k_old.astype(BF16))
        wkt_ref[bi] = jnp.where(win_lane == WINDOW - 1, k_cols[:, bi:bi + 1], pltpu.roll(k_old, WINDOW - 1, 1))
        yield
        s = jnp.where(lane8 >= 1, s, NEG_INF)
        s_new = jnp.sum(qm * k_new, axis=-1, keepdims=True)
        m = jnp.maximum(jnp.maximum(jnp.max(s, axis=-1, keepdims=True), s_new), sink)
        p = jnp.exp(s - m)
        p_new = jnp.exp(s_new - m)
        denom = jnp.sum(p, axis=-1, keepdims=True) + p_new + jnp.exp(sink - m)
        yield
        v_old = cvt_ref[bi]
        o = (_dot_nt(p.astype(BF16), v_old.astype(BF16)) + p_new * v_new) / denom
        att_scr[pl.ds(pl.multiple_of(b * N_Q_HEADS, N_Q_HEADS), N_Q_HEADS), :] = o
        wvt_ref[bi] = jnp.where(win_lane == WINDOW - 1, v_cols[:, bi:bi + 1], pltpu.roll(v_old, WINDOW - 1, 1))

    _interleave([one_row(bi) for bi in range(bt)], skew=0)

    @pl.when(step == last)
    def _():
        y = jnp.zeros((nb, D_MODEL), F32)
        for h in range(N_HGRN_HEADS):
            ks = slice(h * HGRN_D, (h + 1) * HGRN_D)
            o_t = ot_scr[ks, :]
            hn_t = (o_t * lax.rsqrt(jnp.mean(o_t * o_t, axis=0, keepdims=True) + EPS)
                    * goutc_ref[...] * _silu(hgt_scr[ks, :]))
            y = y + _dot_tn(hn_t.astype(BF16), wout_ref[D_ATTN + h * HGRN_D:D_ATTN + (h + 1) * HGRN_D, :])
        for h in range(N_Q_HEADS):
            g = h // GQA_GROUP
            a_h = att_scr[pl.ds(h, nb, stride=N_Q_HEADS), :][:, g * HALF:(g + 1) * HALF]
            y = y + _dot(a_h.astype(BF16), wout_ref[h * HEAD_DIM:(h + 1) * HEAD_DIM, :])
        y_ref[...] = x_ref[...] + _rms(y, gpost_ref[...])


def _sample_mix(x, sinks, g_pre, w_in, lb_raw, g_out_head, w_out, g_post, cache_kt, cache_vt, state_t, pos, layer):
    nb = x.shape[0]
    bt = nb // N_HGRN_HEADS
    cos, s_hi, s_lo = _rope_tables(pos)
    lb_rows = lb_raw.shape[0]
    blk3 = pl.BlockSpec((bt, D_KV, WINDOW), lambda i: (i, 0, 0))
    blk_s = pl.BlockSpec((HGRN_D, HGRN_D, nb), lambda i: (i, 0, 0))
    chan_major = pltpu.VMEM((D_HGRN, nb), F32)
    return pl.pallas_call(
        functools.partial(_sample_mix_kernel, layer=layer, nb=nb),
        grid=(N_HGRN_HEADS,),
        in_specs=[
            _const_spec((N_Q_HEADS, 1)),
            _const_spec((nb, D_MODEL)),
            _const_spec((1, D_MODEL)),
            _const_spec((D_MODEL, IN_COLS)),
            _const_spec((1, LANES)),
            _const_spec((1, LANES)),
            _const_spec((1, LANES)),
            _const_spec((lb_rows, D_HGRN)),
            _const_spec((HGRN_D, 1)),
            _const_spec((D_MODEL, D_MODEL)),
            _const_spec((1, D_MODEL)),
            blk3, blk3, blk_s,
        ],
        out_specs=[pl.BlockSpec((nb, D_MODEL), lambda i: (0, 0)), blk3, blk3, blk_s],
        out_shape=[
            jax.ShapeDtypeStruct((nb, D_MODEL), F32),
            jax.ShapeDtypeStruct(cache_kt.shape, F32),
            jax.ShapeDtypeStruct(cache_vt.shape, F32),
            jax.ShapeDtypeStruct(state_t.shape, F32),
        ],
        scratch_shapes=[
            pltpu.VMEM((nb, D_ATTN), F32),
            pltpu.VMEM((nb, D_KV), F32),
            pltpu.VMEM((nb, D_KV), F32),
            pltpu.VMEM((3, D_KV, nb), BF16),
            pltpu.VMEM((3, D_KV, nb), BF16),
            chan_major, chan_major, chan_major, chan_major, chan_major, chan_major,
            pltpu.VMEM((nb * N_Q_HEADS, LANES), F32),
        ],
        compiler_params=pltpu.CompilerParams(
            dimension_semantics=("arbitrary",), vmem_limit_bytes=VMEM_LIMIT),
        name="sample_mix",
    )(sinks.reshape(N_Q_HEADS, 1), x, g_pre.reshape(1, D_MODEL), w_in, cos, s_hi, s_lo, lb_raw,
      g_out_head.reshape(HGRN_D, 1), w_out, g_post.reshape(1, D_MODEL), cache_kt, cache_vt, state_t)


def kernel(x_prompt, x_sample, cache_win_k, cache_win_v, state_hgrn, ffn1_pre_g, ffn1_post_g, ffn1_w_gu,
           ffn1_w_down, mix_pre_g, mix_post_g, w_in, attn_sinks, hgrn_lb, hgrn_out_g, w_out, ffn2_pre_g,
           ffn2_post_g, ffn2_w_gu, ffn2_w_down):
    batch, seq, _ = x_prompt.shape
    nb, t_s, _ = x_sample.shape
    depth = w_in.shape[0]
    assert t_s == 1 and seq % (MIX_PARTS * MIX_TILE) == 0 and (batch * seq) % FFN_TILE == 0 and nb == LANES
    assert cache_win_k.shape[2:] == (WINDOW, N_KV_HEADS, HEAD_DIM)

    xp = x_prompt.reshape(batch * seq, D_MODEL)
    xs = x_sample.reshape(nb, D_MODEL)
    pos_s = PAST_LEN + np.arange(t_s)
    outs = [[] for _ in range(6)]
    for l in range(depth):
        w_in_l, w_out_l = w_in[l].astype(BF16), w_out[l].astype(BF16)
        w_in_t = w_in[l][:, OFF_HI:].T.astype(BF16)

        xp, xs = _ffn(xp, xs, ffn1_pre_g[l], ffn1_post_g[l], ffn1_w_gu[l], ffn1_w_down[l])

        xp, wkt_p, wvt_p, s_p = _prompt_mix(xp, attn_sinks[l], mix_pre_g[l], w_in_l, w_in_t, hgrn_lb,
                                            hgrn_out_g[l], w_out_l, mix_post_g[l], batch, seq, l)
        to_kt = lambda c: jnp.transpose(c, (0, 2, 3, 1)).reshape(-1, D_KV, WINDOW)
        from_kt = lambda c: jnp.transpose(c.reshape(-1, N_KV_HEADS, HEAD_DIM, WINDOW), (0, 3, 1, 2))
        state_t = jnp.transpose(state_hgrn[l], (1, 2, 3, 0)).reshape(D_HGRN, HGRN_D, nb)
        xs, wkt_s, wvt_s, st_s = _sample_mix(
            xs, attn_sinks[l], mix_pre_g[l], w_in_l, hgrn_lb, hgrn_out_g[l], w_out_l, mix_post_g[l],
            to_kt(cache_win_k[l]), to_kt(cache_win_v[l]), state_t, pos_s, l)
        s_s = jnp.transpose(st_s.reshape(N_HGRN_HEADS, HGRN_D, HGRN_D, nb), (3, 0, 1, 2))

        xp, xs = _ffn(xp, xs, ffn2_pre_g[l], ffn2_post_g[l], ffn2_w_gu[l], ffn2_w_down[l])

        for lst, val in zip(outs, (from_kt(wkt_p), from_kt(wvt_p), s_p, from_kt(wkt_s), from_kt(wvt_s), s_s)):
            lst.append(val)

    return (xp.reshape(batch, seq, D_MODEL), xs.reshape(nb, t_s, D_MODEL)) + tuple(jnp.stack(o) for o in outs)
```

```python
import functools

import jax
import jax.numpy as jnp
import numpy as np
from jax import lax
from jax.experimental import pallas as pl
from jax.experimental.pallas import tpu as pltpu

F32 = jnp.float32
BF16 = jnp.bfloat16

D_MODEL = 1024
D_FF = 2816
HEAD_DIM = 64
N_Q_HEADS = 8
N_KV_HEADS = 2
GQA_GROUP = N_Q_HEADS // N_KV_HEADS
WINDOW = 128
PAST_LEN = 8192
ROT_DIM = HEAD_DIM // 4
ROPE_THETA = 500000.0
N_HGRN_HEADS = 8
HGRN_D = 64
D_ATTN = N_Q_HEADS * HEAD_DIM
D_KV = N_KV_HEADS * HEAD_DIM
D_HGRN = N_HGRN_HEADS * HGRN_D
IN_COLS = D_ATTN + 2 * D_KV + 4 * D_HGRN
OFF_Q, OFF_K, OFF_V = 0, D_ATTN, D_ATTN + D_KV
OFF_HQ = D_ATTN + 2 * D_KV
OFF_HF, OFF_HI, OFF_HG = OFF_HQ + D_HGRN, OFF_HQ + 2 * D_HGRN, OFF_HQ + 3 * D_HGRN
EPS = 1e-6
NEG_INF = -1e30
LOG2E = 1.4426950408889634
LANES = 128
HALF = LANES // 2

FFN_TILE = 1024
FFN_PARTS = 4
FFN_W_STEPS = 8
MIX_TILE = 256
MIX_PARTS = 2
CHUNK = 64
SUB = 16
N_SUB = CHUNK // SUB
VMEM_LIMIT = 56 * 1024 * 1024


def _rms(x, g):
    return x * lax.rsqrt(jnp.mean(x * x, axis=-1, keepdims=True) + EPS) * g


def _silu(x):
    return x * jax.nn.sigmoid(x)


def _dot(a, b):
    return jnp.dot(a, b, preferred_element_type=F32)


def _dot_nt(a, b):
    return lax.dot_general(a, b, (((1,), (1,)), ((), ())), preferred_element_type=F32)


def _dot_tn(a, b):
    return lax.dot_general(a, b, (((0,), (0,)), ((), ())), preferred_element_type=F32)


def _split3(x):
    hi = x.astype(BF16)
    r = x - hi.astype(F32)
    mid = r.astype(BF16)
    lo = (r - mid.astype(F32)).astype(BF16)
    return hi, mid, lo


def _interleave(gens, skew):
    pending, active, rnd = list(gens), [], 0
    while pending or active:
        while pending and (skew == 0 or rnd % skew == 0):
            active.append(pending.pop(0))
            if skew:
                break
        active = [g for g in active if next(g, True) is None]
        rnd += 1


def _const_spec(shape):
    nd = len(shape)
    return pl.BlockSpec(shape, lambda *_: (0,) * nd, pipeline_mode=pl.Buffered(1))


def _ffn_kernel(x_ref, xs_ref, gpre_ref, gpost_ref, wgu32_ref, wd32_ref, o_ref, os_ref, wgu_ref, wd_ref):
    step = pl.program_id(0)

    @pl.when(step < FFN_W_STEPS)
    def _():
        gu_rows, d_rows = wgu32_ref.shape[0], wd32_ref.shape[0]
        wgu_ref[pl.ds(pl.multiple_of(step * gu_rows, gu_rows), gu_rows), :] = wgu32_ref[...].astype(BF16)
        wd_ref[pl.ds(pl.multiple_of(step * d_rows, d_rows), d_rows), :] = wd32_ref[...].astype(BF16)

    def half_step(src_ref, dst_ref, rs):
        x = src_ref[rs, :]
        h = _rms(x, gpre_ref[...]).astype(BF16)
        yield
        gate = _dot(h, wgu_ref[:, :D_FF])
        up = _dot(h, wgu_ref[:, D_FF:])
        yield
        act = (_silu(gate) * up).astype(BF16)
        yield
        y = _dot(act, wd_ref[...])
        yield
        dst_ref[rs, :] = x + 0.5 * _rms(y, gpost_ref[...])

    @pl.when(step >= FFN_W_STEPS)
    def _():
        rows = x_ref.shape[0] // FFN_PARTS
        _interleave([half_step(x_ref, o_ref, slice(r * rows, (r + 1) * rows)) for r in range(FFN_PARTS)], skew=1)

    @pl.when(step == pl.num_programs(0) - 1)
    def _():
        for _ in half_step(xs_ref, os_ref, slice(None)):
            pass


def _ffn(x, xs, g_pre, g_post, w_gu, w_down):
    n, nb = x.shape[0], xs.shape[0]
    tile_map = lambda i: (jnp.maximum(i - FFN_W_STEPS, 0), 0)
    chunk_map = lambda i: (jnp.minimum(i, FFN_W_STEPS - 1), 0)
    return pl.pallas_call(
        _ffn_kernel,
        grid=(FFN_W_STEPS + n // FFN_TILE,),
        in_specs=[
            pl.BlockSpec((FFN_TILE, D_MODEL), tile_map),
            _const_spec((nb, D_MODEL)),
            _const_spec((1, D_MODEL)),
            _const_spec((1, D_MODEL)),
            pl.BlockSpec((D_MODEL // FFN_W_STEPS, 2 * D_FF), chunk_map),
            pl.BlockSpec((D_FF // FFN_W_STEPS, D_MODEL), chunk_map),
        ],
        out_specs=[pl.BlockSpec((FFN_TILE, D_MODEL), tile_map),
                   pl.BlockSpec((nb, D_MODEL), lambda i: (0, 0))],
        out_shape=[jax.ShapeDtypeStruct((n, D_MODEL), F32), jax.ShapeDtypeStruct((nb, D_MODEL), F32)],
        scratch_shapes=[pltpu.VMEM((D_MODEL, 2 * D_FF), BF16), pltpu.VMEM((D_FF, D_MODEL), BF16)],
        compiler_params=pltpu.CompilerParams(
            dimension_semantics=("arbitrary",), vmem_limit_bytes=VMEM_LIMIT),
        name="ffn",
    )(x, xs, g_pre.reshape(1, D_MODEL), g_post.reshape(1, D_MODEL), w_gu, w_down)


def _lower_bound(lb_raw, layer):
    m = jnp.max(lb_raw, axis=0, keepdims=True)
    e = jnp.exp(lb_raw - m)
    return jnp.sum(e[: layer + 1], axis=0, keepdims=True) / jnp.sum(e, axis=0, keepdims=True)


def _rope(x, cos, sin_hi, sin_lo):
    return x * cos + pltpu.roll(x, ROT_DIM // 2, 1) * sin_hi + pltpu.roll(x, LANES - ROT_DIM // 2, 1) * sin_lo


def _rope_tables(pos):
    half = ROT_DIM // 2
    inv = (np.float32(ROPE_THETA) ** (-np.arange(half, dtype=np.float32) / half)).astype(np.float64)
    ang = np.asarray(pos, np.float64)[:, None] * inv[None, :]
    cos, sin = np.cos(ang), np.sin(ang)
    t = ang.shape[0]
    one = np.ones((t, HEAD_DIM - ROT_DIM))
    zero = np.zeros((t, HEAD_DIM - ROT_DIM))
    zh = np.zeros((t, half))
    c = np.concatenate([cos, cos, one], axis=1)
    s_hi = np.concatenate([zh, sin, zero], axis=1)
    s_lo = np.concatenate([-sin, zh, zero], axis=1)
    return tuple(jnp.asarray(np.tile(a, (1, LANES // HEAD_DIM)), F32) for a in (c, s_hi, s_lo))


def _dup_half(x, g, lo_half):
    xr = pltpu.roll(x, HALF, 1)
    return jnp.where(lo_half, x, xr) if g == 0 else jnp.where(lo_half, xr, x)


def _mix_tile(x, rope_tab, seq_start, src, dst, g_scr,
              sink_ref, gpre_ref, win_ref, wint_ref, lb_ref, goutc_ref, wout_ref, gpost_ref, bias_ref, amask_ref,
              layer):
    tb = MIX_TILE
    u = _rms(x, gpre_ref[...]).astype(BF16)

    def proj(off, width):
        return _dot(u, win_ref[:, off:off + width])

    p_attn = proj(OFF_Q, D_ATTN + 2 * D_KV)
    p_hf = proj(OFF_HF, D_HGRN)
    cos, s_hi, s_lo = rope_tab
    scale = HEAD_DIM ** -0.5 * LOG2E
    q_cols = [_rope(p_attn[:, LANES * j: LANES * (j + 1)], cos, s_hi, s_lo) * scale
              for j in range(D_ATTN // LANES)]
    k_rot = _rope(p_attn[:, OFF_K:OFF_K + D_KV], cos, s_hi, s_lo)
    v_new = p_attn[:, OFF_V:OFF_V + D_KV]

    lane = lax.broadcasted_iota(jnp.int32, (WINDOW, LANES), 1)
    lo_half = lane < HALF
    bias = bias_ref[...]
    if seq_start is None:
        bias_first = bias
    else:
        no_prev = jnp.where(seq_start, NEG_INF, 0.0)
        bias_first = jnp.concatenate([bias[:WINDOW] + no_prev, bias[WINDOW:]], axis=0)

    vt_tile = v_new.T.astype(BF16)
    yield
    k_prev, vt_prev = src["k_prev"], src["vt_prev"]
    scores = []
    for i in range(tb // WINDOW):
        r0 = i * WINDOW
        k_cur = [_dup_half(k_rot[r0:r0 + WINDOW], g, lo_half).astype(BF16) for g in range(N_KV_HEADS)]
        vt_keys = jnp.concatenate([vt_prev, vt_tile[:, r0:r0 + WINDOW]], axis=1)
        for g in range(N_KV_HEADS):
            keys = jnp.concatenate([k_prev[g], k_cur[g]], axis=0)
            heads = range(g * GQA_GROUP, (g + 1) * GQA_GROUP)
            qg = jnp.concatenate(
                [jnp.where(lo_half if h % 2 == 0 else ~lo_half, q_cols[h // 2][r0:r0 + WINDOW], 0.0)
                 for h in heads], axis=0).astype(BF16)
            s = _dot_nt(keys, qg) + (bias_first if i == 0 else bias)
            scores.append((s, vt_keys[g * HEAD_DIM:(g + 1) * HEAD_DIM], heads))
        k_prev, vt_prev = k_cur, vt_tile[:, r0:r0 + WINDOW]
    dst["k_prev"], dst["vt_prev"] = k_prev, vt_prev
    yield

    hq = proj(OFF_HQ, D_HGRN) * (HGRN_D ** -0.5)
    hv_t = _dot_nt(wint_ref[:D_HGRN, :], u).astype(BF16)
    hg_act_t = _silu(_dot_nt(wint_ref[D_HGRN:, :], u))
    yield

    att_t = [[None] * (tb // WINDOW) for _ in range(N_Q_HEADS)]
    probs = []
    for s, _, heads in scores:
        sink = jnp.concatenate(
            [jnp.full((1, WINDOW), sink_ref[h] * LOG2E, F32) for h in heads], axis=1)
        m = jnp.maximum(jnp.max(s, axis=0, keepdims=True), sink)
        p = jnp.exp2(s - m)
        denom = jnp.sum(p, axis=0, keepdims=True) + jnp.exp2(sink - m)
        probs.append((p.astype(BF16), 1.0 / denom))
    yield
    for idx, ((p, r_denom), (_, vt_g, heads)) in enumerate(zip(probs, scores)):
        o_t = _dot(vt_g, p) * r_denom
        for j, h in enumerate(heads):
            att_t[h][idx // N_KV_HEADS] = o_t[:, j * WINDOW:(j + 1) * WINDOW]
    att = jnp.concatenate(
        [jnp.concatenate([jnp.concatenate([att_t[2 * j][i], att_t[2 * j + 1][i]], axis=0).T
                          for j in range(N_Q_HEADS // 2)], axis=1)
         for i in range(tb // WINDOW)], axis=0)
    y_att = _dot(att.astype(BF16), wout_ref[:D_ATTN, :])
    yield

    lb = _lower_bound(lb_ref[...], layer)
    f = lb + (1.0 - lb) * jax.nn.sigmoid(p_hf)
    logf = jnp.log(f) * LOG2E
    hk = 1.0 - f

    tr = lax.broadcasted_iota(jnp.int32, (tb, tb), 0)
    tc = lax.broadcasted_iota(jnp.int32, (tb, tb), 1)
    tri = ((tr // CHUNK == tc // CHUNK) & (tc <= tr)).astype(BF16)
    g_cum = sum(_dot(tri, part) for part in _split3(logf))
    g_scr[...] = g_cum
    yield

    def bcast_row(r, rows):
        return jnp.broadcast_to(g_scr[r:r + 1, :], (rows, D_HGRN))

    zeros_sub = jnp.zeros((SUB, D_HGRN), F32)
    g_ref_q = jnp.concatenate(
        [zeros_sub if sb % N_SUB == 0 else bcast_row(sb * SUB - 1, SUB) for sb in range(tb // SUB)], axis=0)
    q_loc = (hq * jnp.exp2(g_cum - g_ref_q)).astype(BF16)
    g_end = jnp.concatenate(
        [bcast_row(c * CHUNK + CHUNK - 1, CHUNK) for c in range(tb // CHUNK)], axis=0)
    k_sub = []
    for i in range(N_SUB):
        live = (i + 1) * SUB
        pieces = []
        for c in range(tb // CHUNK):
            r0 = c * CHUNK
            g_ref_i = 0.0 if i == 0 else bcast_row(r0 + i * SUB - 1, live)
            pieces.append(hk[r0:r0 + live] * jnp.exp2(g_ref_i - g_cum[r0:r0 + live]))
            if live < CHUNK:
                pieces.append(jnp.zeros((CHUNK - live, D_HGRN), F32))
        k_sub.append(jnp.concatenate(pieces, axis=0).astype(BF16))

    n_ch = tb // CHUNK
    gc = [g_scr[c * CHUNK + CHUNK - 1:c * CHUNK + CHUNK, :] for c in range(n_ch)]

    def span(lo, hi):
        if hi <= lo:
            return jnp.ones((CHUNK, D_HGRN), BF16)
        return jnp.broadcast_to(jnp.exp2(sum(gc[lo:hi])).astype(BF16), (CHUNK, D_HGRN))

    zeros_chunk = jnp.zeros((CHUNK, D_HGRN), BF16)
    k_end = (hk * jnp.exp2(g_end - g_cum)).astype(BF16)
    q_glob = (hq * jnp.exp2(g_cum)).astype(BF16)
    k_cross = [k_end * jnp.concatenate(
        [span(cp + 1, c) if cp < c else zeros_chunk for cp in range(n_ch)], axis=0)
        for c in range(1, n_ch)]
    q_tile = q_glob * jnp.concatenate([span(0, c) for c in range(n_ch)], axis=0)
    k_tile_end = k_end * jnp.concatenate([span(cp + 1, n_ch) for cp in range(n_ch)], axis=0)
    decay_tile = jnp.exp2(sum(gc))

    def keep_rows(a, block, wanted):
        zero = jnp.zeros((block, a.shape[1]), a.dtype)
        return jnp.concatenate(
            [a[r * block:(r + 1) * block] if wanted(r) else zero for r in range(a.shape[0] // block)], axis=0)

    q_sub = [keep_rows(q_loc, SUB, lambda r, i=i: r % N_SUB == i) for i in range(N_SUB)]
    q_cross = [keep_rows(q_glob, CHUNK, lambda r, c=c: r == c) for c in range(1, n_ch)]
    yield
    amask_t = amask_ref[...]
    y = y_att
    out_group = N_HGRN_HEADS
    for first in range(0, N_HGRN_HEADS, out_group):
        heads = range(first, first + out_group)
        lanes = {h: slice(h * HGRN_D, (h + 1) * HGRN_D) for h in heads}
        pair = {}
        for h, ls in lanes.items():
            q_stack = jnp.concatenate([q_sub[i][:, ls] for i in range(N_SUB)], axis=1)
            k_stack = jnp.concatenate([k_sub[i][:, ls] for i in range(N_SUB)], axis=1)
            qc_stack = jnp.concatenate([q_cross[c - 1][:, ls] for c in range(1, n_ch)], axis=1)
            kc_stack = jnp.concatenate([k_cross[c - 1][:, ls] for c in range(1, n_ch)], axis=1)
            pair[h] = (_dot_nt(k_stack, q_stack), _dot_nt(kc_stack, qc_stack))
        yield
        o_t = {}
        for h, ls in lanes.items():
            same, cross = pair[h]
            half = tb // 2
            a_t = jnp.concatenate([
                jnp.concatenate([same[:half, :half] * amask_t[:half, :half] + cross[:half, :half],
                                 cross[:half, half:]], axis=1),
                jnp.concatenate([jnp.zeros((half, half), F32),
                                 same[half:, half:] * amask_t[half:, half:] + cross[half:, half:]], axis=1),
            ], axis=0).astype(BF16)
            vt_h = hv_t[ls, :]
            state_h = src["state"][h]
            o_t[h] = _dot(vt_h, a_t) + _dot_nt(state_h.astype(BF16), q_tile[:, ls])
            dst["state"][h] = state_h * decay_tile[:, ls] + _dot(vt_h, k_tile_end[:, ls])
        yield
        hn_t = {h: o_t[h] * lax.rsqrt(jnp.mean(o_t[h] * o_t[h], axis=0, keepdims=True) + EPS)
                * goutc_ref[...] * hg_act_t[ls, :] for h, ls in lanes.items()}
        grp = jnp.concatenate(
            [jnp.concatenate(
                [jnp.concatenate([hn_t[j], hn_t[j + 1]], axis=0)[:, r:r + LANES].T for j in heads[::2]],
                axis=1) for r in range(0, tb, LANES)], axis=0).astype(BF16)
        y = y + _dot(grp, wout_ref[D_ATTN + first * HGRN_D:D_ATTN + (first + out_group) * HGRN_D, :])
        yield

    dst["y"], dst["k_rot"], dst["v_new"] = x + _rms(y, gpost_ref[...]), k_rot, v_new


def _prompt_mix_kernel(sink_ref, x_ref, gpre_ref, win_ref, wint_ref, cos_ref, shi_ref, slo_ref, lb_ref, goutc_ref,
                       wout_ref, gpost_ref, bias_ref, amask_ref,
                       y_ref, wk_ref, wv_ref, s_ref,
                       kk_scr, vt_scr, st_scr, g_scr, *, layer):
    tb = MIX_TILE
    step = pl.program_id(1)
    last = pl.num_programs(1) - 1

    @pl.when(step == 0)
    def _():
        kk_scr[...] = jnp.zeros_like(kk_scr)
        vt_scr[...] = jnp.zeros_like(vt_scr)
        st_scr[...] = jnp.zeros_like(st_scr)

    hand = [{"state": [None] * N_HGRN_HEADS} for _ in range(MIX_PARTS + 1)]
    hand[0]["k_prev"] = [kk_scr[g] for g in range(N_KV_HEADS)]
    hand[0]["vt_prev"] = vt_scr[...]
    hand[0]["state"] = [st_scr[h] for h in range(N_HGRN_HEADS)]
    tiles = []
    for part in range(MIX_PARTS):
        rs = slice(part * tb, (part + 1) * tb)
        tiles.append(_mix_tile(
            x_ref[rs, :], (cos_ref[rs, :], shi_ref[rs, :], slo_ref[rs, :]), step == 0 if part == 0 else None,
            hand[part], hand[part + 1], g_scr.at[part],
            sink_ref, gpre_ref, win_ref, wint_ref, lb_ref, goutc_ref, wout_ref, gpost_ref, bias_ref, amask_ref,
            layer))
    _interleave(tiles, skew=0)
    for part in range(MIX_PARTS):
        y_ref[part * tb:(part + 1) * tb, :] = hand[part + 1]["y"]
    final = hand[MIX_PARTS]
    for g in range(N_KV_HEADS):
        kk_scr[g] = final["k_prev"][g]
    vt_scr[...] = final["vt_prev"]
    for h in range(N_HGRN_HEADS):
        st_scr[h] = final["state"][h]

    @pl.when(step == last)
    def _():
        wk_ref[0] = final["k_rot"][tb - WINDOW:].T
        wv_ref[0] = final["v_new"][tb - WINDOW:].T
        for h in range(N_HGRN_HEADS):
            s_ref[0, h] = final["state"][h].T


def _prompt_mix(x, sinks, g_pre, w_in, w_in_t, lb_raw, g_out_head, w_out, g_post, batch, seq, layer):
    tb = MIX_TILE
    rows = MIX_PARTS * tb
    nt = seq // rows
    cos, s_hi, s_lo = _rope_tables(np.arange(seq))
    tok = lambda b, n: (b * nt + n, 0)
    tab = lambda b, n: (n, 0)
    per_b3 = lambda b, n: (b, 0, 0)
    lb_rows = lb_raw.shape[0]
    key_i = np.arange(2 * WINDOW)[:, None]
    rel = np.arange(WINDOW)[None, :] + WINDOW - key_i
    bias = jnp.asarray(np.tile(np.where((rel >= 0) & (rel < WINDOW), 0.0, NEG_INF), (1, GQA_GROUP)), F32)
    t_i = np.arange(tb)
    amask = jnp.asarray((t_i[:, None] // CHUNK == t_i[None, :] // CHUNK) & (t_i[:, None] <= t_i[None, :]), F32)
    return pl.pallas_call(
        functools.partial(_prompt_mix_kernel, layer=layer),
        grid=(batch, nt),
        in_specs=[
            pl.BlockSpec(memory_space=pltpu.SMEM),
            pl.BlockSpec((rows, D_MODEL), tok),
            _const_spec((1, D_MODEL)),
            _const_spec((D_MODEL, IN_COLS)),
            _const_spec((2 * D_HGRN, D_MODEL)),
            pl.BlockSpec((rows, LANES), tab),
            pl.BlockSpec((rows, LANES), tab),
            pl.BlockSpec((rows, LANES), tab),
            _const_spec((lb_rows, D_HGRN)),
            _const_spec((HGRN_D, 1)),
            _const_spec((D_MODEL, D_MODEL)),
            _const_spec((1, D_MODEL)),
            _const_spec((2 * WINDOW, GQA_GROUP * WINDOW)),
            _const_spec((tb, tb)),
        ],
        out_specs=[
            pl.BlockSpec((rows, D_MODEL), tok),
            pl.BlockSpec((1, D_KV, WINDOW), per_b3),
            pl.BlockSpec((1, D_KV, WINDOW), per_b3),
            pl.BlockSpec((1, N_HGRN_HEADS, HGRN_D, HGRN_D), lambda b, n: (b, 0, 0, 0)),
        ],
        out_shape=[
            jax.ShapeDtypeStruct((batch * seq, D_MODEL), F32),
            jax.ShapeDtypeStruct((batch, D_KV, WINDOW), F32),
            jax.ShapeDtypeStruct((batch, D_KV, WINDOW), F32),
            jax.ShapeDtypeStruct((batch, N_HGRN_HEADS, HGRN_D, HGRN_D), F32),
        ],
        scratch_shapes=[
            pltpu.VMEM((N_KV_HEADS, WINDOW, LANES), BF16),
            pltpu.VMEM((D_KV, WINDOW), BF16),
            pltpu.VMEM((N_HGRN_HEADS, HGRN_D, HGRN_D), F32),
            pltpu.VMEM((MIX_PARTS, tb, D_HGRN), F32),
        ],
        compiler_params=pltpu.CompilerParams(
            dimension_semantics=("arbitrary", "arbitrary"), vmem_limit_bytes=VMEM_LIMIT),
        name="prompt_mix",
    )(sinks, x, g_pre.reshape(1, D_MODEL), w_in, w_in_t, cos, s_hi, s_lo, lb_raw, g_out_head.reshape(HGRN_D, 1),
      w_out, g_post.reshape(1, D_MODEL), bias, amask)


def _sample_mix_kernel(sink_ref, x_ref, gpre_ref, win_ref, cos_ref, shi_ref, slo_ref, lb_ref, goutc_ref,
                       wout_ref, gpost_ref, ckt_ref, cvt_ref, sin_ref,
                       y_ref, wkt_ref, wvt_ref, sout_ref,
                       q_scr, kn_scr, vn_scr, knt_scr, vnt_scr, ft_scr, hkt_scr, hqt_scr, hvt_scr, hgt_scr,
                       ot_scr, att_scr, *, layer, nb):
    step = pl.program_id(0)
    last = pl.num_programs(0) - 1
    bt = nb // N_HGRN_HEADS

    @pl.when(step == 0)
    def _():
        u = _rms(x_ref[...], gpre_ref[...]).astype(BF16)
        proj = _dot(u, win_ref[...])
        cos, s_hi, s_lo = cos_ref[...], shi_ref[...], slo_ref[...]
        scale = HEAD_DIM ** -0.5
        for j in range(D_ATTN // LANES):
            q_scr[:, LANES * j:LANES * (j + 1)] = _rope(
                proj[:, OFF_Q + LANES * j:OFF_Q + LANES * (j + 1)], cos, s_hi, s_lo) * scale
        k_new = _rope(proj[:, OFF_K:OFF_K + D_KV], cos, s_hi, s_lo)
        v_new = proj[:, OFF_V:OFF_V + D_KV]
        kn_scr[...] = k_new
        vn_scr[...] = v_new
        for scr, val in ((knt_scr, k_new), (vnt_scr, v_new)):
            for i, part in enumerate(_split3(val.T)):
                scr[i] = part
        lb = _lower_bound(lb_ref[...], layer)
        f_t = (lb + (1.0 - lb) * jax.nn.sigmoid(proj[:, OFF_HF:OFF_HF + D_HGRN])).T
        ft_scr[...] = f_t
        hkt_scr[...] = 1.0 - f_t
        hqt_scr[...] = (proj[:, OFF_HQ:OFF_HQ + D_HGRN] * (HGRN_D ** -0.5)).T
        hvt_scr[...] = proj[:, OFF_HI:OFF_HI + D_HGRN].T
        hgt_scr[...] = proj[:, OFF_HG:OFF_HG + D_HGRN].T

    base = pl.multiple_of(step * HGRN_D, HGRN_D)
    hv_t = hvt_scr[pl.ds(base, HGRN_D), :]

    def hgrn_row(k, o_acc):
        f_row = ft_scr[pl.ds(base + k, 1), :]
        s_new = f_row * sin_ref[k] + hkt_scr[pl.ds(base + k, 1), :] * hv_t
        sout_ref[k] = s_new
        return o_acc + hqt_scr[pl.ds(base + k, 1), :] * s_new

    ot_scr[pl.ds(base, HGRN_D), :] = lax.fori_loop(
        0, HGRN_D, hgrn_row, jnp.zeros((HGRN_D, nb), F32), unroll=8)

    lane8 = lax.broadcasted_iota(jnp.int32, (N_Q_HEADS, LANES), 1)
    row8 = lax.broadcasted_iota(jnp.int32, (N_Q_HEADS, LANES), 0)
    keep8 = (lane8 >= HALF) == (row8 >= GQA_GROUP)
    win_lane = lax.broadcasted_iota(jnp.int32, (D_KV, WINDOW), 1)
    sink = sink_ref[...]
    b0 = step * bt
    sel = (lax.broadcasted_iota(jnp.int32, (nb, bt), 0)
           == b0 + lax.broadcasted_iota(jnp.int32, (nb, bt), 1)).astype(BF16)
    k_cols = sum(_dot(knt_scr[i], sel) for i in range(3))
    v_cols = sum(_dot(vnt_scr[i], sel) for i in range(3))
    def one_row(bi):
        b = b0 + bi
        q_b = jnp.broadcast_to(q_scr[pl.ds(b, 1), :], (N_Q_HEADS, D_ATTN))
        qm = jnp.zeros((N_Q_HEADS, LANES), F32)
        for h in range(N_Q_HEADS):
            c = q_b[:, LANES * (h // 2):LANES * (h // 2 + 1)]
            if h % 2 != h // GQA_GROUP:
                c = pltpu.roll(c, HALF, 1)
            qm = jnp.where(row8 == h, c, qm)
        qm = jnp.where(keep8, qm, 0.0)
        k_new = kn_scr[pl.ds(b, 1), :]
        v_new = vn_scr[pl.ds(b, 1), :]
        k_old = ckt_ref[bi]
        s = _dot(qm.astype(BF16), k_old.astype(BF16))
        wkt_ref[bi] = jnp.where(win_lane == WINDOW - 1, k_cols[:, bi:bi + 1], pltpu.roll(k_old, WINDOW - 1, 1))
        yield
        s = jnp.where(lane8 >= 1, s, NEG_INF)
        s_new = jnp.sum(qm * k_new, axis=-1, keepdims=True)
        m = jnp.maximum(jnp.maximum(jnp.max(s, axis=-1, keepdims=True), s_new), sink)
        p = jnp.exp(s - m)
        p_new = jnp.exp(s_new - m)
        denom = jnp.sum(p, axis=-1, keepdims=True) + p_new + jnp.exp(sink - m)
        yield
        v_old = cvt_ref[bi]
        o = (_dot_nt(p.astype(BF16), v_old.astype(BF16)) + p_new * v_new) / denom
        att_scr[pl.ds(pl.multiple_of(b * N_Q_HEADS, N_Q_HEADS), N_Q_HEADS), :] = o
        wvt_ref[bi] = jnp.where(win_lane == WINDOW - 1, v_cols[:, bi:bi + 1], pltpu.roll(v_old, WINDOW - 1, 1))

    _interleave([one_row(bi) for bi in range(bt)], skew=0)

    @pl.when(step == last)
    def _():
        y = jnp.zeros((nb, D_MODEL), F32)
        for h in range(N_HGRN_HEADS):
            ks = slice(h * HGRN_D, (h + 1) * HGRN_D)
            o_t = ot_scr[ks, :]
            hn_t = (o_t * lax.rsqrt(jnp.mean(o_t * o_t, axis=0, keepdims=True) + EPS)
                    * goutc_ref[...] * _silu(hgt_scr[ks, :]))
            y = y + _dot_tn(hn_t.astype(BF16), wout_ref[D_ATTN + h * HGRN_D:D_ATTN + (h + 1) * HGRN_D, :])
        for h in range(N_Q_HEADS):
            g = h // GQA_GROUP
            a_h = att_scr[pl.ds(h, nb, stride=N_Q_HEADS), :][:, g * HALF:(g + 1) * HALF]
            y = y + _dot(a_h.astype(BF16), wout_ref[h * HEAD_DIM:(h + 1) * HEAD_DIM, :])
        y_ref[...] = x_ref[...] + _rms(y, gpost_ref[...])


def _sample_mix(x, sinks, g_pre, w_in, lb_raw, g_out_head, w_out, g_post, cache_kt, cache_vt, state_t, pos, layer):
    nb = x.shape[0]
    bt = nb // N_HGRN_HEADS
    cos, s_hi, s_lo = _rope_tables(pos)
    lb_rows = lb_raw.shape[0]
    blk3 = pl.BlockSpec((bt, D_KV, WINDOW), lambda i: (i, 0, 0))
    blk_s = pl.BlockSpec((HGRN_D, HGRN_D, nb), lambda i: (i, 0, 0))
    chan_major = pltpu.VMEM((D_HGRN, nb), F32)
    return pl.pallas_call(
        functools.partial(_sample_mix_kernel, layer=layer, nb=nb),
        grid=(N_HGRN_HEADS,),
        in_specs=[
            _const_spec((N_Q_HEADS, 1)),
            _const_spec((nb, D_MODEL)),
            _const_spec((1, D_MODEL)),
            _const_spec((D_MODEL, IN_COLS)),
            _const_spec((1, LANES)),
            _const_spec((1, LANES)),
            _const_spec((1, LANES)),
            _const_spec((lb_rows, D_HGRN)),
            _const_spec((HGRN_D, 1)),
            _const_spec((D_MODEL, D_MODEL)),
            _const_spec((1, D_MODEL)),
            blk3, blk3, blk_s,
        ],
        out_specs=[pl.BlockSpec((nb, D_MODEL), lambda i: (0, 0)), blk3, blk3, blk_s],
        out_shape=[
            jax.ShapeDtypeStruct((nb, D_MODEL), F32),
            jax.ShapeDtypeStruct(cache_kt.shape, F32),
            jax.ShapeDtypeStruct(cache_vt.shape, F32),
            jax.ShapeDtypeStruct(state_t.shape, F32),
        ],
        scratch_shapes=[
            pltpu.VMEM((nb, D_ATTN), F32),
            pltpu.VMEM((nb, D_KV), F32),
            pltpu.VMEM((nb, D_KV), F32),
            pltpu.VMEM((3, D_KV, nb), BF16),
            pltpu.VMEM((3, D_KV, nb), BF16),
            chan_major, chan_major, chan_major, chan_major, chan_major, chan_major,
            pltpu.VMEM((nb * N_Q_HEADS, LANES), F32),
        ],
        compiler_params=pltpu.CompilerParams(
            dimension_semantics=("arbitrary",), vmem_limit_bytes=VMEM_LIMIT),
        name="sample_mix",
    )(sinks.reshape(N_Q_HEADS, 1), x, g_pre.reshape(1, D_MODEL), w_in, cos, s_hi, s_lo, lb_raw,
      g_out_head.reshape(HGRN_D, 1), w_out, g_post.reshape(1, D_MODEL), cache_kt, cache_vt, state_t)


def kernel(x_prompt, x_sample, cache_win_k, cache_win_v, state_hgrn, ffn1_pre_g, ffn1_post_g, ffn1_w_gu,
           ffn1_w_down, mix_pre_g, mix_post_g, w_in, attn_sinks, hgrn_lb, hgrn_out_g, w_out, ffn2_pre_g,
           ffn2_post_g, ffn2_w_gu, ffn2_w_down):
    batch, seq, _ = x_prompt.shape
    nb, t_s, _ = x_sample.shape
    depth = w_in.shape[0]
    assert t_s == 1 and seq % (MIX_PARTS * MIX_TILE) == 0 and (batch * seq) % FFN_TILE == 0 and nb == LANES
    assert cache_win_k.shape[2:] == (WINDOW, N_KV_HEADS, HEAD_DIM)

    xp = x_prompt.reshape(batch * seq, D_MODEL)
    xs = x_sample.reshape(nb, D_MODEL)
    pos_s = PAST_LEN + np.arange(t_s)
    outs = [[] for _ in range(6)]
    for l in range(depth):
        w_in_l, w_out_l = w_in[l].astype(BF16), w_out[l].astype(BF16)
        w_in_t = w_in[l][:, OFF_HI:].T.astype(BF16)

        xp, xs = _ffn(xp, xs, ffn1_pre_g[l], ffn1_post_g[l], ffn1_w_gu[l], ffn1_w_down[l])

        xp, wkt_p, wvt_p, s_p = _prompt_mix(xp, attn_sinks[l], mix_pre_g[l], w_in_l, w_in_t, hgrn_lb,
                                            hgrn_out_g[l], w_out_l, mix_post_g[l], batch, seq, l)
        to_kt = lambda c: jnp.transpose(c, (0, 2, 3, 1)).reshape(-1, D_KV, WINDOW)
        from_kt = lambda c: jnp.transpose(c.reshape(-1, N_KV_HEADS, HEAD_DIM, WINDOW), (0, 3, 1, 2))
        state_t = jnp.transpose(state_hgrn[l], (1, 2, 3, 0)).reshape(D_HGRN, HGRN_D, nb)
        xs, wkt_s, wvt_s, st_s = _sample_mix(
            xs, attn_sinks[l], mix_pre_g[l], w_in_l, hgrn_lb, hgrn_out_g[l], w_out_l, mix_post_g[l],
            to_kt(cache_win_k[l]), to_kt(cache_win_v[l]), state_t, pos_s, l)
        s_s = jnp.transpose(st_s.reshape(N_HGRN_HEADS, HGRN_D, HGRN_D, nb), (3, 0, 1, 2))

        xp, xs = _ffn(xp, xs, ffn2_pre_g[l], ffn2_post_g[l], ffn2_w_gu[l], ffn2_w_down[l])

        for lst, val in zip(outs, (from_kt(wkt_p), from_kt(wvt_p), s_p, from_kt(wkt_s), from_kt(wvt_s), s_s)):
            lst.append(val)

    return (xp.reshape(batch, seq, D_MODEL), xs.reshape(nb, t_s, D_MODEL)) + tuple(jnp.stack(o) for o in outs)
```
